```python
import jax, jax.numpy as jnp
from jax import lax
import numpy as np

D_MODEL = 1024
BATCH = 8
SEQ = 8192
DEPTH = 2

N_MIXERS = 2
CONV_WIDTH = 31
POOL_WINDOWS = (2, 4, 8, 16)
N_POOL_GROUPS = len(POOL_WINDOWS)
POOL_GROUP_DIM = D_MODEL // N_POOL_GROUPS
D_FF = ((8 * D_MODEL // 3 + 255) // 256) * 256
N_ADA = 6
EPS = 1e-6
N_CONV_LAYERS = (DEPTH + 1) // 2
N_POOL_LAYERS = DEPTH // 2

kernel_name = "hybrid_conformer_conv_multiscale_pool_trunk"


def rms_norm(x, g):
    xf = x.astype(jnp.float32)
    y = xf * lax.rsqrt(jnp.mean(xf * xf, axis=-1, keepdims=True) + EPS)
    return (y * g.astype(jnp.float32)).astype(x.dtype)


def layer_norm(x, g, b):
    xf = x.astype(jnp.float32)
    mu = jnp.mean(xf, axis=-1, keepdims=True)
    var = jnp.mean(jnp.square(xf - mu), axis=-1, keepdims=True)
    y = (xf - mu) * lax.rsqrt(var + EPS)
    return (y * g.astype(jnp.float32) + b.astype(jnp.float32)).astype(x.dtype)


def modulate(h, shift, scale):
    return h * (1.0 + scale[:, None, :]) + shift[:, None, :]


def conformer_conv(h, w1, b1, w_dw, b_dw, ln_g, ln_b, w2, b2):
    u = h @ w1 + b1
    a, g = jnp.split(u, 2, axis=-1)
    u = a * jax.nn.sigmoid(g)
    u = lax.conv_general_dilated(
        u, w_dw[:, None, :].astype(u.dtype),
        window_strides=(1,), padding=[(CONV_WIDTH - 1, 0)],
        dimension_numbers=("NWC", "WIO", "NWC"),
        feature_group_count=D_MODEL) + b_dw
    u = jax.nn.silu(layer_norm(u, ln_g, ln_b))
    return u @ w2 + b2


def multiscale_pool(h, w_grp, ls):
    B, S, D = h.shape
    hf = h.astype(jnp.float32).reshape(B, S, N_POOL_GROUPS, POOL_GROUP_DIM)
    cs = jnp.cumsum(hf, axis=1)
    cs = jnp.concatenate([jnp.zeros_like(cs[:, :1]), cs], axis=1)
    t = jnp.arange(S)
    pooled = []
    for gi, w in enumerate(POOL_WINDOWS):
        lo = jnp.maximum(t + 1 - w, 0)
        win_sum = cs[:, t + 1, gi] - cs[:, lo, gi]
        cnt = (t + 1 - lo).astype(jnp.float32)
        pooled.append(win_sum / cnt[None, :, None])
    pooled = jnp.stack(pooled, axis=2)
    mixed = (pooled - hf).astype(h.dtype)
    y = jnp.einsum("bsgc,gcd->bsgd", mixed, w_grp).reshape(B, S, D)
    return y * ls


def swiglu_ffn(h, w_gate, w_up, w_down):
    return (jax.nn.silu(h @ w_gate) * (h @ w_up)) @ w_down


def _fwd_setup_inputs(seed: int = 0) -> dict:
    key = jax.random.key(seed)
    ks = iter(jax.random.split(key, 32))
    D, F = D_MODEL, D_FF
    nrm = lambda shape, s: jax.random.normal(next(ks), shape, jnp.float32) * s
    return {
        "x": nrm((BATCH, SEQ, D), 1.0),
        "c": nrm((BATCH, D), 1.0),
        "ada_w": nrm((DEPTH, D, N_ADA * D), 0.5 * D ** -0.5),
        "ada_b": nrm((DEPTH, N_ADA * D), 0.01),
        "norm_mix_g": 1.0 + nrm((DEPTH, D), 0.02),
        "norm_ffn_g": 1.0 + nrm((DEPTH, D), 0.02),
        "conv_w1": nrm((N_CONV_LAYERS, D, 2 * D), D ** -0.5),
        "conv_b1": nrm((N_CONV_LAYERS, 2 * D), 0.01),
        "conv_wdw": nrm((N_CONV_LAYERS, CONV_WIDTH, D), CONV_WIDTH ** -0.5),
        "conv_bdw": nrm((N_CONV_LAYERS, D), 0.01),
        "conv_ln_g": 1.0 + nrm((N_CONV_LAYERS, D), 0.02),
        "conv_ln_b": nrm((N_CONV_LAYERS, D), 0.01),
        "conv_w2": nrm((N_CONV_LAYERS, D, D), D ** -0.5),
        "conv_b2": nrm((N_CONV_LAYERS, D), 0.01),
        "pool_w": nrm((N_POOL_LAYERS, N_POOL_GROUPS, POOL_GROUP_DIM, POOL_GROUP_DIM), POOL_GROUP_DIM ** -0.5),
        "pool_ls": 1.0 + nrm((N_POOL_LAYERS, D), 0.02),
        "ffn_w_gate": nrm((DEPTH, D, F), D ** -0.5),
        "ffn_w_up": nrm((DEPTH, D, F), D ** -0.5),
        "ffn_w_down": nrm((DEPTH, F, D), F ** -0.5),
        "final_g": 1.0 + nrm((D,), 0.02),
    }


def _fwd_reference(x, c, ada_w, ada_b, norm_mix_g, norm_ffn_g,
              conv_w1, conv_b1, conv_wdw, conv_bdw, conv_ln_g, conv_ln_b,
              conv_w2, conv_b2, pool_w, pool_ls,
              ffn_w_gate, ffn_w_up, ffn_w_down, final_g):
    c_act = jax.nn.silu(c)
    for i in range(DEPTH):
        mod = c_act @ ada_w[i] + ada_b[i]
        sh_m, sc_m, g_m, sh_f, sc_f, g_f = jnp.split(mod, N_ADA, axis=-1)
        h = modulate(rms_norm(x, norm_mix_g[i]), sh_m, sc_m)
        j = i // N_MIXERS
        if i % N_MIXERS == 0:
            y = conformer_conv(h, conv_w1[j], conv_b1[j], conv_wdw[j], conv_bdw[j],
                               conv_ln_g[j], conv_ln_b[j], conv_w2[j], conv_b2[j])
        else:
            y = multiscale_pool(h, pool_w[j], pool_ls[j])
        x = x + (1.0 + g_m)[:, None, :] * y
        h = modulate(rms_norm(x, norm_ffn_g[i]), sh_f, sc_f)
        y = swiglu_ffn(h, ffn_w_gate[i], ffn_w_up[i], ffn_w_down[i])
        x = x + (1.0 + g_f)[:, None, :] * y
    return rms_norm(x, final_g)


import jax as _jax
import jax.numpy as _jnp

TWIN_FORMAT = 'train_step'
FWD_PARAMS = ['x', 'c', 'ada_w', 'ada_b', 'norm_mix_g', 'norm_ffn_g', 'conv_w1', 'conv_b1', 'conv_wdw', 'conv_bdw', 'conv_ln_g', 'conv_ln_b', 'conv_w2', 'conv_b2', 'pool_w', 'pool_ls', 'ffn_w_gate', 'ffn_w_up', 'ffn_w_down', 'final_g']
TWIN_WEIGHTS = ['ada_w', 'ada_b', 'norm_mix_g', 'norm_ffn_g', 'conv_w1', 'conv_b1', 'conv_wdw', 'conv_bdw', 'conv_ln_g', 'conv_ln_b', 'conv_w2', 'conv_b2', 'pool_w', 'pool_ls', 'ffn_w_gate', 'ffn_w_up', 'ffn_w_down', 'final_g']
TWIN_DIFF_INPUT = 'x'
TWIN_INPUTS = ['x', 'c', 'ada_w', 'ada_b', 'norm_mix_g', 'norm_ffn_g', 'conv_w1', 'conv_b1', 'conv_wdw', 'conv_bdw', 'conv_ln_g', 'conv_ln_b', 'conv_w2', 'conv_b2', 'pool_w', 'pool_ls', 'ffn_w_gate', 'ffn_w_up', 'ffn_w_down', 'final_g', 'loss_target', 'm_ada_w', 'm_ada_b', 'm_norm_mix_g', 'm_norm_ffn_g', 'm_conv_w1', 'm_conv_b1', 'm_conv_wdw', 'm_conv_bdw', 'm_conv_ln_g', 'm_conv_ln_b', 'm_conv_w2', 'm_conv_b2', 'm_pool_w', 'm_pool_ls', 'm_ffn_w_gate', 'm_ffn_w_up', 'm_ffn_w_down', 'm_final_g', 'v_ada_w', 'v_ada_b', 'v_norm_mix_g', 'v_norm_ffn_g', 'v_conv_w1', 'v_conv_b1', 'v_conv_wdw', 'v_conv_bdw', 'v_conv_ln_g', 'v_conv_ln_b', 'v_conv_w2', 'v_conv_b2', 'v_pool_w', 'v_pool_ls', 'v_ffn_w_gate', 'v_ffn_w_up', 'v_ffn_w_down', 'v_final_g']
TWIN_OUTPUTS = ['loss', 'grad_x', 'grad_ada_w', 'grad_ada_b', 'grad_norm_mix_g', 'grad_norm_ffn_g', 'grad_conv_w1', 'grad_conv_b1', 'grad_conv_wdw', 'grad_conv_bdw', 'grad_conv_ln_g', 'grad_conv_ln_b', 'grad_conv_w2', 'grad_conv_b2', 'grad_pool_w', 'grad_pool_ls', 'grad_ffn_w_gate', 'grad_ffn_w_up', 'grad_ffn_w_down', 'grad_final_g', 'delta_ada_w', 'delta_ada_b', 'delta_norm_mix_g', 'delta_norm_ffn_g', 'delta_conv_w1', 'delta_conv_b1', 'delta_conv_wdw', 'delta_conv_bdw', 'delta_conv_ln_g', 'delta_conv_ln_b', 'delta_conv_w2', 'delta_conv_b2', 'delta_pool_w', 'delta_pool_ls', 'delta_ffn_w_gate', 'delta_ffn_w_up', 'delta_ffn_w_down', 'delta_final_g', 'new_m_ada_w', 'new_m_ada_b', 'new_m_norm_mix_g', 'new_m_norm_ffn_g', 'new_m_conv_w1', 'new_m_conv_b1', 'new_m_conv_wdw', 'new_m_conv_bdw', 'new_m_conv_ln_g', 'new_m_conv_ln_b', 'new_m_conv_w2', 'new_m_conv_b2', 'new_m_pool_w', 'new_m_pool_ls', 'new_m_ffn_w_gate', 'new_m_ffn_w_up', 'new_m_ffn_w_down', 'new_m_final_g', 'new_v_ada_w', 'new_v_ada_b', 'new_v_norm_mix_g', 'new_v_norm_ffn_g', 'new_v_conv_w1', 'new_v_conv_b1', 'new_v_conv_wdw', 'new_v_conv_bdw', 'new_v_conv_ln_g', 'new_v_conv_ln_b', 'new_v_conv_w2', 'new_v_conv_b2', 'new_v_pool_w', 'new_v_pool_ls', 'new_v_ffn_w_gate', 'new_v_ffn_w_up', 'new_v_ffn_w_down', 'new_v_final_g']
TWIN_LEAF_KINDS = {'loss': 'loss', 'grad_x': 'grad_x', 'grad_ada_w': 'grad_w', 'grad_ada_b': 'grad_w', 'grad_norm_mix_g': 'grad_w', 'grad_norm_ffn_g': 'grad_w', 'grad_conv_w1': 'grad_w', 'grad_conv_b1': 'grad_w', 'grad_conv_wdw': 'grad_w', 'grad_conv_bdw': 'grad_w', 'grad_conv_ln_g': 'grad_w', 'grad_conv_ln_b': 'grad_w', 'grad_conv_w2': 'grad_w', 'grad_conv_b2': 'grad_w', 'grad_pool_w': 'grad_w', 'grad_pool_ls': 'grad_w', 'grad_ffn_w_gate': 'grad_w', 'grad_ffn_w_up': 'grad_w', 'grad_ffn_w_down': 'grad_w', 'grad_final_g': 'grad_w', 'delta_ada_w': 'delta_w', 'delta_ada_b': 'delta_w', 'delta_norm_mix_g': 'delta_w', 'delta_norm_ffn_g': 'delta_w', 'delta_conv_w1': 'delta_w', 'delta_conv_b1': 'delta_w', 'delta_conv_wdw': 'delta_w', 'delta_conv_bdw': 'delta_w', 'delta_conv_ln_g': 'delta_w', 'delta_conv_ln_b': 'delta_w', 'delta_conv_w2': 'delta_w', 'delta_conv_b2': 'delta_w', 'delta_pool_w': 'delta_w', 'delta_pool_ls': 'delta_w', 'delta_ffn_w_gate': 'delta_w', 'delta_ffn_w_up': 'delta_w', 'delta_ffn_w_down': 'delta_w', 'delta_final_g': 'delta_w', 'new_m_ada_w': 'new_m', 'new_m_ada_b': 'new_m', 'new_m_norm_mix_g': 'new_m', 'new_m_norm_ffn_g': 'new_m', 'new_m_conv_w1': 'new_m', 'new_m_conv_b1': 'new_m', 'new_m_conv_wdw': 'new_m', 'new_m_conv_bdw': 'new_m', 'new_m_conv_ln_g': 'new_m', 'new_m_conv_ln_b': 'new_m', 'new_m_conv_w2': 'new_m', 'new_m_conv_b2': 'new_m', 'new_m_pool_w': 'new_m', 'new_m_pool_ls': 'new_m', 'new_m_ffn_w_gate': 'new_m', 'new_m_ffn_w_up': 'new_m', 'new_m_ffn_w_down': 'new_m', 'new_m_final_g': 'new_m', 'new_v_ada_w': 'new_v', 'new_v_ada_b': 'new_v', 'new_v_norm_mix_g': 'new_v', 'new_v_norm_ffn_g': 'new_v', 'new_v_conv_w1': 'new_v', 'new_v_conv_b1': 'new_v', 'new_v_conv_wdw': 'new_v', 'new_v_conv_bdw': 'new_v', 'new_v_conv_ln_g': 'new_v', 'new_v_conv_ln_b': 'new_v', 'new_v_conv_w2': 'new_v', 'new_v_conv_b2': 'new_v', 'new_v_pool_w': 'new_v', 'new_v_pool_ls': 'new_v', 'new_v_ffn_w_gate': 'new_v', 'new_v_ffn_w_up': 'new_v', 'new_v_ffn_w_down': 'new_v', 'new_v_final_g': 'new_v'}


def _forward(args):
    return _fwd_reference(*[args[k] for k in FWD_PARAMS])


def _output_shape():
    def fwd():
        inp = _fwd_setup_inputs(0)
        return _fwd_reference(*[inp[k] for k in FWD_PARAMS])
    out = _jax.eval_shape(fwd)
    return out.shape, out.dtype

N_MICROBATCH = 1
ADAM_LR = 0.001
ADAM_B1 = 0.9
ADAM_B2 = 0.999
ADAM_EPS = 1e-08
ADAM_WD = 0.01
ADAM_STEP = 10
PER_EXAMPLE_BATCH_AXIS = {'x': 0, 'c': 0, 'loss_target': 0}
SHARED_INPUTS = []
_WEIGHT_DTYPES = {'ada_w': _jnp.float32, 'ada_b': _jnp.float32, 'norm_mix_g': _jnp.float32, 'norm_ffn_g': _jnp.float32, 'conv_w1': _jnp.float32, 'conv_b1': _jnp.float32, 'conv_wdw': _jnp.float32, 'conv_bdw': _jnp.float32, 'conv_ln_g': _jnp.float32, 'conv_ln_b': _jnp.float32, 'conv_w2': _jnp.float32, 'conv_b2': _jnp.float32, 'pool_w': _jnp.float32, 'pool_ls': _jnp.float32, 'ffn_w_gate': _jnp.float32, 'ffn_w_up': _jnp.float32, 'ffn_w_down': _jnp.float32, 'final_g': _jnp.float32}
MOMENT_SCALE = {'ada_w': 9.714233e-02, 'ada_b': 2.035678e-01, 'norm_mix_g': 1.745152e-01, 'norm_ffn_g': 1.783142e-01, 'conv_w1': 1.277553e-01, 'conv_b1': 1.126150e-01, 'conv_wdw': 1.679827e-01, 'conv_bdw': 2.958716e-01, 'conv_ln_g': 1.938702e-01, 'conv_ln_b': 1.552078e-01, 'conv_w2': 1.606217e-01, 'conv_b2': 2.620817e-01, 'pool_w': 1.528608e-01, 'pool_ls': 5.489703e-01, 'ffn_w_gate': 7.824812e-02, 'ffn_w_up': 7.599498e-02, 'ffn_w_down': 1.255941e-01, 'final_g': 6.449961e+01}


def _to_microbatches(a, axis):
    t = _jnp.moveaxis(a, axis, 0)
    t = t.reshape((N_MICROBATCH, t.shape[0] // N_MICROBATCH) + t.shape[1:])
    return _jnp.moveaxis(t, 1, axis + 1)


def setup_inputs(seed: int = 0) -> dict:
    inp = _fwd_setup_inputs(seed)
    key = _jax.random.fold_in(_jax.random.key(seed), 7919)
    shape, _ = _output_shape()
    out = dict(inp)
    out["loss_target"] = _jax.random.normal(_jax.random.fold_in(key, 0), shape, _jnp.float32)
    for i, name in enumerate(TWIN_WEIGHTS):
        w = inp[name].astype(_jnp.float32)
        if MOMENT_SCALE is None:
            s = _jnp.sqrt(_jnp.mean(_jnp.square(w)) + 1e-30)
        else:
            s = MOMENT_SCALE[name]
        km, kv = _jax.random.split(_jax.random.fold_in(key, i + 1))
        out[name] = w
        out["m_" + name] = s * _jax.random.normal(km, w.shape, _jnp.float32)
        out["v_" + name] = (s * s) * _jax.random.uniform(kv, w.shape, _jnp.float32, 0.5, 1.5)
    if N_MICROBATCH > 1:
        for name, axis in PER_EXAMPLE_BATCH_AXIS.items():
            out[name] = _to_microbatches(out[name], axis)
    return {'x': out['x'], 'c': out['c'], 'ada_w': out['ada_w'], 'ada_b': out['ada_b'], 'norm_mix_g': out['norm_mix_g'], 'norm_ffn_g': out['norm_ffn_g'], 'conv_w1': out['conv_w1'], 'conv_b1': out['conv_b1'], 'conv_wdw': out['conv_wdw'], 'conv_bdw': out['conv_bdw'], 'conv_ln_g': out['conv_ln_g'], 'conv_ln_b': out['conv_ln_b'], 'conv_w2': out['conv_w2'], 'conv_b2': out['conv_b2'], 'pool_w': out['pool_w'], 'pool_ls': out['pool_ls'], 'ffn_w_gate': out['ffn_w_gate'], 'ffn_w_up': out['ffn_w_up'], 'ffn_w_down': out['ffn_w_down'], 'final_g': out['final_g'], 'loss_target': out['loss_target'], 'm_ada_w': out['m_ada_w'], 'm_ada_b': out['m_ada_b'], 'm_norm_mix_g': out['m_norm_mix_g'], 'm_norm_ffn_g': out['m_norm_ffn_g'], 'm_conv_w1': out['m_conv_w1'], 'm_conv_b1': out['m_conv_b1'], 'm_conv_wdw': out['m_conv_wdw'], 'm_conv_bdw': out['m_conv_bdw'], 'm_conv_ln_g': out['m_conv_ln_g'], 'm_conv_ln_b': out['m_conv_ln_b'], 'm_conv_w2': out['m_conv_w2'], 'm_conv_b2': out['m_conv_b2'], 'm_pool_w': out['m_pool_w'], 'm_pool_ls': out['m_pool_ls'], 'm_ffn_w_gate': out['m_ffn_w_gate'], 'm_ffn_w_up': out['m_ffn_w_up'], 'm_ffn_w_down': out['m_ffn_w_down'], 'm_final_g': out['m_final_g'], 'v_ada_w': out['v_ada_w'], 'v_ada_b': out['v_ada_b'], 'v_norm_mix_g': out['v_norm_mix_g'], 'v_norm_ffn_g': out['v_norm_ffn_g'], 'v_conv_w1': out['v_conv_w1'], 'v_conv_b1': out['v_conv_b1'], 'v_conv_wdw': out['v_conv_wdw'], 'v_conv_bdw': out['v_conv_bdw'], 'v_conv_ln_g': out['v_conv_ln_g'], 'v_conv_ln_b': out['v_conv_ln_b'], 'v_conv_w2': out['v_conv_w2'], 'v_conv_b2': out['v_conv_b2'], 'v_pool_w': out['v_pool_w'], 'v_pool_ls': out['v_pool_ls'], 'v_ffn_w_gate': out['v_ffn_w_gate'], 'v_ffn_w_up': out['v_ffn_w_up'], 'v_ffn_w_down': out['v_ffn_w_down'], 'v_final_g': out['v_final_g']}


def _loss(weights, diff, rest, loss_target):
    with _jax.named_scope("forward"):
        args = {**rest, TWIN_DIFF_INPUT: diff, **{k: w.astype(_WEIGHT_DTYPES[k]) for k, w in weights.items()}}
        y = _forward(args)
    with _jax.named_scope("loss_head"):
        err = _jnp.square(y.astype(_jnp.float32) - loss_target)
        return 0.5 * _jnp.sum(_jnp.mean(err, axis=-1)) if err.ndim else 0.5 * err


def _adamw(w, g, m, v):
    m = ADAM_B1 * m + (1.0 - ADAM_B1) * g
    v = ADAM_B2 * v + (1.0 - ADAM_B2) * _jnp.square(g)
    m_hat = m / (1.0 - ADAM_B1 ** ADAM_STEP)
    v_hat = v / (1.0 - ADAM_B2 ** ADAM_STEP)
    delta = -ADAM_LR * (m_hat / (_jnp.sqrt(v_hat) + ADAM_EPS) + ADAM_WD * w)
    return delta, m, v


def reference(x, c, ada_w, ada_b, norm_mix_g, norm_ffn_g, conv_w1, conv_b1, conv_wdw, conv_bdw, conv_ln_g, conv_ln_b, conv_w2, conv_b2, pool_w, pool_ls, ffn_w_gate, ffn_w_up, ffn_w_down, final_g, loss_target, m_ada_w, m_ada_b, m_norm_mix_g, m_norm_ffn_g, m_conv_w1, m_conv_b1, m_conv_wdw, m_conv_bdw, m_conv_ln_g, m_conv_ln_b, m_conv_w2, m_conv_b2, m_pool_w, m_pool_ls, m_ffn_w_gate, m_ffn_w_up, m_ffn_w_down, m_final_g, v_ada_w, v_ada_b, v_norm_mix_g, v_norm_ffn_g, v_conv_w1, v_conv_b1, v_conv_wdw, v_conv_bdw, v_conv_ln_g, v_conv_ln_b, v_conv_w2, v_conv_b2, v_pool_w, v_pool_ls, v_ffn_w_gate, v_ffn_w_up, v_ffn_w_down, v_final_g):
    given = dict(x=x, c=c, ada_w=ada_w, ada_b=ada_b, norm_mix_g=norm_mix_g, norm_ffn_g=norm_ffn_g, conv_w1=conv_w1, conv_b1=conv_b1, conv_wdw=conv_wdw, conv_bdw=conv_bdw, conv_ln_g=conv_ln_g, conv_ln_b=conv_ln_b, conv_w2=conv_w2, conv_b2=conv_b2, pool_w=pool_w, pool_ls=pool_ls, ffn_w_gate=ffn_w_gate, ffn_w_up=ffn_w_up, ffn_w_down=ffn_w_down, final_g=final_g, loss_target=loss_target, m_ada_w=m_ada_w, m_ada_b=m_ada_b, m_norm_mix_g=m_norm_mix_g, m_norm_ffn_g=m_norm_ffn_g, m_conv_w1=m_conv_w1, m_conv_b1=m_conv_b1, m_conv_wdw=m_conv_wdw, m_conv_bdw=m_conv_bdw, m_conv_ln_g=m_conv_ln_g, m_conv_ln_b=m_conv_ln_b, m_conv_w2=m_conv_w2, m_conv_b2=m_conv_b2, m_pool_w=m_pool_w, m_pool_ls=m_pool_ls, m_ffn_w_gate=m_ffn_w_gate, m_ffn_w_up=m_ffn_w_up, m_ffn_w_down=m_ffn_w_down, m_final_g=m_final_g, v_ada_w=v_ada_w, v_ada_b=v_ada_b, v_norm_mix_g=v_norm_mix_g, v_norm_ffn_g=v_norm_ffn_g, v_conv_w1=v_conv_w1, v_conv_b1=v_conv_b1, v_conv_wdw=v_conv_wdw, v_conv_bdw=v_conv_bdw, v_conv_ln_g=v_conv_ln_g, v_conv_ln_b=v_conv_ln_b, v_conv_w2=v_conv_w2, v_conv_b2=v_conv_b2, v_pool_w=v_pool_w, v_pool_ls=v_pool_ls, v_ffn_w_gate=v_ffn_w_gate, v_ffn_w_up=v_ffn_w_up, v_ffn_w_down=v_ffn_w_down, v_final_g=v_final_g)
    weights = {n: given[n] for n in TWIN_WEIGHTS}
    shared = {n: given[n] for n in SHARED_INPUTS}
    per_example = {n: given[n] for n in ['x', 'c']}
    grad_fn = _jax.value_and_grad(_loss, argnums=(0, 1))

    def one_microbatch(ex, loss_target):
        ex = dict(ex)
        diff = ex.pop(TWIN_DIFF_INPUT)
        return grad_fn(weights, diff, {**shared, **ex}, loss_target)

    if N_MICROBATCH == 1:
        loss, (grad_w, grad_x) = one_microbatch(per_example, given["loss_target"])
    else:
        def body(carry, xs):
            loss_sum, grad_sum = carry
            l_k, (gw_k, gx_k) = one_microbatch(xs[0], xs[1])
            with _jax.named_scope("update"):
                return (loss_sum + l_k, _jax.tree.map(_jnp.add, grad_sum, gw_k)), gx_k

        init = (_jnp.zeros((), _jnp.float32), _jax.tree.map(_jnp.zeros_like, weights))
        (loss, grad_w), grad_x = _jax.lax.scan(body, init, (per_example, given["loss_target"]))
    with _jax.named_scope("update"):
        delta_w, new_m, new_v = {}, {}, {}
        for n in TWIN_WEIGHTS:
            delta_w[n], new_m[n], new_v[n] = _adamw(weights[n], grad_w[n], given["m_" + n], given["v_" + n])
    return (loss, grad_x, *[grad_w[n] for n in TWIN_WEIGHTS], *[delta_w[n] for n in TWIN_WEIGHTS],
            *[new_m[n] for n in TWIN_WEIGHTS], *[new_v[n] for n in TWIN_WEIGHTS])
```

```python
import functools

import jax
import jax.numpy as jnp
from jax import lax
from jax.experimental import pallas as pl
from jax.experimental.pallas import tpu as pltpu

NDEV = 8
EPS = 1e-6
CONV_WIDTH = 31
POOL_WINDOWS = (2, 4, 8, 16)
CONV_HALO = 32
POOL_HALO = 16
ADAM_LR = 0.001
ADAM_B1 = 0.9
ADAM_B2 = 0.999
ADAM_EPS = 1e-08
ADAM_WD = 0.01
ADAM_STEP = 10
VMEM_LIMIT = 56 * 2**20
MESH = pl.DeviceIdType.MESH
F32 = jnp.float32
BF16 = jnp.bfloat16


def _nt(a, b):
    return lax.dot_general(a, b, (((1,), (1,)), ((), ())), preferred_element_type=F32)


def _nn(a, b):
    return lax.dot_general(a, b, (((1,), (0,)), ((), ())), preferred_element_type=F32)


def _tn(a, b):
    return lax.dot_general(a, b, (((0,), (0,)), ((), ())), preferred_element_type=F32)


def _sum0(v):
    return jnp.sum(v, axis=0, keepdims=True)


def _rms(x):
    rinv = lax.rsqrt(jnp.mean(x * x, axis=-1, keepdims=True) + EPS)
    return x * rinv, rinv


def _rms_mod_bwd(dh, n, rinv, g, sc):
    dhs = dh * (1.0 + sc)
    dn = dhs * g
    dx = rinv * (dn - n * jnp.mean(dn * n, axis=-1, keepdims=True))
    return dx, _sum0(dh), _sum0(dh * (n * g)), _sum0(dhs * n)


def _silu_grad(z, sg):
    return sg * (1.0 + z * (1.0 - sg))


def _params(*sem):
    return pltpu.CompilerParams(dimension_semantics=sem, vmem_limit_bytes=VMEM_LIMIT)


def _row(i, d):
    return pl.BlockSpec((i, d), lambda *_: (0, 0))


def _weight_spec(rows, d, idx):
    return pl.BlockSpec((NDEV, rows, d), lambda *_: (0, idx, 0), pipeline_mode=pl.Buffered(1))


def _my_index():
    return 4 * lax.axis_index("x") + 2 * lax.axis_index("y") + lax.axis_index("c")


def _peer(k):
    x, y, c = lax.axis_index("x"), lax.axis_index("y"), lax.axis_index("c")
    px = 1 - x if k & 4 else x
    py = 1 - y if k & 2 else y
    pc = 1 - c if k & 1 else c
    return (px, py, pc), 4 * px + 2 * py + pc


def _allgather_small(v, name):
    r, c = v.shape

    def body(v_ref, out_ref, send_sems, recv_sems):
        me = _my_index()
        out_ref[me] = v_ref[...]
        copies = []
        for k in range(1, NDEV):
            dev, _ = _peer(k)
            copies.append(pltpu.make_async_remote_copy(
                src_ref=v_ref, dst_ref=out_ref.at[me], send_sem=send_sems.at[k - 1], recv_sem=recv_sems.at[k - 1],
                device_id=dev, device_id_type=MESH))
        for cp in copies:
            cp.start()
        for cp in copies:
            cp.wait()

    return pl.pallas_call(body, name=name,
        out_shape=jax.ShapeDtypeStruct((NDEV, r, c), v.dtype),
        in_specs=[pl.BlockSpec(memory_space=pltpu.VMEM)],
        out_specs=pl.BlockSpec(memory_space=pltpu.VMEM),
        scratch_shapes=[pltpu.SemaphoreType.DMA((NDEV - 1,)), pltpu.SemaphoreType.DMA((NDEV - 1,))],
    )(v)


def _allgather_weights(packed, nchunk):
    r, d = packed.shape
    rc = r // nchunk
    assert rc * nchunk == r and rc % 16 == 0

    def body(w_ref, out_ref, send_sems, recv_sems, local_sem):
        x, y, c = lax.axis_index("x"), lax.axis_index("y"), lax.axis_index("c")
        sibling = (x, y, 1 - c)
        chips = [(1 - x, y), (x, 1 - y), (1 - x, 1 - y)]

        def idx(px, py, pc):
            return 4 * px + 2 * py + pc

        def copy(k, q, block, to, from_input):
            rows = pl.ds(q * rc, rc)
            dst = out_ref.at[idx(*block), rows, :]
            return pltpu.make_async_remote_copy(
                src_ref=w_ref.at[rows, :] if from_input else dst, dst_ref=dst,
                send_sem=send_sems.at[k * nchunk + q], recv_sem=recv_sems.at[k * nchunk + q],
                device_id=to, device_id_type=MESH)

        mine = pltpu.make_async_copy(w_ref, out_ref.at[idx(x, y, c)], local_sem)
        mine.start()
        started = []
        for q in range(nchunk):
            first = [copy(0, q, (x, y, c), sibling, True)]
            first += [copy(1 + j, q, (x, y, c), (*chip, c), True) for j, chip in enumerate(chips)]
            for cp in first:
                cp.start()
            started += first
        for q in range(nchunk):
            for j, chip in enumerate(chips):
                copy(1 + j, q, (*chip, c), (x, y, c), False).wait_recv()
                fwd = copy(4 + j, q, (*chip, c), sibling, False)
                fwd.start()
                started.append(fwd)
        for q in range(nchunk):
            copy(0, q, sibling, (x, y, c), False).wait_recv()
            for j, chip in enumerate(chips):
                copy(4 + j, q, (*chip, 1 - c), (x, y, c), False).wait_recv()
        for cp in started:
            cp.wait_send()
        mine.wait()

    return pl.pallas_call(body, name="allgather_weights",
        out_shape=jax.ShapeDtypeStruct((NDEV, r, d), packed.dtype),
        in_specs=[pl.BlockSpec(memory_space=pl.ANY)],
        out_specs=pl.BlockSpec(memory_space=pl.ANY),
        scratch_shapes=[pltpu.SemaphoreType.DMA((7 * nchunk,)), pltpu.SemaphoreType.DMA((7 * nchunk,)),
                        pltpu.SemaphoreType.DMA],
    )(packed)


def _reduce_scatter(parts):
    n = len(parts)
    rows = [p.shape[1] // NDEV for p in parts]

    def body(*refs):
        ins, outs = refs[:n], refs[n:2 * n]
        send_sems, recv_sems, local_sems = refs[2 * n:]
        me = _my_index()
        copies = []
        for a in range(n):
            own = pltpu.make_async_copy(ins[a].at[:, pl.ds(me * rows[a], rows[a]), :], outs[a].at[me], local_sems.at[a])
            own.start()
            copies.append(own)
            for k in range(1, NDEV):
                dev, p = _peer(k)
                cp = pltpu.make_async_remote_copy(
                    src_ref=ins[a].at[:, pl.ds(p * rows[a], rows[a]), :], dst_ref=outs[a].at[me],
                    send_sem=send_sems.at[a * 7 + k - 1], recv_sem=recv_sems.at[a * 7 + k - 1],
                    device_id=dev, device_id_type=MESH)
                cp.start()
                copies.append(cp)
        for cp in copies:
            cp.wait()

    return pl.pallas_call(body, name="reduce_scatter",
        out_shape=[jax.ShapeDtypeStruct((NDEV, p.shape[0], r, p.shape[2]), p.dtype) for p, r in zip(parts, rows)],
        in_specs=[pl.BlockSpec(memory_space=pl.ANY)] * n,
        out_specs=[pl.BlockSpec(memory_space=pl.ANY)] * n,
        scratch_shapes=[pltpu.SemaphoreType.DMA((7 * n,)), pltpu.SemaphoreType.DMA((7 * n,)),
                        pltpu.SemaphoreType.DMA((n,))],
    )(*parts)


def _ada_forward(c_all, ada_w, bias):
    nl, d, ncol = ada_w.shape

    def body(c_ref, w_ref, b_ref, o_ref):
        cv = c_ref[...]
        ca = cv * jax.nn.sigmoid(cv)
        o_ref[0] = jnp.dot(ca, w_ref[0], preferred_element_type=F32, precision=lax.Precision.HIGHEST) + b_ref[0]

    return pl.pallas_call(body, name="ada_forward", grid=(nl,),
        in_specs=[pl.BlockSpec((NDEV, d), lambda i: (0, 0)), pl.BlockSpec((1, d, ncol), lambda i: (i, 0, 0)),
                  pl.BlockSpec((1, 1, ncol), lambda i: (i, 0, 0))],
        out_specs=pl.BlockSpec((1, NDEV, ncol), lambda i: (i, 0, 0)),
        out_shape=jax.ShapeDtypeStruct((nl, NDEV, ncol), F32),
        compiler_params=_params("arbitrary"),
    )(c_all, ada_w, bias)


def _ada_wgrad(c_all, dmod):
    nl, _, ncol = dmod.shape
    d = c_all.shape[1]
    bd = min(d, 256)

    def body(c_ref, dm_ref, o_ref):
        cv = c_ref[...]
        ca = cv * jax.nn.sigmoid(cv)
        o_ref[0] = lax.dot_general(ca, dm_ref[0], (((0,), (0,)), ((), ())), preferred_element_type=F32,
                                   precision=lax.Precision.HIGHEST)

    return pl.pallas_call(body, name="ada_wgrad", grid=(nl, d // bd),
        in_specs=[pl.BlockSpec((NDEV, bd), lambda i, j: (0, j)), pl.BlockSpec((1, NDEV, ncol), lambda i, j: (i, 0, 0))],
        out_specs=pl.BlockSpec((1, bd, ncol), lambda i, j: (i, j, 0)),
        out_shape=jax.ShapeDtypeStruct((nl, d, ncol), F32),
        compiler_params=_params("arbitrary", "arbitrary"),
    )(c_all, dmod)


def _row_block(r, c, bytes_per_row_elem=4, budget=2 * 2**20):
    if r * c * bytes_per_row_elem <= budget or r % 8:
        return r
    best = 8
    for b in range(8, r + 1, 8):
        if r % b == 0 and b * c * bytes_per_row_elem <= budget:
            best = b
    return best


def _sum_slots(land, name):
    _, r, c = land.shape
    br = _row_block(r, c, 8 * land.dtype.itemsize)

    def body(l_ref, o_ref):
        acc = l_ref[0].astype(F32)
        for s in range(1, NDEV):
            acc = acc + l_ref[s].astype(F32)
        o_ref[...] = acc

    return pl.pallas_call(body, name=name, grid=(r // br,),
        in_specs=[pl.BlockSpec((NDEV, br, c), lambda i: (0, i, 0))],
        out_specs=pl.BlockSpec((br, c), lambda i: (i, 0)),
        out_shape=jax.ShapeDtypeStruct((r, c), F32),
        compiler_params=_params("arbitrary"),
    )(land)


def _adamw(w, g, m, v, name):
    r, c = w.shape
    br = _row_block(r, c)

    def body(w_ref, g_ref, m_ref, v_ref, d_ref, mo_ref, vo_ref):
        gv = g_ref[...]
        m2 = ADAM_B1 * m_ref[...] + (1.0 - ADAM_B1) * gv
        v2 = ADAM_B2 * v_ref[...] + (1.0 - ADAM_B2) * (gv * gv)
        m_hat = m2 / (1.0 - ADAM_B1 ** ADAM_STEP)
        v_hat = v2 / (1.0 - ADAM_B2 ** ADAM_STEP)
        d_ref[...] = -ADAM_LR * (m_hat / (jnp.sqrt(v_hat) + ADAM_EPS) + ADAM_WD * w_ref[...])
        mo_ref[...] = m2
        vo_ref[...] = v2

    spec = pl.BlockSpec((br, c), lambda i: (i, 0))
    return pl.pallas_call(body, name=name, grid=(r // br,),
        in_specs=[spec] * 4, out_specs=[spec] * 3,
        out_shape=[jax.ShapeDtypeStruct((r, c), F32)] * 3,
        compiler_params=_params("arbitrary"),
    )(w, g, m, v)


def _conv_in(x, mod, gmix, gw, w1_rows, w1_idx, b1, tm):
    s, d = x.shape

    def body(x_ref, mod_ref, g_ref, w_ref, b_ref, h_ref, u_ref, glu_ref):
        n, _ = _rms(x_ref[...])
        h = (n * g_ref[...]) * (1.0 + mod_ref[1:2, :]) + mod_ref[0:1, :]
        hb = h.astype(BF16)
        h_ref[...] = hb
        u = _nt(hb, w_ref[...].reshape(NDEV * w1_rows, d)) + b_ref[...]
        u_ref[...] = u
        glu_ref[...] = u[:, :d] * jax.nn.sigmoid(u[:, d:])

    tile = pl.BlockSpec((tm, d), lambda i: (i, 0))
    return pl.pallas_call(body, name="conv_in", grid=(s // tm,),
        in_specs=[tile, _row(8, d), _row(1, d), _weight_spec(w1_rows, d, w1_idx), _row(1, 2 * d)],
        out_specs=[tile, pl.BlockSpec((tm, 2 * d), lambda i: (i, 0)), tile],
        out_shape=[jax.ShapeDtypeStruct((s, d), BF16), jax.ShapeDtypeStruct((s, 2 * d), F32),
                   jax.ShapeDtypeStruct((s, d), F32)],
        compiler_params=_params("arbitrary"),
    )(x, mod, gmix, gw, b1)


def _conv_mid(glu, wdw, bdw, ln_g, ln_b, gw, w2_rows, w2_idx, b2, x, mod, tm):
    s, d = x.shape
    off = CONV_HALO - (CONV_WIDTH - 1)

    def body(glu_ref, halo_ref, wdw_ref, bdw_ref, lng_ref, lnb_ref, w_ref, b2_ref, x_ref, mod_ref,
             dwc_ref, s_ref, y_ref, x1_ref, buf):
        i = pl.program_id(0)
        buf[pl.ds(0, CONV_HALO), :] = jnp.where(i > 0, halo_ref[...], 0.0)
        buf[pl.ds(CONV_HALO, tm), :] = glu_ref[...]
        acc = jnp.zeros((tm, d), F32) + bdw_ref[...]
        for k in range(CONV_WIDTH):
            acc = acc + buf[pl.ds(off + k, tm), :] * wdw_ref[pl.ds(k, 1), :]
        dwc_ref[...] = acc
        mu = jnp.mean(acc, axis=-1, keepdims=True)
        xc = acc - mu
        rstd = lax.rsqrt(jnp.mean(xc * xc, axis=-1, keepdims=True) + EPS)
        ln = (xc * rstd) * lng_ref[...] + lnb_ref[...]
        sb = (ln * jax.nn.sigmoid(ln)).astype(BF16)
        s_ref[...] = sb
        y = _nn(sb, w_ref[...].reshape(NDEV * w2_rows, d)) + b2_ref[...]
        y_ref[...] = y.astype(BF16)
        x1_ref[...] = x_ref[...] + (1.0 + mod_ref[2:3, :]) * y

    tile = pl.BlockSpec((tm, d), lambda i: (i, 0))
    halo = pl.BlockSpec((CONV_HALO, d), lambda i: (jnp.maximum(i * (tm // CONV_HALO) - 1, 0), 0))
    return pl.pallas_call(body, name="conv_mid", grid=(s // tm,),
        in_specs=[tile, halo, _row(CONV_WIDTH, d), _row(1, d), _row(1, d), _row(1, d),
                  _weight_spec(w2_rows, d, w2_idx), _row(1, d), tile, _row(8, d)],
        out_specs=[tile, tile, tile, tile],
        out_shape=[jax.ShapeDtypeStruct((s, d), F32), jax.ShapeDtypeStruct((s, d), BF16),
                   jax.ShapeDtypeStruct((s, d), BF16), jax.ShapeDtypeStruct((s, d), F32)],
        scratch_shapes=[pltpu.VMEM((CONV_HALO + tm, d), F32)],
        compiler_params=_params("arbitrary"),
    )(glu, glu, wdw, bdw, ln_g, ln_b, gw, b2, x, mod)


def _ffn_fwd(x, mod, gffn, gw, fs, idx0, f, tm, name):
    s, d = x.shape

    def body(x_ref, mod_ref, g_ref, wg_ref, wu_ref, wd_ref, h_ref, gg_ref, uu_ref, y_ref, xo_ref):
        xv = x_ref[...]
        n, _ = _rms(xv)
        hb = ((n * g_ref[...]) * (1.0 + mod_ref[4:5, :]) + mod_ref[3:4, :]).astype(BF16)
        h_ref[...] = hb
        gg = _nt(hb, wg_ref[...].reshape(f, d))
        uu = _nt(hb, wu_ref[...].reshape(f, d))
        gg_ref[...] = gg
        uu_ref[...] = uu
        ab = ((gg * jax.nn.sigmoid(gg)) * uu).astype(BF16)
        y = _nn(ab, wd_ref[...].reshape(f, d))
        y_ref[...] = y.astype(BF16)
        xo_ref[...] = xv + (1.0 + mod_ref[5:6, :]) * y

    tile = pl.BlockSpec((tm, d), lambda i: (i, 0))
    wide = pl.BlockSpec((tm, f), lambda i: (i, 0))
    return pl.pallas_call(body, name=name, grid=(s // tm,),
        in_specs=[tile, _row(8, d), _row(1, d), _weight_spec(fs, d, idx0), _weight_spec(fs, d, idx0 + 1),
                  _weight_spec(fs, d, idx0 + 2)],
        out_specs=[tile, wide, wide, tile, tile],
        out_shape=[jax.ShapeDtypeStruct((s, d), BF16), jax.ShapeDtypeStruct((s, f), F32),
                   jax.ShapeDtypeStruct((s, f), F32), jax.ShapeDtypeStruct((s, d), BF16),
                   jax.ShapeDtypeStruct((s, d), F32)],
        compiler_params=_params("arbitrary"),
    )(x, mod, gffn, gw, gw, gw)


def _pool_fwd(x, mod, gmix, pw, ls, tm):
    s, d = x.shape
    dg = d // len(POOL_WINDOWS)

    def body(x_ref, halo_ref, mod_ref, g_ref, pw_ref, ls_ref, mixed_ref, yp_ref, xo_ref, buf):
        i = pl.program_id(0)

        def hfun(xv):
            n, _ = _rms(xv)
            return (n * g_ref[...]) * (1.0 + mod_ref[1:2, :]) + mod_ref[0:1, :]

        xv = x_ref[...]
        h = hfun(xv)
        buf[pl.ds(0, POOL_HALO), :] = jnp.where(i > 0, hfun(halo_ref[...]), 0.0)
        buf[pl.ds(POOL_HALO, tm), :] = h
        t = i * tm + lax.broadcasted_iota(jnp.int32, (tm, 1), 0)
        gate = 1.0 + mod_ref[2:3, :]
        for gi, w in enumerate(POOL_WINDOWS):
            cols = pl.ds(gi * dg, dg)
            ws = buf[pl.ds(POOL_HALO, tm), cols]
            for j in range(1, w):
                ws = ws + buf[pl.ds(POOL_HALO - j, tm), cols]
            inv = 1.0 / jnp.minimum(t + 1, w).astype(F32)
            mb = (ws * inv - h[:, gi * dg:(gi + 1) * dg]).astype(BF16)
            mixed_ref[:, cols] = mb
            yp = _nn(mb, pw_ref[gi])
            yp_ref[:, cols] = yp.astype(BF16)
            xo_ref[:, cols] = xv[:, gi * dg:(gi + 1) * dg] + gate[:, gi * dg:(gi + 1) * dg] * (yp * ls_ref[:, cols])

    tile = pl.BlockSpec((tm, d), lambda i: (i, 0))
    halo = pl.BlockSpec((POOL_HALO, d), lambda i: (jnp.maximum(i * (tm // POOL_HALO) - 1, 0), 0))
    return pl.pallas_call(body, name="pool_fwd", grid=(s // tm,),
        in_specs=[tile, halo, _row(8, d), _row(1, d), pl.BlockSpec((len(POOL_WINDOWS), dg, dg), lambda i: (0, 0, 0)),
                  _row(1, d)],
        out_specs=[tile, tile, tile],
        out_shape=[jax.ShapeDtypeStruct((s, d), BF16), jax.ShapeDtypeStruct((s, d), BF16),
                   jax.ShapeDtypeStruct((s, d), F32)],
        scratch_shapes=[pltpu.VMEM((POOL_HALO + tm, d), F32)],
        compiler_params=_params("arbitrary"),
    )(x, x, mod, gmix, pw, ls)


def _loss_head(x, target, gfin, tm):
    s, d = x.shape
    last = s // tm - 1

    def body(x_ref, t_ref, g_ref, dx_ref, st_ref):
        i = pl.program_id(0)

        @pl.when(i == 0)
        def _():
            st_ref[...] = jnp.zeros_like(st_ref)

        n, rinv = _rms(x_ref[...])
        g = g_ref[...]
        err = n * g - t_ref[...]
        dy = err * (1.0 / d)
        st_ref[0:1, :] += _sum0(dy * n)
        st_ref[1:2, :] += _sum0(err * err) * (0.5 / d)
        dn = dy * g
        dx_ref[...] = rinv * (dn - n * jnp.mean(dn * n, axis=-1, keepdims=True))

        @pl.when(i == last)
        def _():
            st_ref[2:3, :] = jnp.zeros((1, d), F32) + jnp.sum(st_ref[1:2, :], axis=-1, keepdims=True)

    tile = pl.BlockSpec((tm, d), lambda i: (i, 0))
    return pl.pallas_call(body, name="loss_head", grid=(s // tm,),
        in_specs=[tile, tile, _row(1, d)],
        out_specs=[tile, _row(8, d)],
        out_shape=[jax.ShapeDtypeStruct((s, d), F32), jax.ShapeDtypeStruct((8, d), F32)],
        compiler_params=_params("arbitrary"),
    )(x, target, gfin)


def _ffn_bwd(dxo, x, gg, uu, y, mod, gffn, gw, fs, idx0, f, tm, name):
    s, d = x.shape

    def body(dxo_ref, x_ref, gg_ref, uu_ref, y_ref, mod_ref, g_ref, wg_ref, wu_ref, wd_ref,
             dg_ref, du_ref, a_ref, dy_ref, dxi_ref, st_ref):
        @pl.when(pl.program_id(0) == 0)
        def _():
            st_ref[...] = jnp.zeros_like(st_ref)

        dxo_v = dxo_ref[...]
        dyb = (dxo_v * (1.0 + mod_ref[5:6, :])).astype(BF16)
        dy_ref[...] = dyb
        da = _nt(dyb, wd_ref[...].reshape(f, d))
        ggv, uuv = gg_ref[...], uu_ref[...]
        sg = jax.nn.sigmoid(ggv)
        silu = ggv * sg
        a_ref[...] = (silu * uuv).astype(BF16)
        dub = (da * silu).astype(BF16)
        dgb = (da * uuv * _silu_grad(ggv, sg)).astype(BF16)
        du_ref[...] = dub
        dg_ref[...] = dgb
        dh = _nn(dgb, wg_ref[...].reshape(f, d)) + _nn(dub, wu_ref[...].reshape(f, d))
        n, rinv = _rms(x_ref[...])
        dx, dsh, dsc, dgain = _rms_mod_bwd(dh, n, rinv, g_ref[...], mod_ref[4:5, :])
        dxi_ref[...] = dxo_v + dx
        st_ref[0:1, :] += dsh
        st_ref[1:2, :] += dsc
        st_ref[2:3, :] += _sum0(dxo_v * y_ref[...].astype(F32))
        st_ref[3:4, :] += dgain

    tile = pl.BlockSpec((tm, d), lambda i: (i, 0))
    wide = pl.BlockSpec((tm, f), lambda i: (i, 0))
    return pl.pallas_call(body, name=name, grid=(s // tm,),
        in_specs=[tile, tile, wide, wide, tile, _row(8, d), _row(1, d), _weight_spec(fs, d, idx0),
                  _weight_spec(fs, d, idx0 + 1), _weight_spec(fs, d, idx0 + 2)],
        out_specs=[wide, wide, wide, tile, tile, _row(8, d)],
        out_shape=[jax.ShapeDtypeStruct((s, f), BF16)] * 3 + [jax.ShapeDtypeStruct((s, d), BF16),
                   jax.ShapeDtypeStruct((s, d), F32), jax.ShapeDtypeStruct((8, d), F32)],
        compiler_params=_params("arbitrary"),
    )(dxo, x, gg, uu, y, mod, gffn, gw, gw, gw)


def _ffn_wgrad(dgb, dub, ab, h, dyb, fb, ts, name):
    s, f = dgb.shape
    d = h.shape[1]
    last = s // ts - 1

    def body(dg_ref, du_ref, a_ref, h_ref, dy_ref, o_ref, acc):
        t = pl.program_id(1)

        @pl.when(t == 0)
        def _():
            acc[...] = jnp.zeros_like(acc)

        hv = h_ref[...]
        acc[0] += _tn(dg_ref[...], hv)
        acc[1] += _tn(du_ref[...], hv)
        acc[2] += _tn(a_ref[...], dy_ref[...])

        @pl.when(t == last)
        def _():
            o_ref[...] = acc[...].astype(BF16)

    wide = pl.BlockSpec((ts, fb), lambda j, t: (t, j))
    tile = pl.BlockSpec((ts, d), lambda j, t: (t, 0))
    return pl.pallas_call(body, name=name, grid=(f // fb, s // ts),
        in_specs=[wide, wide, wide, tile, tile],
        out_specs=pl.BlockSpec((3, fb, d), lambda j, t: (0, j, 0)),
        out_shape=jax.ShapeDtypeStruct((3, f, d), BF16),
        scratch_shapes=[pltpu.VMEM((3, fb, d), F32)],
        compiler_params=_params("arbitrary", "arbitrary"),
    )(dgb, dub, ab, h, dyb)


def _pool_bwd(dxo, x, yp, mixed, mod, gmix, pw, ls, tm):
    s, d = x.shape
    ng = len(POOL_WINDOWS)
    dg = d // ng
    last = s // tm - 1

    def body(dxo_ref, dxh_ref, x_ref, yp_ref, mixed_ref, mod_ref, g_ref, pw_ref, ls_ref,
             dxi_ref, dpw_ref, st_ref, bufy, bufq, bufh, acc):
        i = pl.program_id(0)

        @pl.when(i == 0)
        def _():
            st_ref[...] = jnp.zeros_like(st_ref)
            acc[...] = jnp.zeros_like(acc)

        gate = 1.0 + mod_ref[2:3, :]
        lsv = ls_ref[...]
        dxo_v = dxo_ref[...]
        st_ref[2:3, :] += _sum0(dxo_v * yp_ref[...].astype(F32))
        bufy[pl.ds(0, tm), :] = (dxo_v * (gate * lsv)).astype(BF16)
        bufy[pl.ds(tm, POOL_HALO), :] = jnp.where(i < last, dxh_ref[...] * (gate * lsv), 0.0).astype(BF16)
        t = i * tm + lax.broadcasted_iota(jnp.int32, (tm + POOL_HALO, 1), 0)
        for gi, w in enumerate(POOL_WINDOWS):
            cols = pl.ds(gi * dg, dg)
            dm = _nt(bufy[:, cols], pw_ref[gi])
            bufq[:, cols] = dm * (1.0 / jnp.minimum(t + 1, w).astype(F32))
            dh = bufq[pl.ds(0, tm), cols] - dm[0:tm, :]
            for j in range(1, w):
                dh = dh + bufq[pl.ds(j, tm), cols]
            bufh[:, cols] = dh
            acc[gi] += _tn(mixed_ref[:, cols], bufy[pl.ds(0, tm), cols])
        n, rinv = _rms(x_ref[...])
        dx, dsh, dsc, dgain = _rms_mod_bwd(bufh[...], n, rinv, g_ref[...], mod_ref[1:2, :])
        dxi_ref[...] = dxo_v + dx
        st_ref[0:1, :] += dsh
        st_ref[1:2, :] += dsc
        st_ref[3:4, :] += dgain

        @pl.when(i == last)
        def _():
            r = st_ref[2:3, :]
            st_ref[4:5, :] = r * lsv
            st_ref[5:6, :] = r * gate
            dpw_ref[...] = acc[...].astype(BF16)

    tile = pl.BlockSpec((tm, d), lambda i: (i, 0))
    nxt = pl.BlockSpec((POOL_HALO, d), lambda i: (jnp.minimum((i + 1) * (tm // POOL_HALO), s // POOL_HALO - 1), 0))
    pws = pl.BlockSpec((ng, dg, dg), lambda i: (0, 0, 0))
    return pl.pallas_call(body, name="pool_bwd", grid=(s // tm,),
        in_specs=[tile, nxt, tile, tile, tile, _row(8, d), _row(1, d), pws, _row(1, d)],
        out_specs=[tile, pws, _row(8, d)],
        out_shape=[jax.ShapeDtypeStruct((s, d), F32), jax.ShapeDtypeStruct((ng, dg, dg), BF16),
                   jax.ShapeDtypeStruct((8, d), F32)],
        scratch_shapes=[pltpu.VMEM((tm + POOL_HALO, d), BF16), pltpu.VMEM((tm + POOL_HALO, d), F32),
                        pltpu.VMEM((tm, d), F32), pltpu.VMEM((ng, dg, dg), F32)],
        compiler_params=_params("arbitrary"),
    )(dxo, dxo, x, yp, mixed, mod, gmix, pw, ls)


def _conv_bwd_mid(dxo, y, dwc, sb, mod, ln_g, ln_b, gw, w2_rows, w2_idx, tm):
    s, d = dwc.shape
    last = s // tm - 1

    def body(dxo_ref, y_ref, dwc_ref, s_ref, mod_ref, lng_ref, lnb_ref, w_ref, dd_ref, dw_ref, st_ref, acc):
        i = pl.program_id(0)

        @pl.when(i == 0)
        def _():
            st_ref[...] = jnp.zeros_like(st_ref)
            acc[...] = jnp.zeros_like(acc)

        dxo_v = dxo_ref[...]
        st_ref[0:1, :] += _sum0(dxo_v * y_ref[...].astype(F32))
        dy = dxo_v * (1.0 + mod_ref[2:3, :])
        st_ref[1:2, :] += _sum0(dy)
        dyb = dy.astype(BF16)
        ds = _nt(dyb, w_ref[...].reshape(NDEV * w2_rows, d))
        acc[...] += _tn(s_ref[...], dyb)
        v = dwc_ref[...]
        mu = jnp.mean(v, axis=-1, keepdims=True)
        xc = v - mu
        rstd = lax.rsqrt(jnp.mean(xc * xc, axis=-1, keepdims=True) + EPS)
        xhat = xc * rstd
        ln = xhat * lng_ref[...] + lnb_ref[...]
        dln = ds * _silu_grad(ln, jax.nn.sigmoid(ln))
        st_ref[2:3, :] += _sum0(dln * xhat)
        st_ref[3:4, :] += _sum0(dln)
        dxh = dln * lng_ref[...]
        dd = rstd * (dxh - jnp.mean(dxh, axis=-1, keepdims=True) - xhat * jnp.mean(dxh * xhat, axis=-1, keepdims=True))
        dd_ref[...] = dd
        st_ref[4:5, :] += _sum0(dd)

        @pl.when(i == last)
        def _():
            dw_ref[...] = acc[...].astype(BF16)

    tile = pl.BlockSpec((tm, d), lambda i: (i, 0))
    return pl.pallas_call(body, name="conv_bwd_mid", grid=(s // tm,),
        in_specs=[tile, tile, tile, tile, _row(8, d), _row(1, d), _row(1, d), _weight_spec(w2_rows, d, w2_idx)],
        out_specs=[tile, pl.BlockSpec((d, d), lambda i: (0, 0)), _row(8, d)],
        out_shape=[jax.ShapeDtypeStruct((s, d), F32), jax.ShapeDtypeStruct((d, d), BF16),
                   jax.ShapeDtypeStruct((8, d), F32)],
        scratch_shapes=[pltpu.VMEM((d, d), F32)],
        compiler_params=_params("arbitrary"),
    )(dxo, y, dwc, sb, mod, ln_g, ln_b, gw)


def _conv_bwd_in(dd, glu, u, hb, x, dxo, wdw, gw, w1_rows, w1_idx, mod, gmix, tm):
    s, d = x.shape
    last = s // tm - 1
    off = CONV_HALO - (CONV_WIDTH - 1)

    def body(dd_ref, ddn_ref, glu_ref, glp_ref, u_ref, h_ref, x_ref, dxo_ref, wdw_ref, w_ref, mod_ref, g_ref,
             dxi_ref, dw_ref, dwdw_ref, st_ref, bufd, bufg, acc):
        i = pl.program_id(0)

        @pl.when(i == 0)
        def _():
            st_ref[...] = jnp.zeros_like(st_ref)
            dwdw_ref[...] = jnp.zeros_like(dwdw_ref)
            acc[...] = jnp.zeros_like(acc)

        ddv = dd_ref[...]
        bufd[pl.ds(0, tm), :] = ddv
        bufd[pl.ds(tm, CONV_HALO), :] = jnp.where(i < last, ddn_ref[...], 0.0)
        bufg[pl.ds(0, CONV_HALO), :] = jnp.where(i > 0, glp_ref[...], 0.0)
        bufg[pl.ds(CONV_HALO, tm), :] = glu_ref[...]
        dglu = jnp.zeros((tm, d), F32)
        for k in range(CONV_WIDTH):
            dglu = dglu + bufd[pl.ds(CONV_WIDTH - 1 - k, tm), :] * wdw_ref[pl.ds(k, 1), :]
            dwdw_ref[pl.ds(k, 1), :] += _sum0(bufg[pl.ds(off + k, tm), :] * ddv)
        uv = u_ref[...]
        a, g = uv[:, :d], uv[:, d:]
        sg = jax.nn.sigmoid(g)
        da = dglu * sg
        dgt = dglu * a * (sg * (1.0 - sg))
        du = jnp.concatenate([da, dgt], axis=1)
        st_ref[0:1, :] += _sum0(du)
        dub = du.astype(BF16)
        w = w_ref[...].reshape(NDEV * w1_rows, d)
        dh = _nn(dub, w)
        acc[...] += _tn(dub, h_ref[...])
        n, rinv = _rms(x_ref[...])
        dx, dsh, dsc, dgain = _rms_mod_bwd(dh, n, rinv, g_ref[...], mod_ref[1:2, :])
        dxi_ref[...] = dxo_ref[...] + dx
        st_ref[1:2, 0:d] += dsh
        st_ref[2:3, 0:d] += dsc
        st_ref[3:4, 0:d] += dgain

        @pl.when(i == last)
        def _():
            dw_ref[...] = acc[...].astype(BF16)

    tile = pl.BlockSpec((tm, d), lambda i: (i, 0))
    prv = pl.BlockSpec((CONV_HALO, d), lambda i: (jnp.maximum(i * (tm // CONV_HALO) - 1, 0), 0))
    nxt = pl.BlockSpec((CONV_HALO, d), lambda i: (jnp.minimum((i + 1) * (tm // CONV_HALO), s // CONV_HALO - 1), 0))
    return pl.pallas_call(body, name="conv_bwd_in", grid=(s // tm,),
        in_specs=[tile, nxt, tile, prv, pl.BlockSpec((tm, 2 * d), lambda i: (i, 0)), tile, tile, tile,
                  _row(CONV_WIDTH, d), _weight_spec(w1_rows, d, w1_idx), _row(8, d), _row(1, d)],
        out_specs=[tile, pl.BlockSpec((2 * d, d), lambda i: (0, 0)), _row(CONV_HALO, d), _row(8, 2 * d)],
        out_shape=[jax.ShapeDtypeStruct((s, d), F32), jax.ShapeDtypeStruct((2 * d, d), BF16),
                   jax.ShapeDtypeStruct((CONV_HALO, d), F32), jax.ShapeDtypeStruct((8, 2 * d), F32)],
        scratch_shapes=[pltpu.VMEM((tm + CONV_HALO, d), F32), pltpu.VMEM((CONV_HALO + tm, d), F32),
                        pltpu.VMEM((2 * d, d), F32)],
        compiler_params=_params("arbitrary"),
    )(dd, dd, glu, glu, u, hb, x, dxo, wdw, gw, mod, gmix)


def _layout(rows):
    offs, cur = [], 0
    for r in rows:
        cur = -(-cur // r) * r
        offs.append(cur)
        cur += r
    return offs, -(-cur // 64) * 64


def kernel(x, c, ada_w, ada_b, norm_mix_g, norm_ffn_g, conv_w1, conv_b1, conv_wdw, conv_bdw, conv_ln_g, conv_ln_b, conv_w2, conv_b2, pool_w, pool_ls, ffn_w_gate, ffn_w_up, ffn_w_down, final_g, loss_target, m_ada_w, m_ada_b, m_norm_mix_g, m_norm_ffn_g, m_conv_w1, m_conv_b1, m_conv_wdw, m_conv_bdw, m_conv_ln_g, m_conv_ln_b, m_conv_w2, m_conv_b2, m_pool_w, m_pool_ls, m_ffn_w_gate, m_ffn_w_up, m_ffn_w_down, m_final_g, v_ada_w, v_ada_b, v_norm_mix_g, v_norm_ffn_g, v_conv_w1, v_conv_b1, v_conv_wdw, v_conv_bdw, v_conv_ln_g, v_conv_ln_b, v_conv_w2, v_conv_b2, v_pool_w, v_pool_ls, v_ffn_w_gate, v_ffn_w_up, v_ffn_w_down, v_final_g):
    _, s, d = x.shape
    f = ffn_w_down.shape[1] * NDEV
    fs = f // NDEV
    r1, r2 = 2 * d // NDEV, d // NDEV
    ng = len(POOL_WINDOWS)
    dg = d // ng
    pr = ng * (dg // NDEV) * dg // d
    ncol = ada_w.shape[2]
    dc = d // NDEV
    tm = min(256, s)
    me = _my_index()
    x0 = x.reshape(s, d)
    target = loss_target.reshape(s, d)

    small = jnp.concatenate([c.reshape(NDEV, dc), conv_wdw[0], pool_ls], axis=0)
    small_all = _allgather_small(small, "allgather_small")
    c_all = small_all[:, 0:NDEV, :].reshape(NDEV, d)
    wdw = small_all[:, NDEV:NDEV + CONV_WIDTH, :].transpose(1, 0, 2).reshape(CONV_WIDTH, d)
    ls = small_all[:, NDEV + CONV_WIDTH, :].reshape(1, d)
    bias = lax.dynamic_slice_in_dim(ada_b, me * ncol, ncol, axis=1)[:, None, :]
    mod_cols = _ada_forward(c_all, ada_w, bias)
    mod_all = _allgather_small(mod_cols.reshape(2 * NDEV, ncol), "allgather_mod")
    mod_mine = lax.dynamic_index_in_dim(mod_all.reshape(NDEV, 2, NDEV, ncol), me, axis=2, keepdims=False)
    mod = mod_mine.transpose(1, 0, 2).reshape(2, 6, d)
    mod = jnp.concatenate([mod, jnp.zeros((2, 2, d), F32)], axis=1)

    offs, total = _layout([fs] * 6 + [pr, r2, r1])
    segs = [ffn_w_gate[0].T, ffn_w_up[0].T, ffn_w_down[0], ffn_w_gate[1].T, ffn_w_up[1].T, ffn_w_down[1],
            pool_w.reshape(pr, d), conv_w2[0], conv_w1[0].T]
    pieces, cur = [], 0
    for o, sg in zip(offs, segs):
        if o > cur:
            pieces.append(jnp.zeros((o - cur, d), BF16))
        pieces.append(sg.astype(BF16))
        cur = o + sg.shape[0]
    if total > cur:
        pieces.append(jnp.zeros((total - cur, d), BF16))
    gw = _allgather_weights(jnp.concatenate(pieces, axis=0), 4)
    idx = [o // r for o, r in zip(offs, [fs] * 6 + [pr, r2, r1])]
    pw = gw[:, offs[6]:offs[6] + pr, :].reshape(NDEV, ng, dg // NDEV, dg).transpose(1, 0, 2, 3).reshape(ng, dg, dg)

    h0, u, glu = _conv_in(x0, mod[0], norm_mix_g[0:1], gw, r1, idx[8], conv_b1, tm)
    dwc, sb, y0, x1 = _conv_mid(glu, wdw, conv_bdw, conv_ln_g, conv_ln_b, gw, r2, idx[7], conv_b2, x0, mod[0], tm)
    h1, gg0, uu0, yf0, x2 = _ffn_fwd(x1, mod[0], norm_ffn_g[0:1], gw, fs, idx[0], f, tm, "ffn_fwd0")
    mixed, yp, x3 = _pool_fwd(x2, mod[1], norm_mix_g[1:2], pw, ls, tm)
    h3, gg1, uu1, yf1, x4 = _ffn_fwd(x3, mod[1], norm_ffn_g[1:2], gw, fs, idx[3], f, tm, "ffn_fwd1")
    dx4, st_loss = _loss_head(x4, target, final_g.reshape(1, d), tm)
    loss = lax.psum(st_loss[2, 0], ("x", "y", "c"))

    fb = f // 2 if (f // 2) % 128 == 0 else f
    ts = min(512, s)
    dgb, dub, ab, dyb, dx3, st_f1 = _ffn_bwd(dx4, x3, gg1, uu1, yf1, mod[1], norm_ffn_g[1:2], gw, fs, idx[3], f, tm, "ffn_bwd1")
    gf1 = _ffn_wgrad(dgb, dub, ab, h3, dyb, fb, ts, "ffn_wgrad1")
    dx2, gpw, st_p = _pool_bwd(dx3, x2, yp, mixed, mod[1], norm_mix_g[1:2], pw, ls, tm)
    dgb, dub, ab, dyb, dx1, st_f0 = _ffn_bwd(dx2, x1, gg0, uu0, yf0, mod[0], norm_ffn_g[0:1], gw, fs, idx[0], f, tm, "ffn_bwd0")
    gf0 = _ffn_wgrad(dgb, dub, ab, h1, dyb, fb, ts, "ffn_wgrad0")
    dd, gw2, st_m = _conv_bwd_mid(dx1, y0, dwc, sb, mod[0], conv_ln_g, conv_ln_b, gw, r2, idx[7], tm)
    dx0, gw1, gwdw, st_c = _conv_bwd_in(dd, glu, u, h0, x0, dx1, wdw, gw, r1, idx[8], mod[0], norm_mix_g[0:1], tm)

    lands = _reduce_scatter([gf0, gf1, gpw, gw2[None], gw1[None]])
    sf0 = _sum_slots(lands[0].reshape(NDEV, 3 * fs, d), "sum_ffn0").reshape(3, fs, d)
    sf1 = _sum_slots(lands[1].reshape(NDEV, 3 * fs, d), "sum_ffn1").reshape(3, fs, d)
    spw = _sum_slots(lands[2].reshape(NDEV, ng * (dg // NDEV), dg), "sum_pool")
    sw2 = _sum_slots(lands[3].reshape(NDEV, r2, d), "sum_w2")
    sw1 = _sum_slots(lands[4].reshape(NDEV, r1, d), "sum_w1")
    g_gate = jnp.stack([sf0[0].T, sf1[0].T])
    g_up = jnp.stack([sf0[1].T, sf1[1].T])
    g_down = jnp.stack([sf0[2], sf1[2]])
    g_w1 = sw1.T[None]
    g_w2 = sw2[None]
    g_pw = spw.reshape(1, ng, dg // NDEV, dg)

    zrow = jnp.zeros((1, d), F32)
    prow = jnp.concatenate([
        st_c[1:3, 0:d], st_m[0:1], st_f0[0:3], st_p[0:2], st_p[4:5], st_f1[0:3],
        st_c[3:4, 0:d], st_p[3:4], st_f0[3:4], st_f1[3:4],
        st_c[0:1, 0:d], st_c[0:1, d:2 * d], st_m[4:5], st_m[2:4], st_m[1:2], st_loss[0:1],
        gwdw[0:CONV_WIDTH], st_p[5:6], zrow], axis=0)
    p_all = _allgather_small(prow, "allgather_stats")
    psum = _sum_slots(p_all, "sum_stats")
    dmod_all = p_all[:, 0:12, :].reshape(NDEV, 2, 6 * d)
    dmod_cols = lax.dynamic_slice_in_dim(dmod_all, me * ncol, ncol, axis=2).transpose(1, 0, 2)
    g_ada_w = _ada_wgrad(c_all, dmod_cols)

    def adam(w, g, m, v, name):
        shp = w.shape
        w2d = (-1, shp[-1])
        dl, mo, vo = _adamw(w.reshape(w2d), g.reshape(w2d), m.reshape(w2d), v.reshape(w2d), name)
        return dl.reshape(shp), mo.reshape(shp), vo.reshape(shp)

    rep_names = ["ada_b", "norm_mix_g", "norm_ffn_g", "conv_b1", "conv_bdw", "conv_ln_g", "conv_ln_b", "conv_b2", "final_g"]
    rep_w = [ada_b, norm_mix_g, norm_ffn_g, conv_b1, conv_bdw, conv_ln_g, conv_ln_b, conv_b2, final_g]
    rep_m = [m_ada_b, m_norm_mix_g, m_norm_ffn_g, m_conv_b1, m_conv_bdw, m_conv_ln_g, m_conv_ln_b, m_conv_b2, m_final_g]
    rep_v = [v_ada_b, v_norm_mix_g, v_norm_ffn_g, v_conv_b1, v_conv_bdw, v_conv_ln_g, v_conv_ln_b, v_conv_b2, v_final_g]
    nrep = sum(w.size for w in rep_w) // d
    pad = jnp.zeros(((-nrep) % 8, d), F32)

    def pack(arrs, fill):
        return jnp.concatenate([a.reshape(-1, d) for a in arrs] + [pad + fill], axis=0)

    rep_g = jnp.concatenate([psum[0:nrep], pad], axis=0)
    rep_d, rep_mo, rep_vo = _adamw(pack(rep_w, 0.0), rep_g, pack(rep_m, 0.0), pack(rep_v, 1.0), "adamw_replicated")

    def unpack(packed):
        out, cur = [], 0
        for w in rep_w:
            k = w.size // d
            out.append(packed[cur:cur + k].reshape(w.shape))
            cur += k
        return out

    rep = dict(zip(rep_names, zip(unpack(psum), unpack(rep_d), unpack(rep_mo), unpack(rep_vo))))

    g_wdw_full = psum[nrep:nrep + CONV_WIDTH]
    g_wdw = lax.dynamic_slice_in_dim(g_wdw_full, me * dc, dc, axis=1)
    g_ls = lax.dynamic_slice_in_dim(psum[nrep + CONV_WIDTH:nrep + CONV_WIDTH + 1], me * dc, dc, axis=1)
    tiny = lambda a, b: jnp.concatenate([a.reshape(CONV_WIDTH, dc), b.reshape(1, dc)], axis=0)
    t_d, t_m, t_v = _adamw(tiny(conv_wdw, pool_ls), tiny(g_wdw, g_ls), tiny(m_conv_wdw, m_pool_ls),
                           tiny(v_conv_wdw, v_pool_ls), "adamw_taps")

    def taps(a):
        return a[0:CONV_WIDTH][None], a[CONV_WIDTH:CONV_WIDTH + 1]

    sharded = {
        "ada_w": (g_ada_w,) + adam(ada_w, g_ada_w, m_ada_w, v_ada_w, "adamw_ada_w"),
        "conv_w1": (g_w1,) + adam(conv_w1, g_w1, m_conv_w1, v_conv_w1, "adamw_w1"),
        "conv_w2": (g_w2,) + adam(conv_w2, g_w2, m_conv_w2, v_conv_w2, "adamw_w2"),
        "pool_w": (g_pw,) + adam(pool_w, g_pw, m_pool_w, v_pool_w, "adamw_pool_w"),
        "ffn_w_gate": (g_gate,) + adam(ffn_w_gate, g_gate, m_ffn_w_gate, v_ffn_w_gate, "adamw_gate"),
        "ffn_w_up": (g_up,) + adam(ffn_w_up, g_up, m_ffn_w_up, v_ffn_w_up, "adamw_up"),
        "ffn_w_down": (g_down,) + adam(ffn_w_down, g_down, m_ffn_w_down, v_ffn_w_down, "adamw_down"),
        "conv_wdw": (g_wdw[None], taps(t_d)[0], taps(t_m)[0], taps(t_v)[0]),
        "pool_ls": (g_ls, taps(t_d)[1], taps(t_m)[1], taps(t_v)[1]),
    }
    every = {**rep, **sharded}
    order = ["ada_w", "ada_b", "norm_mix_g", "norm_ffn_g", "conv_w1", "conv_b1", "conv_wdw", "conv_bdw", "conv_ln_g",
             "conv_ln_b", "conv_w2", "conv_b2", "pool_w", "pool_ls", "ffn_w_gate", "ffn_w_up", "ffn_w_down", "final_g"]
    grads = [every[n][0] for n in order]
    deltas = [every[n][1] for n in order]
    new_m = [every[n][2] for n in order]
    new_v = [every[n][3] for n in order]
    return (loss, dx0.reshape(1, s, d), *grads, *deltas, *new_m, *new_v)
```

```python
import functools

import jax
import jax.numpy as jnp
from jax import lax
from jax.experimental import pallas as pl
from jax.experimental.pallas import tpu as pltpu

NDEV = 8
EPS = 1e-6
CONV_WIDTH = 31
POOL_WINDOWS = (2, 4, 8, 16)
CONV_HALO = 32
POOL_HALO = 16
ADAM_LR = 0.001
ADAM_B1 = 0.9
ADAM_B2 = 0.999
ADAM_EPS = 1e-08
ADAM_WD = 0.01
ADAM_STEP = 10
VMEM_LIMIT = 56 * 2**20
MESH = pl.DeviceIdType.MESH
F32 = jnp.float32
BF16 = jnp.bfloat16


def _nt(a, b):
    return lax.dot_general(a, b, (((1,), (1,)), ((), ())), preferred_element_type=F32)


def _nn(a, b):
    return lax.dot_general(a, b, (((1,), (0,)), ((), ())), preferred_element_type=F32)


def _tn(a, b):
    return lax.dot_general(a, b, (((0,), (0,)), ((), ())), preferred_element_type=F32)


def _sum0(v):
    return jnp.sum(v, axis=0, keepdims=True)


def _rms(x):
    rinv = lax.rsqrt(jnp.mean(x * x, axis=-1, keepdims=True) + EPS)
    return x * rinv, rinv


def _rms_mod_bwd(dh, n, rinv, g, sc):
    dhs = dh * (1.0 + sc)
    dn = dhs * g
    dx = rinv * (dn - n * jnp.mean(dn * n, axis=-1, keepdims=True))
    return dx, _sum0(dh), _sum0(dh * (n * g)), _sum0(dhs * n)


def _silu_grad(z, sg):
    return sg * (1.0 + z * (1.0 - sg))


def _params(*sem):
    return pltpu.CompilerParams(dimension_semantics=sem, vmem_limit_bytes=VMEM_LIMIT)


def _row(i, d):
    return pl.BlockSpec((i, d), lambda *_: (0, 0))


def _weight_spec(rows, d, idx):
    return pl.BlockSpec((NDEV, rows, d), lambda *_: (0, idx, 0), pipeline_mode=pl.Buffered(1))


def _my_index():
    return 4 * lax.axis_index("x") + 2 * lax.axis_index("y") + lax.axis_index("c")


def _peer(k):
    x, y, c = lax.axis_index("x"), lax.axis_index("y"), lax.axis_index("c")
    px = 1 - x if k & 4 else x
    py = 1 - y if k & 2 else y
    pc = 1 - c if k & 1 else c
    return (px, py, pc), 4 * px + 2 * py + pc


def _allgather_small(v, name):
    r, c = v.shape

    def body(v_ref, out_ref, send_sems, recv_sems):
        me = _my_index()
        out_ref[me] = v_ref[...]
        copies = []
        for k in range(1, NDEV):
            dev, _ = _peer(k)
            copies.append(pltpu.make_async_remote_copy(
                src_ref=v_ref, dst_ref=out_ref.at[me], send_sem=send_sems.at[k - 1], recv_sem=recv_sems.at[k - 1],
                device_id=dev, device_id_type=MESH))
        for cp in copies:
            cp.start()
        for cp in copies:
            cp.wait()

    return pl.pallas_call(body, name=name,
        out_shape=jax.ShapeDtypeStruct((NDEV, r, c), v.dtype),
        in_specs=[pl.BlockSpec(memory_space=pltpu.VMEM)],
        out_specs=pl.BlockSpec(memory_space=pltpu.VMEM),
        scratch_shapes=[pltpu.SemaphoreType.DMA((NDEV - 1,)), pltpu.SemaphoreType.DMA((NDEV - 1,))],
    )(v)


def _gather_sems():
    return [pltpu.SemaphoreType.DMA((NDEV - 1,)), pltpu.SemaphoreType.DMA((NDEV - 1,)), pltpu.SemaphoreType.DMA((1,))]


def _gather_copies(src, dst, sems):
    send_sems, recv_sems, local_sem = sems
    me = _my_index()
    copies = [pltpu.make_async_copy(src, dst.at[me], local_sem.at[0])]
    for k in range(1, NDEV):
        dev, _ = _peer(k)
        copies.append(pltpu.make_async_remote_copy(
            src_ref=src, dst_ref=dst.at[me], send_sem=send_sems.at[k - 1], recv_sem=recv_sems.at[k - 1],
            device_id=dev, device_id_type=MESH))
    return copies


def _scatter_sems(n):
    return [pltpu.SemaphoreType.DMA((7 * n,)), pltpu.SemaphoreType.DMA((7 * n,)), pltpu.SemaphoreType.DMA((n,))]


def _scatter_copies(srcs, dsts, sems):
    send_sems, recv_sems, local_sems = sems
    me = _my_index()
    copies = []
    for a, (src, dst) in enumerate(zip(srcs, dsts)):
        r = dst.shape[2]
        copies.append(pltpu.make_async_copy(src.at[:, pl.ds(me * r, r), :], dst.at[me], local_sems.at[a]))
        for k in range(1, NDEV):
            dev, p = _peer(k)
            copies.append(pltpu.make_async_remote_copy(
                src_ref=src.at[:, pl.ds(p * r, r), :], dst_ref=dst.at[me],
                send_sem=send_sems.at[a * 7 + k - 1], recv_sem=recv_sems.at[a * 7 + k - 1],
                device_id=dev, device_id_type=MESH))
    return copies


def _land_shape(part):
    a, r, c = part.shape
    return jax.ShapeDtypeStruct((NDEV, a, r // NDEV, c), part.dtype)


ANY = pl.BlockSpec(memory_space=pl.ANY)


def _allgather_first(small, shard):
    r, c = small.shape

    def body(v_ref, w_ref, out_ref, gw_ref, send_sems, recv_sems, *wsems):
        me = _my_index()
        big = _gather_copies(w_ref, gw_ref, wsems)
        for cp in big:
            cp.start()
        out_ref[me] = v_ref[...]
        copies = []
        for k in range(1, NDEV):
            dev, _ = _peer(k)
            copies.append(pltpu.make_async_remote_copy(
                src_ref=v_ref, dst_ref=out_ref.at[me], send_sem=send_sems.at[k - 1], recv_sem=recv_sems.at[k - 1],
                device_id=dev, device_id_type=MESH))
        for cp in copies:
            cp.start()
        for cp in copies + big:
            cp.wait()

    return pl.pallas_call(body, name="allgather_first",
        out_shape=[jax.ShapeDtypeStruct((NDEV, r, c), small.dtype), jax.ShapeDtypeStruct((NDEV,) + shard.shape, shard.dtype)],
        in_specs=[pl.BlockSpec(memory_space=pltpu.VMEM), ANY],
        out_specs=[pl.BlockSpec(memory_space=pltpu.VMEM), ANY],
        scratch_shapes=[pltpu.SemaphoreType.DMA((NDEV - 1,)), pltpu.SemaphoreType.DMA((NDEV - 1,))] + _gather_sems(),
    )(small, shard)


def _reduce_scatter(parts):
    n = len(parts)

    def body(*refs):
        copies = _scatter_copies(refs[:n], refs[n:2 * n], refs[2 * n:])
        for cp in copies:
            cp.start()
        for cp in copies:
            cp.wait()

    return pl.pallas_call(body, name="reduce_scatter",
        out_shape=[_land_shape(p) for p in parts],
        in_specs=[ANY] * n, out_specs=[ANY] * n,
        scratch_shapes=_scatter_sems(n),
    )(*parts)


class _Ride:
    def __init__(self, kind, srcs):
        self.kind, self.srcs = kind, list(srcs)
        if kind == "gather":
            self.out_shape = [jax.ShapeDtypeStruct((NDEV,) + a.shape, a.dtype) for a in self.srcs]
            self.sems = _gather_sems()
        else:
            self.out_shape = [_land_shape(a) for a in self.srcs]
            self.sems = _scatter_sems(len(self.srcs))

    def copies(self, ins, outs, sems):
        if self.kind == "gather":
            return _gather_copies(ins[0], outs[0], sems)
        return _scatter_copies(ins, outs, sems)


def _call(kernel_body, *, name, grid, in_specs, out_specs, out_shape, args, scratch_shapes=(), ride=None):
    if ride is None:
        body = functools.partial(kernel_body)
        res = pl.pallas_call(body, name=name, grid=grid, in_specs=in_specs, out_specs=out_specs, out_shape=out_shape,
                             scratch_shapes=list(scratch_shapes), compiler_params=_params("arbitrary"))(*args)
        return list(res), []
    n_in, n_out, n_sc, nr = len(in_specs), len(out_specs), len(scratch_shapes), len(ride.srcs)
    last = grid[0] - 1

    def body(*refs):
        ins, refs = refs[:n_in], refs[n_in:]
        rin, refs = refs[:nr], refs[nr:]
        outs, refs = refs[:n_out], refs[n_out:]
        rout, refs = refs[:nr], refs[nr:]
        scratch, rsems = refs[:n_sc], refs[n_sc:]
        i = pl.program_id(0)

        @pl.when(i == 0)
        def _():
            for cp in ride.copies(rin, rout, rsems):
                cp.start()

        kernel_body(*ins, *outs, *scratch)

        @pl.when(i == last)
        def _():
            for cp in ride.copies(rin, rout, rsems):
                cp.wait()

    res = pl.pallas_call(body, name=name, grid=grid,
        in_specs=list(in_specs) + [ANY] * nr, out_specs=list(out_specs) + [ANY] * nr,
        out_shape=list(out_shape) + ride.out_shape,
        scratch_shapes=list(scratch_shapes) + ride.sems, compiler_params=_params("arbitrary"),
    )(*args, *ride.srcs)
    return res[:n_out], res[n_out:]


def _ada_forward(c_all, ada_w, bias):
    nl, d, ncol = ada_w.shape

    def body(c_ref, w_ref, b_ref, o_ref):
        cv = c_ref[...]
        ca = cv * jax.nn.sigmoid(cv)
        o_ref[0] = jnp.dot(ca, w_ref[0], preferred_element_type=F32, precision=lax.Precision.HIGHEST) + b_ref[0]

    return pl.pallas_call(body, name="ada_forward", grid=(nl,),
        in_specs=[pl.BlockSpec((NDEV, d), lambda i: (0, 0)), pl.BlockSpec((1, d, ncol), lambda i: (i, 0, 0)),
                  pl.BlockSpec((1, 1, ncol), lambda i: (i, 0, 0))],
        out_specs=pl.BlockSpec((1, NDEV, ncol), lambda i: (i, 0, 0)),
        out_shape=jax.ShapeDtypeStruct((nl, NDEV, ncol), F32),
        compiler_params=_params("arbitrary"),
    )(c_all, ada_w, bias)


def _ada_wgrad(c_all, dmod):
    nl, _, ncol = dmod.shape
    d = c_all.shape[1]
    bd = min(d, 256)

    def body(c_ref, dm_ref, o_ref):
        cv = c_ref[...]
        ca = cv * jax.nn.sigmoid(cv)
        o_ref[0] = lax.dot_general(ca, dm_ref[0], (((0,), (0,)), ((), ())), preferred_element_type=F32,
                                   precision=lax.Precision.HIGHEST)

    return pl.pallas_call(body, name="ada_wgrad", grid=(nl, d // bd),
        in_specs=[pl.BlockSpec((NDEV, bd), lambda i, j: (0, j)), pl.BlockSpec((1, NDEV, ncol), lambda i, j: (i, 0, 0))],
        out_specs=pl.BlockSpec((1, bd, ncol), lambda i, j: (i, j, 0)),
        out_shape=jax.ShapeDtypeStruct((nl, d, ncol), F32),
        compiler_params=_params("arbitrary", "arbitrary"),
    )(c_all, dmod)


def _row_block(r, c, bytes_per_row_elem=4, budget=2 * 2**20):
    if r * c * bytes_per_row_elem <= budget or r % 8:
        return r
    best = 8
    for b in range(8, r + 1, 8):
        if r % b == 0 and b * c * bytes_per_row_elem <= budget:
            best = b
    return best


def _sum_slots(land, name):
    _, r, c = land.shape
    br = _row_block(r, c, 8 * land.dtype.itemsize)

    def body(l_ref, o_ref):
        acc = l_ref[0].astype(F32)
        for s in range(1, NDEV):
            acc = acc + l_ref[s].astype(F32)
        o_ref[...] = acc

    return pl.pallas_call(body, name=name, grid=(r // br,),
        in_specs=[pl.BlockSpec((NDEV, br, c), lambda i: (0, i, 0))],
        out_specs=pl.BlockSpec((br, c), lambda i: (i, 0)),
        out_shape=jax.ShapeDtypeStruct((r, c), F32),
        compiler_params=_params("arbitrary"),
    )(land)


def _adamw(w, g, m, v, name):
    r, c = w.shape
    br = _row_block(r, c)

    def body(w_ref, g_ref, m_ref, v_ref, d_ref, mo_ref, vo_ref):
        gv = g_ref[...]
        m2 = ADAM_B1 * m_ref[...] + (1.0 - ADAM_B1) * gv
        v2 = ADAM_B2 * v_ref[...] + (1.0 - ADAM_B2) * (gv * gv)
        m_hat = m2 / (1.0 - ADAM_B1 ** ADAM_STEP)
        v_hat = v2 / (1.0 - ADAM_B2 ** ADAM_STEP)
        d_ref[...] = -ADAM_LR * (m_hat / (jnp.sqrt(v_hat) + ADAM_EPS) + ADAM_WD * w_ref[...])
        mo_ref[...] = m2
        vo_ref[...] = v2

    spec = pl.BlockSpec((br, c), lambda i: (i, 0))
    return pl.pallas_call(body, name=name, grid=(r // br,),
        in_specs=[spec] * 4, out_specs=[spec] * 3,
        out_shape=[jax.ShapeDtypeStruct((r, c), F32)] * 3,
        compiler_params=_params("arbitrary"),
    )(w, g, m, v)


CONV_ROWS = 64
LANES = 128


def _shifted_copies(buf, sh, n):
    sh[0] = buf[...]
    for r in range(1, 8):
        sh[r, pl.ds(0, n - 8), :] = buf[pl.ds(r, n - 8), :]


def _window(sh, o, rows, cols):
    return sh[o % 8, pl.ds(o - o % 8, rows), cols]


def _conv_in(x, mod, gmix, gw, w1_rows, w1_idx, b1, tm, ride):
    s, d = x.shape

    def kernel_body(x_ref, mod_ref, g_ref, w_ref, b_ref, h_ref, u_ref, glu_ref):
        n, _ = _rms(x_ref[...])
        h = (n * g_ref[...]) * (1.0 + mod_ref[1:2, :]) + mod_ref[0:1, :]
        hb = h.astype(BF16)
        h_ref[...] = hb
        u = _nt(hb, w_ref[...].reshape(NDEV * w1_rows, d)) + b_ref[...]
        u_ref[...] = u
        glu_ref[...] = u[:, :d] * jax.nn.sigmoid(u[:, d:])

    tile = pl.BlockSpec((tm, d), lambda i: (i, 0))
    return _call(kernel_body, name="conv_in", grid=(s // tm,),
        in_specs=[tile, _row(8, d), _row(1, d), _weight_spec(w1_rows, d, w1_idx), _row(1, 2 * d)],
        out_specs=[tile, pl.BlockSpec((tm, 2 * d), lambda i: (i, 0)), tile],
        out_shape=[jax.ShapeDtypeStruct((s, d), BF16), jax.ShapeDtypeStruct((s, 2 * d), F32),
                   jax.ShapeDtypeStruct((s, d), F32)],
        args=(x, mod, gmix, gw, b1), ride=ride)


def _conv_mid(glu, wdw, bdw, ln_g, ln_b, gw, w2_rows, w2_idx, b2, x, mod, tm, ride):
    s, d = x.shape
    off = CONV_HALO - (CONV_WIDTH - 1)
    rc = min(CONV_ROWS, tm)

    def kernel_body(glu_ref, halo_ref, wdw_ref, bdw_ref, lng_ref, lnb_ref, w_ref, b2_ref, x_ref, mod_ref,
                    dwc_ref, s_ref, y_ref, x1_ref, buf, sh):
        i = pl.program_id(0)
        buf[pl.ds(0, CONV_HALO), :] = jnp.where(i > 0, halo_ref[...], 0.0)
        buf[pl.ds(CONV_HALO, tm), :] = glu_ref[...]
        _shifted_copies(buf, sh, CONV_HALO + tm)
        for cb in range(d // LANES):
            cols = pl.ds(cb * LANES, LANES)
            taps = wdw_ref[:, cols]
            for r in range(tm // rc):
                part = jnp.zeros((rc, LANES), F32) + bdw_ref[:, cols]
                for k in range(CONV_WIDTH):
                    part = part + _window(sh, r * rc + off + k, rc, cols) * taps[k:k + 1, :]
                dwc_ref[pl.ds(r * rc, rc), cols] = part
        acc = dwc_ref[...]
        mu = jnp.mean(acc, axis=-1, keepdims=True)
        xc = acc - mu
        rstd = lax.rsqrt(jnp.mean(xc * xc, axis=-1, keepdims=True) + EPS)
        ln = (xc * rstd) * lng_ref[...] + lnb_ref[...]
        sb = (ln * jax.nn.sigmoid(ln)).astype(BF16)
        s_ref[...] = sb
        y = _nn(sb, w_ref[...].reshape(NDEV * w2_rows, d)) + b2_ref[...]
        y_ref[...] = y.astype(BF16)
        x1_ref[...] = x_ref[...] + (1.0 + mod_ref[2:3, :]) * y

    tile = pl.BlockSpec((tm, d), lambda i: (i, 0))
    halo = pl.BlockSpec((CONV_HALO, d), lambda i: (jnp.maximum(i * (tm // CONV_HALO) - 1, 0), 0))
    return _call(kernel_body, name="conv_mid", grid=(s // tm,),
        in_specs=[tile, halo, _row(CONV_WIDTH, d), _row(1, d), _row(1, d), _row(1, d),
                  _weight_spec(w2_rows, d, w2_idx), _row(1, d), tile, _row(8, d)],
        out_specs=[tile, tile, tile, tile],
        out_shape=[jax.ShapeDtypeStruct((s, d), F32), jax.ShapeDtypeStruct((s, d), BF16),
                   jax.ShapeDtypeStruct((s, d), BF16), jax.ShapeDtypeStruct((s, d), F32)],
        scratch_shapes=[pltpu.VMEM((CONV_HALO + tm, d), F32), pltpu.VMEM((8, CONV_HALO + tm, d), F32)],
        args=(glu, glu, wdw, bdw, ln_g, ln_b, gw, b2, x, mod), ride=ride)


def _ffn_fwd(x, mod, gffn, weights, fs, f, tm, name, ride=None):
    s, d = x.shape

    def kernel_body(x_ref, mod_ref, g_ref, wg_ref, wu_ref, wd_ref, h_ref, gg_ref, uu_ref, y_ref, xo_ref):
        xv = x_ref[...]
        n, _ = _rms(xv)
        hb = ((n * g_ref[...]) * (1.0 + mod_ref[4:5, :]) + mod_ref[3:4, :]).astype(BF16)
        h_ref[...] = hb
        gg = _nt(hb, wg_ref[...].reshape(f, d))
        uu = _nt(hb, wu_ref[...].reshape(f, d))
        gg_ref[...] = gg
        uu_ref[...] = uu
        ab = ((gg * jax.nn.sigmoid(gg)) * uu).astype(BF16)
        y = _nn(ab, wd_ref[...].reshape(f, d))
        y_ref[...] = y.astype(BF16)
        xo_ref[...] = xv + (1.0 + mod_ref[5:6, :]) * y

    tile = pl.BlockSpec((tm, d), lambda i: (i, 0))
    wide = pl.BlockSpec((tm, f), lambda i: (i, 0))
    return _call(kernel_body, name=name, grid=(s // tm,),
        in_specs=[tile, _row(8, d), _row(1, d)] + [_weight_spec(fs, d, idx) for _, idx in weights],
        out_specs=[tile, wide, wide, tile, tile],
        out_shape=[jax.ShapeDtypeStruct((s, d), BF16), jax.ShapeDtypeStruct((s, f), F32),
                   jax.ShapeDtypeStruct((s, f), F32), jax.ShapeDtypeStruct((s, d), BF16),
                   jax.ShapeDtypeStruct((s, d), F32)],
        args=(x, mod, gffn) + tuple(a for a, _ in weights), ride=ride)


def _pool_fwd(x, mod, gmix, pw, ls, tm):
    s, d = x.shape
    dg = d // len(POOL_WINDOWS)

    def body(x_ref, halo_ref, mod_ref, g_ref, pw_ref, ls_ref, mixed_ref, yp_ref, xo_ref, buf):
        i = pl.program_id(0)

        def hfun(xv):
            n, _ = _rms(xv)
            return (n * g_ref[...]) * (1.0 + mod_ref[1:2, :]) + mod_ref[0:1, :]

        xv = x_ref[...]
        h = hfun(xv)
        buf[pl.ds(0, POOL_HALO), :] = jnp.where(i > 0, hfun(halo_ref[...]), 0.0)
        buf[pl.ds(POOL_HALO, tm), :] = h
        t = i * tm + lax.broadcasted_iota(jnp.int32, (tm, 1), 0)
        gate = 1.0 + mod_ref[2:3, :]
        for gi, w in enumerate(POOL_WINDOWS):
            cols = pl.ds(gi * dg, dg)
            ws = buf[pl.ds(POOL_HALO, tm), cols]
            for j in range(1, w):
                ws = ws + buf[pl.ds(POOL_HALO - j, tm), cols]
            inv = 1.0 / jnp.minimum(t + 1, w).astype(F32)
            mb = (ws * inv - h[:, gi * dg:(gi + 1) * dg]).astype(BF16)
            mixed_ref[:, cols] = mb
            yp = _nn(mb, pw_ref[gi])
            yp_ref[:, cols] = yp.astype(BF16)
            xo_ref[:, cols] = xv[:, gi * dg:(gi + 1) * dg] + gate[:, gi * dg:(gi + 1) * dg] * (yp * ls_ref[:, cols])

    tile = pl.BlockSpec((tm, d), lambda i: (i, 0))
    halo = pl.BlockSpec((POOL_HALO, d), lambda i: (jnp.maximum(i * (tm // POOL_HALO) - 1, 0), 0))
    return pl.pallas_call(body, name="pool_fwd", grid=(s // tm,),
        in_specs=[tile, halo, _row(8, d), _row(1, d), pl.BlockSpec((len(POOL_WINDOWS), dg, dg), lambda i: (0, 0, 0)),
                  _row(1, d)],
        out_specs=[tile, tile, tile],
        out_shape=[jax.ShapeDtypeStruct((s, d), BF16), jax.ShapeDtypeStruct((s, d), BF16),
                   jax.ShapeDtypeStruct((s, d), F32)],
        scratch_shapes=[pltpu.VMEM((POOL_HALO + tm, d), F32)],
        compiler_params=_params("arbitrary"),
    )(x, x, mod, gmix, pw, ls)


def _loss_head(x, target, gfin, tm):
    s, d = x.shape
    last = s // tm - 1

    def body(x_ref, t_ref, g_ref, dx_ref, st_ref):
        i = pl.program_id(0)

        @pl.when(i == 0)
        def _():
            st_ref[...] = jnp.zeros_like(st_ref)

        n, rinv = _rms(x_ref[...])
        g = g_ref[...]
        err = n * g - t_ref[...]
        dy = err * (1.0 / d)
        st_ref[0:1, :] += _sum0(dy * n)
        st_ref[1:2, :] += _sum0(err * err) * (0.5 / d)
        dn = dy * g
        dx_ref[...] = rinv * (dn - n * jnp.mean(dn * n, axis=-1, keepdims=True))

        @pl.when(i == last)
        def _():
            st_ref[2:3, :] = jnp.zeros((1, d), F32) + jnp.sum(st_ref[1:2, :], axis=-1, keepdims=True)

    tile = pl.BlockSpec((tm, d), lambda i: (i, 0))
    return pl.pallas_call(body, name="loss_head", grid=(s // tm,),
        in_specs=[tile, tile, _row(1, d)],
        out_specs=[tile, _row(8, d)],
        out_shape=[jax.ShapeDtypeStruct((s, d), F32), jax.ShapeDtypeStruct((8, d), F32)],
        compiler_params=_params("arbitrary"),
    )(x, target, gfin)


def _ffn_bwd(dxo, x, gg, uu, y, mod, gffn, weights, fs, f, tm, name, ride=None):
    s, d = x.shape

    def kernel_body(dxo_ref, x_ref, gg_ref, uu_ref, y_ref, mod_ref, g_ref, wg_ref, wu_ref, wd_ref,
                    dg_ref, du_ref, a_ref, dy_ref, dxi_ref, st_ref):
        @pl.when(pl.program_id(0) == 0)
        def _():
            st_ref[...] = jnp.zeros_like(st_ref)

        dxo_v = dxo_ref[...]
        dyb = (dxo_v * (1.0 + mod_ref[5:6, :])).astype(BF16)
        dy_ref[...] = dyb
        da = _nt(dyb, wd_ref[...].reshape(f, d))
        ggv, uuv = gg_ref[...], uu_ref[...]
        sg = jax.nn.sigmoid(ggv)
        silu = ggv * sg
        a_ref[...] = (silu * uuv).astype(BF16)
        dub = (da * silu).astype(BF16)
        dgb = (da * uuv * _silu_grad(ggv, sg)).astype(BF16)
        du_ref[...] = dub
        dg_ref[...] = dgb
        dh = _nn(dgb, wg_ref[...].reshape(f, d)) + _nn(dub, wu_ref[...].reshape(f, d))
        n, rinv = _rms(x_ref[...])
        dx, dsh, dsc, dgain = _rms_mod_bwd(dh, n, rinv, g_ref[...], mod_ref[4:5, :])
        dxi_ref[...] = dxo_v + dx
        st_ref[0:1, :] += dsh
        st_ref[1:2, :] += dsc
        st_ref[2:3, :] += _sum0(dxo_v * y_ref[...].astype(F32))
        st_ref[3:4, :] += dgain

    tile = pl.BlockSpec((tm, d), lambda i: (i, 0))
    wide = pl.BlockSpec((tm, f), lambda i: (i, 0))
    return _call(kernel_body, name=name, grid=(s // tm,),
        in_specs=[tile, tile, wide, wide, tile, _row(8, d), _row(1, d)] + [_weight_spec(fs, d, idx) for _, idx in weights],
        out_specs=[wide, wide, wide, tile, tile, _row(8, d)],
        out_shape=[jax.ShapeDtypeStruct((s, f), BF16)] * 3 + [jax.ShapeDtypeStruct((s, d), BF16),
                   jax.ShapeDtypeStruct((s, d), F32), jax.ShapeDtypeStruct((8, d), F32)],
        args=(dxo, x, gg, uu, y, mod, gffn) + tuple(a for a, _ in weights), ride=ride)


def _ffn_wgrad(dgb, dub, ab, h, dyb, fb, ts, name):
    s, f = dgb.shape
    d = h.shape[1]
    last = s // ts - 1

    def body(dg_ref, du_ref, a_ref, h_ref, dy_ref, o_ref, acc):
        t = pl.program_id(1)

        @pl.when(t == 0)
        def _():
            acc[...] = jnp.zeros_like(acc)

        hv = h_ref[...]
        acc[0] += _tn(dg_ref[...], hv)
        acc[1] += _tn(du_ref[...], hv)
        acc[2] += _tn(a_ref[...], dy_ref[...])

        @pl.when(t == last)
        def _():
            o_ref[...] = acc[...].astype(BF16)

    wide = pl.BlockSpec((ts, fb), lambda j, t: (t, j))
    tile = pl.BlockSpec((ts, d), lambda j, t: (t, 0))
    return pl.pallas_call(body, name=name, grid=(f // fb, s // ts),
        in_specs=[wide, wide, wide, tile, tile],
        out_specs=pl.BlockSpec((3, fb, d), lambda j, t: (0, j, 0)),
        out_shape=jax.ShapeDtypeStruct((3, f, d), BF16),
        scratch_shapes=[pltpu.VMEM((3, fb, d), F32)],
        compiler_params=_params("arbitrary", "arbitrary"),
    )(dgb, dub, ab, h, dyb)


def _pool_bwd(dxo, x, yp, mixed, mod, gmix, pw, ls, tm):
    s, d = x.shape
    ng = len(POOL_WINDOWS)
    dg = d // ng
    last = s // tm - 1

    def body(dxo_ref, dxh_ref, x_ref, yp_ref, mixed_ref, mod_ref, g_ref, pw_ref, ls_ref,
             dxi_ref, dpw_ref, st_ref, bufy, bufq, bufh, acc):
        i = pl.program_id(0)

        @pl.when(i == 0)
        def _():
            st_ref[...] = jnp.zeros_like(st_ref)
            acc[...] = jnp.zeros_like(acc)

        gate = 1.0 + mod_ref[2:3, :]
        lsv = ls_ref[...]
        dxo_v = dxo_ref[...]
        st_ref[2:3, :] += _sum0(dxo_v * yp_ref[...].astype(F32))
        bufy[pl.ds(0, tm), :] = (dxo_v * (gate * lsv)).astype(BF16)
        bufy[pl.ds(tm, POOL_HALO), :] = jnp.where(i < last, dxh_ref[...] * (gate * lsv), 0.0).astype(BF16)
        t = i * tm + lax.broadcasted_iota(jnp.int32, (tm + POOL_HALO, 1), 0)
        for gi, w in enumerate(POOL_WINDOWS):
            cols = pl.ds(gi * dg, dg)
            dm = _nt(bufy[:, cols], pw_ref[gi])
            bufq[:, cols] = dm * (1.0 / jnp.minimum(t + 1, w).astype(F32))
            dh = bufq[pl.ds(0, tm), cols] - dm[0:tm, :]
            for j in range(1, w):
                dh = dh + bufq[pl.ds(j, tm), cols]
            bufh[:, cols] = dh
            acc[gi] += _tn(mixed_ref[:, cols], bufy[pl.ds(0, tm), cols])
        n, rinv = _rms(x_ref[...])
        dx, dsh, dsc, dgain = _rms_mod_bwd(bufh[...], n, rinv, g_ref[...], mod_ref[1:2, :])
        dxi_ref[...] = dxo_v + dx
        st_ref[0:1, :] += dsh
        st_ref[1:2, :] += dsc
        st_ref[3:4, :] += dgain

        @pl.when(i == last)
        def _():
            r = st_ref[2:3, :]
            st_ref[4:5, :] = r * lsv
            st_ref[5:6, :] = r * gate
            dpw_ref[...] = acc[...].astype(BF16)

    tile = pl.BlockSpec((tm, d), lambda i: (i, 0))
    nxt = pl.BlockSpec((POOL_HALO, d), lambda i: (jnp.minimum((i + 1) * (tm // POOL_HALO), s // POOL_HALO - 1), 0))
    pws = pl.BlockSpec((ng, dg, dg), lambda i: (0, 0, 0))
    return pl.pallas_call(body, name="pool_bwd", grid=(s // tm,),
        in_specs=[tile, nxt, tile, tile, tile, _row(8, d), _row(1, d), pws, _row(1, d)],
        out_specs=[tile, pws, _row(8, d)],
        out_shape=[jax.ShapeDtypeStruct((s, d), F32), jax.ShapeDtypeStruct((ng, dg, dg), BF16),
                   jax.ShapeDtypeStruct((8, d), F32)],
        scratch_shapes=[pltpu.VMEM((tm + POOL_HALO, d), BF16), pltpu.VMEM((tm + POOL_HALO, d), F32),
                        pltpu.VMEM((tm, d), F32), pltpu.VMEM((ng, dg, dg), F32)],
        compiler_params=_params("arbitrary"),
    )(dxo, dxo, x, yp, mixed, mod, gmix, pw, ls)


def _conv_bwd_mid(dxo, y, dwc, sb, mod, ln_g, ln_b, gw, w2_rows, w2_idx, tm):
    s, d = dwc.shape
    last = s // tm - 1

    def body(dxo_ref, y_ref, dwc_ref, s_ref, mod_ref, lng_ref, lnb_ref, w_ref, dd_ref, dw_ref, st_ref, acc):
        i = pl.program_id(0)

        @pl.when(i == 0)
        def _():
            st_ref[...] = jnp.zeros_like(st_ref)
            acc[...] = jnp.zeros_like(acc)

        dxo_v = dxo_ref[...]
        st_ref[0:1, :] += _sum0(dxo_v * y_ref[...].astype(F32))
        dy = dxo_v * (1.0 + mod_ref[2:3, :])
        st_ref[1:2, :] += _sum0(dy)
        dyb = dy.astype(BF16)
        ds = _nt(dyb, w_ref[...].reshape(NDEV * w2_rows, d))
        acc[...] += _tn(s_ref[...], dyb)
        v = dwc_ref[...]
        mu = jnp.mean(v, axis=-1, keepdims=True)
        xc = v - mu
        rstd = lax.rsqrt(jnp.mean(xc * xc, axis=-1, keepdims=True) + EPS)
        xhat = xc * rstd
        ln = xhat * lng_ref[...] + lnb_ref[...]
        dln = ds * _silu_grad(ln, jax.nn.sigmoid(ln))
        st_ref[2:3, :] += _sum0(dln * xhat)
        st_ref[3:4, :] += _sum0(dln)
        dxh = dln * lng_ref[...]
        dd = rstd * (dxh - jnp.mean(dxh, axis=-1, keepdims=True) - xhat * jnp.mean(dxh * xhat, axis=-1, keepdims=True))
        dd_ref[...] = dd
        st_ref[4:5, :] += _sum0(dd)

        @pl.when(i == last)
        def _():
            dw_ref[...] = acc[...].astype(BF16)

    tile = pl.BlockSpec((tm, d), lambda i: (i, 0))
    return pl.pallas_call(body, name="conv_bwd_mid", grid=(s // tm,),
        in_specs=[tile, tile, tile, tile, _row(8, d), _row(1, d), _row(1, d), _weight_spec(w2_rows, d, w2_idx)],
        out_specs=[tile, pl.BlockSpec((d, d), lambda i: (0, 0)), _row(8, d)],
        out_shape=[jax.ShapeDtypeStruct((s, d), F32), jax.ShapeDtypeStruct((d, d), BF16),
                   jax.ShapeDtypeStruct((8, d), F32)],
        scratch_shapes=[pltpu.VMEM((d, d), F32)],
        compiler_params=_params("arbitrary"),
    )(dxo, y, dwc, sb, mod, ln_g, ln_b, gw)


def _conv_bwd_in(dd, glu, u, hb, x, dxo, wdw, gw, w1_rows, w1_idx, mod, gmix, tm, ride):
    s, d = x.shape
    last = s // tm - 1
    off = CONV_HALO - (CONV_WIDTH - 1)
    rc = min(CONV_ROWS, tm)
    rw = 32

    def kernel_body(dd_ref, ddn_ref, glu_ref, glp_ref, u_ref, h_ref, x_ref, dxo_ref, wdw_ref, w_ref, mod_ref, g_ref,
                    dxi_ref, dw_ref, dwdw_ref, st_ref, bufd, bufg, shd, shg, dgl, accw, acc):
        i = pl.program_id(0)

        @pl.when(i == 0)
        def _():
            st_ref[...] = jnp.zeros_like(st_ref)
            accw[...] = jnp.zeros_like(accw)
            acc[...] = jnp.zeros_like(acc)

        bufd[pl.ds(0, tm), :] = dd_ref[...]
        bufd[pl.ds(tm, CONV_HALO), :] = jnp.where(i < last, ddn_ref[...], 0.0)
        bufg[pl.ds(0, CONV_HALO), :] = jnp.where(i > 0, glp_ref[...], 0.0)
        bufg[pl.ds(CONV_HALO, tm), :] = glu_ref[...]
        _shifted_copies(bufd, shd, tm + CONV_HALO)
        _shifted_copies(bufg, shg, CONV_HALO + tm)
        for cb in range(d // LANES):
            cols = pl.ds(cb * LANES, LANES)
            taps = wdw_ref[:, cols]
            for r in range(tm // rc):
                part = jnp.zeros((rc, LANES), F32)
                for k in range(CONV_WIDTH):
                    part = part + _window(shd, r * rc + CONV_WIDTH - 1 - k, rc, cols) * taps[k:k + 1, :]
                dgl[pl.ds(r * rc, rc), cols] = part
            sums = [jnp.zeros((8, LANES), F32)] * CONV_WIDTH
            for r in range(tm // rw):
                ddc = bufd[pl.ds(r * rw, rw), cols]
                for k in range(CONV_WIDTH):
                    p = _window(shg, r * rw + off + k, rw, cols) * ddc
                    for q in range(rw // 8):
                        sums[k] = sums[k] + p[q * 8:(q + 1) * 8, :]
            for k in range(CONV_WIDTH):
                accw[k, :, cols] += sums[k]
        dglu = dgl[...]
        uv = u_ref[...]
        a, g = uv[:, :d], uv[:, d:]
        sg = jax.nn.sigmoid(g)
        da = dglu * sg
        dgt = dglu * a * (sg * (1.0 - sg))
        du = jnp.concatenate([da, dgt], axis=1)
        st_ref[0:1, :] += _sum0(du)
        dub = du.astype(BF16)
        w = w_ref[...].reshape(NDEV * w1_rows, d)
        dh = _nn(dub, w)
        acc[...] += _tn(dub, h_ref[...])
        n, rinv = _rms(x_ref[...])
        dx, dsh, dsc, dgain = _rms_mod_bwd(dh, n, rinv, g_ref[...], mod_ref[1:2, :])
        dxi_ref[...] = dxo_ref[...] + dx
        st_ref[1:2, 0:d] += dsh
        st_ref[2:3, 0:d] += dsc
        st_ref[3:4, 0:d] += dgain

        @pl.when(i == last)
        def _():
            dw_ref[...] = acc[...].astype(BF16)
            dwdw_ref[...] = jnp.sum(accw[...], axis=1)

    tile = pl.BlockSpec((tm, d), lambda i: (i, 0))
    prv = pl.BlockSpec((CONV_HALO, d), lambda i: (jnp.maximum(i * (tm // CONV_HALO) - 1, 0), 0))
    nxt = pl.BlockSpec((CONV_HALO, d), lambda i: (jnp.minimum((i + 1) * (tm // CONV_HALO), s // CONV_HALO - 1), 0))
    return _call(kernel_body, name="conv_bwd_in", grid=(s // tm,),
        in_specs=[tile, nxt, tile, prv, pl.BlockSpec((tm, 2 * d), lambda i: (i, 0)), tile, tile, tile,
                  _row(CONV_WIDTH, d), _weight_spec(w1_rows, d, w1_idx), _row(8, d), _row(1, d)],
        out_specs=[tile, pl.BlockSpec((2 * d, d), lambda i: (0, 0)), _row(CONV_HALO, d), _row(8, 2 * d)],
        out_shape=[jax.ShapeDtypeStruct((s, d), F32), jax.ShapeDtypeStruct((2 * d, d), BF16),
                   jax.ShapeDtypeStruct((CONV_HALO, d), F32), jax.ShapeDtypeStruct((8, 2 * d), F32)],
        scratch_shapes=[pltpu.VMEM((tm + CONV_HALO, d), F32), pltpu.VMEM((CONV_HALO + tm, d), F32),
                        pltpu.VMEM((8, tm + CONV_HALO, d), F32), pltpu.VMEM((8, CONV_HALO + tm, d), F32),
                        pltpu.VMEM((tm, d), F32), pltpu.VMEM((CONV_HALO, 8, d), F32), pltpu.VMEM((2 * d, d), F32)],
        args=(dd, dd, glu, glu, u, hb, x, dxo, wdw, gw, mod, gmix), ride=ride)


def kernel(x, c, ada_w, ada_b, norm_mix_g, norm_ffn_g, conv_w1, conv_b1, conv_wdw, conv_bdw, conv_ln_g, conv_ln_b, conv_w2, conv_b2, pool_w, pool_ls, ffn_w_gate, ffn_w_up, ffn_w_down, final_g, loss_target, m_ada_w, m_ada_b, m_norm_mix_g, m_norm_ffn_g, m_conv_w1, m_conv_b1, m_conv_wdw, m_conv_bdw, m_conv_ln_g, m_conv_ln_b, m_conv_w2, m_conv_b2, m_pool_w, m_pool_ls, m_ffn_w_gate, m_ffn_w_up, m_ffn_w_down, m_final_g, v_ada_w, v_ada_b, v_norm_mix_g, v_norm_ffn_g, v_conv_w1, v_conv_b1, v_conv_wdw, v_conv_bdw, v_conv_ln_g, v_conv_ln_b, v_conv_w2, v_conv_b2, v_pool_w, v_pool_ls, v_ffn_w_gate, v_ffn_w_up, v_ffn_w_down, v_final_g):
    _, s, d = x.shape
    f = ffn_w_down.shape[1] * NDEV
    fs = f // NDEV
    r1, r2 = 2 * d // NDEV, d // NDEV
    ng = len(POOL_WINDOWS)
    dg = d // ng
    pr = ng * (dg // NDEV) * dg // d
    ncol = ada_w.shape[2]
    dc = d // NDEV
    tm = min(256, s)
    me = _my_index()
    x0 = x.reshape(s, d)
    target = loss_target.reshape(s, d)

    small = jnp.concatenate([c.reshape(NDEV, dc), conv_wdw[0], pool_ls], axis=0)
    shard_a = jnp.concatenate([conv_w1[0].T, conv_w2[0]], axis=0).astype(BF16)
    small_all, gwa = _allgather_first(small, shard_a)
    c_all = small_all[:, 0:NDEV, :].reshape(NDEV, d)
    wdw = small_all[:, NDEV:NDEV + CONV_WIDTH, :].transpose(1, 0, 2).reshape(CONV_WIDTH, d)
    ls = small_all[:, NDEV + CONV_WIDTH, :].reshape(1, d)
    bias = lax.dynamic_slice_in_dim(ada_b, me * ncol, ncol, axis=1)[:, None, :]
    mod_cols = _ada_forward(c_all, ada_w, bias)
    mod_all = _allgather_small(mod_cols.reshape(2 * NDEV, ncol), "allgather_mod")
    mod_mine = lax.dynamic_index_in_dim(mod_all.reshape(NDEV, 2, NDEV, ncol), me, axis=2, keepdims=False)
    mod = mod_mine.transpose(1, 0, 2).reshape(2, 6, d)
    mod = jnp.concatenate([mod, jnp.zeros((2, 2, d), F32)], axis=1)

    shard_b1 = ffn_w_gate[0].T.astype(BF16)
    shard_b2 = jnp.concatenate([ffn_w_up[0].T, ffn_w_down[0]], axis=0).astype(BF16)
    shard_c = jnp.concatenate([ffn_w_gate[1].T, ffn_w_up[1].T, ffn_w_down[1], pool_w.reshape(pr, d)], axis=0).astype(BF16)
    w1_at, w2_at = (gwa, 0), (gwa, r1 // r2)

    (h0, u, glu), (gwb1,) = _conv_in(x0, mod[0], norm_mix_g[0:1], w1_at[0], r1, w1_at[1], conv_b1, tm,
                                    _Ride("gather", [shard_b1]))
    (dwc, sb, y0, x1), (gwb2,) = _conv_mid(glu, wdw, conv_bdw, conv_ln_g, conv_ln_b, w2_at[0], r2, w2_at[1], conv_b2,
                                           x0, mod[0], tm, _Ride("gather", [shard_b2]))
    ffn0_w = [(gwb1, 0), (gwb2, 0), (gwb2, 1)]
    (h1, gg0, uu0, yf0, x2), (gwc,) = _ffn_fwd(x1, mod[0], norm_ffn_g[0:1], ffn0_w, fs, f, tm, "ffn_fwd0",
                                               _Ride("gather", [shard_c]))
    ffn1_w = [(gwc, 0), (gwc, 1), (gwc, 2)]
    pw = gwc[:, 3 * fs:3 * fs + pr, :].reshape(NDEV, ng, dg // NDEV, dg).transpose(1, 0, 2, 3).reshape(ng, dg, dg)
    mixed, yp, x3 = _pool_fwd(x2, mod[1], norm_mix_g[1:2], pw, ls, tm)
    (h3, gg1, uu1, yf1, x4), _ = _ffn_fwd(x3, mod[1], norm_ffn_g[1:2], ffn1_w, fs, f, tm, "ffn_fwd1")
    dx4, st_loss = _loss_head(x4, target, final_g.reshape(1, d), tm)
    loss = lax.psum(st_loss[2, 0], ("x", "y", "c"))

    fb = f // 2 if (f // 2) % 128 == 0 else f
    ts = min(512, s)
    (dgb, dub, ab, dyb, dx3, st_f1), _ = _ffn_bwd(dx4, x3, gg1, uu1, yf1, mod[1], norm_ffn_g[1:2], ffn1_w, fs, f, tm,
                                                  "ffn_bwd1")
    gf1 = _ffn_wgrad(dgb, dub, ab, h3, dyb, fb, ts, "ffn_wgrad1")
    dx2, gpw, st_p = _pool_bwd(dx3, x2, yp, mixed, mod[1], norm_mix_g[1:2], pw, ls, tm)
    (dgb, dub, ab, dyb, dx1, st_f0), (land_f1,) = _ffn_bwd(dx2, x1, gg0, uu0, yf0, mod[0], norm_ffn_g[0:1], ffn0_w, fs, f,
                                                           tm, "ffn_bwd0", _Ride("scatter", [gf1]))
    gf0 = _ffn_wgrad(dgb, dub, ab, h1, dyb, fb, ts, "ffn_wgrad0")
    dd, gw2, st_m = _conv_bwd_mid(dx1, y0, dwc, sb, mod[0], conv_ln_g, conv_ln_b, w2_at[0], r2, w2_at[1], tm)
    (dx0, gw1, gwdw, st_c), (land_f0, land_pw, land_w2) = _conv_bwd_in(
        dd, glu, u, h0, x0, dx1, wdw, w1_at[0], r1, w1_at[1], mod[0], norm_mix_g[0:1], tm,
        _Ride("scatter", [gf0, gpw, gw2[None]]))
    (land_w1,) = _reduce_scatter([gw1[None]])
    sf0 = _sum_slots(land_f0.reshape(NDEV, 3 * fs, d), "sum_ffn0").reshape(3, fs, d)
    sf1 = _sum_slots(land_f1.reshape(NDEV, 3 * fs, d), "sum_ffn1").reshape(3, fs, d)
    spw = _sum_slots(land_pw.reshape(NDEV, ng * (dg // NDEV), dg), "sum_pool")
    sw2 = _sum_slots(land_w2.reshape(NDEV, r2, d), "sum_w2")
    sw1 = _sum_slots(land_w1.reshape(NDEV, r1, d), "sum_w1")
    g_gate = jnp.stack([sf0[0].T, sf1[0].T])
    g_up = jnp.stack([sf0[1].T, sf1[1].T])
    g_down = jnp.stack([sf0[2], sf1[2]])
    g_w1 = sw1.T[None]
    g_w2 = sw2[None]
    g_pw = spw.reshape(1, ng, dg // NDEV, dg)

    zrow = jnp.zeros((1, d), F32)
    prow = jnp.concatenate([
        st_c[1:3, 0:d], st_m[0:1], st_f0[0:3], st_p[0:2], st_p[4:5], st_f1[0:3],
        st_c[3:4, 0:d], st_p[3:4], st_f0[3:4], st_f1[3:4],
        st_c[0:1, 0:d], st_c[0:1, d:2 * d], st_m[4:5], st_m[2:4], st_m[1:2], st_loss[0:1],
        gwdw[0:CONV_WIDTH], st_p[5:6], zrow], axis=0)
    p_all = _allgather_small(prow, "allgather_stats")
    psum = _sum_slots(p_all, "sum_stats")
    dmod_all = p_all[:, 0:12, :].reshape(NDEV, 2, 6 * d)
    dmod_cols = lax.dynamic_slice_in_dim(dmod_all, me * ncol, ncol, axis=2).transpose(1, 0, 2)
    g_ada_w = _ada_wgrad(c_all, dmod_cols)

    def adam(w, g, m, v, name):
        shp = w.shape
        w2d = (-1, shp[-1])
        dl, mo, vo = _adamw(w.reshape(w2d), g.reshape(w2d), m.reshape(w2d), v.reshape(w2d), name)
        return dl.reshape(shp), mo.reshape(shp), vo.reshape(shp)

    rep_names = ["ada_b", "norm_mix_g", "norm_ffn_g", "conv_b1", "conv_bdw", "conv_ln_g", "conv_ln_b", "conv_b2", "final_g"]
    rep_w = [ada_b, norm_mix_g, norm_ffn_g, conv_b1, conv_bdw, conv_ln_g, conv_ln_b, conv_b2, final_g]
    rep_m = [m_ada_b, m_norm_mix_g, m_norm_ffn_g, m_conv_b1, m_conv_bdw, m_conv_ln_g, m_conv_ln_b, m_conv_b2, m_final_g]
    rep_v = [v_ada_b, v_norm_mix_g, v_norm_ffn_g, v_conv_b1, v_conv_bdw, v_conv_ln_g, v_conv_ln_b, v_conv_b2, v_final_g]
    nrep = sum(w.size for w in rep_w) // d
    pad = jnp.zeros(((-nrep) % 8, d), F32)

    def pack(arrs, fill):
        return jnp.concatenate([a.reshape(-1, d) for a in arrs] + [pad + fill], axis=0)

    rep_g = jnp.concatenate([psum[0:nrep], pad], axis=0)
    rep_d, rep_mo, rep_vo = _adamw(pack(rep_w, 0.0), rep_g, pack(rep_m, 0.0), pack(rep_v, 1.0), "adamw_replicated")

    def unpack(packed):
        out, cur = [], 0
        for w in rep_w:
            k = w.size // d
            out.append(packed[cur:cur + k].reshape(w.shape))
            cur += k
        return out

    rep = dict(zip(rep_names, zip(unpack(psum), unpack(rep_d), unpack(rep_mo), unpack(rep_vo))))

    g_wdw_full = psum[nrep:nrep + CONV_WIDTH]
    g_wdw = lax.dynamic_slice_in_dim(g_wdw_full, me * dc, dc, axis=1)
    g_ls = lax.dynamic_slice_in_dim(psum[nrep + CONV_WIDTH:nrep + CONV_WIDTH + 1], me * dc, dc, axis=1)
    tiny = lambda a, b: jnp.concatenate([a.reshape(CONV_WIDTH, dc), b.reshape(1, dc)], axis=0)
    t_d, t_m, t_v = _adamw(tiny(conv_wdw, pool_ls), tiny(g_wdw, g_ls), tiny(m_conv_wdw, m_pool_ls),
                           tiny(v_conv_wdw, v_pool_ls), "adamw_taps")

    def taps(a):
        return a[0:CONV_WIDTH][None], a[CONV_WIDTH:CONV_WIDTH + 1]

    sharded = {
        "ada_w": (g_ada_w,) + adam(ada_w, g_ada_w, m_ada_w, v_ada_w, "adamw_ada_w"),
        "conv_w1": (g_w1,) + adam(conv_w1, g_w1, m_conv_w1, v_conv_w1, "adamw_w1"),
        "conv_w2": (g_w2,) + adam(conv_w2, g_w2, m_conv_w2, v_conv_w2, "adamw_w2"),
        "pool_w": (g_pw,) + adam(pool_w, g_pw, m_pool_w, v_pool_w, "adamw_pool_w"),
        "ffn_w_gate": (g_gate,) + adam(ffn_w_gate, g_gate, m_ffn_w_gate, v_ffn_w_gate, "adamw_gate"),
        "ffn_w_up": (g_up,) + adam(ffn_w_up, g_up, m_ffn_w_up, v_ffn_w_up, "adamw_up"),
        "ffn_w_down": (g_down,) + adam(ffn_w_down, g_down, m_ffn_w_down, v_ffn_w_down, "adamw_down"),
        "conv_wdw": (g_wdw[None], taps(t_d)[0], taps(t_m)[0], taps(t_v)[0]),
        "pool_ls": (g_ls, taps(t_d)[1], taps(t_m)[1], taps(t_v)[1]),
    }
    every = {**rep, **sharded}
    order = ["ada_w", "ada_b", "norm_mix_g", "norm_ffn_g", "conv_w1", "conv_b1", "conv_wdw", "conv_bdw", "conv_ln_g",
             "conv_ln_b", "conv_w2", "conv_b2", "pool_w", "pool_ls", "ffn_w_gate", "ffn_w_up", "ffn_w_down", "final_g"]
    grads = [every[n][0] for n in order]
    deltas = [every[n][1] for n in order]
    new_m = [every[n][2] for n in order]
    new_v = [every[n][3] for n in order]
    return (loss, dx0.reshape(1, s, d), *grads, *deltas, *new_m, *new_v)
```

```python
import functools

import jax
import jax.numpy as jnp
from jax import lax
from jax.experimental import pallas as pl
from jax.experimental.pallas import tpu as pltpu

NDEV = 8
EPS = 1e-6
CONV_WIDTH = 31
POOL_WINDOWS = (2, 4, 8, 16)
CONV_HALO = 32
POOL_HALO = 16
ADAM_LR = 0.001
ADAM_B1 = 0.9
ADAM_B2 = 0.999
ADAM_EPS = 1e-08
ADAM_WD = 0.01
ADAM_STEP = 10
VMEM_LIMIT = 56 * 2**20
MESH = pl.DeviceIdType.MESH
F32 = jnp.float32
BF16 = jnp.bfloat16


def _nt(a, b):
    return lax.dot_general(a, b, (((1,), (1,)), ((), ())), preferred_element_type=F32)


def _nn(a, b):
    return lax.dot_general(a, b, (((1,), (0,)), ((), ())), preferred_element_type=F32)


def _tn(a, b):
    return lax.dot_general(a, b, (((0,), (0,)), ((), ())), preferred_element_type=F32)


def _sum0(v):
    return jnp.sum(v, axis=0, keepdims=True)


def _rms(x):
    rinv = lax.rsqrt(jnp.mean(x * x, axis=-1, keepdims=True) + EPS)
    return x * rinv, rinv


def _rms_mod_bwd(dh, n, rinv, g, sc):
    dhs = dh * (1.0 + sc)
    dn = dhs * g
    dx = rinv * (dn - n * jnp.mean(dn * n, axis=-1, keepdims=True))
    return dx, _sum0(dh), _sum0(dh * (n * g)), _sum0(dhs * n)


def _silu_grad(z, sg):
    return sg * (1.0 + z * (1.0 - sg))


def _params(*sem):
    return pltpu.CompilerParams(dimension_semantics=sem, vmem_limit_bytes=VMEM_LIMIT)


def _row(i, d):
    return pl.BlockSpec((i, d), lambda *_: (0, 0))


def _weight_spec(rows, d, idx):
    return pl.BlockSpec((NDEV, rows, d), lambda *_: (0, idx, 0), pipeline_mode=pl.Buffered(1))


def _my_index():
    return 4 * lax.axis_index("x") + 2 * lax.axis_index("y") + lax.axis_index("c")


def _peer(k):
    x, y, c = lax.axis_index("x"), lax.axis_index("y"), lax.axis_index("c")
    px = 1 - x if k & 4 else x
    py = 1 - y if k & 2 else y
    pc = 1 - c if k & 1 else c
    return (px, py, pc), 4 * px + 2 * py + pc


def _allgather_small(v, name):
    r, c = v.shape

    def body(v_ref, out_ref, send_sems, recv_sems):
        me = _my_index()
        out_ref[me] = v_ref[...]
        copies = []
        for k in range(1, NDEV):
            dev, _ = _peer(k)
            copies.append(pltpu.make_async_remote_copy(
                src_ref=v_ref, dst_ref=out_ref.at[me], send_sem=send_sems.at[k - 1], recv_sem=recv_sems.at[k - 1],
                device_id=dev, device_id_type=MESH))
        for cp in copies:
            cp.start()
        for cp in copies:
            cp.wait()

    return pl.pallas_call(body, name=name,
        out_shape=jax.ShapeDtypeStruct((NDEV, r, c), v.dtype),
        in_specs=[pl.BlockSpec(memory_space=pltpu.VMEM)],
        out_specs=pl.BlockSpec(memory_space=pltpu.VMEM),
        scratch_shapes=[pltpu.SemaphoreType.DMA((NDEV - 1,)), pltpu.SemaphoreType.DMA((NDEV - 1,))],
    )(v)


def _gather_sems():
    return [pltpu.SemaphoreType.DMA((NDEV - 1,)), pltpu.SemaphoreType.DMA((NDEV - 1,)), pltpu.SemaphoreType.DMA((1,))]


def _gather_copies(src, dst, sems):
    send_sems, recv_sems, local_sem = sems
    me = _my_index()
    copies = [pltpu.make_async_copy(src, dst.at[me], local_sem.at[0])]
    for k in range(1, NDEV):
        dev, _ = _peer(k)
        copies.append(pltpu.make_async_remote_copy(
            src_ref=src, dst_ref=dst.at[me], send_sem=send_sems.at[k - 1], recv_sem=recv_sems.at[k - 1],
            device_id=dev, device_id_type=MESH))
    return copies


def _scatter_sems(n):
    return [pltpu.SemaphoreType.DMA((7 * n,)), pltpu.SemaphoreType.DMA((7 * n,)), pltpu.SemaphoreType.DMA((n,))]


def _scatter_copies(srcs, dsts, sems):
    send_sems, recv_sems, local_sems = sems
    me = _my_index()
    copies = []
    for a, (src, dst) in enumerate(zip(srcs, dsts)):
        r = dst.shape[2]
        copies.append(pltpu.make_async_copy(src.at[:, pl.ds(me * r, r), :], dst.at[me], local_sems.at[a]))
        for k in range(1, NDEV):
            dev, p = _peer(k)
            copies.append(pltpu.make_async_remote_copy(
                src_ref=src.at[:, pl.ds(p * r, r), :], dst_ref=dst.at[me],
                send_sem=send_sems.at[a * 7 + k - 1], recv_sem=recv_sems.at[a * 7 + k - 1],
                device_id=dev, device_id_type=MESH))
    return copies


def _land_shape(part):
    a, r, c = part.shape
    return jax.ShapeDtypeStruct((NDEV, a, r // NDEV, c), part.dtype)


ANY = pl.BlockSpec(memory_space=pl.ANY)


def _allgather_first(small, shard):
    r, c = small.shape

    def body(v_ref, w_ref, out_ref, gw_ref, send_sems, recv_sems, *wsems):
        me = _my_index()
        big = _gather_copies(w_ref, gw_ref, wsems)
        for cp in big:
            cp.start()
        out_ref[me] = v_ref[...]
        copies = []
        for k in range(1, NDEV):
            dev, _ = _peer(k)
            copies.append(pltpu.make_async_remote_copy(
                src_ref=v_ref, dst_ref=out_ref.at[me], send_sem=send_sems.at[k - 1], recv_sem=recv_sems.at[k - 1],
                device_id=dev, device_id_type=MESH))
        for cp in copies:
            cp.start()
        for cp in copies + big:
            cp.wait()

    return pl.pallas_call(body, name="allgather_first",
        out_shape=[jax.ShapeDtypeStruct((NDEV, r, c), small.dtype), jax.ShapeDtypeStruct((NDEV,) + shard.shape, shard.dtype)],
        in_specs=[pl.BlockSpec(memory_space=pltpu.VMEM), ANY],
        out_specs=[pl.BlockSpec(memory_space=pltpu.VMEM), ANY],
        scratch_shapes=[pltpu.SemaphoreType.DMA((NDEV - 1,)), pltpu.SemaphoreType.DMA((NDEV - 1,))] + _gather_sems(),
    )(small, shard)


def _reduce_scatter(parts):
    n = len(parts)

    def body(*refs):
        copies = _scatter_copies(refs[:n], refs[n:2 * n], refs[2 * n:])
        for cp in copies:
            cp.start()
        for cp in copies:
            cp.wait()

    return pl.pallas_call(body, name="reduce_scatter",
        out_shape=[_land_shape(p) for p in parts],
        in_specs=[ANY] * n, out_specs=[ANY] * n,
        scratch_shapes=_scatter_sems(n),
    )(*parts)


class _Ride:
    def __init__(self, kind, srcs):
        self.kind, self.srcs = kind, list(srcs)
        if kind == "gather":
            self.out_shape = [jax.ShapeDtypeStruct((NDEV,) + a.shape, a.dtype) for a in self.srcs]
            self.sems = _gather_sems()
        else:
            self.out_shape = [_land_shape(a) for a in self.srcs]
            self.sems = _scatter_sems(len(self.srcs))

    def copies(self, ins, outs, sems):
        if self.kind == "gather":
            return _gather_copies(ins[0], outs[0], sems)
        return _scatter_copies(ins, outs, sems)


def _call(kernel_body, *, name, grid, in_specs, out_specs, out_shape, args, scratch_shapes=(), ride=None):
    sem = ("arbitrary",) * len(grid)
    if ride is None:
        body = functools.partial(kernel_body)
        res = pl.pallas_call(body, name=name, grid=grid, in_specs=in_specs, out_specs=out_specs, out_shape=out_shape,
                             scratch_shapes=list(scratch_shapes), compiler_params=_params(*sem))(*args)
        return list(res), []
    n_in, n_out, n_sc, nr = len(in_specs), len(out_specs), len(scratch_shapes), len(ride.srcs)

    def body(*refs):
        ins, refs = refs[:n_in], refs[n_in:]
        rin, refs = refs[:nr], refs[nr:]
        outs, refs = refs[:n_out], refs[n_out:]
        rout, refs = refs[:nr], refs[nr:]
        scratch, rsems = refs[:n_sc], refs[n_sc:]
        first, last = True, True
        for axis, extent in enumerate(grid):
            first &= pl.program_id(axis) == 0
            last &= pl.program_id(axis) == extent - 1

        @pl.when(first)
        def _():
            for cp in ride.copies(rin, rout, rsems):
                cp.start()

        kernel_body(*ins, *outs, *scratch)

        @pl.when(last)
        def _():
            for cp in ride.copies(rin, rout, rsems):
                cp.wait()

    res = pl.pallas_call(body, name=name, grid=grid,
        in_specs=list(in_specs) + [ANY] * nr, out_specs=list(out_specs) + [ANY] * nr,
        out_shape=list(out_shape) + ride.out_shape,
        scratch_shapes=list(scratch_shapes) + ride.sems, compiler_params=_params(*sem),
    )(*args, *ride.srcs)
    return res[:n_out], res[n_out:]


def _ada_forward(c_all, ada_w, bias):
    nl, d, ncol = ada_w.shape

    def body(c_ref, w_ref, b_ref, o_ref):
        cv = c_ref[...]
        ca = cv * jax.nn.sigmoid(cv)
        o_ref[0] = jnp.dot(ca, w_ref[0], preferred_element_type=F32, precision=lax.Precision.HIGHEST) + b_ref[0]

    return pl.pallas_call(body, name="ada_forward", grid=(nl,),
        in_specs=[pl.BlockSpec((NDEV, d), lambda i: (0, 0)), pl.BlockSpec((1, d, ncol), lambda i: (i, 0, 0)),
                  pl.BlockSpec((1, 1, ncol), lambda i: (i, 0, 0))],
        out_specs=pl.BlockSpec((1, NDEV, ncol), lambda i: (i, 0, 0)),
        out_shape=jax.ShapeDtypeStruct((nl, NDEV, ncol), F32),
        compiler_params=_params("arbitrary"),
    )(c_all, ada_w, bias)


def _ada_wgrad(c_all, dmod):
    nl, _, ncol = dmod.shape
    d = c_all.shape[1]
    bd = min(d, 256)

    def body(c_ref, dm_ref, o_ref):
        cv = c_ref[...]
        ca = cv * jax.nn.sigmoid(cv)
        o_ref[0] = lax.dot_general(ca, dm_ref[0], (((0,), (0,)), ((), ())), preferred_element_type=F32,
                                   precision=lax.Precision.HIGHEST)

    return pl.pallas_call(body, name="ada_wgrad", grid=(nl, d // bd),
        in_specs=[pl.BlockSpec((NDEV, bd), lambda i, j: (0, j)), pl.BlockSpec((1, NDEV, ncol), lambda i, j: (i, 0, 0))],
        out_specs=pl.BlockSpec((1, bd, ncol), lambda i, j: (i, j, 0)),
        out_shape=jax.ShapeDtypeStruct((nl, d, ncol), F32),
        compiler_params=_params("arbitrary", "arbitrary"),
    )(c_all, dmod)


def _row_block(r, c, bytes_per_row_elem=4, budget=2 * 2**20):
    if r * c * bytes_per_row_elem <= budget or r % 8:
        return r
    best = 8
    for b in range(8, r + 1, 8):
        if r % b == 0 and b * c * bytes_per_row_elem <= budget:
            best = b
    return best


def _sum_slots(land, name):
    _, r, c = land.shape
    br = _row_block(r, c, 8 * land.dtype.itemsize)

    def body(l_ref, o_ref):
        acc = l_ref[0].astype(F32)
        for s in range(1, NDEV):
            acc = acc + l_ref[s].astype(F32)
        o_ref[...] = acc

    return pl.pallas_call(body, name=name, grid=(r // br,),
        in_specs=[pl.BlockSpec((NDEV, br, c), lambda i: (0, i, 0))],
        out_specs=pl.BlockSpec((br, c), lambda i: (i, 0)),
        out_shape=jax.ShapeDtypeStruct((r, c), F32),
        compiler_params=_params("arbitrary"),
    )(land)


def _adamw(w, g, m, v, name):
    r, c = w.shape
    br = _row_block(r, c)

    def body(w_ref, g_ref, m_ref, v_ref, d_ref, mo_ref, vo_ref):
        d_ref[...], mo_ref[...], vo_ref[...] = _adam_math(w_ref[...], g_ref[...], m_ref[...], v_ref[...])

    spec = pl.BlockSpec((br, c), lambda i: (i, 0))
    return pl.pallas_call(body, name=name, grid=(r // br,),
        in_specs=[spec] * 4, out_specs=[spec] * 3,
        out_shape=[jax.ShapeDtypeStruct((r, c), F32)] * 3,
        compiler_params=_params("arbitrary"),
    )(w, g, m, v)


def _adam_math(w, g, m, v):
    m2 = ADAM_B1 * m + (1.0 - ADAM_B1) * g
    v2 = ADAM_B2 * v + (1.0 - ADAM_B2) * (g * g)
    m_hat = m2 / (1.0 - ADAM_B1 ** ADAM_STEP)
    v_hat = v2 / (1.0 - ADAM_B2 ** ADAM_STEP)
    return -ADAM_LR * (m_hat / (jnp.sqrt(v_hat) + ADAM_EPS) + ADAM_WD * w), m2, v2


def _slot_sum(land_ref, *lead):
    acc = land_ref[(0,) + lead].astype(F32)
    for s in range(1, NDEV):
        acc = acc + land_ref[(s,) + lead].astype(F32)
    return acc


def _finalize(land, w, m, v, transposed, name):
    _, r, c = land.shape
    cb = 256 if (transposed and c % 256 == 0) else c
    wblk = pl.BlockSpec((cb, r), lambda i: (i, 0)) if transposed else pl.BlockSpec((r, cb), lambda i: (0, i))

    def body(l_ref, w_ref, m_ref, v_ref, g_ref, d_ref, mo_ref, vo_ref):
        g = _slot_sum(l_ref)
        g = g.T if transposed else g
        g_ref[...] = g
        d_ref[...], mo_ref[...], vo_ref[...] = _adam_math(w_ref[...], g, m_ref[...], v_ref[...])

    return pl.pallas_call(body, name=name, grid=(c // cb,),
        in_specs=[pl.BlockSpec((NDEV, r, cb), lambda i: (0, 0, i)), wblk, wblk, wblk], out_specs=[wblk] * 4,
        out_shape=[jax.ShapeDtypeStruct(w.shape, F32)] * 4,
        compiler_params=_params("arbitrary"),
    )(land, w, m, v)


def _finalize_ffn(land0, land1, wg, wu, wd, mg, mu, md, vg, vu, vd, ride):
    nl, d, fs = wg.shape
    db = min(256, d)

    def kernel_body(l0_ref, l1_ref, wg_ref, wu_ref, wd_ref, mg_ref, mu_ref, md_ref, vg_ref, vu_ref, vd_ref, *outs):
        layer = pl.program_id(0)
        triples = [(wg_ref, mg_ref, vg_ref, True), (wu_ref, mu_ref, vu_ref, True), (wd_ref, md_ref, vd_ref, False)]

        def run(land_ref):
            for j, (w_ref, m_ref, v_ref, transposed) in enumerate(triples):
                g = _slot_sum(land_ref, j)
                g = g.T if transposed else g
                delta, m2, v2 = _adam_math(w_ref[0], g, m_ref[0], v_ref[0])
                for o_ref, val in zip(outs[j::3], (g, delta, m2, v2)):
                    o_ref[0] = val

        @pl.when(layer == 0)
        def _():
            run(l0_ref)

        @pl.when(layer == 1)
        def _():
            run(l1_ref)

    tblk = pl.BlockSpec((1, db, fs), lambda l, i: (l, i, 0))
    pblk = pl.BlockSpec((1, fs, db), lambda l, i: (l, 0, i))
    lblk = [pl.BlockSpec((NDEV, 3, fs, db), lambda l, i: (0, 0, 0, i * (1 - l))),
            pl.BlockSpec((NDEV, 3, fs, db), lambda l, i: (0, 0, 0, i * l))]
    shapes = [jax.ShapeDtypeStruct(a.shape, F32) for a in (wg, wu, wd)]
    return _call(kernel_body, name="finalize_ffn", grid=(nl, d // db),
        in_specs=lblk + [tblk, tblk, pblk] * 3, out_specs=[tblk, tblk, pblk] * 4, out_shape=shapes * 4,
        args=(land0, land1, wg, wu, wd, mg, mu, md, vg, vu, vd), ride=ride)


CONV_ROWS = 64
LANES = 128


def _shifted_copies(buf, sh, n):
    sh[0] = buf[...]
    for r in range(1, 8):
        sh[r, pl.ds(0, n - 8), :] = buf[pl.ds(r, n - 8), :]


def _window(sh, o, rows, cols):
    return sh[o % 8, pl.ds(o - o % 8, rows), cols]


def _conv_in(x, mod, gmix, gw, w1_rows, w1_idx, b1, tm, ride):
    s, d = x.shape

    def kernel_body(x_ref, mod_ref, g_ref, w_ref, b_ref, h_ref, u_ref, glu_ref):
        n, _ = _rms(x_ref[...])
        h = (n * g_ref[...]) * (1.0 + mod_ref[1:2, :]) + mod_ref[0:1, :]
        hb = h.astype(BF16)
        h_ref[...] = hb
        u = _nt(hb, w_ref[...].reshape(NDEV * w1_rows, d)) + b_ref[...]
        u_ref[...] = u
        glu_ref[...] = u[:, :d] * jax.nn.sigmoid(u[:, d:])

    tile = pl.BlockSpec((tm, d), lambda i: (i, 0))
    return _call(kernel_body, name="conv_in", grid=(s // tm,),
        in_specs=[tile, _row(8, d), _row(1, d), _weight_spec(w1_rows, d, w1_idx), _row(1, 2 * d)],
        out_specs=[tile, pl.BlockSpec((tm, 2 * d), lambda i: (i, 0)), tile],
        out_shape=[jax.ShapeDtypeStruct((s, d), BF16), jax.ShapeDtypeStruct((s, 2 * d), F32),
                   jax.ShapeDtypeStruct((s, d), F32)],
        args=(x, mod, gmix, gw, b1), ride=ride)


def _conv_mid(glu, wdw, bdw, ln_g, ln_b, gw, w2_rows, w2_idx, b2, x, mod, tm, ride):
    s, d = x.shape
    off = CONV_HALO - (CONV_WIDTH - 1)
    rc = min(CONV_ROWS, tm)

    def kernel_body(glu_ref, halo_ref, wdw_ref, bdw_ref, lng_ref, lnb_ref, w_ref, b2_ref, x_ref, mod_ref,
                    dwc_ref, s_ref, y_ref, x1_ref, buf, sh):
        i = pl.program_id(0)
        buf[pl.ds(0, CONV_HALO), :] = jnp.where(i > 0, halo_ref[...], 0.0)
        buf[pl.ds(CONV_HALO, tm), :] = glu_ref[...]
        _shifted_copies(buf, sh, CONV_HALO + tm)
        for cb in range(d // LANES):
            cols = pl.ds(cb * LANES, LANES)
            taps = wdw_ref[:, cols]
            for r in range(tm // rc):
                part = jnp.zeros((rc, LANES), F32) + bdw_ref[:, cols]
                for k in range(CONV_WIDTH):
                    part = part + _window(sh, r * rc + off + k, rc, cols) * taps[k:k + 1, :]
                dwc_ref[pl.ds(r * rc, rc), cols] = part
        acc = dwc_ref[...]
        mu = jnp.mean(acc, axis=-1, keepdims=True)
        xc = acc - mu
        rstd = lax.rsqrt(jnp.mean(xc * xc, axis=-1, keepdims=True) + EPS)
        ln = (xc * rstd) * lng_ref[...] + lnb_ref[...]
        sb = (ln * jax.nn.sigmoid(ln)).astype(BF16)
        s_ref[...] = sb
        y = _nn(sb, w_ref[...].reshape(NDEV * w2_rows, d)) + b2_ref[...]
        y_ref[...] = y.astype(BF16)
        x1_ref[...] = x_ref[...] + (1.0 + mod_ref[2:3, :]) * y

    tile = pl.BlockSpec((tm, d), lambda i: (i, 0))
    halo = pl.BlockSpec((CONV_HALO, d), lambda i: (jnp.maximum(i * (tm // CONV_HALO) - 1, 0), 0))
    return _call(kernel_body, name="conv_mid", grid=(s // tm,),
        in_specs=[tile, halo, _row(CONV_WIDTH, d), _row(1, d), _row(1, d), _row(1, d),
                  _weight_spec(w2_rows, d, w2_idx), _row(1, d), tile, _row(8, d)],
        out_specs=[tile, tile, tile, tile],
        out_shape=[jax.ShapeDtypeStruct((s, d), F32), jax.ShapeDtypeStruct((s, d), BF16),
                   jax.ShapeDtypeStruct((s, d), BF16), jax.ShapeDtypeStruct((s, d), F32)],
        scratch_shapes=[pltpu.VMEM((CONV_HALO + tm, d), F32), pltpu.VMEM((8, CONV_HALO + tm, d), F32)],
        args=(glu, glu, wdw, bdw, ln_g, ln_b, gw, b2, x, mod), ride=ride)


def _ffn_fwd(x, mod, gffn, weights, fs, f, tm, name, ride=None):
    s, d = x.shape

    def kernel_body(x_ref, mod_ref, g_ref, wg_ref, wu_ref, wd_ref, h_ref, gg_ref, uu_ref, y_ref, xo_ref):
        xv = x_ref[...]
        n, _ = _rms(xv)
        hb = ((n * g_ref[...]) * (1.0 + mod_ref[4:5, :]) + mod_ref[3:4, :]).astype(BF16)
        h_ref[...] = hb
        gg = _nt(hb, wg_ref[...].reshape(f, d))
        uu = _nt(hb, wu_ref[...].reshape(f, d))
        gg_ref[...] = gg.astype(BF16)
        uu_ref[...] = uu.astype(BF16)
        ab = ((gg * jax.nn.sigmoid(gg)) * uu).astype(BF16)
        y = _nn(ab, wd_ref[...].reshape(f, d))
        y_ref[...] = y.astype(BF16)
        xo_ref[...] = xv + (1.0 + mod_ref[5:6, :]) * y

    tile = pl.BlockSpec((tm, d), lambda i: (i, 0))
    wide = pl.BlockSpec((tm, f), lambda i: (i, 0))
    return _call(kernel_body, name=name, grid=(s // tm,),
        in_specs=[tile, _row(8, d), _row(1, d)] + [_weight_spec(fs, d, idx) for _, idx in weights],
        out_specs=[tile, wide, wide, tile, tile],
        out_shape=[jax.ShapeDtypeStruct((s, d), BF16), jax.ShapeDtypeStruct((s, f), BF16),
                   jax.ShapeDtypeStruct((s, f), BF16), jax.ShapeDtypeStruct((s, d), BF16),
                   jax.ShapeDtypeStruct((s, d), F32)],
        args=(x, mod, gffn) + tuple(a for a, _ in weights), ride=ride)


def _pool_fwd(x, mod, gmix, pw, ls, tm, ride):
    s, d = x.shape
    dg = d // len(POOL_WINDOWS)

    def body(x_ref, halo_ref, mod_ref, g_ref, pw_ref, ls_ref, mixed_ref, yp_ref, xo_ref, buf):
        i = pl.program_id(0)

        def hfun(xv):
            n, _ = _rms(xv)
            return (n * g_ref[...]) * (1.0 + mod_ref[1:2, :]) + mod_ref[0:1, :]

        xv = x_ref[...]
        h = hfun(xv)
        buf[pl.ds(0, POOL_HALO), :] = jnp.where(i > 0, hfun(halo_ref[...]), 0.0)
        buf[pl.ds(POOL_HALO, tm), :] = h
        t = i * tm + lax.broadcasted_iota(jnp.int32, (tm, 1), 0)
        gate = 1.0 + mod_ref[2:3, :]
        for gi, w in enumerate(POOL_WINDOWS):
            cols = pl.ds(gi * dg, dg)
            ws = buf[pl.ds(POOL_HALO, tm), cols]
            for j in range(1, w):
                ws = ws + buf[pl.ds(POOL_HALO - j, tm), cols]
            inv = 1.0 / jnp.minimum(t + 1, w).astype(F32)
            mb = (ws * inv - h[:, gi * dg:(gi + 1) * dg]).astype(BF16)
            mixed_ref[:, cols] = mb
            yp = _nn(mb, pw_ref[gi])
            yp_ref[:, cols] = yp.astype(BF16)
            xo_ref[:, cols] = xv[:, gi * dg:(gi + 1) * dg] + gate[:, gi * dg:(gi + 1) * dg] * (yp * ls_ref[:, cols])

    tile = pl.BlockSpec((tm, d), lambda i: (i, 0))
    halo = pl.BlockSpec((POOL_HALO, d), lambda i: (jnp.maximum(i * (tm // POOL_HALO) - 1, 0), 0))
    return _call(body, name="pool_fwd", grid=(s // tm,),
        in_specs=[tile, halo, _row(8, d), _row(1, d), pl.BlockSpec((len(POOL_WINDOWS), dg, dg), lambda i: (0, 0, 0)),
                  _row(1, d)],
        out_specs=[tile, tile, tile],
        out_shape=[jax.ShapeDtypeStruct((s, d), BF16), jax.ShapeDtypeStruct((s, d), BF16),
                   jax.ShapeDtypeStruct((s, d), F32)],
        scratch_shapes=[pltpu.VMEM((POOL_HALO + tm, d), F32)],
        args=(x, x, mod, gmix, pw, ls), ride=ride)


def _loss_head(x, target, gfin, tm):
    s, d = x.shape
    last = s // tm - 1

    def body(x_ref, t_ref, g_ref, dx_ref, st_ref):
        i = pl.program_id(0)

        @pl.when(i == 0)
        def _():
            st_ref[...] = jnp.zeros_like(st_ref)

        n, rinv = _rms(x_ref[...])
        g = g_ref[...]
        err = n * g - t_ref[...]
        dy = err * (1.0 / d)
        st_ref[0:1, :] += _sum0(dy * n)
        st_ref[1:2, :] += _sum0(err * err) * (0.5 / d)
        dn = dy * g
        dx_ref[...] = rinv * (dn - n * jnp.mean(dn * n, axis=-1, keepdims=True))

        @pl.when(i == last)
        def _():
            st_ref[2:3, :] = jnp.zeros((1, d), F32) + jnp.sum(st_ref[1:2, :], axis=-1, keepdims=True)

    tile = pl.BlockSpec((tm, d), lambda i: (i, 0))
    return pl.pallas_call(body, name="loss_head", grid=(s // tm,),
        in_specs=[tile, tile, _row(1, d)],
        out_specs=[tile, _row(8, d)],
        out_shape=[jax.ShapeDtypeStruct((s, d), F32), jax.ShapeDtypeStruct((8, d), F32)],
        compiler_params=_params("arbitrary"),
    )(x, target, gfin)


def _ffn_bwd(dxo, x, gg, uu, y, mod, gffn, weights, fs, f, tm, name, ride=None):
    s, d = x.shape

    def kernel_body(dxo_ref, x_ref, gg_ref, uu_ref, y_ref, mod_ref, g_ref, wg_ref, wu_ref, wd_ref,
                    dg_ref, du_ref, a_ref, dy_ref, dxi_ref, st_ref):
        @pl.when(pl.program_id(0) == 0)
        def _():
            st_ref[...] = jnp.zeros_like(st_ref)

        dxo_v = dxo_ref[...]
        dyb = (dxo_v * (1.0 + mod_ref[5:6, :])).astype(BF16)
        dy_ref[...] = dyb
        da = _nt(dyb, wd_ref[...].reshape(f, d))
        ggv, uuv = gg_ref[...].astype(F32), uu_ref[...].astype(F32)
        sg = jax.nn.sigmoid(ggv)
        silu = ggv * sg
        a_ref[...] = (silu * uuv).astype(BF16)
        dub = (da * silu).astype(BF16)
        dgb = (da * uuv * _silu_grad(ggv, sg)).astype(BF16)
        du_ref[...] = dub
        dg_ref[...] = dgb
        dh = _nn(dgb, wg_ref[...].reshape(f, d)) + _nn(dub, wu_ref[...].reshape(f, d))
        n, rinv = _rms(x_ref[...])
        dx, dsh, dsc, dgain = _rms_mod_bwd(dh, n, rinv, g_ref[...], mod_ref[4:5, :])
        dxi_ref[...] = dxo_v + dx
        st_ref[0:1, :] += dsh
        st_ref[1:2, :] += dsc
        st_ref[2:3, :] += _sum0(dxo_v * y_ref[...].astype(F32))
        st_ref[3:4, :] += dgain

    tile = pl.BlockSpec((tm, d), lambda i: (i, 0))
    wide = pl.BlockSpec((tm, f), lambda i: (i, 0))
    return _call(kernel_body, name=name, grid=(s // tm,),
        in_specs=[tile, tile, wide, wide, tile, _row(8, d), _row(1, d)] + [_weight_spec(fs, d, idx) for _, idx in weights],
        out_specs=[wide, wide, wide, tile, tile, _row(8, d)],
        out_shape=[jax.ShapeDtypeStruct((s, f), BF16)] * 3 + [jax.ShapeDtypeStruct((s, d), BF16),
                   jax.ShapeDtypeStruct((s, d), F32), jax.ShapeDtypeStruct((8, d), F32)],
        args=(dxo, x, gg, uu, y, mod, gffn) + tuple(a for a, _ in weights), ride=ride)


def _ffn_wgrad(dgb, dub, ab, h, dyb, fb, ts, name):
    s, f = dgb.shape
    d = h.shape[1]
    last = s // ts - 1

    def body(dg_ref, du_ref, a_ref, h_ref, dy_ref, o_ref, acc):
        t = pl.program_id(1)

        @pl.when(t == 0)
        def _():
            acc[...] = jnp.zeros_like(acc)

        hv = h_ref[...]
        acc[0] += _tn(dg_ref[...], hv)
        acc[1] += _tn(du_ref[...], hv)
        acc[2] += _tn(a_ref[...], dy_ref[...])

        @pl.when(t == last)
        def _():
            o_ref[...] = acc[...].astype(BF16)

    wide = pl.BlockSpec((ts, fb), lambda j, t: (t, j))
    tile = pl.BlockSpec((ts, d), lambda j, t: (t, 0))
    return pl.pallas_call(body, name=name, grid=(f // fb, s // ts),
        in_specs=[wide, wide, wide, tile, tile],
        out_specs=pl.BlockSpec((3, fb, d), lambda j, t: (0, j, 0)),
        out_shape=jax.ShapeDtypeStruct((3, f, d), BF16),
        scratch_shapes=[pltpu.VMEM((3, fb, d), F32)],
        compiler_params=_params("arbitrary", "arbitrary"),
    )(dgb, dub, ab, h, dyb)


def _pool_bwd(dxo, x, yp, mixed, mod, gmix, pw, ls, tm):
    s, d = x.shape
    ng = len(POOL_WINDOWS)
    dg = d // ng
    last = s // tm - 1

    def body(dxo_ref, dxh_ref, x_ref, yp_ref, mixed_ref, mod_ref, g_ref, pw_ref, ls_ref,
             dxi_ref, dpw_ref, st_ref, bufy, bufq, bufh, acc):
        i = pl.program_id(0)

        @pl.when(i == 0)
        def _():
            st_ref[...] = jnp.zeros_like(st_ref)
            acc[...] = jnp.zeros_like(acc)

        gate = 1.0 + mod_ref[2:3, :]
        lsv = ls_ref[...]
        dxo_v = dxo_ref[...]
        st_ref[2:3, :] += _sum0(dxo_v * yp_ref[...].astype(F32))
        bufy[pl.ds(0, tm), :] = (dxo_v * (gate * lsv)).astype(BF16)
        bufy[pl.ds(tm, POOL_HALO), :] = jnp.where(i < last, dxh_ref[...] * (gate * lsv), 0.0).astype(BF16)
        t = i * tm + lax.broadcasted_iota(jnp.int32, (tm + POOL_HALO, 1), 0)
        for gi, w in enumerate(POOL_WINDOWS):
            cols = pl.ds(gi * dg, dg)
            dm = _nt(bufy[:, cols], pw_ref[gi])
            bufq[:, cols] = dm * (1.0 / jnp.minimum(t + 1, w).astype(F32))
            dh = bufq[pl.ds(0, tm), cols] - dm[0:tm, :]
            for j in range(1, w):
                dh = dh + bufq[pl.ds(j, tm), cols]
            bufh[:, cols] = dh
            acc[gi] += _tn(mixed_ref[:, cols], bufy[pl.ds(0, tm), cols])
        n, rinv = _rms(x_ref[...])
        dx, dsh, dsc, dgain = _rms_mod_bwd(bufh[...], n, rinv, g_ref[...], mod_ref[1:2, :])
        dxi_ref[...] = dxo_v + dx
        st_ref[0:1, :] += dsh
        st_ref[1:2, :] += dsc
        st_ref[3:4, :] += dgain

        @pl.when(i == last)
        def _():
            r = st_ref[2:3, :]
            st_ref[4:5, :] = r * lsv
            st_ref[5:6, :] = r * gate
            dpw_ref[...] = acc[...].astype(BF16)

    tile = pl.BlockSpec((tm, d), lambda i: (i, 0))
    nxt = pl.BlockSpec((POOL_HALO, d), lambda i: (jnp.minimum((i + 1) * (tm // POOL_HALO), s // POOL_HALO - 1), 0))
    pws = pl.BlockSpec((ng, dg, dg), lambda i: (0, 0, 0))
    return pl.pallas_call(body, name="pool_bwd", grid=(s // tm,),
        in_specs=[tile, nxt, tile, tile, tile, _row(8, d), _row(1, d), pws, _row(1, d)],
        out_specs=[tile, pws, _row(8, d)],
        out_shape=[jax.ShapeDtypeStruct((s, d), F32), jax.ShapeDtypeStruct((ng, dg, dg), BF16),
                   jax.ShapeDtypeStruct((8, d), F32)],
        scratch_shapes=[pltpu.VMEM((tm + POOL_HALO, d), BF16), pltpu.VMEM((tm + POOL_HALO, d), F32),
                        pltpu.VMEM((tm, d), F32), pltpu.VMEM((ng, dg, dg), F32)],
        compiler_params=_params("arbitrary"),
    )(dxo, dxo, x, yp, mixed, mod, gmix, pw, ls)


def _conv_bwd_mid(dxo, y, dwc, sb, mod, ln_g, ln_b, gw, w2_rows, w2_idx, tm):
    s, d = dwc.shape
    last = s // tm - 1

    def body(dxo_ref, y_ref, dwc_ref, s_ref, mod_ref, lng_ref, lnb_ref, w_ref, dd_ref, dw_ref, st_ref, acc):
        i = pl.program_id(0)

        @pl.when(i == 0)
        def _():
            st_ref[...] = jnp.zeros_like(st_ref)
            acc[...] = jnp.zeros_like(acc)

        dxo_v = dxo_ref[...]
        st_ref[0:1, :] += _sum0(dxo_v * y_ref[...].astype(F32))
        dy = dxo_v * (1.0 + mod_ref[2:3, :])
        st_ref[1:2, :] += _sum0(dy)
        dyb = dy.astype(BF16)
        ds = _nt(dyb, w_ref[...].reshape(NDEV * w2_rows, d))
        acc[...] += _tn(s_ref[...], dyb)
        v = dwc_ref[...]
        mu = jnp.mean(v, axis=-1, keepdims=True)
        xc = v - mu
        rstd = lax.rsqrt(jnp.mean(xc * xc, axis=-1, keepdims=True) + EPS)
        xhat = xc * rstd
        ln = xhat * lng_ref[...] + lnb_ref[...]
        dln = ds * _silu_grad(ln, jax.nn.sigmoid(ln))
        st_ref[2:3, :] += _sum0(dln * xhat)
        st_ref[3:4, :] += _sum0(dln)
        dxh = dln * lng_ref[...]
        dd = rstd * (dxh - jnp.mean(dxh, axis=-1, keepdims=True) - xhat * jnp.mean(dxh * xhat, axis=-1, keepdims=True))
        dd_ref[...] = dd
        st_ref[4:5, :] += _sum0(dd)

        @pl.when(i == last)
        def _():
            dw_ref[...] = acc[...].astype(BF16)

    tile = pl.BlockSpec((tm, d), lambda i: (i, 0))
    return pl.pallas_call(body, name="conv_bwd_mid", grid=(s // tm,),
        in_specs=[tile, tile, tile, tile, _row(8, d), _row(1, d), _row(1, d), _weight_spec(w2_rows, d, w2_idx)],
        out_specs=[tile, pl.BlockSpec((d, d), lambda i: (0, 0)), _row(8, d)],
        out_shape=[jax.ShapeDtypeStruct((s, d), F32), jax.ShapeDtypeStruct((d, d), BF16),
                   jax.ShapeDtypeStruct((8, d), F32)],
        scratch_shapes=[pltpu.VMEM((d, d), F32)],
        compiler_params=_params("arbitrary"),
    )(dxo, y, dwc, sb, mod, ln_g, ln_b, gw)


def _conv_bwd_in(dd, glu, u, hb, x, dxo, wdw, gw, w1_rows, w1_idx, mod, gmix, tm, ride):
    s, d = x.shape
    last = s // tm - 1
    off = CONV_HALO - (CONV_WIDTH - 1)
    rw = 32
    tap_group = 16

    def kernel_body(dd_ref, ddn_ref, glu_ref, glp_ref, u_ref, h_ref, x_ref, dxo_ref, wdw_ref, w_ref, mod_ref, g_ref,
                    dxi_ref, dw_ref, dwdw_ref, st_ref, bufd, bufg, shd, shg, dgl, accw, acc):
        i = pl.program_id(0)

        @pl.when(i == 0)
        def _():
            st_ref[...] = jnp.zeros_like(st_ref)
            accw[...] = jnp.zeros_like(accw)
            acc[...] = jnp.zeros_like(acc)

        bufd[pl.ds(0, tm), :] = dd_ref[...]
        bufd[pl.ds(tm, CONV_HALO), :] = jnp.where(i < last, ddn_ref[...], 0.0)
        bufg[pl.ds(0, CONV_HALO), :] = jnp.where(i > 0, glp_ref[...], 0.0)
        bufg[pl.ds(CONV_HALO, tm), :] = glu_ref[...]
        _shifted_copies(bufd, shd, tm + CONV_HALO)
        _shifted_copies(bufg, shg, CONV_HALO + tm)
        for cb in range(d // LANES):
            cols = pl.ds(cb * LANES, LANES)
            taps = wdw_ref[:, cols]
            for r in range(tm // rw):
                part = jnp.zeros((rw, LANES), F32)
                for k in range(CONV_WIDTH):
                    part = part + _window(shd, r * rw + CONV_WIDTH - 1 - k, rw, cols) * taps[k:k + 1, :]
                dgl[pl.ds(r * rw, rw), cols] = part
            for k0 in range(0, CONV_WIDTH, tap_group):
                group = range(k0, min(CONV_WIDTH, k0 + tap_group))
                sums = {k: jnp.zeros((8, LANES), F32) for k in group}
                for r in range(tm // rw):
                    ddc = bufd[pl.ds(r * rw, rw), cols]
                    for k in group:
                        p = _window(shg, r * rw + off + k, rw, cols) * ddc
                        for q in range(rw // 8):
                            sums[k] = sums[k] + p[q * 8:(q + 1) * 8, :]
                for k in group:
                    accw[k, :, cols] += sums[k]
        dglu = dgl[...]
        uv = u_ref[...]
        a, g = uv[:, :d], uv[:, d:]
        sg = jax.nn.sigmoid(g)
        da = dglu * sg
        dgt = dglu * a * (sg * (1.0 - sg))
        du = jnp.concatenate([da, dgt], axis=1)
        st_ref[0:1, :] += _sum0(du)
        dub = du.astype(BF16)
        w = w_ref[...].reshape(NDEV * w1_rows, d)
        dh = _nn(dub, w)
        acc[...] += _tn(dub, h_ref[...])
        n, rinv = _rms(x_ref[...])
        dx, dsh, dsc, dgain = _rms_mod_bwd(dh, n, rinv, g_ref[...], mod_ref[1:2, :])
        dxi_ref[...] = dxo_ref[...] + dx
        st_ref[1:2, 0:d] += dsh
        st_ref[2:3, 0:d] += dsc
        st_ref[3:4, 0:d] += dgain

        @pl.when(i == last)
        def _():
            dw_ref[...] = acc[...].astype(BF16)
            dwdw_ref[...] = jnp.sum(accw[...], axis=1)

    tile = pl.BlockSpec((tm, d), lambda i: (i, 0))
    prv = pl.BlockSpec((CONV_HALO, d), lambda i: (jnp.maximum(i * (tm // CONV_HALO) - 1, 0), 0))
    nxt = pl.BlockSpec((CONV_HALO, d), lambda i: (jnp.minimum((i + 1) * (tm // CONV_HALO), s // CONV_HALO - 1), 0))
    return _call(kernel_body, name="conv_bwd_in", grid=(s // tm,),
        in_specs=[tile, nxt, tile, prv, pl.BlockSpec((tm, 2 * d), lambda i: (i, 0)), tile, tile, tile,
                  _row(CONV_WIDTH, d), _weight_spec(w1_rows, d, w1_idx), _row(8, d), _row(1, d)],
        out_specs=[tile, pl.BlockSpec((2 * d, d), lambda i: (0, 0)), _row(CONV_HALO, d), _row(8, 2 * d)],
        out_shape=[jax.ShapeDtypeStruct((s, d), F32), jax.ShapeDtypeStruct((2 * d, d), BF16),
                   jax.ShapeDtypeStruct((CONV_HALO, d), F32), jax.ShapeDtypeStruct((8, 2 * d), F32)],
        scratch_shapes=[pltpu.VMEM((tm + CONV_HALO, d), F32), pltpu.VMEM((CONV_HALO + tm, d), F32),
                        pltpu.VMEM((8, tm + CONV_HALO, d), F32), pltpu.VMEM((8, CONV_HALO + tm, d), F32),
                        pltpu.VMEM((tm, d), F32), pltpu.VMEM((CONV_HALO, 8, d), F32), pltpu.VMEM((2 * d, d), F32)],
        args=(dd, dd, glu, glu, u, hb, x, dxo, wdw, gw, mod, gmix), ride=ride)


def kernel(x, c, ada_w, ada_b, norm_mix_g, norm_ffn_g, conv_w1, conv_b1, conv_wdw, conv_bdw, conv_ln_g, conv_ln_b, conv_w2, conv_b2, pool_w, pool_ls, ffn_w_gate, ffn_w_up, ffn_w_down, final_g, loss_target, m_ada_w, m_ada_b, m_norm_mix_g, m_norm_ffn_g, m_conv_w1, m_conv_b1, m_conv_wdw, m_conv_bdw, m_conv_ln_g, m_conv_ln_b, m_conv_w2, m_conv_b2, m_pool_w, m_pool_ls, m_ffn_w_gate, m_ffn_w_up, m_ffn_w_down, m_final_g, v_ada_w, v_ada_b, v_norm_mix_g, v_norm_ffn_g, v_conv_w1, v_conv_b1, v_conv_wdw, v_conv_bdw, v_conv_ln_g, v_conv_ln_b, v_conv_w2, v_conv_b2, v_pool_w, v_pool_ls, v_ffn_w_gate, v_ffn_w_up, v_ffn_w_down, v_final_g):
    _, s, d = x.shape
    f = ffn_w_down.shape[1] * NDEV
    fs = f // NDEV
    r1, r2 = 2 * d // NDEV, d // NDEV
    ng = len(POOL_WINDOWS)
    dg = d // ng
    pr = ng * (dg // NDEV) * dg // d
    ncol = ada_w.shape[2]
    dc = d // NDEV
    tm = min(256, s)
    me = _my_index()
    x0 = x.reshape(s, d)
    target = loss_target.reshape(s, d)

    small = jnp.concatenate([c.reshape(NDEV, dc), conv_wdw[0], pool_ls], axis=0)
    shard_a = jnp.concatenate([conv_w1[0].T, conv_w2[0]], axis=0).astype(BF16)
    small_all, gwa = _allgather_first(small, shard_a)
    c_all = small_all[:, 0:NDEV, :].reshape(NDEV, d)
    wdw = small_all[:, NDEV:NDEV + CONV_WIDTH, :].transpose(1, 0, 2).reshape(CONV_WIDTH, d)
    ls = small_all[:, NDEV + CONV_WIDTH, :].reshape(1, d)
    bias = lax.dynamic_slice_in_dim(ada_b, me * ncol, ncol, axis=1)[:, None, :]
    mod_cols = _ada_forward(c_all, ada_w, bias)
    mod_all = _allgather_small(mod_cols.reshape(2 * NDEV, ncol), "allgather_mod")
    mod_mine = lax.dynamic_index_in_dim(mod_all.reshape(NDEV, 2, NDEV, ncol), me, axis=2, keepdims=False)
    mod = mod_mine.transpose(1, 0, 2).reshape(2, 6, d)
    mod = jnp.concatenate([mod, jnp.zeros((2, 2, d), F32)], axis=1)

    shard_b1 = ffn_w_gate[0].T.astype(BF16)
    shard_b2 = jnp.concatenate([ffn_w_up[0].T, ffn_w_down[0]], axis=0).astype(BF16)
    shard_c = jnp.concatenate([ffn_w_gate[1].T, ffn_w_up[1].T, pool_w.reshape(pr, d)], axis=0).astype(BF16)
    shard_d = ffn_w_down[1].astype(BF16)
    w1_at, w2_at = (gwa, 0), (gwa, r1 // r2)

    (h0, u, glu), (gwb1,) = _conv_in(x0, mod[0], norm_mix_g[0:1], w1_at[0], r1, w1_at[1], conv_b1, tm,
                                    _Ride("gather", [shard_b1]))
    (dwc, sb, y0, x1), (gwb2,) = _conv_mid(glu, wdw, conv_bdw, conv_ln_g, conv_ln_b, w2_at[0], r2, w2_at[1], conv_b2,
                                           x0, mod[0], tm, _Ride("gather", [shard_b2]))
    ffn0_w = [(gwb1, 0), (gwb2, 0), (gwb2, 1)]
    (h1, gg0, uu0, yf0, x2), (gwc,) = _ffn_fwd(x1, mod[0], norm_ffn_g[0:1], ffn0_w, fs, f, tm, "ffn_fwd0",
                                               _Ride("gather", [shard_c]))
    pw = gwc[:, 2 * fs:2 * fs + pr, :].reshape(NDEV, ng, dg // NDEV, dg).transpose(1, 0, 2, 3).reshape(ng, dg, dg)
    (mixed, yp, x3), (gwd,) = _pool_fwd(x2, mod[1], norm_mix_g[1:2], pw, ls, tm, _Ride("gather", [shard_d]))
    ffn1_w = [(gwc, 0), (gwc, 1), (gwd, 0)]
    (h3, gg1, uu1, yf1, x4), _ = _ffn_fwd(x3, mod[1], norm_ffn_g[1:2], ffn1_w, fs, f, tm, "ffn_fwd1")
    dx4, st_loss = _loss_head(x4, target, final_g.reshape(1, d), tm)
    loss = lax.psum(st_loss[2, 0], ("x", "y", "c"))

    fb = f // 2 if (f // 2) % 128 == 0 else f
    ts = min(512, s)
    (dgb, dub, ab, dyb, dx3, st_f1), _ = _ffn_bwd(dx4, x3, gg1, uu1, yf1, mod[1], norm_ffn_g[1:2], ffn1_w, fs, f, tm,
                                                  "ffn_bwd1")
    gf1 = _ffn_wgrad(dgb, dub, ab, h3, dyb, fb, ts, "ffn_wgrad1")
    dx2, gpw, st_p = _pool_bwd(dx3, x2, yp, mixed, mod[1], norm_mix_g[1:2], pw, ls, tm)
    (dgb, dub, ab, dyb, dx1, st_f0), (land_f1,) = _ffn_bwd(dx2, x1, gg0, uu0, yf0, mod[0], norm_ffn_g[0:1], ffn0_w, fs, f,
                                                           tm, "ffn_bwd0", _Ride("scatter", [gf1]))
    gf0 = _ffn_wgrad(dgb, dub, ab, h1, dyb, fb, ts, "ffn_wgrad0")
    dd, gw2, st_m = _conv_bwd_mid(dx1, y0, dwc, sb, mod[0], conv_ln_g, conv_ln_b, w2_at[0], r2, w2_at[1], tm)
    (dx0, gw1, gwdw, st_c), (land_f0, land_pw, land_w2) = _conv_bwd_in(
        dd, glu, u, h0, x0, dx1, wdw, w1_at[0], r1, w1_at[1], mod[0], norm_mix_g[0:1], tm,
        _Ride("scatter", [gf0, gpw, gw2[None]]))

    ffn_out, (land_w1,) = _finalize_ffn(land_f0, land_f1, ffn_w_gate, ffn_w_up, ffn_w_down,
                                        m_ffn_w_gate, m_ffn_w_up, m_ffn_w_down, v_ffn_w_gate, v_ffn_w_up, v_ffn_w_down,
                                        _Ride("scatter", [gw1[None]]))
    fin_w1 = _finalize(land_w1.reshape(NDEV, r1, d), conv_w1[0], m_conv_w1[0], v_conv_w1[0], True, "finalize_w1")
    fin_w2 = _finalize(land_w2.reshape(NDEV, r2, d), conv_w2[0], m_conv_w2[0], v_conv_w2[0], False, "finalize_w2")
    pshape = (ng * (dg // NDEV), dg)
    fin_pw = _finalize(land_pw.reshape((NDEV,) + pshape), pool_w.reshape(pshape), m_pool_w.reshape(pshape),
                       v_pool_w.reshape(pshape), False, "finalize_pool_w")

    zrow = jnp.zeros((1, d), F32)
    prow = jnp.concatenate([
        st_c[1:3, 0:d], st_m[0:1], st_f0[0:3], st_p[0:2], st_p[4:5], st_f1[0:3],
        st_c[3:4, 0:d], st_p[3:4], st_f0[3:4], st_f1[3:4],
        st_c[0:1, 0:d], st_c[0:1, d:2 * d], st_m[4:5], st_m[2:4], st_m[1:2], st_loss[0:1],
        gwdw[0:CONV_WIDTH], st_p[5:6], zrow], axis=0)
    p_all = _allgather_small(prow, "allgather_stats")
    psum = _sum_slots(p_all, "sum_stats")
    dmod_all = p_all[:, 0:12, :].reshape(NDEV, 2, 6 * d)
    dmod_cols = lax.dynamic_slice_in_dim(dmod_all, me * ncol, ncol, axis=2).transpose(1, 0, 2)
    g_ada_w = _ada_wgrad(c_all, dmod_cols)

    def adam(w, g, m, v, name):
        shp = w.shape
        w2d = (-1, shp[-1])
        dl, mo, vo = _adamw(w.reshape(w2d), g.reshape(w2d), m.reshape(w2d), v.reshape(w2d), name)
        return dl.reshape(shp), mo.reshape(shp), vo.reshape(shp)

    rep_names = ["ada_b", "norm_mix_g", "norm_ffn_g", "conv_b1", "conv_bdw", "conv_ln_g", "conv_ln_b", "conv_b2", "final_g"]
    rep_w = [ada_b, norm_mix_g, norm_ffn_g, conv_b1, conv_bdw, conv_ln_g, conv_ln_b, conv_b2, final_g]
    rep_m = [m_ada_b, m_norm_mix_g, m_norm_ffn_g, m_conv_b1, m_conv_bdw, m_conv_ln_g, m_conv_ln_b, m_conv_b2, m_final_g]
    rep_v = [v_ada_b, v_norm_mix_g, v_norm_ffn_g, v_conv_b1, v_conv_bdw, v_conv_ln_g, v_conv_ln_b, v_conv_b2, v_final_g]
    nrep = sum(w.size for w in rep_w) // d
    pad = jnp.zeros(((-nrep) % 8, d), F32)

    def pack(arrs, fill):
        return jnp.concatenate([a.reshape(-1, d) for a in arrs] + [pad + fill], axis=0)

    rep_g = jnp.concatenate([psum[0:nrep], pad], axis=0)
    rep_d, rep_mo, rep_vo = _adamw(pack(rep_w, 0.0), rep_g, pack(rep_m, 0.0), pack(rep_v, 1.0), "adamw_replicated")

    def unpack(packed):
        out, cur = [], 0
        for w in rep_w:
            k = w.size // d
            out.append(packed[cur:cur + k].reshape(w.shape))
            cur += k
        return out

    rep = dict(zip(rep_names, zip(unpack(psum), unpack(rep_d), unpack(rep_mo), unpack(rep_vo))))

    g_wdw_full = psum[nrep:nrep + CONV_WIDTH]
    g_wdw = lax.dynamic_slice_in_dim(g_wdw_full, me * dc, dc, axis=1)
    g_ls = lax.dynamic_slice_in_dim(psum[nrep + CONV_WIDTH:nrep + CONV_WIDTH + 1], me * dc, dc, axis=1)
    tiny = lambda a, b: jnp.concatenate([a.reshape(CONV_WIDTH, dc), b.reshape(1, dc)], axis=0)
    t_d, t_m, t_v = _adamw(tiny(conv_wdw, pool_ls), tiny(g_wdw, g_ls), tiny(m_conv_wdw, m_pool_ls),
                           tiny(v_conv_wdw, v_pool_ls), "adamw_taps")

    def taps(a):
        return a[0:CONV_WIDTH][None], a[CONV_WIDTH:CONV_WIDTH + 1]

    sharded = {
        "ada_w": (g_ada_w,) + adam(ada_w, g_ada_w, m_ada_w, v_ada_w, "adamw_ada_w"),
        "conv_w1": tuple(a[None] for a in fin_w1),
        "conv_w2": tuple(a[None] for a in fin_w2),
        "pool_w": tuple(a.reshape(pool_w.shape) for a in fin_pw),
        "ffn_w_gate": tuple(ffn_out[0::3]),
        "ffn_w_up": tuple(ffn_out[1::3]),
        "ffn_w_down": tuple(ffn_out[2::3]),
        "conv_wdw": (g_wdw[None], taps(t_d)[0], taps(t_m)[0], taps(t_v)[0]),
        "pool_ls": (g_ls, taps(t_d)[1], taps(t_m)[1], taps(t_v)[1]),
    }
    every = {**rep, **sharded}
    order = ["ada_w", "ada_b", "norm_mix_g", "norm_ffn_g", "conv_w1", "conv_b1", "conv_wdw", "conv_bdw", "conv_ln_g",
             "conv_ln_b", "conv_w2", "conv_b2", "pool_w", "pool_ls", "ffn_w_gate", "ffn_w_up", "ffn_w_down", "final_g"]
    grads = [every[n][0] for n in order]
    deltas = [every[n][1] for n in order]
    new_m = [every[n][2] for n in order]
    new_v = [every[n][3] for n in order]
    return (loss, dx0.reshape(1, s, d), *grads, *deltas, *new_m, *new_v)
```

```python
import functools

import jax
import jax.numpy as jnp
from jax import lax
from jax.experimental import pallas as pl
from jax.experimental.pallas import tpu as pltpu

NDEV = 8
EPS = 1e-6
CONV_WIDTH = 31
POOL_WINDOWS = (2, 4, 8, 16)
CONV_HALO = 32
POOL_HALO = 16
ADAM_LR = 0.001
ADAM_B1 = 0.9
ADAM_B2 = 0.999
ADAM_EPS = 1e-08
ADAM_WD = 0.01
ADAM_STEP = 10
VMEM_LIMIT = 56 * 2**20
MESH = pl.DeviceIdType.MESH
F32 = jnp.float32
BF16 = jnp.bfloat16


def _nt(a, b):
    return lax.dot_general(a, b, (((1,), (1,)), ((), ())), preferred_element_type=F32)


def _nn(a, b):
    return lax.dot_general(a, b, (((1,), (0,)), ((), ())), preferred_element_type=F32)


def _tn(a, b):
    return lax.dot_general(a, b, (((0,), (0,)), ((), ())), preferred_element_type=F32)


def _sum0(v):
    return jnp.sum(v, axis=0, keepdims=True)


def _rms(x):
    rinv = lax.rsqrt(jnp.mean(x * x, axis=-1, keepdims=True) + EPS)
    return x * rinv, rinv


def _rms_mod_bwd(dh, n, rinv, g, sc):
    dhs = dh * (1.0 + sc)
    dn = dhs * g
    dx = rinv * (dn - n * jnp.mean(dn * n, axis=-1, keepdims=True))
    return dx, _sum0(dh), _sum0(dh * (n * g)), _sum0(dhs * n)


def _silu_grad(z, sg):
    return sg * (1.0 + z * (1.0 - sg))


def _params(*sem):
    return pltpu.CompilerParams(dimension_semantics=sem, vmem_limit_bytes=VMEM_LIMIT)


def _row(i, d):
    return pl.BlockSpec((i, d), lambda *_: (0, 0))


def _weight_spec(rows, d, idx):
    return pl.BlockSpec((NDEV, rows, d), lambda *_: (0, idx, 0), pipeline_mode=pl.Buffered(1))


def _my_index():
    return 4 * lax.axis_index("x") + 2 * lax.axis_index("y") + lax.axis_index("c")


def _peer(k):
    x, y, c = lax.axis_index("x"), lax.axis_index("y"), lax.axis_index("c")
    px = 1 - x if k & 4 else x
    py = 1 - y if k & 2 else y
    pc = 1 - c if k & 1 else c
    return (px, py, pc), 4 * px + 2 * py + pc


def _allgather_small(v, name):
    r, c = v.shape

    def body(v_ref, out_ref, send_sems, recv_sems):
        me = _my_index()
        out_ref[me] = v_ref[...]
        copies = []
        for k in range(1, NDEV):
            dev, _ = _peer(k)
            copies.append(pltpu.make_async_remote_copy(
                src_ref=v_ref, dst_ref=out_ref.at[me], send_sem=send_sems.at[k - 1], recv_sem=recv_sems.at[k - 1],
                device_id=dev, device_id_type=MESH))
        for cp in copies:
            cp.start()
        for cp in copies:
            cp.wait()

    return pl.pallas_call(body, name=name,
        out_shape=jax.ShapeDtypeStruct((NDEV, r, c), v.dtype),
        in_specs=[pl.BlockSpec(memory_space=pltpu.VMEM)],
        out_specs=pl.BlockSpec(memory_space=pltpu.VMEM),
        scratch_shapes=[pltpu.SemaphoreType.DMA((NDEV - 1,)), pltpu.SemaphoreType.DMA((NDEV - 1,))],
    )(v)


def _gather_sems():
    return [pltpu.SemaphoreType.DMA((NDEV - 1,)), pltpu.SemaphoreType.DMA((NDEV - 1,)), pltpu.SemaphoreType.DMA((1,))]


class _Gather:
    def __init__(self, srcs, dsts, sems):
        self.src, self.dst = srcs[0], dsts[0]
        self.send, self.recv, self.local = sems
        x, y, c = lax.axis_index("x"), lax.axis_index("y"), lax.axis_index("c")
        self.me, self.sibling, self.core = (x, y, c), (x, y, 1 - c), c
        self.chips = [(1 - x, y), (x, 1 - y), (1 - x, 1 - y)]

    def _copy(self, k, block, to, from_input=False):
        slot = self.dst.at[4 * block[0] + 2 * block[1] + block[2]]
        return pltpu.make_async_remote_copy(
            src_ref=self.src if from_input else slot, dst_ref=slot, send_sem=self.send.at[k], recv_sem=self.recv.at[k],
            device_id=to, device_id_type=MESH)

    def _own(self):
        return pltpu.make_async_copy(self.src, self.dst.at[_my_index()], self.local.at[0])

    def _first(self):
        return [self._copy(0, self.me, self.sibling, True)] + [
            self._copy(1 + j, self.me, (*chip, self.core), True) for j, chip in enumerate(self.chips)]

    def start(self):
        self._own().start()
        for cp in self._first():
            cp.start()

    def forward(self):
        for j, chip in enumerate(self.chips):
            self._copy(1 + j, (*chip, self.core), self.me).wait_recv()
            self._copy(4 + j, (*chip, self.core), self.sibling).start()

    def finish(self):
        self._copy(0, self.sibling, self.me).wait_recv()
        for j, chip in enumerate(self.chips):
            self._copy(4 + j, (*chip, 1 - self.core), self.me).wait_recv()
        for cp in self._first():
            cp.wait_send()
        for j, chip in enumerate(self.chips):
            self._copy(4 + j, (*chip, self.core), self.sibling).wait_send()
        self._own().wait()


def _scatter_sems(n):
    return [pltpu.SemaphoreType.DMA((7 * n,)), pltpu.SemaphoreType.DMA((7 * n,)), pltpu.SemaphoreType.DMA((n,))]


class _Scatter:
    def __init__(self, srcs, dsts, sems):
        send_sems, recv_sems, local_sems = sems
        me = _my_index()
        self.copies = []
        for a, (src, dst) in enumerate(zip(srcs, dsts)):
            r = dst.shape[2]
            self.copies.append(pltpu.make_async_copy(src.at[:, pl.ds(me * r, r), :], dst.at[me], local_sems.at[a]))
            for k in range(1, NDEV):
                dev, p = _peer(k)
                self.copies.append(pltpu.make_async_remote_copy(
                    src_ref=src.at[:, pl.ds(p * r, r), :], dst_ref=dst.at[me],
                    send_sem=send_sems.at[a * 7 + k - 1], recv_sem=recv_sems.at[a * 7 + k - 1],
                    device_id=dev, device_id_type=MESH))

    def start(self):
        for cp in self.copies:
            cp.start()

    def forward(self):
        pass

    def finish(self):
        for cp in self.copies:
            cp.wait()


def _land_shape(part):
    a, r, c = part.shape
    return jax.ShapeDtypeStruct((NDEV, a, r // NDEV, c), part.dtype)


ANY = pl.BlockSpec(memory_space=pl.ANY)


class _Ride:
    def __init__(self, kind, srcs):
        self.kind, self.srcs = kind, list(srcs)
        if kind == "gather":
            self.out_shape = [jax.ShapeDtypeStruct((NDEV,) + a.shape, a.dtype) for a in self.srcs]
            self.sems = _gather_sems()
        else:
            self.out_shape = [_land_shape(a) for a in self.srcs]
            self.sems = _scatter_sems(len(self.srcs))

    def exchange(self, ins, outs, sems):
        return (_Gather if self.kind == "gather" else _Scatter)(ins, outs, sems)


def _small_exchange(small, ride, name):
    r, c = small.shape
    nr = len(ride.srcs)

    def body(v_ref, *refs):
        rin, refs = refs[:nr], refs[nr:]
        out_ref, refs = refs[0], refs[1:]
        rout, refs = refs[:nr], refs[nr:]
        send_sems, recv_sems, rsems = refs[0], refs[1], refs[2:]
        big = ride.exchange(rin, rout, rsems)
        big.start()
        me = _my_index()
        out_ref[me] = v_ref[...]
        copies = []
        for k in range(1, NDEV):
            dev, _ = _peer(k)
            copies.append(pltpu.make_async_remote_copy(
                src_ref=v_ref, dst_ref=out_ref.at[me], send_sem=send_sems.at[k - 1], recv_sem=recv_sems.at[k - 1],
                device_id=dev, device_id_type=MESH))
        for cp in copies:
            cp.start()
        for cp in copies:
            cp.wait()
        big.forward()
        big.finish()

    res = pl.pallas_call(body, name=name,
        out_shape=[jax.ShapeDtypeStruct((NDEV, r, c), small.dtype)] + ride.out_shape,
        in_specs=[pl.BlockSpec(memory_space=pltpu.VMEM)] + [ANY] * nr,
        out_specs=[pl.BlockSpec(memory_space=pltpu.VMEM)] + [ANY] * nr,
        scratch_shapes=[pltpu.SemaphoreType.DMA((NDEV - 1,)), pltpu.SemaphoreType.DMA((NDEV - 1,))] + ride.sems,
    )(small, *ride.srcs)
    return res[0], res[1:]


def _call(kernel_body, *, name, grid, in_specs, out_specs, out_shape, args, scratch_shapes=(), ride=None):
    sem = ("arbitrary",) * len(grid)
    if ride is None:
        body = functools.partial(kernel_body)
        res = pl.pallas_call(body, name=name, grid=grid, in_specs=in_specs, out_specs=out_specs, out_shape=out_shape,
                             scratch_shapes=list(scratch_shapes), compiler_params=_params(*sem))(*args)
        return list(res), []
    n_in, n_out, n_sc, nr = len(in_specs), len(out_specs), len(scratch_shapes), len(ride.srcs)

    def body(*refs):
        ins, refs = refs[:n_in], refs[n_in:]
        rin, refs = refs[:nr], refs[nr:]
        outs, refs = refs[:n_out], refs[n_out:]
        rout, refs = refs[:nr], refs[nr:]
        scratch, rsems = refs[:n_sc], refs[n_sc:]
        first, last = True, True
        for axis, extent in enumerate(grid):
            first &= pl.program_id(axis) == 0
            last &= pl.program_id(axis) == extent - 1
        middle = last if len(grid) > 1 else pl.program_id(0) == (3 * grid[0]) // 4
        exchange = ride.exchange(rin, rout, rsems)
        pl.when(first)(exchange.start)
        pl.when(middle)(exchange.forward)
        kernel_body(*ins, *outs, *scratch)
        pl.when(last)(exchange.finish)

    res = pl.pallas_call(body, name=name, grid=grid,
        in_specs=list(in_specs) + [ANY] * nr, out_specs=list(out_specs) + [ANY] * nr,
        out_shape=list(out_shape) + ride.out_shape,
        scratch_shapes=list(scratch_shapes) + ride.sems, compiler_params=_params(*sem),
    )(*args, *ride.srcs)
    return res[:n_out], res[n_out:]


def _ada_forward(c_all, ada_w, bias):
    nl, d, ncol = ada_w.shape

    def body(c_ref, w_ref, b_ref, o_ref):
        cv = c_ref[...]
        ca = cv * jax.nn.sigmoid(cv)
        o_ref[0] = jnp.dot(ca, w_ref[0], preferred_element_type=F32, precision=lax.Precision.HIGHEST) + b_ref[0]

    return pl.pallas_call(body, name="ada_forward", grid=(nl,),
        in_specs=[pl.BlockSpec((NDEV, d), lambda i: (0, 0)), pl.BlockSpec((1, d, ncol), lambda i: (i, 0, 0)),
                  pl.BlockSpec((1, 1, ncol), lambda i: (i, 0, 0))],
        out_specs=pl.BlockSpec((1, NDEV, ncol), lambda i: (i, 0, 0)),
        out_shape=jax.ShapeDtypeStruct((nl, NDEV, ncol), F32),
        compiler_params=_params("arbitrary"),
    )(c_all, ada_w, bias)


def _ada_wgrad(c_all, dmod):
    nl, _, ncol = dmod.shape
    d = c_all.shape[1]
    bd = min(d, 256)

    def body(c_ref, dm_ref, o_ref):
        cv = c_ref[...]
        ca = cv * jax.nn.sigmoid(cv)
        o_ref[0] = lax.dot_general(ca, dm_ref[0], (((0,), (0,)), ((), ())), preferred_element_type=F32,
                                   precision=lax.Precision.HIGHEST)

    return pl.pallas_call(body, name="ada_wgrad", grid=(nl, d // bd),
        in_specs=[pl.BlockSpec((NDEV, bd), lambda i, j: (0, j)), pl.BlockSpec((1, NDEV, ncol), lambda i, j: (i, 0, 0))],
        out_specs=pl.BlockSpec((1, bd, ncol), lambda i, j: (i, j, 0)),
        out_shape=jax.ShapeDtypeStruct((nl, d, ncol), F32),
        compiler_params=_params("arbitrary", "arbitrary"),
    )(c_all, dmod)


def _row_block(r, c, bytes_per_row_elem=4, budget=2 * 2**20):
    if r * c * bytes_per_row_elem <= budget or r % 8:
        return r
    best = 8
    for b in range(8, r + 1, 8):
        if r % b == 0 and b * c * bytes_per_row_elem <= budget:
            best = b
    return best


def _sum_slots(land, name):
    _, r, c = land.shape
    br = _row_block(r, c, 8 * land.dtype.itemsize)

    def body(l_ref, o_ref):
        acc = l_ref[0].astype(F32)
        for s in range(1, NDEV):
            acc = acc + l_ref[s].astype(F32)
        o_ref[...] = acc

    return pl.pallas_call(body, name=name, grid=(r // br,),
        in_specs=[pl.BlockSpec((NDEV, br, c), lambda i: (0, i, 0))],
        out_specs=pl.BlockSpec((br, c), lambda i: (i, 0)),
        out_shape=jax.ShapeDtypeStruct((r, c), F32),
        compiler_params=_params("arbitrary"),
    )(land)


def _adamw(w, g, m, v, name):
    r, c = w.shape
    br = _row_block(r, c)

    def body(w_ref, g_ref, m_ref, v_ref, d_ref, mo_ref, vo_ref):
        d_ref[...], mo_ref[...], vo_ref[...] = _adam_math(w_ref[...], g_ref[...], m_ref[...], v_ref[...])

    spec = pl.BlockSpec((br, c), lambda i: (i, 0))
    return pl.pallas_call(body, name=name, grid=(r // br,),
        in_specs=[spec] * 4, out_specs=[spec] * 3,
        out_shape=[jax.ShapeDtypeStruct((r, c), F32)] * 3,
        compiler_params=_params("arbitrary"),
    )(w, g, m, v)


def _adam_math(w, g, m, v):
    m2 = ADAM_B1 * m + (1.0 - ADAM_B1) * g
    v2 = ADAM_B2 * v + (1.0 - ADAM_B2) * (g * g)
    m_hat = m2 / (1.0 - ADAM_B1 ** ADAM_STEP)
    v_hat = v2 / (1.0 - ADAM_B2 ** ADAM_STEP)
    return -ADAM_LR * (m_hat / (jnp.sqrt(v_hat) + ADAM_EPS) + ADAM_WD * w), m2, v2


def _slot_sum(land_ref, *lead):
    acc = land_ref[(0,) + lead].astype(F32)
    for s in range(1, NDEV):
        acc = acc + land_ref[(s,) + lead].astype(F32)
    return acc


def _finalize(land, w, m, v, transposed, name):
    _, r, c = land.shape
    cb = 256 if (transposed and c % 256 == 0) else c
    wblk = pl.BlockSpec((cb, r), lambda i: (i, 0)) if transposed else pl.BlockSpec((r, cb), lambda i: (0, i))

    def body(l_ref, w_ref, m_ref, v_ref, g_ref, d_ref, mo_ref, vo_ref):
        g = _slot_sum(l_ref)
        g = g.T if transposed else g
        g_ref[...] = g
        d_ref[...], mo_ref[...], vo_ref[...] = _adam_math(w_ref[...], g, m_ref[...], v_ref[...])

    return pl.pallas_call(body, name=name, grid=(c // cb,),
        in_specs=[pl.BlockSpec((NDEV, r, cb), lambda i: (0, 0, i)), wblk, wblk, wblk], out_specs=[wblk] * 4,
        out_shape=[jax.ShapeDtypeStruct(w.shape, F32)] * 4,
        compiler_params=_params("arbitrary"),
    )(land, w, m, v)


def _finalize_ffn(land0, land1, wg, wu, wd, mg, mu, md, vg, vu, vd, ride):
    nl, d, fs = wg.shape
    db = min(256, d)

    def kernel_body(l0_ref, l1_ref, wg_ref, wu_ref, wd_ref, mg_ref, mu_ref, md_ref, vg_ref, vu_ref, vd_ref, *outs):
        layer = pl.program_id(0)
        triples = [(wg_ref, mg_ref, vg_ref, True), (wu_ref, mu_ref, vu_ref, True), (wd_ref, md_ref, vd_ref, False)]

        def run(land_ref):
            for j, (w_ref, m_ref, v_ref, transposed) in enumerate(triples):
                g = _slot_sum(land_ref, j)
                g = g.T if transposed else g
                delta, m2, v2 = _adam_math(w_ref[0], g, m_ref[0], v_ref[0])
                for o_ref, val in zip(outs[j::3], (g, delta, m2, v2)):
                    o_ref[0] = val

        @pl.when(layer == 0)
        def _():
            run(l0_ref)

        @pl.when(layer == 1)
        def _():
            run(l1_ref)

    tblk = pl.BlockSpec((1, db, fs), lambda l, i: (l, i, 0))
    pblk = pl.BlockSpec((1, fs, db), lambda l, i: (l, 0, i))
    lblk = [pl.BlockSpec((NDEV, 3, fs, db), lambda l, i: (0, 0, 0, i * (1 - l))),
            pl.BlockSpec((NDEV, 3, fs, db), lambda l, i: (0, 0, 0, i * l))]
    shapes = [jax.ShapeDtypeStruct(a.shape, F32) for a in (wg, wu, wd)]
    return _call(kernel_body, name="finalize_ffn", grid=(nl, d // db),
        in_specs=lblk + [tblk, tblk, pblk] * 3, out_specs=[tblk, tblk, pblk] * 4, out_shape=shapes * 4,
        args=(land0, land1, wg, wu, wd, mg, mu, md, vg, vu, vd), ride=ride)


CONV_ROWS = 64
LANES = 128


def _shifted_copies(buf, sh, n):
    sh[0] = buf[...]
    for r in range(1, 8):
        sh[r, pl.ds(0, n - 8), :] = buf[pl.ds(r, n - 8), :]


def _window(sh, o, rows, cols):
    return sh[o % 8, pl.ds(o - o % 8, rows), cols]


def _conv_in(x, mod, gmix, gw, w1_rows, w1_idx, b1, tm, ride):
    s, d = x.shape

    def kernel_body(x_ref, mod_ref, g_ref, w_ref, b_ref, h_ref, u_ref, glu_ref):
        n, _ = _rms(x_ref[...])
        h = (n * g_ref[...]) * (1.0 + mod_ref[1:2, :]) + mod_ref[0:1, :]
        hb = h.astype(BF16)
        h_ref[...] = hb
        u = _nt(hb, w_ref[...].reshape(NDEV * w1_rows, d)) + b_ref[...]
        u_ref[...] = u
        glu_ref[...] = u[:, :d] * jax.nn.sigmoid(u[:, d:])

    tile = pl.BlockSpec((tm, d), lambda i: (i, 0))
    return _call(kernel_body, name="conv_in", grid=(s // tm,),
        in_specs=[tile, _row(8, d), _row(1, d), _weight_spec(w1_rows, d, w1_idx), _row(1, 2 * d)],
        out_specs=[tile, pl.BlockSpec((tm, 2 * d), lambda i: (i, 0)), tile],
        out_shape=[jax.ShapeDtypeStruct((s, d), BF16), jax.ShapeDtypeStruct((s, 2 * d), F32),
                   jax.ShapeDtypeStruct((s, d), F32)],
        args=(x, mod, gmix, gw, b1), ride=ride)


def _conv_mid(glu, wdw, bdw, ln_g, ln_b, gw, w2_rows, w2_idx, b2, x, mod, tm, ride):
    s, d = x.shape
    off = CONV_HALO - (CONV_WIDTH - 1)
    rc = min(CONV_ROWS, tm)

    def kernel_body(glu_ref, halo_ref, wdw_ref, bdw_ref, lng_ref, lnb_ref, w_ref, b2_ref, x_ref, mod_ref,
                    dwc_ref, s_ref, y_ref, x1_ref, buf, sh):
        i = pl.program_id(0)
        buf[pl.ds(0, CONV_HALO), :] = jnp.where(i > 0, halo_ref[...], 0.0)
        buf[pl.ds(CONV_HALO, tm), :] = glu_ref[...]
        _shifted_copies(buf, sh, CONV_HALO + tm)
        for cb in range(d // LANES):
            cols = pl.ds(cb * LANES, LANES)
            taps = wdw_ref[:, cols]
            for r in range(tm // rc):
                part = jnp.zeros((rc, LANES), F32) + bdw_ref[:, cols]
                for k in range(CONV_WIDTH):
                    part = part + _window(sh, r * rc + off + k, rc, cols) * taps[k:k + 1, :]
                dwc_ref[pl.ds(r * rc, rc), cols] = part
        acc = dwc_ref[...]
        mu = jnp.mean(acc, axis=-1, keepdims=True)
        xc = acc - mu
        rstd = lax.rsqrt(jnp.mean(xc * xc, axis=-1, keepdims=True) + EPS)
        ln = (xc * rstd) * lng_ref[...] + lnb_ref[...]
        sb = (ln * jax.nn.sigmoid(ln)).astype(BF16)
        s_ref[...] = sb
        y = _nn(sb, w_ref[...].reshape(NDEV * w2_rows, d)) + b2_ref[...]
        y_ref[...] = y.astype(BF16)
        x1_ref[...] = x_ref[...] + (1.0 + mod_ref[2:3, :]) * y

    tile = pl.BlockSpec((tm, d), lambda i: (i, 0))
    halo = pl.BlockSpec((CONV_HALO, d), lambda i: (jnp.maximum(i * (tm // CONV_HALO) - 1, 0), 0))
    return _call(kernel_body, name="conv_mid", grid=(s // tm,),
        in_specs=[tile, halo, _row(CONV_WIDTH, d), _row(1, d), _row(1, d), _row(1, d),
                  _weight_spec(w2_rows, d, w2_idx), _row(1, d), tile, _row(8, d)],
        out_specs=[tile, tile, tile, tile],
        out_shape=[jax.ShapeDtypeStruct((s, d), F32), jax.ShapeDtypeStruct((s, d), BF16),
                   jax.ShapeDtypeStruct((s, d), BF16), jax.ShapeDtypeStruct((s, d), F32)],
        scratch_shapes=[pltpu.VMEM((CONV_HALO + tm, d), F32), pltpu.VMEM((8, CONV_HALO + tm, d), F32)],
        args=(glu, glu, wdw, bdw, ln_g, ln_b, gw, b2, x, mod), ride=ride)


def _ffn_fwd(x, mod, gffn, weights, fs, f, tm, name, ride=None, loss=None):
    s, d = x.shape

    last = s // tm - 1

    def kernel_body(x_ref, mod_ref, g_ref, wg_ref, wu_ref, wd_ref, *rest):
        xv = x_ref[...]
        n, _ = _rms(xv)
        hb = ((n * g_ref[...]) * (1.0 + mod_ref[4:5, :]) + mod_ref[3:4, :]).astype(BF16)
        gg = _nt(hb, wg_ref[...].reshape(f, d))
        uu = _nt(hb, wu_ref[...].reshape(f, d))
        ab = ((gg * jax.nn.sigmoid(gg)) * uu).astype(BF16)
        y = _nn(ab, wd_ref[...].reshape(f, d))
        xo = xv + (1.0 + mod_ref[5:6, :]) * y
        if loss is None:
            h_ref, gg_ref, uu_ref, y_ref, xo_ref = rest
            xo_ref[...] = xo
        else:
            t_ref, gfin_ref, h_ref, gg_ref, uu_ref, y_ref, dx_ref, st_ref = rest
            dx_ref[...] = _loss_tile(xo, t_ref[...], gfin_ref[...], st_ref, pl.program_id(0), last)
        h_ref[...] = hb
        gg_ref[...] = gg.astype(BF16)
        uu_ref[...] = uu.astype(BF16)
        y_ref[...] = y.astype(BF16)

    tile = pl.BlockSpec((tm, d), lambda i: (i, 0))
    wide = pl.BlockSpec((tm, f), lambda i: (i, 0))
    extra_in = [] if loss is None else [tile, _row(1, d)]
    extra_out = [] if loss is None else [_row(8, d)]
    return _call(kernel_body, name=name, grid=(s // tm,),
        in_specs=[tile, _row(8, d), _row(1, d)] + [_weight_spec(fs, d, idx) for _, idx in weights] + extra_in,
        out_specs=[tile, wide, wide, tile, tile] + extra_out,
        out_shape=[jax.ShapeDtypeStruct((s, d), BF16), jax.ShapeDtypeStruct((s, f), BF16),
                   jax.ShapeDtypeStruct((s, f), BF16), jax.ShapeDtypeStruct((s, d), BF16),
                   jax.ShapeDtypeStruct((s, d), F32)] + [jax.ShapeDtypeStruct((8, d), F32)] * len(extra_out),
        args=(x, mod, gffn) + tuple(a for a, _ in weights) + (() if loss is None else tuple(loss)), ride=ride)


def _loss_tile(x, target, g, st_ref, i, last):
    d = x.shape[1]

    @pl.when(i == 0)
    def _():
        st_ref[...] = jnp.zeros_like(st_ref)

    n, rinv = _rms(x)
    err = n * g - target
    dy = err * (1.0 / d)
    st_ref[0:1, :] += _sum0(dy * n)
    st_ref[1:2, :] += _sum0(err * err) * (0.5 / d)

    @pl.when(i == last)
    def _():
        st_ref[2:3, :] = jnp.zeros((1, d), F32) + jnp.sum(st_ref[1:2, :], axis=-1, keepdims=True)

    dn = dy * g
    return rinv * (dn - n * jnp.mean(dn * n, axis=-1, keepdims=True))


def _pool_fwd(x, mod, gmix, pw, ls, tm, ride):
    s, d = x.shape
    dg = d // len(POOL_WINDOWS)

    def body(x_ref, halo_ref, mod_ref, g_ref, pw_ref, ls_ref, mixed_ref, yp_ref, xo_ref, buf):
        i = pl.program_id(0)

        def hfun(xv):
            n, _ = _rms(xv)
            return (n * g_ref[...]) * (1.0 + mod_ref[1:2, :]) + mod_ref[0:1, :]

        xv = x_ref[...]
        h = hfun(xv)
        buf[pl.ds(0, POOL_HALO), :] = jnp.where(i > 0, hfun(halo_ref[...]), 0.0)
        buf[pl.ds(POOL_HALO, tm), :] = h
        t = i * tm + lax.broadcasted_iota(jnp.int32, (tm, 1), 0)
        gate = 1.0 + mod_ref[2:3, :]
        for gi, w in enumerate(POOL_WINDOWS):
            cols = pl.ds(gi * dg, dg)
            ws = buf[pl.ds(POOL_HALO, tm), cols]
            for j in range(1, w):
                ws = ws + buf[pl.ds(POOL_HALO - j, tm), cols]
            inv = 1.0 / jnp.minimum(t + 1, w).astype(F32)
            mb = (ws * inv - h[:, gi * dg:(gi + 1) * dg]).astype(BF16)
            mixed_ref[:, cols] = mb
            yp = _nn(mb, pw_ref[gi])
            yp_ref[:, cols] = yp.astype(BF16)
            xo_ref[:, cols] = xv[:, gi * dg:(gi + 1) * dg] + gate[:, gi * dg:(gi + 1) * dg] * (yp * ls_ref[:, cols])

    tile = pl.BlockSpec((tm, d), lambda i: (i, 0))
    halo = pl.BlockSpec((POOL_HALO, d), lambda i: (jnp.maximum(i * (tm // POOL_HALO) - 1, 0), 0))
    return _call(body, name="pool_fwd", grid=(s // tm,),
        in_specs=[tile, halo, _row(8, d), _row(1, d), pl.BlockSpec((len(POOL_WINDOWS), dg, dg), lambda i: (0, 0, 0)),
                  _row(1, d)],
        out_specs=[tile, tile, tile],
        out_shape=[jax.ShapeDtypeStruct((s, d), BF16), jax.ShapeDtypeStruct((s, d), BF16),
                   jax.ShapeDtypeStruct((s, d), F32)],
        scratch_shapes=[pltpu.VMEM((POOL_HALO + tm, d), F32)],
        args=(x, x, mod, gmix, pw, ls), ride=ride)


def _ffn_bwd(dxo, x, gg, uu, y, mod, gffn, weights, fs, f, tm, name, ride=None):
    s, d = x.shape

    def kernel_body(dxo_ref, x_ref, gg_ref, uu_ref, y_ref, mod_ref, g_ref, wg_ref, wu_ref, wd_ref,
                    dg_ref, du_ref, a_ref, dy_ref, dxi_ref, st_ref):
        @pl.when(pl.program_id(0) == 0)
        def _():
            st_ref[...] = jnp.zeros_like(st_ref)

        dxo_v = dxo_ref[...]
        dyb = (dxo_v * (1.0 + mod_ref[5:6, :])).astype(BF16)
        dy_ref[...] = dyb
        da = _nt(dyb, wd_ref[...].reshape(f, d))
        ggv, uuv = gg_ref[...].astype(F32), uu_ref[...].astype(F32)
        sg = jax.nn.sigmoid(ggv)
        silu = ggv * sg
        a_ref[...] = (silu * uuv).astype(BF16)
        dub = (da * silu).astype(BF16)
        dgb = (da * uuv * _silu_grad(ggv, sg)).astype(BF16)
        du_ref[...] = dub
        dg_ref[...] = dgb
        dh = _nn(dgb, wg_ref[...].reshape(f, d)) + _nn(dub, wu_ref[...].reshape(f, d))
        n, rinv = _rms(x_ref[...])
        dx, dsh, dsc, dgain = _rms_mod_bwd(dh, n, rinv, g_ref[...], mod_ref[4:5, :])
        dxi_ref[...] = dxo_v + dx
        st_ref[0:1, :] += dsh
        st_ref[1:2, :] += dsc
        st_ref[2:3, :] += _sum0(dxo_v * y_ref[...].astype(F32))
        st_ref[3:4, :] += dgain

    tile = pl.BlockSpec((tm, d), lambda i: (i, 0))
    wide = pl.BlockSpec((tm, f), lambda i: (i, 0))
    return _call(kernel_body, name=name, grid=(s // tm,),
        in_specs=[tile, tile, wide, wide, tile, _row(8, d), _row(1, d)] + [_weight_spec(fs, d, idx) for _, idx in weights],
        out_specs=[wide, wide, wide, tile, tile, _row(8, d)],
        out_shape=[jax.ShapeDtypeStruct((s, f), BF16)] * 3 + [jax.ShapeDtypeStruct((s, d), BF16),
                   jax.ShapeDtypeStruct((s, d), F32), jax.ShapeDtypeStruct((8, d), F32)],
        args=(dxo, x, gg, uu, y, mod, gffn) + tuple(a for a, _ in weights), ride=ride)


def _ffn_wgrad(dgb, dub, ab, h, dyb, fb, ts, name):
    s, f = dgb.shape
    d = h.shape[1]
    last = s // ts - 1

    def body(dg_ref, du_ref, a_ref, h_ref, dy_ref, o_ref, acc):
        t = pl.program_id(1)

        @pl.when(t == 0)
        def _():
            acc[...] = jnp.zeros_like(acc)

        hv = h_ref[...]
        acc[0] += _tn(dg_ref[...], hv)
        acc[1] += _tn(du_ref[...], hv)
        acc[2] += _tn(a_ref[...], dy_ref[...])

        @pl.when(t == last)
        def _():
            o_ref[...] = acc[...].astype(BF16)

    wide = pl.BlockSpec((ts, fb), lambda j, t: (t, j))
    tile = pl.BlockSpec((ts, d), lambda j, t: (t, 0))
    return pl.pallas_call(body, name=name, grid=(f // fb, s // ts),
        in_specs=[wide, wide, wide, tile, tile],
        out_specs=pl.BlockSpec((3, fb, d), lambda j, t: (0, j, 0)),
        out_shape=jax.ShapeDtypeStruct((3, f, d), BF16),
        scratch_shapes=[pltpu.VMEM((3, fb, d), F32)],
        compiler_params=_params("arbitrary", "arbitrary"),
    )(dgb, dub, ab, h, dyb)


def _pool_bwd(dxo, x, yp, mixed, mod, gmix, pw, ls, tm):
    s, d = x.shape
    ng = len(POOL_WINDOWS)
    dg = d // ng
    last = s // tm - 1

    def body(dxo_ref, dxh_ref, x_ref, yp_ref, mixed_ref, mod_ref, g_ref, pw_ref, ls_ref,
             dxi_ref, dpw_ref, st_ref, bufy, bufq, bufh, acc):
        i = pl.program_id(0)

        @pl.when(i == 0)
        def _():
            st_ref[...] = jnp.zeros_like(st_ref)
            acc[...] = jnp.zeros_like(acc)

        gate = 1.0 + mod_ref[2:3, :]
        lsv = ls_ref[...]
        dxo_v = dxo_ref[...]
        st_ref[2:3, :] += _sum0(dxo_v * yp_ref[...].astype(F32))
        bufy[pl.ds(0, tm), :] = (dxo_v * (gate * lsv)).astype(BF16)
        bufy[pl.ds(tm, POOL_HALO), :] = jnp.where(i < last, dxh_ref[...] * (gate * lsv), 0.0).astype(BF16)
        t = i * tm + lax.broadcasted_iota(jnp.int32, (tm + POOL_HALO, 1), 0)
        for gi, w in enumerate(POOL_WINDOWS):
            cols = pl.ds(gi * dg, dg)
            dm = _nt(bufy[:, cols], pw_ref[gi])
            bufq[:, cols] = dm * (1.0 / jnp.minimum(t + 1, w).astype(F32))
            dh = bufq[pl.ds(0, tm), cols] - dm[0:tm, :]
            for j in range(1, w):
                dh = dh + bufq[pl.ds(j, tm), cols]
            bufh[:, cols] = dh
            acc[gi] += _tn(mixed_ref[:, cols], bufy[pl.ds(0, tm), cols])
        n, rinv = _rms(x_ref[...])
        dx, dsh, dsc, dgain = _rms_mod_bwd(bufh[...], n, rinv, g_ref[...], mod_ref[1:2, :])
        dxi_ref[...] = dxo_v + dx
        st_ref[0:1, :] += dsh
        st_ref[1:2, :] += dsc
        st_ref[3:4, :] += dgain

        @pl.when(i == last)
        def _():
            r = st_ref[2:3, :]
            st_ref[4:5, :] = r * lsv
            st_ref[5:6, :] = r * gate
            dpw_ref[...] = acc[...].astype(BF16)

    tile = pl.BlockSpec((tm, d), lambda i: (i, 0))
    nxt = pl.BlockSpec((POOL_HALO, d), lambda i: (jnp.minimum((i + 1) * (tm // POOL_HALO), s // POOL_HALO - 1), 0))
    pws = pl.BlockSpec((ng, dg, dg), lambda i: (0, 0, 0))
    return pl.pallas_call(body, name="pool_bwd", grid=(s // tm,),
        in_specs=[tile, nxt, tile, tile, tile, _row(8, d), _row(1, d), pws, _row(1, d)],
        out_specs=[tile, pws, _row(8, d)],
        out_shape=[jax.ShapeDtypeStruct((s, d), F32), jax.ShapeDtypeStruct((ng, dg, dg), BF16),
                   jax.ShapeDtypeStruct((8, d), F32)],
        scratch_shapes=[pltpu.VMEM((tm + POOL_HALO, d), BF16), pltpu.VMEM((tm + POOL_HALO, d), F32),
                        pltpu.VMEM((tm, d), F32), pltpu.VMEM((ng, dg, dg), F32)],
        compiler_params=_params("arbitrary"),
    )(dxo, dxo, x, yp, mixed, mod, gmix, pw, ls)


def _conv_bwd_mid(dxo, y, dwc, sb, mod, ln_g, ln_b, gw, w2_rows, w2_idx, tm):
    s, d = dwc.shape
    last = s // tm - 1

    def body(dxo_ref, y_ref, dwc_ref, s_ref, mod_ref, lng_ref, lnb_ref, w_ref, dd_ref, dw_ref, st_ref, acc):
        i = pl.program_id(0)

        @pl.when(i == 0)
        def _():
            st_ref[...] = jnp.zeros_like(st_ref)
            acc[...] = jnp.zeros_like(acc)

        dxo_v = dxo_ref[...]
        st_ref[0:1, :] += _sum0(dxo_v * y_ref[...].astype(F32))
        dy = dxo_v * (1.0 + mod_ref[2:3, :])
        st_ref[1:2, :] += _sum0(dy)
        dyb = dy.astype(BF16)
        ds = _nt(dyb, w_ref[...].reshape(NDEV * w2_rows, d))
        acc[...] += _tn(s_ref[...], dyb)
        v = dwc_ref[...]
        mu = jnp.mean(v, axis=-1, keepdims=True)
        xc = v - mu
        rstd = lax.rsqrt(jnp.mean(xc * xc, axis=-1, keepdims=True) + EPS)
        xhat = xc * rstd
        ln = xhat * lng_ref[...] + lnb_ref[...]
        dln = ds * _silu_grad(ln, jax.nn.sigmoid(ln))
        st_ref[2:3, :] += _sum0(dln * xhat)
        st_ref[3:4, :] += _sum0(dln)
        dxh = dln * lng_ref[...]
        dd = rstd * (dxh - jnp.mean(dxh, axis=-1, keepdims=True) - xhat * jnp.mean(dxh * xhat, axis=-1, keepdims=True))
        dd_ref[...] = dd
        st_ref[4:5, :] += _sum0(dd)

        @pl.when(i == last)
        def _():
            dw_ref[...] = acc[...].astype(BF16)

    tile = pl.BlockSpec((tm, d), lambda i: (i, 0))
    return pl.pallas_call(body, name="conv_bwd_mid", grid=(s // tm,),
        in_specs=[tile, tile, tile, tile, _row(8, d), _row(1, d), _row(1, d), _weight_spec(w2_rows, d, w2_idx)],
        out_specs=[tile, pl.BlockSpec((d, d), lambda i: (0, 0)), _row(8, d)],
        out_shape=[jax.ShapeDtypeStruct((s, d), F32), jax.ShapeDtypeStruct((d, d), BF16),
                   jax.ShapeDtypeStruct((8, d), F32)],
        scratch_shapes=[pltpu.VMEM((d, d), F32)],
        compiler_params=_params("arbitrary"),
    )(dxo, y, dwc, sb, mod, ln_g, ln_b, gw)


def _conv_bwd_in(dd, glu, u, hb, x, dxo, wdw, gw, w1_rows, w1_idx, mod, gmix, tm, ride):
    s, d = x.shape
    last = s // tm - 1
    off = CONV_HALO - (CONV_WIDTH - 1)
    rw = 32
    tap_group = 16

    def kernel_body(dd_ref, ddn_ref, glu_ref, glp_ref, u_ref, h_ref, x_ref, dxo_ref, wdw_ref, w_ref, mod_ref, g_ref,
                    dxi_ref, dw_ref, dwdw_ref, st_ref, bufd, bufg, shd, shg, dgl, accw, acc):
        i = pl.program_id(0)

        @pl.when(i == 0)
        def _():
            st_ref[...] = jnp.zeros_like(st_ref)
            accw[...] = jnp.zeros_like(accw)
            acc[...] = jnp.zeros_like(acc)

        bufd[pl.ds(0, tm), :] = dd_ref[...]
        bufd[pl.ds(tm, CONV_HALO), :] = jnp.where(i < last, ddn_ref[...], 0.0)
        bufg[pl.ds(0, CONV_HALO), :] = jnp.where(i > 0, glp_ref[...], 0.0)
        bufg[pl.ds(CONV_HALO, tm), :] = glu_ref[...]
        _shifted_copies(bufd, shd, tm + CONV_HALO)
        _shifted_copies(bufg, shg, CONV_HALO + tm)
        for cb in range(d // LANES):
            cols = pl.ds(cb * LANES, LANES)
            taps = wdw_ref[:, cols]
            for r in range(tm // rw):
                part = jnp.zeros((rw, LANES), F32)
                for k in range(CONV_WIDTH):
                    part = part + _window(shd, r * rw + CONV_WIDTH - 1 - k, rw, cols) * taps[k:k + 1, :]
                dgl[pl.ds(r * rw, rw), cols] = part
            for k0 in range(0, CONV_WIDTH, tap_group):
                group = range(k0, min(CONV_WIDTH, k0 + tap_group))
                sums = {k: jnp.zeros((8, LANES), F32) for k in group}
                for r in range(tm // rw):
                    ddc = bufd[pl.ds(r * rw, rw), cols]
                    for k in group:
                        p = _window(shg, r * rw + off + k, rw, cols) * ddc
                        for q in range(rw // 8):
                            sums[k] = sums[k] + p[q * 8:(q + 1) * 8, :]
                for k in group:
                    accw[k, :, cols] += sums[k]
        dglu = dgl[...]
        uv = u_ref[...]
        a, g = uv[:, :d], uv[:, d:]
        sg = jax.nn.sigmoid(g)
        da = dglu * sg
        dgt = dglu * a * (sg * (1.0 - sg))
        du = jnp.concatenate([da, dgt], axis=1)
        st_ref[0:1, :] += _sum0(du)
        dub = du.astype(BF16)
        w = w_ref[...].reshape(NDEV * w1_rows, d)
        dh = _nn(dub, w)
        acc[...] += _tn(dub, h_ref[...])
        n, rinv = _rms(x_ref[...])
        dx, dsh, dsc, dgain = _rms_mod_bwd(dh, n, rinv, g_ref[...], mod_ref[1:2, :])
        dxi_ref[...] = dxo_ref[...] + dx
        st_ref[1:2, 0:d] += dsh
        st_ref[2:3, 0:d] += dsc
        st_ref[3:4, 0:d] += dgain

        @pl.when(i == last)
        def _():
            dw_ref[...] = acc[...].astype(BF16)
            dwdw_ref[...] = jnp.sum(accw[...], axis=1)

    tile = pl.BlockSpec((tm, d), lambda i: (i, 0))
    prv = pl.BlockSpec((CONV_HALO, d), lambda i: (jnp.maximum(i * (tm // CONV_HALO) - 1, 0), 0))
    nxt = pl.BlockSpec((CONV_HALO, d), lambda i: (jnp.minimum((i + 1) * (tm // CONV_HALO), s // CONV_HALO - 1), 0))
    return _call(kernel_body, name="conv_bwd_in", grid=(s // tm,),
        in_specs=[tile, nxt, tile, prv, pl.BlockSpec((tm, 2 * d), lambda i: (i, 0)), tile, tile, tile,
                  _row(CONV_WIDTH, d), _weight_spec(w1_rows, d, w1_idx), _row(8, d), _row(1, d)],
        out_specs=[tile, pl.BlockSpec((2 * d, d), lambda i: (0, 0)), _row(CONV_HALO, d), _row(8, 2 * d)],
        out_shape=[jax.ShapeDtypeStruct((s, d), F32), jax.ShapeDtypeStruct((2 * d, d), BF16),
                   jax.ShapeDtypeStruct((CONV_HALO, d), F32), jax.ShapeDtypeStruct((8, 2 * d), F32)],
        scratch_shapes=[pltpu.VMEM((tm + CONV_HALO, d), F32), pltpu.VMEM((CONV_HALO + tm, d), F32),
                        pltpu.VMEM((8, tm + CONV_HALO, d), F32), pltpu.VMEM((8, CONV_HALO + tm, d), F32),
                        pltpu.VMEM((tm, d), F32), pltpu.VMEM((CONV_HALO, 8, d), F32), pltpu.VMEM((2 * d, d), F32)],
        args=(dd, dd, glu, glu, u, hb, x, dxo, wdw, gw, mod, gmix), ride=ride)


def kernel(x, c, ada_w, ada_b, norm_mix_g, norm_ffn_g, conv_w1, conv_b1, conv_wdw, conv_bdw, conv_ln_g, conv_ln_b, conv_w2, conv_b2, pool_w, pool_ls, ffn_w_gate, ffn_w_up, ffn_w_down, final_g, loss_target, m_ada_w, m_ada_b, m_norm_mix_g, m_norm_ffn_g, m_conv_w1, m_conv_b1, m_conv_wdw, m_conv_bdw, m_conv_ln_g, m_conv_ln_b, m_conv_w2, m_conv_b2, m_pool_w, m_pool_ls, m_ffn_w_gate, m_ffn_w_up, m_ffn_w_down, m_final_g, v_ada_w, v_ada_b, v_norm_mix_g, v_norm_ffn_g, v_conv_w1, v_conv_b1, v_conv_wdw, v_conv_bdw, v_conv_ln_g, v_conv_ln_b, v_conv_w2, v_conv_b2, v_pool_w, v_pool_ls, v_ffn_w_gate, v_ffn_w_up, v_ffn_w_down, v_final_g):
    _, s, d = x.shape
    f = ffn_w_down.shape[1] * NDEV
    fs = f // NDEV
    r1, r2 = 2 * d // NDEV, d // NDEV
    ng = len(POOL_WINDOWS)
    dg = d // ng
    pr = ng * (dg // NDEV) * dg // d
    ncol = ada_w.shape[2]
    dc = d // NDEV
    tm = min(256, s)
    me = _my_index()
    x0 = x.reshape(s, d)
    target = loss_target.reshape(s, d)

    small = jnp.concatenate([c.reshape(NDEV, dc), conv_wdw[0], pool_ls], axis=0)
    shard_a = jnp.concatenate([conv_w1[0].T, conv_w2[0]], axis=0).astype(BF16)
    small_all, (gwa,) = _small_exchange(small, _Ride("gather", [shard_a]), "allgather_first")
    c_all = small_all[:, 0:NDEV, :].reshape(NDEV, d)
    wdw = small_all[:, NDEV:NDEV + CONV_WIDTH, :].transpose(1, 0, 2).reshape(CONV_WIDTH, d)
    ls = small_all[:, NDEV + CONV_WIDTH, :].reshape(1, d)
    bias = lax.dynamic_slice_in_dim(ada_b, me * ncol, ncol, axis=1)[:, None, :]
    mod_cols = _ada_forward(c_all, ada_w, bias)
    mod_all = _allgather_small(mod_cols.reshape(2 * NDEV, ncol), "allgather_mod")
    mod_mine = lax.dynamic_index_in_dim(mod_all.reshape(NDEV, 2, NDEV, ncol), me, axis=2, keepdims=False)
    mod = mod_mine.transpose(1, 0, 2).reshape(2, 6, d)
    mod = jnp.concatenate([mod, jnp.zeros((2, 2, d), F32)], axis=1)

    shard_b1 = ffn_w_gate[0].T.astype(BF16)
    shard_b2 = jnp.concatenate([ffn_w_up[0].T, ffn_w_down[0]], axis=0).astype(BF16)
    shard_c = jnp.concatenate([ffn_w_gate[1].T, ffn_w_up[1].T, pool_w.reshape(pr, d)], axis=0).astype(BF16)
    shard_d = ffn_w_down[1].astype(BF16)
    w1_at, w2_at = (gwa, 0), (gwa, r1 // r2)

    (h0, u, glu), (gwb1,) = _conv_in(x0, mod[0], norm_mix_g[0:1], w1_at[0], r1, w1_at[1], conv_b1, tm,
                                    _Ride("gather", [shard_b1]))
    (dwc, sb, y0, x1), (gwb2,) = _conv_mid(glu, wdw, conv_bdw, conv_ln_g, conv_ln_b, w2_at[0], r2, w2_at[1], conv_b2,
                                           x0, mod[0], tm, _Ride("gather", [shard_b2]))
    ffn0_w = [(gwb1, 0), (gwb2, 0), (gwb2, 1)]
    (h1, gg0, uu0, yf0, x2), (gwc,) = _ffn_fwd(x1, mod[0], norm_ffn_g[0:1], ffn0_w, fs, f, tm, "ffn_fwd0",
                                               _Ride("gather", [shard_c]))
    pw = gwc[:, 2 * fs:2 * fs + pr, :].reshape(NDEV, ng, dg // NDEV, dg).transpose(1, 0, 2, 3).reshape(ng, dg, dg)
    (mixed, yp, x3), (gwd,) = _pool_fwd(x2, mod[1], norm_mix_g[1:2], pw, ls, tm, _Ride("gather", [shard_d]))
    ffn1_w = [(gwc, 0), (gwc, 1), (gwd, 0)]
    (h3, gg1, uu1, yf1, dx4, st_loss), _ = _ffn_fwd(x3, mod[1], norm_ffn_g[1:2], ffn1_w, fs, f, tm, "ffn_fwd1",
                                                    loss=(target, final_g.reshape(1, d)))
    loss = lax.psum(st_loss[2, 0], ("x", "y", "c"))

    fb = f // 2 if (f // 2) % 128 == 0 else f
    ts = min(512, s)
    (dgb, dub, ab, dyb, dx3, st_f1), _ = _ffn_bwd(dx4, x3, gg1, uu1, yf1, mod[1], norm_ffn_g[1:2], ffn1_w, fs, f, tm,
                                                  "ffn_bwd1")
    gf1 = _ffn_wgrad(dgb, dub, ab, h3, dyb, fb, ts, "ffn_wgrad1")
    dx2, gpw, st_p = _pool_bwd(dx3, x2, yp, mixed, mod[1], norm_mix_g[1:2], pw, ls, tm)
    (dgb, dub, ab, dyb, dx1, st_f0), (land_f1,) = _ffn_bwd(dx2, x1, gg0, uu0, yf0, mod[0], norm_ffn_g[0:1], ffn0_w, fs, f,
                                                           tm, "ffn_bwd0", _Ride("scatter", [gf1]))
    gf0 = _ffn_wgrad(dgb, dub, ab, h1, dyb, fb, ts, "ffn_wgrad0")
    dd, gw2, st_m = _conv_bwd_mid(dx1, y0, dwc, sb, mod[0], conv_ln_g, conv_ln_b, w2_at[0], r2, w2_at[1], tm)
    (dx0, gw1, gwdw, st_c), (land_f0, land_pw, land_w2) = _conv_bwd_in(
        dd, glu, u, h0, x0, dx1, wdw, w1_at[0], r1, w1_at[1], mod[0], norm_mix_g[0:1], tm,
        _Ride("scatter", [gf0, gpw, gw2[None]]))

    zrow = jnp.zeros((1, d), F32)
    prow = jnp.concatenate([
        st_c[1:3, 0:d], st_m[0:1], st_f0[0:3], st_p[0:2], st_p[4:5], st_f1[0:3],
        st_c[3:4, 0:d], st_p[3:4], st_f0[3:4], st_f1[3:4],
        st_c[0:1, 0:d], st_c[0:1, d:2 * d], st_m[4:5], st_m[2:4], st_m[1:2], st_loss[0:1],
        gwdw[0:CONV_WIDTH], st_p[5:6], zrow], axis=0)
    p_all, (land_w1,) = _small_exchange(prow, _Ride("scatter", [gw1[None]]), "allgather_stats")
    psum = _sum_slots(p_all, "sum_stats")

    ffn_out, _ = _finalize_ffn(land_f0, land_f1, ffn_w_gate, ffn_w_up, ffn_w_down,
                               m_ffn_w_gate, m_ffn_w_up, m_ffn_w_down, v_ffn_w_gate, v_ffn_w_up, v_ffn_w_down, None)
    fin_w1 = _finalize(land_w1.reshape(NDEV, r1, d), conv_w1[0], m_conv_w1[0], v_conv_w1[0], True, "finalize_w1")
    fin_w2 = _finalize(land_w2.reshape(NDEV, r2, d), conv_w2[0], m_conv_w2[0], v_conv_w2[0], False, "finalize_w2")
    pshape = (ng * (dg // NDEV), dg)
    fin_pw = _finalize(land_pw.reshape((NDEV,) + pshape), pool_w.reshape(pshape), m_pool_w.reshape(pshape),
                       v_pool_w.reshape(pshape), False, "finalize_pool_w")
    dmod_all = p_all[:, 0:12, :].reshape(NDEV, 2, 6 * d)
    dmod_cols = lax.dynamic_slice_in_dim(dmod_all, me * ncol, ncol, axis=2).transpose(1, 0, 2)
    g_ada_w = _ada_wgrad(c_all, dmod_cols)

    def adam(w, g, m, v, name):
        shp = w.shape
        w2d = (-1, shp[-1])
        dl, mo, vo = _adamw(w.reshape(w2d), g.reshape(w2d), m.reshape(w2d), v.reshape(w2d), name)
        return dl.reshape(shp), mo.reshape(shp), vo.reshape(shp)

    rep_names = ["ada_b", "norm_mix_g", "norm_ffn_g", "conv_b1", "conv_bdw", "conv_ln_g", "conv_ln_b", "conv_b2", "final_g"]
    rep_w = [ada_b, norm_mix_g, norm_ffn_g, conv_b1, conv_bdw, conv_ln_g, conv_ln_b, conv_b2, final_g]
    rep_m = [m_ada_b, m_norm_mix_g, m_norm_ffn_g, m_conv_b1, m_conv_bdw, m_conv_ln_g, m_conv_ln_b, m_conv_b2, m_final_g]
    rep_v = [v_ada_b, v_norm_mix_g, v_norm_ffn_g, v_conv_b1, v_conv_bdw, v_conv_ln_g, v_conv_ln_b, v_conv_b2, v_final_g]
    nrep = sum(w.size for w in rep_w) // d
    pad = jnp.zeros(((-nrep) % 8, d), F32)

    def pack(arrs, fill):
        return jnp.concatenate([a.reshape(-1, d) for a in arrs] + [pad + fill], axis=0)

    rep_g = jnp.concatenate([psum[0:nrep], pad], axis=0)
    rep_d, rep_mo, rep_vo = _adamw(pack(rep_w, 0.0), rep_g, pack(rep_m, 0.0), pack(rep_v, 1.0), "adamw_replicated")

    def unpack(packed):
        out, cur = [], 0
        for w in rep_w:
            k = w.size // d
            out.append(packed[cur:cur + k].reshape(w.shape))
            cur += k
        return out

    rep = dict(zip(rep_names, zip(unpack(psum), unpack(rep_d), unpack(rep_mo), unpack(rep_vo))))

    g_wdw_full = psum[nrep:nrep + CONV_WIDTH]
    g_wdw = lax.dynamic_slice_in_dim(g_wdw_full, me * dc, dc, axis=1)
    g_ls = lax.dynamic_slice_in_dim(psum[nrep + CONV_WIDTH:nrep + CONV_WIDTH + 1], me * dc, dc, axis=1)
    tiny = lambda a, b: jnp.concatenate([a.reshape(CONV_WIDTH, dc), b.reshape(1, dc)], axis=0)
    t_d, t_m, t_v = _adamw(tiny(conv_wdw, pool_ls), tiny(g_wdw, g_ls), tiny(m_conv_wdw, m_pool_ls),
                           tiny(v_conv_wdw, v_pool_ls), "adamw_taps")

    def taps(a):
        return a[0:CONV_WIDTH][None], a[CONV_WIDTH:CONV_WIDTH + 1]

    sharded = {
        "ada_w": (g_ada_w,) + adam(ada_w, g_ada_w, m_ada_w, v_ada_w, "adamw_ada_w"),
        "conv_w1": tuple(a[None] for a in fin_w1),
        "conv_w2": tuple(a[None] for a in fin_w2),
        "pool_w": tuple(a.reshape(pool_w.shape) for a in fin_pw),
        "ffn_w_gate": tuple(ffn_out[0::3]),
        "ffn_w_up": tuple(ffn_out[1::3]),
        "ffn_w_down": tuple(ffn_out[2::3]),
        "conv_wdw": (g_wdw[None], taps(t_d)[0], taps(t_m)[0], taps(t_v)[0]),
        "pool_ls": (g_ls, taps(t_d)[1], taps(t_m)[1], taps(t_v)[1]),
    }
    every = {**rep, **sharded}
    order = ["ada_w", "ada_b", "norm_mix_g", "norm_ffn_g", "conv_w1", "conv_b1", "conv_wdw", "conv_bdw", "conv_ln_g",
             "conv_ln_b", "conv_w2", "conv_b2", "pool_w", "pool_ls", "ffn_w_gate", "ffn_w_up", "ffn_w_down", "final_g"]
    grads = [every[n][0] for n in order]
    deltas = [every[n][1] for n in order]
    new_m = [every[n][2] for n in order]
    new_v = [every[n][3] for n in order]
    return (loss, dx0.reshape(1, s, d), *grads, *deltas, *new_m, *new_v)
```

```python
import jax
import jax.numpy as jnp
from jax import lax
from jax.experimental import pallas as pl
from jax.experimental.pallas import tpu as pltpu

NDEV = 8
EPS = 1e-6
CONV_WIDTH = 31
POOL_WINDOWS = (2, 4, 8, 16)
CONV_HALO = 32
POOL_HALO = 16
ADAM_LR = 0.001
ADAM_B1 = 0.9
ADAM_B2 = 0.999
ADAM_EPS = 1e-08
ADAM_WD = 0.01
ADAM_STEP = 10
VMEM_LIMIT = 56 * 2**20
MESH = pl.DeviceIdType.MESH
F32 = jnp.float32
BF16 = jnp.bfloat16


def _nt(a, b):
    return lax.dot_general(a, b, (((1,), (1,)), ((), ())), preferred_element_type=F32)


def _nn(a, b):
    return lax.dot_general(a, b, (((1,), (0,)), ((), ())), preferred_element_type=F32)


def _tn(a, b):
    return lax.dot_general(a, b, (((0,), (0,)), ((), ())), preferred_element_type=F32)


def _sum0(v):
    return jnp.sum(v, axis=0, keepdims=True)


def _rms(x):
    rinv = lax.rsqrt(jnp.mean(x * x, axis=-1, keepdims=True) + EPS)
    return x * rinv, rinv


def _rms_mod_bwd(dh, n, rinv, g, sc):
    dhs = dh * (1.0 + sc)
    dn = dhs * g
    dx = rinv * (dn - n * jnp.mean(dn * n, axis=-1, keepdims=True))
    return dx, _sum0(dh), _sum0(dh * (n * g)), _sum0(dhs * n)


def _silu_grad(z, sg):
    return sg * (1.0 + z * (1.0 - sg))


def _params(*sem):
    return pltpu.CompilerParams(dimension_semantics=sem, vmem_limit_bytes=VMEM_LIMIT)


def _row(i, d):
    return pl.BlockSpec((i, d), lambda *_: (0, 0))


def _weight_spec(rows, d, idx):
    return pl.BlockSpec((NDEV, rows, d), lambda *_: (0, idx, 0), pipeline_mode=pl.Buffered(1))


def _my_index():
    return 4 * lax.axis_index("x") + 2 * lax.axis_index("y") + lax.axis_index("c")


def _peer(k):
    x, y, c = lax.axis_index("x"), lax.axis_index("y"), lax.axis_index("c")
    px = 1 - x if k & 4 else x
    py = 1 - y if k & 2 else y
    pc = 1 - c if k & 1 else c
    return (px, py, pc), 4 * px + 2 * py + pc


def _allgather_small(v, name):
    r, c = v.shape

    def body(v_ref, out_ref, send_sems, recv_sems):
        me = _my_index()
        out_ref[me] = v_ref[...]
        copies = []
        for k in range(1, NDEV):
            dev, _ = _peer(k)
            copies.append(pltpu.make_async_remote_copy(
                src_ref=v_ref, dst_ref=out_ref.at[me], send_sem=send_sems.at[k - 1], recv_sem=recv_sems.at[k - 1],
                device_id=dev, device_id_type=MESH))
        for cp in copies:
            cp.start()
        for cp in copies:
            cp.wait()

    return pl.pallas_call(body, name=name,
        out_shape=jax.ShapeDtypeStruct((NDEV, r, c), v.dtype),
        in_specs=[pl.BlockSpec(memory_space=pltpu.VMEM)],
        out_specs=pl.BlockSpec(memory_space=pltpu.VMEM),
        scratch_shapes=[pltpu.SemaphoreType.DMA((NDEV - 1,)), pltpu.SemaphoreType.DMA((NDEV - 1,))],
    )(v)


def _gather_sems():
    return [pltpu.SemaphoreType.DMA((NDEV - 1,)), pltpu.SemaphoreType.DMA((NDEV - 1,)), pltpu.SemaphoreType.DMA((1,))]


class _Gather:
    def __init__(self, srcs, dsts, sems):
        self.src, self.dst = srcs[0], dsts[0]
        self.send, self.recv, self.local = sems
        x, y, c = lax.axis_index("x"), lax.axis_index("y"), lax.axis_index("c")
        self.me, self.sibling, self.core = (x, y, c), (x, y, 1 - c), c
        self.chips = [(1 - x, y), (x, 1 - y), (1 - x, 1 - y)]

    def _copy(self, k, block, to, from_input=False):
        slot = self.dst.at[4 * block[0] + 2 * block[1] + block[2]]
        return pltpu.make_async_remote_copy(
            src_ref=self.src if from_input else slot, dst_ref=slot, send_sem=self.send.at[k], recv_sem=self.recv.at[k],
            device_id=to, device_id_type=MESH)

    def _own(self):
        return pltpu.make_async_copy(self.src, self.dst.at[_my_index()], self.local.at[0])

    def _first(self):
        return [self._copy(0, self.me, self.sibling, True)] + [
            self._copy(1 + j, self.me, (*chip, self.core), True) for j, chip in enumerate(self.chips)]

    def start(self):
        self._own().start()
        for cp in self._first():
            cp.start()

    def forward(self):
        for j, chip in enumerate(self.chips):
            self._copy(1 + j, (*chip, self.core), self.me).wait_recv()
            self._copy(4 + j, (*chip, self.core), self.sibling).start()

    def finish(self):
        self._copy(0, self.sibling, self.me).wait_recv()
        for j, chip in enumerate(self.chips):
            self._copy(4 + j, (*chip, 1 - self.core), self.me).wait_recv()
        for cp in self._first():
            cp.wait_send()
        for j, chip in enumerate(self.chips):
            self._copy(4 + j, (*chip, self.core), self.sibling).wait_send()
        self._own().wait()


def _scatter_sems(n):
    return [pltpu.SemaphoreType.DMA((7 * n,)), pltpu.SemaphoreType.DMA((7 * n,)), pltpu.SemaphoreType.DMA((n,))]


class _Scatter:
    def __init__(self, srcs, dsts, sems):
        send_sems, recv_sems, local_sems = sems
        me = _my_index()
        self.copies = []
        for a, (src, dst) in enumerate(zip(srcs, dsts)):
            r = dst.shape[2]
            self.copies.append(pltpu.make_async_copy(src.at[:, pl.ds(me * r, r), :], dst.at[me], local_sems.at[a]))
            for k in range(1, NDEV):
                dev, p = _peer(k)
                self.copies.append(pltpu.make_async_remote_copy(
                    src_ref=src.at[:, pl.ds(p * r, r), :], dst_ref=dst.at[me],
                    send_sem=send_sems.at[a * 7 + k - 1], recv_sem=recv_sems.at[a * 7 + k - 1],
                    device_id=dev, device_id_type=MESH))

    def start(self):
        for cp in self.copies:
            cp.start()

    def forward(self):
        pass

    def finish(self):
        for cp in self.copies:
            cp.wait()


def _land_shape(part):
    a, r, c = part.shape
    return jax.ShapeDtypeStruct((NDEV, a, r // NDEV, c), part.dtype)


ANY = pl.BlockSpec(memory_space=pl.ANY)


class _Ride:
    def __init__(self, kind, srcs):
        self.kind, self.srcs = kind, list(srcs)
        if kind == "gather":
            self.out_shape = [jax.ShapeDtypeStruct((NDEV,) + a.shape, a.dtype) for a in self.srcs]
            self.sems = _gather_sems()
        else:
            self.out_shape = [_land_shape(a) for a in self.srcs]
            self.sems = _scatter_sems(len(self.srcs))

    def exchange(self, ins, outs, sems):
        return (_Gather if self.kind == "gather" else _Scatter)(ins, outs, sems)


def _small_exchange(small, ride, name):
    r, c = small.shape
    nr = len(ride.srcs)

    def body(v_ref, *refs):
        rin, refs = refs[:nr], refs[nr:]
        out_ref, refs = refs[0], refs[1:]
        rout, refs = refs[:nr], refs[nr:]
        send_sems, recv_sems, rsems = refs[0], refs[1], refs[2:]
        big = ride.exchange(rin, rout, rsems)
        big.start()
        me = _my_index()
        out_ref[me] = v_ref[...]
        copies = []
        for k in range(1, NDEV):
            dev, _ = _peer(k)
            copies.append(pltpu.make_async_remote_copy(
                src_ref=v_ref, dst_ref=out_ref.at[me], send_sem=send_sems.at[k - 1], recv_sem=recv_sems.at[k - 1],
                device_id=dev, device_id_type=MESH))
        for cp in copies:
            cp.start()
        for cp in copies:
            cp.wait()
        big.forward()
        big.finish()

    res = pl.pallas_call(body, name=name,
        out_shape=[jax.ShapeDtypeStruct((NDEV, r, c), small.dtype)] + ride.out_shape,
        in_specs=[pl.BlockSpec(memory_space=pltpu.VMEM)] + [ANY] * nr,
        out_specs=[pl.BlockSpec(memory_space=pltpu.VMEM)] + [ANY] * nr,
        scratch_shapes=[pltpu.SemaphoreType.DMA((NDEV - 1,)), pltpu.SemaphoreType.DMA((NDEV - 1,))] + ride.sems,
    )(small, *ride.srcs)
    return res[0], res[1:]


def _call(kernel_body, *, name, grid, in_specs, out_specs, out_shape, args, scratch_shapes=(), ride=None, guest=None):
    n_in, n_out, n_sc = len(in_specs), len(out_specs), len(scratch_shapes)
    nr = len(ride.srcs) if ride else 0
    ng = 0 if guest is None else 1
    in_specs, out_specs, out_shape = list(in_specs), list(out_specs), list(out_shape)
    scratch_shapes, args = list(scratch_shapes), list(args)
    if guest is not None:
        a, b = guest
        rows, f, d = a.shape[0] // grid[0], a.shape[1], b.shape[1]
        in_specs += [pl.BlockSpec((rows, f), lambda i: (i, 0)), pl.BlockSpec((rows, d), lambda i: (i, 0))]
        out_specs += [pl.BlockSpec((1, f, d), lambda i: (0, 0, 0))]
        out_shape += [jax.ShapeDtypeStruct((1, f, d), BF16)]
        args += [a, b]
    if ride is not None:
        in_specs += [ANY] * nr
        out_specs += [ANY] * nr
        out_shape += ride.out_shape
        args += ride.srcs
    scratch_shapes += ([pltpu.VMEM((f, d), F32)] if guest is not None else []) + (ride.sems if ride else [])

    def body(*refs):
        ins, refs = refs[:n_in], refs[n_in:]
        gin, refs = refs[:2 * ng], refs[2 * ng:]
        rin, refs = refs[:nr], refs[nr:]
        outs, refs = refs[:n_out], refs[n_out:]
        gout, refs = refs[:ng], refs[ng:]
        rout, refs = refs[:nr], refs[nr:]
        scratch, refs = refs[:n_sc], refs[n_sc:]
        gacc, rsems = refs[:ng], refs[ng:]
        first, last = True, True
        for axis, extent in enumerate(grid):
            first &= pl.program_id(axis) == 0
            last &= pl.program_id(axis) == extent - 1
        if ride is not None:
            middle = last if len(grid) > 1 else pl.program_id(0) == (3 * grid[0]) // 4
            exchange = ride.exchange(rin, rout, rsems)
            pl.when(first)(exchange.start)
            pl.when(middle)(exchange.forward)
        if guest is not None:
            @pl.when(first)
            def _():
                gacc[0][...] = jnp.zeros_like(gacc[0])

            gacc[0][...] += _tn(gin[0][...], gin[1][...])
        kernel_body(*ins, *outs, *scratch)
        if guest is not None:
            @pl.when(last)
            def _():
                gout[0][0] = gacc[0][...].astype(BF16)
        if ride is not None:
            pl.when(last)(exchange.finish)

    res = pl.pallas_call(body, name=name, grid=grid, in_specs=in_specs, out_specs=out_specs, out_shape=out_shape,
                         scratch_shapes=scratch_shapes, compiler_params=_params(*(("arbitrary",) * len(grid))))(*args)
    return res[:n_out], res[n_out:]


def _ada_forward(c_all, ada_w, bias):
    nl, d, ncol = ada_w.shape

    def body(c_ref, w_ref, b_ref, o_ref):
        cv = c_ref[...]
        ca = cv * jax.nn.sigmoid(cv)
        o_ref[0] = jnp.dot(ca, w_ref[0], preferred_element_type=F32, precision=lax.Precision.HIGHEST) + b_ref[0]

    return pl.pallas_call(body, name="ada_forward", grid=(nl,),
        in_specs=[pl.BlockSpec((NDEV, d), lambda i: (0, 0)), pl.BlockSpec((1, d, ncol), lambda i: (i, 0, 0)),
                  pl.BlockSpec((1, 1, ncol), lambda i: (i, 0, 0))],
        out_specs=pl.BlockSpec((1, NDEV, ncol), lambda i: (i, 0, 0)),
        out_shape=jax.ShapeDtypeStruct((nl, NDEV, ncol), F32),
        compiler_params=_params("arbitrary"),
    )(c_all, ada_w, bias)


def _ada_wgrad(c_all, dmod):
    nl, _, ncol = dmod.shape
    d = c_all.shape[1]
    bd = min(d, 256)

    def body(c_ref, dm_ref, o_ref):
        cv = c_ref[...]
        ca = cv * jax.nn.sigmoid(cv)
        o_ref[0] = lax.dot_general(ca, dm_ref[0], (((0,), (0,)), ((), ())), preferred_element_type=F32,
                                   precision=lax.Precision.HIGHEST)

    return pl.pallas_call(body, name="ada_wgrad", grid=(nl, d // bd),
        in_specs=[pl.BlockSpec((NDEV, bd), lambda i, j: (0, j)), pl.BlockSpec((1, NDEV, ncol), lambda i, j: (i, 0, 0))],
        out_specs=pl.BlockSpec((1, bd, ncol), lambda i, j: (i, j, 0)),
        out_shape=jax.ShapeDtypeStruct((nl, d, ncol), F32),
        compiler_params=_params("arbitrary", "arbitrary"),
    )(c_all, dmod)


def _row_block(r, c, bytes_per_row_elem=4, budget=2 * 2**20):
    if r * c * bytes_per_row_elem <= budget or r % 8:
        return r
    best = 8
    for b in range(8, r + 1, 8):
        if r % b == 0 and b * c * bytes_per_row_elem <= budget:
            best = b
    return best


def _sum_slots(land, name):
    _, r, c = land.shape
    br = _row_block(r, c, 8 * land.dtype.itemsize)

    def body(l_ref, o_ref):
        acc = l_ref[0].astype(F32)
        for s in range(1, NDEV):
            acc = acc + l_ref[s].astype(F32)
        o_ref[...] = acc

    return pl.pallas_call(body, name=name, grid=(r // br,),
        in_specs=[pl.BlockSpec((NDEV, br, c), lambda i: (0, i, 0))],
        out_specs=pl.BlockSpec((br, c), lambda i: (i, 0)),
        out_shape=jax.ShapeDtypeStruct((r, c), F32),
        compiler_params=_params("arbitrary"),
    )(land)


def _adamw(w, g, m, v, name):
    r, c = w.shape
    br = _row_block(r, c)

    def body(w_ref, g_ref, m_ref, v_ref, d_ref, mo_ref, vo_ref):
        d_ref[...], mo_ref[...], vo_ref[...] = _adam_math(w_ref[...], g_ref[...], m_ref[...], v_ref[...])

    spec = pl.BlockSpec((br, c), lambda i: (i, 0))
    return pl.pallas_call(body, name=name, grid=(r // br,),
        in_specs=[spec] * 4, out_specs=[spec] * 3,
        out_shape=[jax.ShapeDtypeStruct((r, c), F32)] * 3,
        compiler_params=_params("arbitrary"),
    )(w, g, m, v)


def _adam_math(w, g, m, v):
    m2 = ADAM_B1 * m + (1.0 - ADAM_B1) * g
    v2 = ADAM_B2 * v + (1.0 - ADAM_B2) * (g * g)
    m_hat = m2 / (1.0 - ADAM_B1 ** ADAM_STEP)
    v_hat = v2 / (1.0 - ADAM_B2 ** ADAM_STEP)
    return -ADAM_LR * (m_hat / (jnp.sqrt(v_hat) + ADAM_EPS) + ADAM_WD * w), m2, v2


def _slot_sum(land_ref, *lead):
    acc = land_ref[(0,) + lead].astype(F32)
    for s in range(1, NDEV):
        acc = acc + land_ref[(s,) + lead].astype(F32)
    return acc


def _finalize(land, w, m, v, transposed, name):
    _, r, c = land.shape
    cb = 256 if (transposed and c % 256 == 0) else c
    wblk = pl.BlockSpec((cb, r), lambda i: (i, 0)) if transposed else pl.BlockSpec((r, cb), lambda i: (0, i))

    def body(l_ref, w_ref, m_ref, v_ref, g_ref, d_ref, mo_ref, vo_ref):
        g = _slot_sum(l_ref)
        g = g.T if transposed else g
        g_ref[...] = g
        d_ref[...], mo_ref[...], vo_ref[...] = _adam_math(w_ref[...], g, m_ref[...], v_ref[...])

    return pl.pallas_call(body, name=name, grid=(c // cb,),
        in_specs=[pl.BlockSpec((NDEV, r, cb), lambda i: (0, 0, i)), wblk, wblk, wblk], out_specs=[wblk] * 4,
        out_shape=[jax.ShapeDtypeStruct(w.shape, F32)] * 4,
        compiler_params=_params("arbitrary"),
    )(land, w, m, v)


def _finalize_ffn(lands, wg, wu, wd, mg, mu, md, vg, vu, vd):
    nl, d, fs = wg.shape
    db = min(256, d)

    def kernel_body(g0_ref, ud0_ref, g1_ref, ud1_ref, wg_ref, wu_ref, wd_ref, mg_ref, mu_ref, md_ref,
                    vg_ref, vu_ref, vd_ref, *outs):
        layer = pl.program_id(0)
        triples = [(wg_ref, mg_ref, vg_ref, True), (wu_ref, mu_ref, vu_ref, True), (wd_ref, md_ref, vd_ref, False)]

        def run(gate_ref, updown_ref):
            sources = [(gate_ref, 0), (updown_ref, 0), (updown_ref, 1)]
            for j, ((w_ref, m_ref, v_ref, transposed), (land_ref, k)) in enumerate(zip(triples, sources)):
                g = _slot_sum(land_ref, k)
                g = g.T if transposed else g
                delta, m2, v2 = _adam_math(w_ref[0], g, m_ref[0], v_ref[0])
                for o_ref, val in zip(outs[j::3], (g, delta, m2, v2)):
                    o_ref[0] = val

        @pl.when(layer == 0)
        def _():
            run(g0_ref, ud0_ref)

        @pl.when(layer == 1)
        def _():
            run(g1_ref, ud1_ref)

    tblk = pl.BlockSpec((1, db, fs), lambda l, i: (l, i, 0))
    pblk = pl.BlockSpec((1, fs, db), lambda l, i: (l, 0, i))
    lblk = [pl.BlockSpec((NDEV, 1, fs, db), lambda l, i: (0, 0, 0, i * (1 - l))),
            pl.BlockSpec((NDEV, 2, fs, db), lambda l, i: (0, 0, 0, i * (1 - l))),
            pl.BlockSpec((NDEV, 1, fs, db), lambda l, i: (0, 0, 0, i * l)),
            pl.BlockSpec((NDEV, 2, fs, db), lambda l, i: (0, 0, 0, i * l))]
    shapes = [jax.ShapeDtypeStruct(a.shape, F32) for a in (wg, wu, wd)]
    outs, _ = _call(kernel_body, name="finalize_ffn", grid=(nl, d // db),
        in_specs=lblk + [tblk, tblk, pblk] * 3, out_specs=[tblk, tblk, pblk] * 4, out_shape=shapes * 4,
        args=(*lands[0], *lands[1], wg, wu, wd, mg, mu, md, vg, vu, vd))
    return outs


CONV_ROWS = 64
LANES = 128


def _shifted_copies(buf, sh, n):
    sh[0] = buf[...]
    for r in range(1, 8):
        sh[r, pl.ds(0, n - 8), :] = buf[pl.ds(r, n - 8), :]


def _window(sh, o, rows, cols):
    return sh[o % 8, pl.ds(o - o % 8, rows), cols]


def _conv_in(x, mod, gmix, gw, w1_rows, w1_idx, b1, tm, ride):
    s, d = x.shape

    def kernel_body(x_ref, mod_ref, g_ref, w_ref, b_ref, h_ref, u_ref, glu_ref):
        n, _ = _rms(x_ref[...])
        h = (n * g_ref[...]) * (1.0 + mod_ref[1:2, :]) + mod_ref[0:1, :]
        hb = h.astype(BF16)
        h_ref[...] = hb
        u = _nt(hb, w_ref[...].reshape(NDEV * w1_rows, d)) + b_ref[...]
        u_ref[...] = u
        glu_ref[...] = u[:, :d] * jax.nn.sigmoid(u[:, d:])

    tile = pl.BlockSpec((tm, d), lambda i: (i, 0))
    return _call(kernel_body, name="conv_in", grid=(s // tm,),
        in_specs=[tile, _row(8, d), _row(1, d), _weight_spec(w1_rows, d, w1_idx), _row(1, 2 * d)],
        out_specs=[tile, pl.BlockSpec((tm, 2 * d), lambda i: (i, 0)), tile],
        out_shape=[jax.ShapeDtypeStruct((s, d), BF16), jax.ShapeDtypeStruct((s, 2 * d), F32),
                   jax.ShapeDtypeStruct((s, d), F32)],
        args=(x, mod, gmix, gw, b1), ride=ride)


def _conv_mid(glu, wdw, bdw, ln_g, ln_b, gw, w2_rows, w2_idx, b2, x, mod, tm, ride):
    s, d = x.shape
    off = CONV_HALO - (CONV_WIDTH - 1)
    rc = min(CONV_ROWS, tm)

    def kernel_body(glu_ref, halo_ref, wdw_ref, bdw_ref, lng_ref, lnb_ref, w_ref, b2_ref, x_ref, mod_ref,
                    dwc_ref, s_ref, y_ref, x1_ref, buf, sh):
        i = pl.program_id(0)
        buf[pl.ds(0, CONV_HALO), :] = jnp.where(i > 0, halo_ref[...], 0.0)
        buf[pl.ds(CONV_HALO, tm), :] = glu_ref[...]
        _shifted_copies(buf, sh, CONV_HALO + tm)
        for cb in range(d // LANES):
            cols = pl.ds(cb * LANES, LANES)
            taps = wdw_ref[:, cols]
            for r in range(tm // rc):
                part = jnp.zeros((rc, LANES), F32) + bdw_ref[:, cols]
                for k in range(CONV_WIDTH):
                    part = part + _window(sh, r * rc + off + k, rc, cols) * taps[k:k + 1, :]
                dwc_ref[pl.ds(r * rc, rc), cols] = part
        acc = dwc_ref[...]
        mu = jnp.mean(acc, axis=-1, keepdims=True)
        xc = acc - mu
        rstd = lax.rsqrt(jnp.mean(xc * xc, axis=-1, keepdims=True) + EPS)
        ln = (xc * rstd) * lng_ref[...] + lnb_ref[...]
        sb = (ln * jax.nn.sigmoid(ln)).astype(BF16)
        s_ref[...] = sb
        y = _nn(sb, w_ref[...].reshape(NDEV * w2_rows, d)) + b2_ref[...]
        y_ref[...] = y.astype(BF16)
        x1_ref[...] = x_ref[...] + (1.0 + mod_ref[2:3, :]) * y

    tile = pl.BlockSpec((tm, d), lambda i: (i, 0))
    halo = pl.BlockSpec((CONV_HALO, d), lambda i: (jnp.maximum(i * (tm // CONV_HALO) - 1, 0), 0))
    return _call(kernel_body, name="conv_mid", grid=(s // tm,),
        in_specs=[tile, halo, _row(CONV_WIDTH, d), _row(1, d), _row(1, d), _row(1, d),
                  _weight_spec(w2_rows, d, w2_idx), _row(1, d), tile, _row(8, d)],
        out_specs=[tile, tile, tile, tile],
        out_shape=[jax.ShapeDtypeStruct((s, d), F32), jax.ShapeDtypeStruct((s, d), BF16),
                   jax.ShapeDtypeStruct((s, d), BF16), jax.ShapeDtypeStruct((s, d), F32)],
        scratch_shapes=[pltpu.VMEM((CONV_HALO + tm, d), F32), pltpu.VMEM((8, CONV_HALO + tm, d), F32)],
        args=(glu, glu, wdw, bdw, ln_g, ln_b, gw, b2, x, mod), ride=ride)


def _ffn_fwd(x, mod, gffn, weights, fs, f, tm, name, ride=None, loss=None):
    s, d = x.shape

    last = s // tm - 1

    def kernel_body(x_ref, mod_ref, g_ref, wg_ref, wu_ref, wd_ref, *rest):
        xv = x_ref[...]
        n, _ = _rms(xv)
        hb = ((n * g_ref[...]) * (1.0 + mod_ref[4:5, :]) + mod_ref[3:4, :]).astype(BF16)
        gg = _nt(hb, wg_ref[...].reshape(f, d))
        uu = _nt(hb, wu_ref[...].reshape(f, d))
        ab = ((gg * jax.nn.sigmoid(gg)) * uu).astype(BF16)
        y = _nn(ab, wd_ref[...].reshape(f, d))
        xo = xv + (1.0 + mod_ref[5:6, :]) * y
        if loss is None:
            h_ref, gg_ref, uu_ref, y_ref, xo_ref = rest
            xo_ref[...] = xo
        else:
            t_ref, gfin_ref, h_ref, gg_ref, uu_ref, y_ref, dx_ref, st_ref = rest
            dx_ref[...] = _loss_tile(xo, t_ref[...], gfin_ref[...], st_ref, pl.program_id(0), last)
        h_ref[...] = hb
        gg_ref[...] = gg.astype(BF16)
        uu_ref[...] = uu.astype(BF16)
        y_ref[...] = y.astype(BF16)

    tile = pl.BlockSpec((tm, d), lambda i: (i, 0))
    wide = pl.BlockSpec((tm, f), lambda i: (i, 0))
    extra_in = [] if loss is None else [tile, _row(1, d)]
    extra_out = [] if loss is None else [_row(8, d)]
    return _call(kernel_body, name=name, grid=(s // tm,),
        in_specs=[tile, _row(8, d), _row(1, d)] + [_weight_spec(fs, d, idx) for _, idx in weights] + extra_in,
        out_specs=[tile, wide, wide, tile, tile] + extra_out,
        out_shape=[jax.ShapeDtypeStruct((s, d), BF16), jax.ShapeDtypeStruct((s, f), BF16),
                   jax.ShapeDtypeStruct((s, f), BF16), jax.ShapeDtypeStruct((s, d), BF16),
                   jax.ShapeDtypeStruct((s, d), F32)] + [jax.ShapeDtypeStruct((8, d), F32)] * len(extra_out),
        args=(x, mod, gffn) + tuple(a for a, _ in weights) + (() if loss is None else tuple(loss)), ride=ride)


def _loss_tile(x, target, g, st_ref, i, last):
    d = x.shape[1]

    @pl.when(i == 0)
    def _():
        st_ref[...] = jnp.zeros_like(st_ref)

    n, rinv = _rms(x)
    err = n * g - target
    dy = err * (1.0 / d)
    st_ref[0:1, :] += _sum0(dy * n)
    st_ref[1:2, :] += _sum0(err * err) * (0.5 / d)

    @pl.when(i == last)
    def _():
        st_ref[2:3, :] = jnp.zeros((1, d), F32) + jnp.sum(st_ref[1:2, :], axis=-1, keepdims=True)

    dn = dy * g
    return rinv * (dn - n * jnp.mean(dn * n, axis=-1, keepdims=True))


def _pool_fwd(x, mod, gmix, pw, ls, tm, ride):
    s, d = x.shape
    dg = d // len(POOL_WINDOWS)

    def body(x_ref, halo_ref, mod_ref, g_ref, pw_ref, ls_ref, mixed_ref, yp_ref, xo_ref, buf):
        i = pl.program_id(0)

        def hfun(xv):
            n, _ = _rms(xv)
            return (n * g_ref[...]) * (1.0 + mod_ref[1:2, :]) + mod_ref[0:1, :]

        xv = x_ref[...]
        h = hfun(xv)
        buf[pl.ds(0, POOL_HALO), :] = jnp.where(i > 0, hfun(halo_ref[...]), 0.0)
        buf[pl.ds(POOL_HALO, tm), :] = h
        t = i * tm + lax.broadcasted_iota(jnp.int32, (tm, 1), 0)
        gate = 1.0 + mod_ref[2:3, :]
        for gi, w in enumerate(POOL_WINDOWS):
            cols = pl.ds(gi * dg, dg)
            ws = buf[pl.ds(POOL_HALO, tm), cols]
            for j in range(1, w):
                ws = ws + buf[pl.ds(POOL_HALO - j, tm), cols]
            inv = 1.0 / jnp.minimum(t + 1, w).astype(F32)
            mb = (ws * inv - h[:, gi * dg:(gi + 1) * dg]).astype(BF16)
            mixed_ref[:, cols] = mb
            yp = _nn(mb, pw_ref[gi])
            yp_ref[:, cols] = yp.astype(BF16)
            xo_ref[:, cols] = xv[:, gi * dg:(gi + 1) * dg] + gate[:, gi * dg:(gi + 1) * dg] * (yp * ls_ref[:, cols])

    tile = pl.BlockSpec((tm, d), lambda i: (i, 0))
    halo = pl.BlockSpec((POOL_HALO, d), lambda i: (jnp.maximum(i * (tm // POOL_HALO) - 1, 0), 0))
    return _call(body, name="pool_fwd", grid=(s // tm,),
        in_specs=[tile, halo, _row(8, d), _row(1, d), pl.BlockSpec((len(POOL_WINDOWS), dg, dg), lambda i: (0, 0, 0)),
                  _row(1, d)],
        out_specs=[tile, tile, tile],
        out_shape=[jax.ShapeDtypeStruct((s, d), BF16), jax.ShapeDtypeStruct((s, d), BF16),
                   jax.ShapeDtypeStruct((s, d), F32)],
        scratch_shapes=[pltpu.VMEM((POOL_HALO + tm, d), F32)],
        args=(x, x, mod, gmix, pw, ls), ride=ride)


def _ffn_bwd(dxo, x, gg, uu, y, mod, gffn, weights, fs, f, tm, name, ride=None):
    s, d = x.shape

    def kernel_body(dxo_ref, x_ref, gg_ref, uu_ref, y_ref, mod_ref, g_ref, wg_ref, wu_ref, wd_ref,
                    dg_ref, du_ref, a_ref, dy_ref, dxi_ref, st_ref):
        @pl.when(pl.program_id(0) == 0)
        def _():
            st_ref[...] = jnp.zeros_like(st_ref)

        dxo_v = dxo_ref[...]
        dyb = (dxo_v * (1.0 + mod_ref[5:6, :])).astype(BF16)
        dy_ref[...] = dyb
        da = _nt(dyb, wd_ref[...].reshape(f, d))
        ggv, uuv = gg_ref[...].astype(F32), uu_ref[...].astype(F32)
        sg = jax.nn.sigmoid(ggv)
        silu = ggv * sg
        a_ref[...] = (silu * uuv).astype(BF16)
        dub = (da * silu).astype(BF16)
        dgb = (da * uuv * _silu_grad(ggv, sg)).astype(BF16)
        du_ref[...] = dub
        dg_ref[...] = dgb
        dh = _nn(dgb, wg_ref[...].reshape(f, d)) + _nn(dub, wu_ref[...].reshape(f, d))
        n, rinv = _rms(x_ref[...])
        dx, dsh, dsc, dgain = _rms_mod_bwd(dh, n, rinv, g_ref[...], mod_ref[4:5, :])
        dxi_ref[...] = dxo_v + dx
        st_ref[0:1, :] += dsh
        st_ref[1:2, :] += dsc
        st_ref[2:3, :] += _sum0(dxo_v * y_ref[...].astype(F32))
        st_ref[3:4, :] += dgain

    tile = pl.BlockSpec((tm, d), lambda i: (i, 0))
    wide = pl.BlockSpec((tm, f), lambda i: (i, 0))
    return _call(kernel_body, name=name, grid=(s // tm,),
        in_specs=[tile, tile, wide, wide, tile, _row(8, d), _row(1, d)] + [_weight_spec(fs, d, idx) for _, idx in weights],
        out_specs=[wide, wide, wide, tile, tile, _row(8, d)],
        out_shape=[jax.ShapeDtypeStruct((s, f), BF16)] * 3 + [jax.ShapeDtypeStruct((s, d), BF16),
                   jax.ShapeDtypeStruct((s, d), F32), jax.ShapeDtypeStruct((8, d), F32)],
        args=(dxo, x, gg, uu, y, mod, gffn) + tuple(a for a, _ in weights), ride=ride)


def _ffn_wgrad(dub, ab, h, dyb, fb, ts, name):
    s, f = dub.shape
    d = h.shape[1]
    last = s // ts - 1

    def body(du_ref, a_ref, h_ref, dy_ref, o_ref, acc):
        t = pl.program_id(1)

        @pl.when(t == 0)
        def _():
            acc[...] = jnp.zeros_like(acc)

        acc[0] += _tn(du_ref[...], h_ref[...])
        acc[1] += _tn(a_ref[...], dy_ref[...])

        @pl.when(t == last)
        def _():
            o_ref[...] = acc[...].astype(BF16)

    wide = pl.BlockSpec((ts, fb), lambda j, t: (t, j))
    tile = pl.BlockSpec((ts, d), lambda j, t: (t, 0))
    return pl.pallas_call(body, name=name, grid=(f // fb, s // ts),
        in_specs=[wide, wide, tile, tile],
        out_specs=pl.BlockSpec((2, fb, d), lambda j, t: (0, j, 0)),
        out_shape=jax.ShapeDtypeStruct((2, f, d), BF16),
        scratch_shapes=[pltpu.VMEM((2, fb, d), F32)],
        compiler_params=_params("arbitrary", "arbitrary"),
    )(dub, ab, h, dyb)


def _pool_bwd(dxo, x, yp, mixed, mod, gmix, pw, ls, tm, guest):
    s, d = x.shape
    ng = len(POOL_WINDOWS)
    dg = d // ng
    last = s // tm - 1

    def body(dxo_ref, dxh_ref, x_ref, yp_ref, mixed_ref, mod_ref, g_ref, pw_ref, ls_ref,
             dxi_ref, dpw_ref, st_ref, bufy, bufq, bufh, acc):
        i = pl.program_id(0)

        @pl.when(i == 0)
        def _():
            st_ref[...] = jnp.zeros_like(st_ref)
            acc[...] = jnp.zeros_like(acc)

        gate = 1.0 + mod_ref[2:3, :]
        lsv = ls_ref[...]
        dxo_v = dxo_ref[...]
        st_ref[2:3, :] += _sum0(dxo_v * yp_ref[...].astype(F32))
        bufy[pl.ds(0, tm), :] = (dxo_v * (gate * lsv)).astype(BF16)
        bufy[pl.ds(tm, POOL_HALO), :] = jnp.where(i < last, dxh_ref[...] * (gate * lsv), 0.0).astype(BF16)
        t = i * tm + lax.broadcasted_iota(jnp.int32, (tm + POOL_HALO, 1), 0)
        for gi, w in enumerate(POOL_WINDOWS):
            cols = pl.ds(gi * dg, dg)
            dm = _nt(bufy[:, cols], pw_ref[gi])
            bufq[:, cols] = dm * (1.0 / jnp.minimum(t + 1, w).astype(F32))
            dh = bufq[pl.ds(0, tm), cols] - dm[0:tm, :]
            for j in range(1, w):
                dh = dh + bufq[pl.ds(j, tm), cols]
            bufh[:, cols] = dh
            acc[gi] += _tn(mixed_ref[:, cols], bufy[pl.ds(0, tm), cols])
        n, rinv = _rms(x_ref[...])
        dx, dsh, dsc, dgain = _rms_mod_bwd(bufh[...], n, rinv, g_ref[...], mod_ref[1:2, :])
        dxi_ref[...] = dxo_v + dx
        st_ref[0:1, :] += dsh
        st_ref[1:2, :] += dsc
        st_ref[3:4, :] += dgain

        @pl.when(i == last)
        def _():
            r = st_ref[2:3, :]
            st_ref[4:5, :] = r * lsv
            st_ref[5:6, :] = r * gate
            dpw_ref[...] = acc[...].astype(BF16)

    tile = pl.BlockSpec((tm, d), lambda i: (i, 0))
    nxt = pl.BlockSpec((POOL_HALO, d), lambda i: (jnp.minimum((i + 1) * (tm // POOL_HALO), s // POOL_HALO - 1), 0))
    pws = pl.BlockSpec((ng, dg, dg), lambda i: (0, 0, 0))
    return _call(body, name="pool_bwd", grid=(s // tm,),
        in_specs=[tile, nxt, tile, tile, tile, _row(8, d), _row(1, d), pws, _row(1, d)],
        out_specs=[tile, pws, _row(8, d)],
        out_shape=[jax.ShapeDtypeStruct((s, d), F32), jax.ShapeDtypeStruct((ng, dg, dg), BF16),
                   jax.ShapeDtypeStruct((8, d), F32)],
        scratch_shapes=[pltpu.VMEM((tm + POOL_HALO, d), BF16), pltpu.VMEM((tm + POOL_HALO, d), F32),
                        pltpu.VMEM((tm, d), F32), pltpu.VMEM((ng, dg, dg), F32)],
        args=(dxo, dxo, x, yp, mixed, mod, gmix, pw, ls), guest=guest)


def _conv_bwd_mid(dxo, y, dwc, sb, mod, ln_g, ln_b, gw, w2_rows, w2_idx, tm, guest):
    s, d = dwc.shape
    last = s // tm - 1

    def body(dxo_ref, y_ref, dwc_ref, s_ref, mod_ref, lng_ref, lnb_ref, w_ref, dd_ref, dw_ref, st_ref, acc):
        i = pl.program_id(0)

        @pl.when(i == 0)
        def _():
            st_ref[...] = jnp.zeros_like(st_ref)
            acc[...] = jnp.zeros_like(acc)

        dxo_v = dxo_ref[...]
        st_ref[0:1, :] += _sum0(dxo_v * y_ref[...].astype(F32))
        dy = dxo_v * (1.0 + mod_ref[2:3, :])
        st_ref[1:2, :] += _sum0(dy)
        dyb = dy.astype(BF16)
        ds = _nt(dyb, w_ref[...].reshape(NDEV * w2_rows, d))
        acc[...] += _tn(s_ref[...], dyb)
        v = dwc_ref[...]
        mu = jnp.mean(v, axis=-1, keepdims=True)
        xc = v - mu
        rstd = lax.rsqrt(jnp.mean(xc * xc, axis=-1, keepdims=True) + EPS)
        xhat = xc * rstd
        ln = xhat * lng_ref[...] + lnb_ref[...]
        dln = ds * _silu_grad(ln, jax.nn.sigmoid(ln))
        st_ref[2:3, :] += _sum0(dln * xhat)
        st_ref[3:4, :] += _sum0(dln)
        dxh = dln * lng_ref[...]
        dd = rstd * (dxh - jnp.mean(dxh, axis=-1, keepdims=True) - xhat * jnp.mean(dxh * xhat, axis=-1, keepdims=True))
        dd_ref[...] = dd
        st_ref[4:5, :] += _sum0(dd)

        @pl.when(i == last)
        def _():
            dw_ref[...] = acc[...].astype(BF16)

    tile = pl.BlockSpec((tm, d), lambda i: (i, 0))
    return _call(body, name="conv_bwd_mid", grid=(s // tm,),
        in_specs=[tile, tile, tile, tile, _row(8, d), _row(1, d), _row(1, d), _weight_spec(w2_rows, d, w2_idx)],
        out_specs=[tile, pl.BlockSpec((d, d), lambda i: (0, 0)), _row(8, d)],
        out_shape=[jax.ShapeDtypeStruct((s, d), F32), jax.ShapeDtypeStruct((d, d), BF16),
                   jax.ShapeDtypeStruct((8, d), F32)],
        scratch_shapes=[pltpu.VMEM((d, d), F32)],
        args=(dxo, y, dwc, sb, mod, ln_g, ln_b, gw), guest=guest)


def _conv_bwd_in(dd, glu, u, hb, x, dxo, wdw, gw, w1_rows, w1_idx, mod, gmix, tm, ride):
    s, d = x.shape
    last = s // tm - 1
    off = CONV_HALO - (CONV_WIDTH - 1)
    rw = 32
    tap_group = 16

    def kernel_body(dd_ref, ddn_ref, glu_ref, glp_ref, u_ref, h_ref, x_ref, dxo_ref, wdw_ref, w_ref, mod_ref, g_ref,
                    dxi_ref, dw_ref, dwdw_ref, st_ref, bufd, bufg, shd, shg, dgl, accw, acc):
        i = pl.program_id(0)

        @pl.when(i == 0)
        def _():
            st_ref[...] = jnp.zeros_like(st_ref)
            accw[...] = jnp.zeros_like(accw)
            acc[...] = jnp.zeros_like(acc)

        bufd[pl.ds(0, tm), :] = dd_ref[...]
        bufd[pl.ds(tm, CONV_HALO), :] = jnp.where(i < last, ddn_ref[...], 0.0)
        bufg[pl.ds(0, CONV_HALO), :] = jnp.where(i > 0, glp_ref[...], 0.0)
        bufg[pl.ds(CONV_HALO, tm), :] = glu_ref[...]
        _shifted_copies(bufd, shd, tm + CONV_HALO)
        _shifted_copies(bufg, shg, CONV_HALO + tm)
        for cb in range(d // LANES):
            cols = pl.ds(cb * LANES, LANES)
            taps = wdw_ref[:, cols]
            for r in range(tm // rw):
                part = jnp.zeros((rw, LANES), F32)
                for k in range(CONV_WIDTH):
                    part = part + _window(shd, r * rw + CONV_WIDTH - 1 - k, rw, cols) * taps[k:k + 1, :]
                dgl[pl.ds(r * rw, rw), cols] = part
            for k0 in range(0, CONV_WIDTH, tap_group):
                group = range(k0, min(CONV_WIDTH, k0 + tap_group))
                sums = {k: jnp.zeros((8, LANES), F32) for k in group}
                for r in range(tm // rw):
                    ddc = bufd[pl.ds(r * rw, rw), cols]
                    for k in group:
                        p = _window(shg, r * rw + off + k, rw, cols) * ddc
                        for q in range(rw // 8):
                            sums[k] = sums[k] + p[q * 8:(q + 1) * 8, :]
                for k in group:
                    accw[k, :, cols] += sums[k]
        dglu = dgl[...]
        uv = u_ref[...]
        a, g = uv[:, :d], uv[:, d:]
        sg = jax.nn.sigmoid(g)
        da = dglu * sg
        dgt = dglu * a * (sg * (1.0 - sg))
        du = jnp.concatenate([da, dgt], axis=1)
        st_ref[0:1, :] += _sum0(du)
        dub = du.astype(BF16)
        w = w_ref[...].reshape(NDEV * w1_rows, d)
        dh = _nn(dub, w)
        acc[...] += _tn(dub, h_ref[...])
        n, rinv = _rms(x_ref[...])
        dx, dsh, dsc, dgain = _rms_mod_bwd(dh, n, rinv, g_ref[...], mod_ref[1:2, :])
        dxi_ref[...] = dxo_ref[...] + dx
        st_ref[1:2, 0:d] += dsh
        st_ref[2:3, 0:d] += dsc
        st_ref[3:4, 0:d] += dgain

        @pl.when(i == last)
        def _():
            dw_ref[...] = acc[...].astype(BF16)
            dwdw_ref[...] = jnp.sum(accw[...], axis=1)

    tile = pl.BlockSpec((tm, d), lambda i: (i, 0))
    prv = pl.BlockSpec((CONV_HALO, d), lambda i: (jnp.maximum(i * (tm // CONV_HALO) - 1, 0), 0))
    nxt = pl.BlockSpec((CONV_HALO, d), lambda i: (jnp.minimum((i + 1) * (tm // CONV_HALO), s // CONV_HALO - 1), 0))
    return _call(kernel_body, name="conv_bwd_in", grid=(s // tm,),
        in_specs=[tile, nxt, tile, prv, pl.BlockSpec((tm, 2 * d), lambda i: (i, 0)), tile, tile, tile,
                  _row(CONV_WIDTH, d), _weight_spec(w1_rows, d, w1_idx), _row(8, d), _row(1, d)],
        out_specs=[tile, pl.BlockSpec((2 * d, d), lambda i: (0, 0)), _row(CONV_HALO, d), _row(8, 2 * d)],
        out_shape=[jax.ShapeDtypeStruct((s, d), F32), jax.ShapeDtypeStruct((2 * d, d), BF16),
                   jax.ShapeDtypeStruct((CONV_HALO, d), F32), jax.ShapeDtypeStruct((8, 2 * d), F32)],
        scratch_shapes=[pltpu.VMEM((tm + CONV_HALO, d), F32), pltpu.VMEM((CONV_HALO + tm, d), F32),
                        pltpu.VMEM((8, tm + CONV_HALO, d), F32), pltpu.VMEM((8, CONV_HALO + tm, d), F32),
                        pltpu.VMEM((tm, d), F32), pltpu.VMEM((CONV_HALO, 8, d), F32), pltpu.VMEM((2 * d, d), F32)],
        args=(dd, dd, glu, glu, u, hb, x, dxo, wdw, gw, mod, gmix), ride=ride)


def kernel(x, c, ada_w, ada_b, norm_mix_g, norm_ffn_g, conv_w1, conv_b1, conv_wdw, conv_bdw, conv_ln_g, conv_ln_b, conv_w2, conv_b2, pool_w, pool_ls, ffn_w_gate, ffn_w_up, ffn_w_down, final_g, loss_target, m_ada_w, m_ada_b, m_norm_mix_g, m_norm_ffn_g, m_conv_w1, m_conv_b1, m_conv_wdw, m_conv_bdw, m_conv_ln_g, m_conv_ln_b, m_conv_w2, m_conv_b2, m_pool_w, m_pool_ls, m_ffn_w_gate, m_ffn_w_up, m_ffn_w_down, m_final_g, v_ada_w, v_ada_b, v_norm_mix_g, v_norm_ffn_g, v_conv_w1, v_conv_b1, v_conv_wdw, v_conv_bdw, v_conv_ln_g, v_conv_ln_b, v_conv_w2, v_conv_b2, v_pool_w, v_pool_ls, v_ffn_w_gate, v_ffn_w_up, v_ffn_w_down, v_final_g):
    _, s, d = x.shape
    f = ffn_w_down.shape[1] * NDEV
    fs = f // NDEV
    r1, r2 = 2 * d // NDEV, d // NDEV
    ng = len(POOL_WINDOWS)
    dg = d // ng
    pr = ng * (dg // NDEV) * dg // d
    ncol = ada_w.shape[2]
    dc = d // NDEV
    tm = min(256, s)
    me = _my_index()
    x0 = x.reshape(s, d)
    target = loss_target.reshape(s, d)

    small = jnp.concatenate([c.reshape(NDEV, dc), conv_wdw[0], pool_ls], axis=0)
    shard_a = jnp.concatenate([conv_w1[0].T, conv_w2[0]], axis=0).astype(BF16)
    small_all, (gwa,) = _small_exchange(small, _Ride("gather", [shard_a]), "allgather_first")
    c_all = small_all[:, 0:NDEV, :].reshape(NDEV, d)
    wdw = small_all[:, NDEV:NDEV + CONV_WIDTH, :].transpose(1, 0, 2).reshape(CONV_WIDTH, d)
    ls = small_all[:, NDEV + CONV_WIDTH, :].reshape(1, d)
    bias = lax.dynamic_slice_in_dim(ada_b, me * ncol, ncol, axis=1)[:, None, :]
    mod_cols = _ada_forward(c_all, ada_w, bias)
    mod_all = _allgather_small(mod_cols.reshape(2 * NDEV, ncol), "allgather_mod")
    mod_mine = lax.dynamic_index_in_dim(mod_all.reshape(NDEV, 2, NDEV, ncol), me, axis=2, keepdims=False)
    mod = mod_mine.transpose(1, 0, 2).reshape(2, 6, d)
    mod = jnp.concatenate([mod, jnp.zeros((2, 2, d), F32)], axis=1)

    shard_b1 = ffn_w_gate[0].T.astype(BF16)
    shard_b2 = jnp.concatenate([ffn_w_up[0].T, ffn_w_down[0]], axis=0).astype(BF16)
    shard_c = jnp.concatenate([ffn_w_gate[1].T, ffn_w_up[1].T, pool_w.reshape(pr, d)], axis=0).astype(BF16)
    shard_d = ffn_w_down[1].astype(BF16)
    w1_at, w2_at = (gwa, 0), (gwa, r1 // r2)

    (h0, u, glu), (gwb1,) = _conv_in(x0, mod[0], norm_mix_g[0:1], w1_at[0], r1, w1_at[1], conv_b1, tm,
                                    _Ride("gather", [shard_b1]))
    (dwc, sb, y0, x1), (gwb2,) = _conv_mid(glu, wdw, conv_bdw, conv_ln_g, conv_ln_b, w2_at[0], r2, w2_at[1], conv_b2,
                                           x0, mod[0], tm, _Ride("gather", [shard_b2]))
    ffn0_w = [(gwb1, 0), (gwb2, 0), (gwb2, 1)]
    (h1, gg0, uu0, yf0, x2), (gwc,) = _ffn_fwd(x1, mod[0], norm_ffn_g[0:1], ffn0_w, fs, f, tm, "ffn_fwd0",
                                               _Ride("gather", [shard_c]))
    pw = gwc[:, 2 * fs:2 * fs + pr, :].reshape(NDEV, ng, dg // NDEV, dg).transpose(1, 0, 2, 3).reshape(ng, dg, dg)
    (mixed, yp, x3), (gwd,) = _pool_fwd(x2, mod[1], norm_mix_g[1:2], pw, ls, tm, _Ride("gather", [shard_d]))
    ffn1_w = [(gwc, 0), (gwc, 1), (gwd, 0)]
    (h3, gg1, uu1, yf1, dx4, st_loss), _ = _ffn_fwd(x3, mod[1], norm_ffn_g[1:2], ffn1_w, fs, f, tm, "ffn_fwd1",
                                                    loss=(target, final_g.reshape(1, d)))
    loss = lax.psum(st_loss[2, 0], ("x", "y", "c"))

    fb = f // 2 if (f // 2) % 128 == 0 else f
    ts = min(512, s)
    (dgb, dub, ab, dyb, dx3, st_f1), _ = _ffn_bwd(dx4, x3, gg1, uu1, yf1, mod[1], norm_ffn_g[1:2], ffn1_w, fs, f, tm,
                                                  "ffn_bwd1")
    gud1 = _ffn_wgrad(dub, ab, h3, dyb, fb, ts, "ffn_wgrad1")
    (dx2, gpw, st_p), (gg1w,) = _pool_bwd(dx3, x2, yp, mixed, mod[1], norm_mix_g[1:2], pw, ls, tm, (dgb, h3))
    (dgb, dub, ab, dyb, dx1, st_f0), land_f1 = _ffn_bwd(dx2, x1, gg0, uu0, yf0, mod[0], norm_ffn_g[0:1], ffn0_w, fs, f,
                                                        tm, "ffn_bwd0", _Ride("scatter", [gg1w, gud1]))
    gud0 = _ffn_wgrad(dub, ab, h1, dyb, fb, ts, "ffn_wgrad0")
    (dd, gw2, st_m), (gg0w,) = _conv_bwd_mid(dx1, y0, dwc, sb, mod[0], conv_ln_g, conv_ln_b, w2_at[0], r2, w2_at[1], tm,
                                             (dgb, h1))
    (dx0, gw1, gwdw, st_c), (*land_f0, land_pw, land_w2) = _conv_bwd_in(
        dd, glu, u, h0, x0, dx1, wdw, w1_at[0], r1, w1_at[1], mod[0], norm_mix_g[0:1], tm,
        _Ride("scatter", [gg0w, gud0, gpw, gw2[None]]))

    zrow = jnp.zeros((1, d), F32)
    prow = jnp.concatenate([
        st_c[1:3, 0:d], st_m[0:1], st_f0[0:3], st_p[0:2], st_p[4:5], st_f1[0:3],
        st_c[3:4, 0:d], st_p[3:4], st_f0[3:4], st_f1[3:4],
        st_c[0:1, 0:d], st_c[0:1, d:2 * d], st_m[4:5], st_m[2:4], st_m[1:2], st_loss[0:1],
        gwdw[0:CONV_WIDTH], st_p[5:6], zrow], axis=0)
    p_all, (land_w1,) = _small_exchange(prow, _Ride("scatter", [gw1[None]]), "allgather_stats")
    psum = _sum_slots(p_all, "sum_stats")

    ffn_out = _finalize_ffn([land_f0, land_f1], ffn_w_gate, ffn_w_up, ffn_w_down,
                            m_ffn_w_gate, m_ffn_w_up, m_ffn_w_down, v_ffn_w_gate, v_ffn_w_up, v_ffn_w_down)
    fin_w1 = _finalize(land_w1.reshape(NDEV, r1, d), conv_w1[0], m_conv_w1[0], v_conv_w1[0], True, "finalize_w1")
    fin_w2 = _finalize(land_w2.reshape(NDEV, r2, d), conv_w2[0], m_conv_w2[0], v_conv_w2[0], False, "finalize_w2")
    pshape = (ng * (dg // NDEV), dg)
    fin_pw = _finalize(land_pw.reshape((NDEV,) + pshape), pool_w.reshape(pshape), m_pool_w.reshape(pshape),
                       v_pool_w.reshape(pshape), False, "finalize_pool_w")
    dmod_all = p_all[:, 0:12, :].reshape(NDEV, 2, 6 * d)
    dmod_cols = lax.dynamic_slice_in_dim(dmod_all, me * ncol, ncol, axis=2).transpose(1, 0, 2)
    g_ada_w = _ada_wgrad(c_all, dmod_cols)

    def adam(w, g, m, v, name):
        shp = w.shape
        w2d = (-1, shp[-1])
        dl, mo, vo = _adamw(w.reshape(w2d), g.reshape(w2d), m.reshape(w2d), v.reshape(w2d), name)
        return dl.reshape(shp), mo.reshape(shp), vo.reshape(shp)

    rep_names = ["ada_b", "norm_mix_g", "norm_ffn_g", "conv_b1", "conv_bdw", "conv_ln_g", "conv_ln_b", "conv_b2", "final_g"]
    rep_w = [ada_b, norm_mix_g, norm_ffn_g, conv_b1, conv_bdw, conv_ln_g, conv_ln_b, conv_b2, final_g]
    rep_m = [m_ada_b, m_norm_mix_g, m_norm_ffn_g, m_conv_b1, m_conv_bdw, m_conv_ln_g, m_conv_ln_b, m_conv_b2, m_final_g]
    rep_v = [v_ada_b, v_norm_mix_g, v_norm_ffn_g, v_conv_b1, v_conv_bdw, v_conv_ln_g, v_conv_ln_b, v_conv_b2, v_final_g]
    nrep = sum(w.size for w in rep_w) // d
    pad = jnp.zeros(((-nrep) % 8, d), F32)

    def pack(arrs, fill):
        return jnp.concatenate([a.reshape(-1, d) for a in arrs] + [pad + fill], axis=0)

    rep_g = jnp.concatenate([psum[0:nrep], pad], axis=0)
    rep_d, rep_mo, rep_vo = _adamw(pack(rep_w, 0.0), rep_g, pack(rep_m, 0.0), pack(rep_v, 1.0), "adamw_replicated")

    def unpack(packed):
        out, cur = [], 0
        for w in rep_w:
            k = w.size // d
            out.append(packed[cur:cur + k].reshape(w.shape))
            cur += k
        return out

    rep = dict(zip(rep_names, zip(unpack(psum), unpack(rep_d), unpack(rep_mo), unpack(rep_vo))))

    g_wdw_full = psum[nrep:nrep + CONV_WIDTH]
    g_wdw = lax.dynamic_slice_in_dim(g_wdw_full, me * dc, dc, axis=1)
    g_ls = lax.dynamic_slice_in_dim(psum[nrep + CONV_WIDTH:nrep + CONV_WIDTH + 1], me * dc, dc, axis=1)
    tiny = lambda a, b: jnp.concatenate([a.reshape(CONV_WIDTH, dc), b.reshape(1, dc)], axis=0)
    t_d, t_m, t_v = _adamw(tiny(conv_wdw, pool_ls), tiny(g_wdw, g_ls), tiny(m_conv_wdw, m_pool_ls),
                           tiny(v_conv_wdw, v_pool_ls), "adamw_taps")

    def taps(a):
        return a[0:CONV_WIDTH][None], a[CONV_WIDTH:CONV_WIDTH + 1]

    sharded = {
        "ada_w": (g_ada_w,) + adam(ada_w, g_ada_w, m_ada_w, v_ada_w, "adamw_ada_w"),
        "conv_w1": tuple(a[None] for a in fin_w1),
        "conv_w2": tuple(a[None] for a in fin_w2),
        "pool_w": tuple(a.reshape(pool_w.shape) for a in fin_pw),
        "ffn_w_gate": tuple(ffn_out[0::3]),
        "ffn_w_up": tuple(ffn_out[1::3]),
        "ffn_w_down": tuple(ffn_out[2::3]),
        "conv_wdw": (g_wdw[None], taps(t_d)[0], taps(t_m)[0], taps(t_v)[0]),
        "pool_ls": (g_ls, taps(t_d)[1], taps(t_m)[1], taps(t_v)[1]),
    }
    every = {**rep, **sharded}
    order = ["ada_w", "ada_b", "norm_mix_g", "norm_ffn_g", "conv_w1", "conv_b1", "conv_wdw", "conv_bdw", "conv_ln_g",
             "conv_ln_b", "conv_w2", "conv_b2", "pool_w", "pool_ls", "ffn_w_gate", "ffn_w_up", "ffn_w_down", "final_g"]
    grads = [every[n][0] for n in order]
    deltas = [every[n][1] for n in order]
    new_m = [every[n][2] for n in order]
    new_v = [every[n][3] for n in order]
    return (loss, dx0.reshape(1, s, d), *grads, *deltas, *new_m, *new_v)
```

```python
import jax
import jax.numpy as jnp
from jax import lax
from jax.experimental import pallas as pl
from jax.experimental.pallas import tpu as pltpu

NDEV = 8
EPS = 1e-6
CONV_WIDTH = 31
POOL_WINDOWS = (2, 4, 8, 16)
CONV_HALO = 32
POOL_HALO = 16
ADAM_LR = 0.001
ADAM_B1 = 0.9
ADAM_B2 = 0.999
ADAM_EPS = 1e-08
ADAM_WD = 0.01
ADAM_STEP = 10
VMEM_LIMIT = 56 * 2**20
MESH = pl.DeviceIdType.MESH
F32 = jnp.float32
BF16 = jnp.bfloat16


def _nt(a, b):
    return lax.dot_general(a, b, (((1,), (1,)), ((), ())), preferred_element_type=F32)


def _nn(a, b):
    return lax.dot_general(a, b, (((1,), (0,)), ((), ())), preferred_element_type=F32)


def _tn(a, b):
    return lax.dot_general(a, b, (((0,), (0,)), ((), ())), preferred_element_type=F32)


def _sum0(v):
    return jnp.sum(v, axis=0, keepdims=True)


def _rms(x):
    rinv = lax.rsqrt(jnp.mean(x * x, axis=-1, keepdims=True) + EPS)
    return x * rinv, rinv


def _rms_mod_bwd(dh, n, rinv, g, sc):
    dhs = dh * (1.0 + sc)
    dn = dhs * g
    dx = rinv * (dn - n * jnp.mean(dn * n, axis=-1, keepdims=True))
    return dx, _sum0(dh), _sum0(dh * (n * g)), _sum0(dhs * n)


def _silu_grad(z, sg):
    return sg * (1.0 + z * (1.0 - sg))


def _params(*sem):
    return pltpu.CompilerParams(dimension_semantics=sem, vmem_limit_bytes=VMEM_LIMIT)


def _row(i, d):
    return pl.BlockSpec((i, d), lambda *_: (0, 0))


def _weight_spec(rows, d, idx):
    return pl.BlockSpec((NDEV, rows, d), lambda *_: (0, idx, 0), pipeline_mode=pl.Buffered(1))


def _my_index():
    return 4 * lax.axis_index("x") + 2 * lax.axis_index("y") + lax.axis_index("c")


def _peer(k):
    x, y, c = lax.axis_index("x"), lax.axis_index("y"), lax.axis_index("c")
    px = 1 - x if k & 4 else x
    py = 1 - y if k & 2 else y
    pc = 1 - c if k & 1 else c
    return (px, py, pc), 4 * px + 2 * py + pc


def _allgather_small(v, name):
    r, c = v.shape

    def body(v_ref, out_ref, send_sems, recv_sems):
        me = _my_index()
        out_ref[me] = v_ref[...]
        copies = []
        for k in range(1, NDEV):
            dev, _ = _peer(k)
            copies.append(pltpu.make_async_remote_copy(
                src_ref=v_ref, dst_ref=out_ref.at[me], send_sem=send_sems.at[k - 1], recv_sem=recv_sems.at[k - 1],
                device_id=dev, device_id_type=MESH))
        for cp in copies:
            cp.start()
        for cp in copies:
            cp.wait()

    return pl.pallas_call(body, name=name,
        out_shape=jax.ShapeDtypeStruct((NDEV, r, c), v.dtype),
        in_specs=[pl.BlockSpec(memory_space=pltpu.VMEM)],
        out_specs=pl.BlockSpec(memory_space=pltpu.VMEM),
        scratch_shapes=[pltpu.SemaphoreType.DMA((NDEV - 1,)), pltpu.SemaphoreType.DMA((NDEV - 1,))],
    )(v)


def _gather_sems():
    return [pltpu.SemaphoreType.DMA((NDEV - 1,)), pltpu.SemaphoreType.DMA((NDEV - 1,)), pltpu.SemaphoreType.DMA((1,))]


class _Gather:
    def __init__(self, srcs, dsts, sems):
        self.src, self.dst = srcs[0], dsts[0]
        self.send, self.recv, self.local = sems
        x, y, c = lax.axis_index("x"), lax.axis_index("y"), lax.axis_index("c")
        self.me, self.sibling, self.core = (x, y, c), (x, y, 1 - c), c
        self.chips = [(1 - x, y), (x, 1 - y), (1 - x, 1 - y)]

    def _copy(self, k, block, to, from_input=False):
        slot = self.dst.at[4 * block[0] + 2 * block[1] + block[2]]
        return pltpu.make_async_remote_copy(
            src_ref=self.src if from_input else slot, dst_ref=slot, send_sem=self.send.at[k], recv_sem=self.recv.at[k],
            device_id=to, device_id_type=MESH)

    def _own(self):
        return pltpu.make_async_copy(self.src, self.dst.at[_my_index()], self.local.at[0])

    def _first(self):
        return [self._copy(0, self.me, self.sibling, True)] + [
            self._copy(1 + j, self.me, (*chip, self.core), True) for j, chip in enumerate(self.chips)]

    def start(self):
        self._own().start()
        for cp in self._first():
            cp.start()

    def forward(self):
        for j, chip in enumerate(self.chips):
            self._copy(1 + j, (*chip, self.core), self.me).wait_recv()
            self._copy(4 + j, (*chip, self.core), self.sibling).start()

    def finish(self):
        self._copy(0, self.sibling, self.me).wait_recv()
        for j, chip in enumerate(self.chips):
            self._copy(4 + j, (*chip, 1 - self.core), self.me).wait_recv()
        for cp in self._first():
            cp.wait_send()
        for j, chip in enumerate(self.chips):
            self._copy(4 + j, (*chip, self.core), self.sibling).wait_send()
        self._own().wait()


def _scatter_sems(n):
    return [pltpu.SemaphoreType.DMA((7 * n,)), pltpu.SemaphoreType.DMA((7 * n,)), pltpu.SemaphoreType.DMA((n,))]


class _Scatter:
    def __init__(self, srcs, dsts, sems):
        send_sems, recv_sems, local_sems = sems
        me = _my_index()
        self.copies = []
        for a, (src, dst) in enumerate(zip(srcs, dsts)):
            r = dst.shape[2]
            self.copies.append(pltpu.make_async_copy(src.at[:, pl.ds(me * r, r), :], dst.at[me], local_sems.at[a]))
            for k in range(1, NDEV):
                dev, p = _peer(k)
                self.copies.append(pltpu.make_async_remote_copy(
                    src_ref=src.at[:, pl.ds(p * r, r), :], dst_ref=dst.at[me],
                    send_sem=send_sems.at[a * 7 + k - 1], recv_sem=recv_sems.at[a * 7 + k - 1],
                    device_id=dev, device_id_type=MESH))

    def start(self):
        for cp in self.copies:
            cp.start()

    def forward(self):
        pass

    def finish(self):
        for cp in self.copies:
            cp.wait()


def _land_shape(part):
    a, r, c = part.shape
    return jax.ShapeDtypeStruct((NDEV, a, r // NDEV, c), part.dtype)


ANY = pl.BlockSpec(memory_space=pl.ANY)


class _Ride:
    def __init__(self, kind, srcs):
        self.kind, self.srcs = kind, list(srcs)
        if kind == "gather":
            self.out_shape = [jax.ShapeDtypeStruct((NDEV,) + a.shape, a.dtype) for a in self.srcs]
            self.sems = _gather_sems()
        else:
            self.out_shape = [_land_shape(a) for a in self.srcs]
            self.sems = _scatter_sems(len(self.srcs))

    def exchange(self, ins, outs, sems):
        return (_Gather if self.kind == "gather" else _Scatter)(ins, outs, sems)


def _small_exchange(small, ride, name):
    r, c = small.shape
    nr = len(ride.srcs)

    def body(v_ref, *refs):
        rin, refs = refs[:nr], refs[nr:]
        out_ref, refs = refs[0], refs[1:]
        rout, refs = refs[:nr], refs[nr:]
        send_sems, recv_sems, rsems = refs[0], refs[1], refs[2:]
        big = ride.exchange(rin, rout, rsems)
        big.start()
        me = _my_index()
        out_ref[me] = v_ref[...]
        copies = []
        for k in range(1, NDEV):
            dev, _ = _peer(k)
            copies.append(pltpu.make_async_remote_copy(
                src_ref=v_ref, dst_ref=out_ref.at[me], send_sem=send_sems.at[k - 1], recv_sem=recv_sems.at[k - 1],
                device_id=dev, device_id_type=MESH))
        for cp in copies:
            cp.start()
        for cp in copies:
            cp.wait()
        big.forward()
        big.finish()

    res = pl.pallas_call(body, name=name,
        out_shape=[jax.ShapeDtypeStruct((NDEV, r, c), small.dtype)] + ride.out_shape,
        in_specs=[pl.BlockSpec(memory_space=pltpu.VMEM)] + [ANY] * nr,
        out_specs=[pl.BlockSpec(memory_space=pltpu.VMEM)] + [ANY] * nr,
        scratch_shapes=[pltpu.SemaphoreType.DMA((NDEV - 1,)), pltpu.SemaphoreType.DMA((NDEV - 1,))] + ride.sems,
    )(small, *ride.srcs)
    return res[0], res[1:]


def _call(kernel_body, *, name, grid, in_specs, out_specs, out_shape, args, scratch_shapes=(), ride=None):
    n_in, n_out, n_sc = len(in_specs), len(out_specs), len(scratch_shapes)
    nr = len(ride.srcs) if ride else 0
    in_specs, out_specs, out_shape = list(in_specs), list(out_specs), list(out_shape)
    scratch_shapes, args = list(scratch_shapes), list(args)
    if ride is not None:
        in_specs += [ANY] * nr
        out_specs += [ANY] * nr
        out_shape += ride.out_shape
        args += ride.srcs
        scratch_shapes += ride.sems

    def body(*refs):
        ins, refs = refs[:n_in], refs[n_in:]
        rin, refs = refs[:nr], refs[nr:]
        outs, refs = refs[:n_out], refs[n_out:]
        rout, refs = refs[:nr], refs[nr:]
        scratch, rsems = refs[:n_sc], refs[n_sc:]
        first, last = True, True
        for axis, extent in enumerate(grid):
            first &= pl.program_id(axis) == 0
            last &= pl.program_id(axis) == extent - 1
        if ride is not None:
            middle = last if len(grid) > 1 else pl.program_id(0) == (3 * grid[0]) // 4
            exchange = ride.exchange(rin, rout, rsems)
            pl.when(first)(exchange.start)
            pl.when(middle)(exchange.forward)
        kernel_body(*ins, *outs, *scratch)
        if ride is not None:
            pl.when(last)(exchange.finish)

    res = pl.pallas_call(body, name=name, grid=grid, in_specs=in_specs, out_specs=out_specs, out_shape=out_shape,
                         scratch_shapes=scratch_shapes, compiler_params=_params(*(("arbitrary",) * len(grid))))(*args)
    return res[:n_out], res[n_out:]


def _ada_forward(c_all, ada_w, bias):
    nl, d, ncol = ada_w.shape

    def body(c_ref, w_ref, b_ref, o_ref):
        cv = c_ref[...]
        ca = cv * jax.nn.sigmoid(cv)
        o_ref[0] = jnp.dot(ca, w_ref[0], preferred_element_type=F32, precision=lax.Precision.HIGHEST) + b_ref[0]

    return pl.pallas_call(body, name="ada_forward", grid=(nl,),
        in_specs=[pl.BlockSpec((NDEV, d), lambda i: (0, 0)), pl.BlockSpec((1, d, ncol), lambda i: (i, 0, 0)),
                  pl.BlockSpec((1, 1, ncol), lambda i: (i, 0, 0))],
        out_specs=pl.BlockSpec((1, NDEV, ncol), lambda i: (i, 0, 0)),
        out_shape=jax.ShapeDtypeStruct((nl, NDEV, ncol), F32),
        compiler_params=_params("arbitrary"),
    )(c_all, ada_w, bias)


def _ada_wgrad(c_all, dmod):
    nl, _, ncol = dmod.shape
    d = c_all.shape[1]
    bd = min(d, 256)

    def body(c_ref, dm_ref, o_ref):
        cv = c_ref[...]
        ca = cv * jax.nn.sigmoid(cv)
        o_ref[0] = lax.dot_general(ca, dm_ref[0], (((0,), (0,)), ((), ())), preferred_element_type=F32,
                                   precision=lax.Precision.HIGHEST)

    return pl.pallas_call(body, name="ada_wgrad", grid=(nl, d // bd),
        in_specs=[pl.BlockSpec((NDEV, bd), lambda i, j: (0, j)), pl.BlockSpec((1, NDEV, ncol), lambda i, j: (i, 0, 0))],
        out_specs=pl.BlockSpec((1, bd, ncol), lambda i, j: (i, j, 0)),
        out_shape=jax.ShapeDtypeStruct((nl, d, ncol), F32),
        compiler_params=_params("arbitrary", "arbitrary"),
    )(c_all, dmod)


def _row_block(r, c, bytes_per_row_elem=4, budget=2 * 2**20):
    if r * c * bytes_per_row_elem <= budget or r % 8:
        return r
    best = 8
    for b in range(8, r + 1, 8):
        if r % b == 0 and b * c * bytes_per_row_elem <= budget:
            best = b
    return best


def _sum_slots(land, name):
    _, r, c = land.shape
    br = _row_block(r, c, 8 * land.dtype.itemsize)

    def body(l_ref, o_ref):
        acc = l_ref[0].astype(F32)
        for s in range(1, NDEV):
            acc = acc + l_ref[s].astype(F32)
        o_ref[...] = acc

    return pl.pallas_call(body, name=name, grid=(r // br,),
        in_specs=[pl.BlockSpec((NDEV, br, c), lambda i: (0, i, 0))],
        out_specs=pl.BlockSpec((br, c), lambda i: (i, 0)),
        out_shape=jax.ShapeDtypeStruct((r, c), F32),
        compiler_params=_params("arbitrary"),
    )(land)


def _adamw(w, g, m, v, name):
    r, c = w.shape
    br = _row_block(r, c)

    def body(w_ref, g_ref, m_ref, v_ref, d_ref, mo_ref, vo_ref):
        d_ref[...], mo_ref[...], vo_ref[...] = _adam_math(w_ref[...], g_ref[...], m_ref[...], v_ref[...])

    spec = pl.BlockSpec((br, c), lambda i: (i, 0))
    return pl.pallas_call(body, name=name, grid=(r // br,),
        in_specs=[spec] * 4, out_specs=[spec] * 3,
        out_shape=[jax.ShapeDtypeStruct((r, c), F32)] * 3,
        compiler_params=_params("arbitrary"),
    )(w, g, m, v)


def _adam_math(w, g, m, v):
    m2 = ADAM_B1 * m + (1.0 - ADAM_B1) * g
    v2 = ADAM_B2 * v + (1.0 - ADAM_B2) * (g * g)
    m_hat = m2 / (1.0 - ADAM_B1 ** ADAM_STEP)
    v_hat = v2 / (1.0 - ADAM_B2 ** ADAM_STEP)
    return -ADAM_LR * (m_hat / (jnp.sqrt(v_hat) + ADAM_EPS) + ADAM_WD * w), m2, v2


def _slot_sum(land_ref, *lead):
    acc = land_ref[(0,) + lead].astype(F32)
    for s in range(1, NDEV):
        acc = acc + land_ref[(s,) + lead].astype(F32)
    return acc


def _finalize(land, w, m, v, transposed, name):
    _, r, c = land.shape
    cb = 256 if (transposed and c % 256 == 0) else c
    wblk = pl.BlockSpec((cb, r), lambda i: (i, 0)) if transposed else pl.BlockSpec((r, cb), lambda i: (0, i))

    def body(l_ref, w_ref, m_ref, v_ref, g_ref, d_ref, mo_ref, vo_ref):
        g = _slot_sum(l_ref)
        g = g.T if transposed else g
        g_ref[...] = g
        d_ref[...], mo_ref[...], vo_ref[...] = _adam_math(w_ref[...], g, m_ref[...], v_ref[...])

    return pl.pallas_call(body, name=name, grid=(c // cb,),
        in_specs=[pl.BlockSpec((NDEV, r, cb), lambda i: (0, 0, i)), wblk, wblk, wblk], out_specs=[wblk] * 4,
        out_shape=[jax.ShapeDtypeStruct(w.shape, F32)] * 4,
        compiler_params=_params("arbitrary"),
    )(land, w, m, v)


def _finalize_ffn(land0, land1, wg, wu, wd, mg, mu, md, vg, vu, vd):
    nl, fs, d = wg.shape
    db = min(256, d)

    def kernel_body(l0_ref, l1_ref, wg_ref, wu_ref, wd_ref, mg_ref, mu_ref, md_ref, vg_ref, vu_ref, vd_ref, *outs):
        layer = pl.program_id(0)
        triples = [(wg_ref, mg_ref, vg_ref), (wu_ref, mu_ref, vu_ref), (wd_ref, md_ref, vd_ref)]

        def run(land_ref):
            for j, (w_ref, m_ref, v_ref) in enumerate(triples):
                g = _slot_sum(land_ref, j)
                delta, m2, v2 = _adam_math(w_ref[0], g, m_ref[0], v_ref[0])
                for o_ref, val in zip(outs[j::3], (g, delta, m2, v2)):
                    o_ref[0] = val

        @pl.when(layer == 0)
        def _():
            run(l0_ref)

        @pl.when(layer == 1)
        def _():
            run(l1_ref)

    blk = pl.BlockSpec((1, fs, db), lambda l, i: (l, 0, i))
    lblk = [pl.BlockSpec((NDEV, 3, fs, db), lambda l, i: (0, 0, 0, i * (1 - l))),
            pl.BlockSpec((NDEV, 3, fs, db), lambda l, i: (0, 0, 0, i * l))]
    outs, _ = _call(kernel_body, name="finalize_ffn", grid=(nl, d // db),
        in_specs=lblk + [blk] * 9, out_specs=[blk] * 12, out_shape=[jax.ShapeDtypeStruct(wg.shape, F32)] * 12,
        args=(land0, land1, wg, wu, wd, mg, mu, md, vg, vu, vd))
    return outs


CONV_ROWS = 64
LANES = 128


def _shifted_copies(buf, sh, n):
    sh[0] = buf[...]
    for r in range(1, 8):
        sh[r, pl.ds(0, n - 8), :] = buf[pl.ds(r, n - 8), :]


def _window(sh, o, rows, cols):
    return sh[o % 8, pl.ds(o - o % 8, rows), cols]


def _conv_in(x, mod, gmix, gw, w1_rows, w1_idx, b1, tm, ride):
    s, d = x.shape

    def kernel_body(x_ref, mod_ref, g_ref, w_ref, b_ref, h_ref, u_ref, glu_ref):
        n, _ = _rms(x_ref[...])
        h = (n * g_ref[...]) * (1.0 + mod_ref[1:2, :]) + mod_ref[0:1, :]
        hb = h.astype(BF16)
        h_ref[...] = hb
        u = _nt(hb, w_ref[...].reshape(NDEV * w1_rows, d)) + b_ref[...]
        u_ref[...] = u
        glu_ref[...] = u[:, :d] * jax.nn.sigmoid(u[:, d:])

    tile = pl.BlockSpec((tm, d), lambda i: (i, 0))
    return _call(kernel_body, name="conv_in", grid=(s // tm,),
        in_specs=[tile, _row(8, d), _row(1, d), _weight_spec(w1_rows, d, w1_idx), _row(1, 2 * d)],
        out_specs=[tile, pl.BlockSpec((tm, 2 * d), lambda i: (i, 0)), tile],
        out_shape=[jax.ShapeDtypeStruct((s, d), BF16), jax.ShapeDtypeStruct((s, 2 * d), F32),
                   jax.ShapeDtypeStruct((s, d), F32)],
        args=(x, mod, gmix, gw, b1), ride=ride)


def _conv_mid(glu, wdw, bdw, ln_g, ln_b, gw, w2_rows, w2_idx, b2, x, mod, tm, ride):
    s, d = x.shape
    off = CONV_HALO - (CONV_WIDTH - 1)
    rc = min(CONV_ROWS, tm)

    def kernel_body(glu_ref, halo_ref, wdw_ref, bdw_ref, lng_ref, lnb_ref, w_ref, b2_ref, x_ref, mod_ref,
                    dwc_ref, s_ref, y_ref, x1_ref, buf, sh):
        i = pl.program_id(0)
        buf[pl.ds(0, CONV_HALO), :] = jnp.where(i > 0, halo_ref[...], 0.0)
        buf[pl.ds(CONV_HALO, tm), :] = glu_ref[...]
        _shifted_copies(buf, sh, CONV_HALO + tm)
        for cb in range(d // LANES):
            cols = pl.ds(cb * LANES, LANES)
            taps = wdw_ref[:, cols]
            for r in range(tm // rc):
                part = jnp.zeros((rc, LANES), F32) + bdw_ref[:, cols]
                for k in range(CONV_WIDTH):
                    part = part + _window(sh, r * rc + off + k, rc, cols) * taps[k:k + 1, :]
                dwc_ref[pl.ds(r * rc, rc), cols] = part
        acc = dwc_ref[...]
        mu = jnp.mean(acc, axis=-1, keepdims=True)
        xc = acc - mu
        rstd = lax.rsqrt(jnp.mean(xc * xc, axis=-1, keepdims=True) + EPS)
        ln = (xc * rstd) * lng_ref[...] + lnb_ref[...]
        sb = (ln * jax.nn.sigmoid(ln)).astype(BF16)
        s_ref[...] = sb
        y = _nn(sb, w_ref[...].reshape(NDEV * w2_rows, d)) + b2_ref[...]
        y_ref[...] = y.astype(BF16)
        x1_ref[...] = x_ref[...] + (1.0 + mod_ref[2:3, :]) * y

    tile = pl.BlockSpec((tm, d), lambda i: (i, 0))
    halo = pl.BlockSpec((CONV_HALO, d), lambda i: (jnp.maximum(i * (tm // CONV_HALO) - 1, 0), 0))
    return _call(kernel_body, name="conv_mid", grid=(s // tm,),
        in_specs=[tile, halo, _row(CONV_WIDTH, d), _row(1, d), _row(1, d), _row(1, d),
                  _weight_spec(w2_rows, d, w2_idx), _row(1, d), tile, _row(8, d)],
        out_specs=[tile, tile, tile, tile],
        out_shape=[jax.ShapeDtypeStruct((s, d), F32), jax.ShapeDtypeStruct((s, d), BF16),
                   jax.ShapeDtypeStruct((s, d), BF16), jax.ShapeDtypeStruct((s, d), F32)],
        scratch_shapes=[pltpu.VMEM((CONV_HALO + tm, d), F32), pltpu.VMEM((8, CONV_HALO + tm, d), F32)],
        args=(glu, glu, wdw, bdw, ln_g, ln_b, gw, b2, x, mod), ride=ride)


def _ffn_fwd(x, mod, gffn, weights, fs, f, tm, name, ride=None, loss=None):
    s, d = x.shape

    last = s // tm - 1

    def kernel_body(x_ref, mod_ref, g_ref, wg_ref, wu_ref, wd_ref, *rest):
        xv = x_ref[...]
        n, _ = _rms(xv)
        hb = ((n * g_ref[...]) * (1.0 + mod_ref[4:5, :]) + mod_ref[3:4, :]).astype(BF16)
        gg = _nt(hb, wg_ref[...].reshape(f, d))
        uu = _nt(hb, wu_ref[...].reshape(f, d))
        ab = ((gg * jax.nn.sigmoid(gg)) * uu).astype(BF16)
        y = _nn(ab, wd_ref[...].reshape(f, d))
        xo = xv + (1.0 + mod_ref[5:6, :]) * y
        if loss is None:
            h_ref, gg_ref, uu_ref, y_ref, xo_ref = rest
            xo_ref[...] = xo
        else:
            t_ref, gfin_ref, h_ref, gg_ref, uu_ref, y_ref, dx_ref, st_ref = rest
            dx_ref[...] = _loss_tile(xo, t_ref[...], gfin_ref[...], st_ref, pl.program_id(0), last)
        h_ref[...] = hb
        gg_ref[...] = gg.astype(BF16)
        uu_ref[...] = uu.astype(BF16)
        y_ref[...] = y.astype(BF16)

    tile = pl.BlockSpec((tm, d), lambda i: (i, 0))
    wide = pl.BlockSpec((tm, f), lambda i: (i, 0))
    extra_in = [] if loss is None else [tile, _row(1, d)]
    extra_out = [] if loss is None else [_row(8, d)]
    return _call(kernel_body, name=name, grid=(s // tm,),
        in_specs=[tile, _row(8, d), _row(1, d)] + [_weight_spec(fs, d, idx) for _, idx in weights] + extra_in,
        out_specs=[tile, wide, wide, tile, tile] + extra_out,
        out_shape=[jax.ShapeDtypeStruct((s, d), BF16), jax.ShapeDtypeStruct((s, f), BF16),
                   jax.ShapeDtypeStruct((s, f), BF16), jax.ShapeDtypeStruct((s, d), BF16),
                   jax.ShapeDtypeStruct((s, d), F32)] + [jax.ShapeDtypeStruct((8, d), F32)] * len(extra_out),
        args=(x, mod, gffn) + tuple(a for a, _ in weights) + (() if loss is None else tuple(loss)), ride=ride)


def _loss_tile(x, target, g, st_ref, i, last):
    d = x.shape[1]

    @pl.when(i == 0)
    def _():
        st_ref[...] = jnp.zeros_like(st_ref)

    n, rinv = _rms(x)
    err = n * g - target
    dy = err * (1.0 / d)
    st_ref[0:1, :] += _sum0(dy * n)
    st_ref[1:2, :] += _sum0(err * err) * (0.5 / d)

    @pl.when(i == last)
    def _():
        st_ref[2:3, :] = jnp.zeros((1, d), F32) + jnp.sum(st_ref[1:2, :], axis=-1, keepdims=True)

    dn = dy * g
    return rinv * (dn - n * jnp.mean(dn * n, axis=-1, keepdims=True))


def _pool_fwd(x, mod, gmix, pw, ls, tm, ride):
    s, d = x.shape
    dg = d // len(POOL_WINDOWS)

    def body(x_ref, halo_ref, mod_ref, g_ref, pw_ref, ls_ref, mixed_ref, yp_ref, xo_ref, buf):
        i = pl.program_id(0)

        def hfun(xv):
            n, _ = _rms(xv)
            return (n * g_ref[...]) * (1.0 + mod_ref[1:2, :]) + mod_ref[0:1, :]

        xv = x_ref[...]
        h = hfun(xv)
        buf[pl.ds(0, POOL_HALO), :] = jnp.where(i > 0, hfun(halo_ref[...]), 0.0)
        buf[pl.ds(POOL_HALO, tm), :] = h
        t = i * tm + lax.broadcasted_iota(jnp.int32, (tm, 1), 0)
        gate = 1.0 + mod_ref[2:3, :]
        for gi, w in enumerate(POOL_WINDOWS):
            cols = pl.ds(gi * dg, dg)
            ws = buf[pl.ds(POOL_HALO, tm), cols]
            for j in range(1, w):
                ws = ws + buf[pl.ds(POOL_HALO - j, tm), cols]
            inv = 1.0 / jnp.minimum(t + 1, w).astype(F32)
            mb = (ws * inv - h[:, gi * dg:(gi + 1) * dg]).astype(BF16)
            mixed_ref[:, cols] = mb
            yp = _nn(mb, pw_ref[gi])
            yp_ref[:, cols] = yp.astype(BF16)
            xo_ref[:, cols] = xv[:, gi * dg:(gi + 1) * dg] + gate[:, gi * dg:(gi + 1) * dg] * (yp * ls_ref[:, cols])

    tile = pl.BlockSpec((tm, d), lambda i: (i, 0))
    halo = pl.BlockSpec((POOL_HALO, d), lambda i: (jnp.maximum(i * (tm // POOL_HALO) - 1, 0), 0))
    return _call(body, name="pool_fwd", grid=(s // tm,),
        in_specs=[tile, halo, _row(8, d), _row(1, d), pl.BlockSpec((len(POOL_WINDOWS), dg, dg), lambda i: (0, 0, 0)),
                  _row(1, d)],
        out_specs=[tile, tile, tile],
        out_shape=[jax.ShapeDtypeStruct((s, d), BF16), jax.ShapeDtypeStruct((s, d), BF16),
                   jax.ShapeDtypeStruct((s, d), F32)],
        scratch_shapes=[pltpu.VMEM((POOL_HALO + tm, d), F32)],
        args=(x, x, mod, gmix, pw, ls), ride=ride)


def _ffn_bwd(dxo, x, gg, uu, y, mod, gffn, weights, fs, f, tm, name, ride=None):
    s, d = x.shape

    def kernel_body(dxo_ref, x_ref, gg_ref, uu_ref, y_ref, mod_ref, g_ref, wg_ref, wu_ref, wd_ref,
                    dg_ref, du_ref, a_ref, dy_ref, dxi_ref, st_ref):
        @pl.when(pl.program_id(0) == 0)
        def _():
            st_ref[...] = jnp.zeros_like(st_ref)

        dxo_v = dxo_ref[...]
        dyb = (dxo_v * (1.0 + mod_ref[5:6, :])).astype(BF16)
        dy_ref[...] = dyb
        da = _nt(dyb, wd_ref[...].reshape(f, d))
        ggv, uuv = gg_ref[...].astype(F32), uu_ref[...].astype(F32)
        sg = jax.nn.sigmoid(ggv)
        silu = ggv * sg
        a_ref[...] = (silu * uuv).astype(BF16)
        dub = (da * silu).astype(BF16)
        dgb = (da * uuv * _silu_grad(ggv, sg)).astype(BF16)
        du_ref[...] = dub
        dg_ref[...] = dgb
        dh = _nn(dgb, wg_ref[...].reshape(f, d)) + _nn(dub, wu_ref[...].reshape(f, d))
        n, rinv = _rms(x_ref[...])
        dx, dsh, dsc, dgain = _rms_mod_bwd(dh, n, rinv, g_ref[...], mod_ref[4:5, :])
        dxi_ref[...] = dxo_v + dx
        st_ref[0:1, :] += dsh
        st_ref[1:2, :] += dsc
        st_ref[2:3, :] += _sum0(dxo_v * y_ref[...].astype(F32))
        st_ref[3:4, :] += dgain

    tile = pl.BlockSpec((tm, d), lambda i: (i, 0))
    wide = pl.BlockSpec((tm, f), lambda i: (i, 0))
    return _call(kernel_body, name=name, grid=(s // tm,),
        in_specs=[tile, tile, wide, wide, tile, _row(8, d), _row(1, d)] + [_weight_spec(fs, d, idx) for _, idx in weights],
        out_specs=[wide, wide, wide, tile, tile, _row(8, d)],
        out_shape=[jax.ShapeDtypeStruct((s, f), BF16)] * 3 + [jax.ShapeDtypeStruct((s, d), BF16),
                   jax.ShapeDtypeStruct((s, d), F32), jax.ShapeDtypeStruct((8, d), F32)],
        args=(dxo, x, gg, uu, y, mod, gffn) + tuple(a for a, _ in weights), ride=ride)


def _ffn_wgrad(dgb, dub, ab, h, dyb, fb, ts, name):
    s, f = dgb.shape
    d = h.shape[1]
    last = s // ts - 1

    def body(dg_ref, du_ref, a_ref, h_ref, dy_ref, o_ref, acc):
        t = pl.program_id(1)

        @pl.when(t == 0)
        def _():
            acc[...] = jnp.zeros_like(acc)

        hv = h_ref[...]
        acc[0] += _tn(dg_ref[...], hv)
        acc[1] += _tn(du_ref[...], hv)
        acc[2] += _tn(a_ref[...], dy_ref[...])

        @pl.when(t == last)
        def _():
            o_ref[...] = acc[...].astype(BF16)

    wide = pl.BlockSpec((ts, fb), lambda j, t: (t, j))
    tile = pl.BlockSpec((ts, d), lambda j, t: (t, 0))
    return pl.pallas_call(body, name=name, grid=(f // fb, s // ts),
        in_specs=[wide, wide, wide, tile, tile],
        out_specs=pl.BlockSpec((3, fb, d), lambda j, t: (0, j, 0)),
        out_shape=jax.ShapeDtypeStruct((3, f, d), BF16),
        scratch_shapes=[pltpu.VMEM((3, fb, d), F32)],
        compiler_params=_params("arbitrary", "arbitrary"),
    )(dgb, dub, ab, h, dyb)


def _pool_bwd(dxo, x, yp, mixed, mod, gmix, pw, ls, tm):
    s, d = x.shape
    ng = len(POOL_WINDOWS)
    dg = d // ng
    last = s // tm - 1

    def body(dxo_ref, dxh_ref, x_ref, yp_ref, mixed_ref, mod_ref, g_ref, pw_ref, ls_ref,
             dxi_ref, dpw_ref, st_ref, bufy, bufq, bufh, acc):
        i = pl.program_id(0)

        @pl.when(i == 0)
        def _():
            st_ref[...] = jnp.zeros_like(st_ref)
            acc[...] = jnp.zeros_like(acc)

        gate = 1.0 + mod_ref[2:3, :]
        lsv = ls_ref[...]
        dxo_v = dxo_ref[...]
        st_ref[2:3, :] += _sum0(dxo_v * yp_ref[...].astype(F32))
        bufy[pl.ds(0, tm), :] = (dxo_v * (gate * lsv)).astype(BF16)
        bufy[pl.ds(tm, POOL_HALO), :] = jnp.where(i < last, dxh_ref[...] * (gate * lsv), 0.0).astype(BF16)
        t = i * tm + lax.broadcasted_iota(jnp.int32, (tm + POOL_HALO, 1), 0)
        for gi, w in enumerate(POOL_WINDOWS):
            cols = pl.ds(gi * dg, dg)
            dm = _nt(bufy[:, cols], pw_ref[gi])
            bufq[:, cols] = dm * (1.0 / jnp.minimum(t + 1, w).astype(F32))
            dh = bufq[pl.ds(0, tm), cols] - dm[0:tm, :]
            for j in range(1, w):
                dh = dh + bufq[pl.ds(j, tm), cols]
            bufh[:, cols] = dh
            acc[gi] += _tn(mixed_ref[:, cols], bufy[pl.ds(0, tm), cols])
        n, rinv = _rms(x_ref[...])
        dx, dsh, dsc, dgain = _rms_mod_bwd(bufh[...], n, rinv, g_ref[...], mod_ref[1:2, :])
        dxi_ref[...] = dxo_v + dx
        st_ref[0:1, :] += dsh
        st_ref[1:2, :] += dsc
        st_ref[3:4, :] += dgain

        @pl.when(i == last)
        def _():
            r = st_ref[2:3, :]
            st_ref[4:5, :] = r * lsv
            st_ref[5:6, :] = r * gate
            dpw_ref[...] = acc[...].astype(BF16)

    tile = pl.BlockSpec((tm, d), lambda i: (i, 0))
    nxt = pl.BlockSpec((POOL_HALO, d), lambda i: (jnp.minimum((i + 1) * (tm // POOL_HALO), s // POOL_HALO - 1), 0))
    pws = pl.BlockSpec((ng, dg, dg), lambda i: (0, 0, 0))
    return _call(body, name="pool_bwd", grid=(s // tm,),
        in_specs=[tile, nxt, tile, tile, tile, _row(8, d), _row(1, d), pws, _row(1, d)],
        out_specs=[tile, pws, _row(8, d)],
        out_shape=[jax.ShapeDtypeStruct((s, d), F32), jax.ShapeDtypeStruct((ng, dg, dg), BF16),
                   jax.ShapeDtypeStruct((8, d), F32)],
        scratch_shapes=[pltpu.VMEM((tm + POOL_HALO, d), BF16), pltpu.VMEM((tm + POOL_HALO, d), F32),
                        pltpu.VMEM((tm, d), F32), pltpu.VMEM((ng, dg, dg), F32)],
        args=(dxo, dxo, x, yp, mixed, mod, gmix, pw, ls))[0]


def _conv_bwd_mid(dxo, y, dwc, sb, mod, ln_g, ln_b, gw, w2_rows, w2_idx, tm):
    s, d = dwc.shape
    last = s // tm - 1

    def body(dxo_ref, y_ref, dwc_ref, s_ref, mod_ref, lng_ref, lnb_ref, w_ref, dd_ref, dw_ref, st_ref, acc):
        i = pl.program_id(0)

        @pl.when(i == 0)
        def _():
            st_ref[...] = jnp.zeros_like(st_ref)
            acc[...] = jnp.zeros_like(acc)

        dxo_v = dxo_ref[...]
        st_ref[0:1, :] += _sum0(dxo_v * y_ref[...].astype(F32))
        dy = dxo_v * (1.0 + mod_ref[2:3, :])
        st_ref[1:2, :] += _sum0(dy)
        dyb = dy.astype(BF16)
        ds = _nt(dyb, w_ref[...].reshape(NDEV * w2_rows, d))
        acc[...] += _tn(s_ref[...], dyb)
        v = dwc_ref[...]
        mu = jnp.mean(v, axis=-1, keepdims=True)
        xc = v - mu
        rstd = lax.rsqrt(jnp.mean(xc * xc, axis=-1, keepdims=True) + EPS)
        xhat = xc * rstd
        ln = xhat * lng_ref[...] + lnb_ref[...]
        dln = ds * _silu_grad(ln, jax.nn.sigmoid(ln))
        st_ref[2:3, :] += _sum0(dln * xhat)
        st_ref[3:4, :] += _sum0(dln)
        dxh = dln * lng_ref[...]
        dd = rstd * (dxh - jnp.mean(dxh, axis=-1, keepdims=True) - xhat * jnp.mean(dxh * xhat, axis=-1, keepdims=True))
        dd_ref[...] = dd
        st_ref[4:5, :] += _sum0(dd)

        @pl.when(i == last)
        def _():
            dw_ref[...] = acc[...].astype(BF16)

    tile = pl.BlockSpec((tm, d), lambda i: (i, 0))
    return _call(body, name="conv_bwd_mid", grid=(s // tm,),
        in_specs=[tile, tile, tile, tile, _row(8, d), _row(1, d), _row(1, d), _weight_spec(w2_rows, d, w2_idx)],
        out_specs=[tile, pl.BlockSpec((d, d), lambda i: (0, 0)), _row(8, d)],
        out_shape=[jax.ShapeDtypeStruct((s, d), F32), jax.ShapeDtypeStruct((d, d), BF16),
                   jax.ShapeDtypeStruct((8, d), F32)],
        scratch_shapes=[pltpu.VMEM((d, d), F32)],
        args=(dxo, y, dwc, sb, mod, ln_g, ln_b, gw))[0]


def _conv_bwd_in(dd, glu, u, hb, x, dxo, wdw, gw, w1_rows, w1_idx, mod, gmix, tm, ride):
    s, d = x.shape
    last = s // tm - 1
    off = CONV_HALO - (CONV_WIDTH - 1)
    rw = 32
    tap_group = 16

    def kernel_body(dd_ref, ddn_ref, glu_ref, glp_ref, u_ref, h_ref, x_ref, dxo_ref, wdw_ref, w_ref, mod_ref, g_ref,
                    dxi_ref, dw_ref, dwdw_ref, st_ref, bufd, bufg, shd, shg, dgl, accw, acc):
        i = pl.program_id(0)

        @pl.when(i == 0)
        def _():
            st_ref[...] = jnp.zeros_like(st_ref)
            accw[...] = jnp.zeros_like(accw)
            acc[...] = jnp.zeros_like(acc)

        bufd[pl.ds(0, tm), :] = dd_ref[...]
        bufd[pl.ds(tm, CONV_HALO), :] = jnp.where(i < last, ddn_ref[...], 0.0)
        bufg[pl.ds(0, CONV_HALO), :] = jnp.where(i > 0, glp_ref[...], 0.0)
        bufg[pl.ds(CONV_HALO, tm), :] = glu_ref[...]
        _shifted_copies(bufd, shd, tm + CONV_HALO)
        _shifted_copies(bufg, shg, CONV_HALO + tm)
        for cb in range(d // LANES):
            cols = pl.ds(cb * LANES, LANES)
            taps = wdw_ref[:, cols]
            for r in range(tm // rw):
                part = jnp.zeros((rw, LANES), F32)
                for k in range(CONV_WIDTH):
                    part = part + _window(shd, r * rw + CONV_WIDTH - 1 - k, rw, cols) * taps[k:k + 1, :]
                dgl[pl.ds(r * rw, rw), cols] = part
            for k0 in range(0, CONV_WIDTH, tap_group):
                group = range(k0, min(CONV_WIDTH, k0 + tap_group))
                sums = {k: jnp.zeros((8, LANES), F32) for k in group}
                for r in range(tm // rw):
                    ddc = bufd[pl.ds(r * rw, rw), cols]
                    for k in group:
                        p = _window(shg, r * rw + off + k, rw, cols) * ddc
                        for q in range(rw // 8):
                            sums[k] = sums[k] + p[q * 8:(q + 1) * 8, :]
                for k in group:
                    accw[k, :, cols] += sums[k]
        dglu = dgl[...]
        uv = u_ref[...]
        a, g = uv[:, :d], uv[:, d:]
        sg = jax.nn.sigmoid(g)
        da = dglu * sg
        dgt = dglu * a * (sg * (1.0 - sg))
        du = jnp.concatenate([da, dgt], axis=1)
        st_ref[0:1, :] += _sum0(du)
        dub = du.astype(BF16)
        w = w_ref[...].reshape(NDEV * w1_rows, d)
        dh = _nn(dub, w)
        acc[...] += _tn(dub, h_ref[...])
        n, rinv = _rms(x_ref[...])
        dx, dsh, dsc, dgain = _rms_mod_bwd(dh, n, rinv, g_ref[...], mod_ref[1:2, :])
        dxi_ref[...] = dxo_ref[...] + dx
        st_ref[1:2, 0:d] += dsh
        st_ref[2:3, 0:d] += dsc
        st_ref[3:4, 0:d] += dgain

        @pl.when(i == last)
        def _():
            dw_ref[...] = acc[...].astype(BF16)
            dwdw_ref[...] = jnp.sum(accw[...], axis=1)

    tile = pl.BlockSpec((tm, d), lambda i: (i, 0))
    prv = pl.BlockSpec((CONV_HALO, d), lambda i: (jnp.maximum(i * (tm // CONV_HALO) - 1, 0), 0))
    nxt = pl.BlockSpec((CONV_HALO, d), lambda i: (jnp.minimum((i + 1) * (tm // CONV_HALO), s // CONV_HALO - 1), 0))
    return _call(kernel_body, name="conv_bwd_in", grid=(s // tm,),
        in_specs=[tile, nxt, tile, prv, pl.BlockSpec((tm, 2 * d), lambda i: (i, 0)), tile, tile, tile,
                  _row(CONV_WIDTH, d), _weight_spec(w1_rows, d, w1_idx), _row(8, d), _row(1, d)],
        out_specs=[tile, pl.BlockSpec((2 * d, d), lambda i: (0, 0)), _row(CONV_HALO, d), _row(8, 2 * d)],
        out_shape=[jax.ShapeDtypeStruct((s, d), F32), jax.ShapeDtypeStruct((2 * d, d), BF16),
                   jax.ShapeDtypeStruct((CONV_HALO, d), F32), jax.ShapeDtypeStruct((8, 2 * d), F32)],
        scratch_shapes=[pltpu.VMEM((tm + CONV_HALO, d), F32), pltpu.VMEM((CONV_HALO + tm, d), F32),
                        pltpu.VMEM((8, tm + CONV_HALO, d), F32), pltpu.VMEM((8, CONV_HALO + tm, d), F32),
                        pltpu.VMEM((tm, d), F32), pltpu.VMEM((CONV_HALO, 8, d), F32), pltpu.VMEM((2 * d, d), F32)],
        args=(dd, dd, glu, glu, u, hb, x, dxo, wdw, gw, mod, gmix), ride=ride)


def kernel(x, c, ada_w, ada_b, norm_mix_g, norm_ffn_g, conv_w1, conv_b1, conv_wdw, conv_bdw, conv_ln_g, conv_ln_b, conv_w2, conv_b2, pool_w, pool_ls, ffn_w_gate, ffn_w_up, ffn_w_down, final_g, loss_target, m_ada_w, m_ada_b, m_norm_mix_g, m_norm_ffn_g, m_conv_w1, m_conv_b1, m_conv_wdw, m_conv_bdw, m_conv_ln_g, m_conv_ln_b, m_conv_w2, m_conv_b2, m_pool_w, m_pool_ls, m_ffn_w_gate, m_ffn_w_up, m_ffn_w_down, m_final_g, v_ada_w, v_ada_b, v_norm_mix_g, v_norm_ffn_g, v_conv_w1, v_conv_b1, v_conv_wdw, v_conv_bdw, v_conv_ln_g, v_conv_ln_b, v_conv_w2, v_conv_b2, v_pool_w, v_pool_ls, v_ffn_w_gate, v_ffn_w_up, v_ffn_w_down, v_final_g):
    _, s, d = x.shape
    f = ffn_w_down.shape[1] * NDEV
    fs = f // NDEV
    r1, r2 = 2 * d // NDEV, d // NDEV
    ng = len(POOL_WINDOWS)
    dg = d // ng
    pr = ng * (dg // NDEV) * dg // d
    ncol = ada_w.shape[2]
    dc = d // NDEV
    tm = min(256, s)
    me = _my_index()
    x0 = x.reshape(s, d)
    target = loss_target.reshape(s, d)

    small = jnp.concatenate([c.reshape(NDEV, dc), conv_wdw[0], pool_ls], axis=0)
    shard_a = jnp.concatenate([conv_w1[0].T, conv_w2[0]], axis=0).astype(BF16)
    small_all, (gwa,) = _small_exchange(small, _Ride("gather", [shard_a]), "allgather_first")
    c_all = small_all[:, 0:NDEV, :].reshape(NDEV, d)
    wdw = small_all[:, NDEV:NDEV + CONV_WIDTH, :].transpose(1, 0, 2).reshape(CONV_WIDTH, d)
    ls = small_all[:, NDEV + CONV_WIDTH, :].reshape(1, d)
    bias = lax.dynamic_slice_in_dim(ada_b, me * ncol, ncol, axis=1)[:, None, :]
    mod_cols = _ada_forward(c_all, ada_w, bias)
    mod_all = _allgather_small(mod_cols.reshape(2 * NDEV, ncol), "allgather_mod")
    mod_mine = lax.dynamic_index_in_dim(mod_all.reshape(NDEV, 2, NDEV, ncol), me, axis=2, keepdims=False)
    mod = mod_mine.transpose(1, 0, 2).reshape(2, 6, d)
    mod = jnp.concatenate([mod, jnp.zeros((2, 2, d), F32)], axis=1)

    shard_b1 = ffn_w_gate[0].T.astype(BF16)
    shard_b2 = jnp.concatenate([ffn_w_up[0].T, ffn_w_down[0]], axis=0).astype(BF16)
    shard_c = jnp.concatenate([ffn_w_gate[1].T, ffn_w_up[1].T, pool_w.reshape(pr, d)], axis=0).astype(BF16)
    shard_d = ffn_w_down[1].astype(BF16)
    w1_at, w2_at = (gwa, 0), (gwa, r1 // r2)

    (h0, u, glu), (gwb1,) = _conv_in(x0, mod[0], norm_mix_g[0:1], w1_at[0], r1, w1_at[1], conv_b1, tm,
                                    _Ride("gather", [shard_b1]))
    (dwc, sb, y0, x1), (gwb2,) = _conv_mid(glu, wdw, conv_bdw, conv_ln_g, conv_ln_b, w2_at[0], r2, w2_at[1], conv_b2,
                                           x0, mod[0], tm, _Ride("gather", [shard_b2]))
    ffn0_w = [(gwb1, 0), (gwb2, 0), (gwb2, 1)]
    (h1, gg0, uu0, yf0, x2), (gwc,) = _ffn_fwd(x1, mod[0], norm_ffn_g[0:1], ffn0_w, fs, f, tm, "ffn_fwd0",
                                               _Ride("gather", [shard_c]))
    pw = gwc[:, 2 * fs:2 * fs + pr, :].reshape(NDEV, ng, dg // NDEV, dg).transpose(1, 0, 2, 3).reshape(ng, dg, dg)
    (mixed, yp, x3), (gwd,) = _pool_fwd(x2, mod[1], norm_mix_g[1:2], pw, ls, tm, _Ride("gather", [shard_d]))
    ffn1_w = [(gwc, 0), (gwc, 1), (gwd, 0)]
    (h3, gg1, uu1, yf1, dx4, st_loss), _ = _ffn_fwd(x3, mod[1], norm_ffn_g[1:2], ffn1_w, fs, f, tm, "ffn_fwd1",
                                                    loss=(target, final_g.reshape(1, d)))

    fb = f // 2 if (f // 2) % 128 == 0 else f
    ts = min(512, s)
    (dgb, dub, ab, dyb, dx3, st_f1), _ = _ffn_bwd(dx4, x3, gg1, uu1, yf1, mod[1], norm_ffn_g[1:2], ffn1_w, fs, f, tm,
                                                  "ffn_bwd1")
    gf1 = _ffn_wgrad(dgb, dub, ab, h3, dyb, fb, ts, "ffn_wgrad1")
    dx2, gpw, st_p = _pool_bwd(dx3, x2, yp, mixed, mod[1], norm_mix_g[1:2], pw, ls, tm)
    (dgb, dub, ab, dyb, dx1, st_f0), (land_f1,) = _ffn_bwd(dx2, x1, gg0, uu0, yf0, mod[0], norm_ffn_g[0:1], ffn0_w, fs, f,
                                                           tm, "ffn_bwd0", _Ride("scatter", [gf1]))
    gf0 = _ffn_wgrad(dgb, dub, ab, h1, dyb, fb, ts, "ffn_wgrad0")
    dd, gw2, st_m = _conv_bwd_mid(dx1, y0, dwc, sb, mod[0], conv_ln_g, conv_ln_b, w2_at[0], r2, w2_at[1], tm)
    (dx0, gw1, gwdw, st_c), (land_f0, land_pw, land_w2) = _conv_bwd_in(
        dd, glu, u, h0, x0, dx1, wdw, w1_at[0], r1, w1_at[1], mod[0], norm_mix_g[0:1], tm,
        _Ride("scatter", [gf0, gpw, gw2[None]]))

    prow = jnp.concatenate([
        st_c[1:3, 0:d], st_m[0:1], st_f0[0:3], st_p[0:2], st_p[4:5], st_f1[0:3],
        st_c[3:4, 0:d], st_p[3:4], st_f0[3:4], st_f1[3:4],
        st_c[0:1, 0:d], st_c[0:1, d:2 * d], st_m[4:5], st_m[2:4], st_m[1:2], st_loss[0:1],
        gwdw[0:CONV_WIDTH], st_p[5:6], st_loss[2:3]], axis=0)
    p_all, (land_w1,) = _small_exchange(prow, _Ride("scatter", [gw1[None]]), "allgather_stats")
    psum = _sum_slots(p_all, "sum_stats")
    loss = psum[prow.shape[0] - 1, 0]

    tr = lambda a: jnp.swapaxes(a, 1, 2)
    ffn_out = _finalize_ffn(land_f0, land_f1, tr(ffn_w_gate), tr(ffn_w_up), ffn_w_down,
                            tr(m_ffn_w_gate), tr(m_ffn_w_up), m_ffn_w_down, tr(v_ffn_w_gate), tr(v_ffn_w_up), v_ffn_w_down)
    fin_w1 = _finalize(land_w1.reshape(NDEV, r1, d), conv_w1[0], m_conv_w1[0], v_conv_w1[0], True, "finalize_w1")
    fin_w2 = _finalize(land_w2.reshape(NDEV, r2, d), conv_w2[0], m_conv_w2[0], v_conv_w2[0], False, "finalize_w2")
    pshape = (ng * (dg // NDEV), dg)
    fin_pw = _finalize(land_pw.reshape((NDEV,) + pshape), pool_w.reshape(pshape), m_pool_w.reshape(pshape),
                       v_pool_w.reshape(pshape), False, "finalize_pool_w")
    dmod_all = p_all[:, 0:12, :].reshape(NDEV, 2, 6 * d)
    dmod_cols = lax.dynamic_slice_in_dim(dmod_all, me * ncol, ncol, axis=2).transpose(1, 0, 2)
    g_ada_w = _ada_wgrad(c_all, dmod_cols)

    def adam(w, g, m, v, name):
        shp = w.shape
        w2d = (-1, shp[-1])
        dl, mo, vo = _adamw(w.reshape(w2d), g.reshape(w2d), m.reshape(w2d), v.reshape(w2d), name)
        return dl.reshape(shp), mo.reshape(shp), vo.reshape(shp)

    rep_names = ["ada_b", "norm_mix_g", "norm_ffn_g", "conv_b1", "conv_bdw", "conv_ln_g", "conv_ln_b", "conv_b2", "final_g"]
    rep_w = [ada_b, norm_mix_g, norm_ffn_g, conv_b1, conv_bdw, conv_ln_g, conv_ln_b, conv_b2, final_g]
    rep_m = [m_ada_b, m_norm_mix_g, m_norm_ffn_g, m_conv_b1, m_conv_bdw, m_conv_ln_g, m_conv_ln_b, m_conv_b2, m_final_g]
    rep_v = [v_ada_b, v_norm_mix_g, v_norm_ffn_g, v_conv_b1, v_conv_bdw, v_conv_ln_g, v_conv_ln_b, v_conv_b2, v_final_g]
    nrep = sum(w.size for w in rep_w) // d
    pad = jnp.zeros(((-nrep) % 8, d), F32)

    def pack(arrs, fill):
        return jnp.concatenate([a.reshape(-1, d) for a in arrs] + [pad + fill], axis=0)

    rep_g = jnp.concatenate([psum[0:nrep], pad], axis=0)
    rep_d, rep_mo, rep_vo = _adamw(pack(rep_w, 0.0), rep_g, pack(rep_m, 0.0), pack(rep_v, 1.0), "adamw_replicated")

    def unpack(packed):
        out, cur = [], 0
        for w in rep_w:
            k = w.size // d
            out.append(packed[cur:cur + k].reshape(w.shape))
            cur += k
        return out

    rep = dict(zip(rep_names, zip(unpack(psum), unpack(rep_d), unpack(rep_mo), unpack(rep_vo))))

    g_wdw_full = psum[nrep:nrep + CONV_WIDTH]
    g_wdw = lax.dynamic_slice_in_dim(g_wdw_full, me * dc, dc, axis=1)
    g_ls = lax.dynamic_slice_in_dim(psum[nrep + CONV_WIDTH:nrep + CONV_WIDTH + 1], me * dc, dc, axis=1)
    tiny = lambda a, b: jnp.concatenate([a.reshape(CONV_WIDTH, dc), b.reshape(1, dc)], axis=0)
    t_d, t_m, t_v = _adamw(tiny(conv_wdw, pool_ls), tiny(g_wdw, g_ls), tiny(m_conv_wdw, m_pool_ls),
                           tiny(v_conv_wdw, v_pool_ls), "adamw_taps")

    def taps(a):
        return a[0:CONV_WIDTH][None], a[CONV_WIDTH:CONV_WIDTH + 1]

    sharded = {
        "ada_w": (g_ada_w,) + adam(ada_w, g_ada_w, m_ada_w, v_ada_w, "adamw_ada_w"),
        "conv_w1": tuple(a[None] for a in fin_w1),
        "conv_w2": tuple(a[None] for a in fin_w2),
        "pool_w": tuple(a.reshape(pool_w.shape) for a in fin_pw),
        "ffn_w_gate": tuple(tr(a) for a in ffn_out[0::3]),
        "ffn_w_up": tuple(tr(a) for a in ffn_out[1::3]),
        "ffn_w_down": tuple(ffn_out[2::3]),
        "conv_wdw": (g_wdw[None], taps(t_d)[0], taps(t_m)[0], taps(t_v)[0]),
        "pool_ls": (g_ls, taps(t_d)[1], taps(t_m)[1], taps(t_v)[1]),
    }
    every = {**rep, **sharded}
    order = ["ada_w", "ada_b", "norm_mix_g", "norm_ffn_g", "conv_w1", "conv_b1", "conv_wdw", "conv_bdw", "conv_ln_g",
             "conv_ln_b", "conv_w2", "conv_b2", "pool_w", "pool_ls", "ffn_w_gate", "ffn_w_up", "ffn_w_down", "final_g"]
    grads = [every[n][0] for n in order]
    deltas = [every[n][1] for n in order]
    new_m = [every[n][2] for n in order]
    new_v = [every[n][3] for n in order]
    return (loss, dx0.reshape(1, s, d), *grads, *deltas, *new_m, *new_v)
```

```python
import jax
import jax.numpy as jnp
from jax import lax
from jax.experimental import pallas as pl
from jax.experimental.pallas import tpu as pltpu

NDEV = 8
EPS = 1e-6
CONV_WIDTH = 31
POOL_WINDOWS = (2, 4, 8, 16)
CONV_HALO = 32
POOL_HALO = 16
ADAM_LR = 0.001
ADAM_B1 = 0.9
ADAM_B2 = 0.999
ADAM_EPS = 1e-08
ADAM_WD = 0.01
ADAM_STEP = 10
VMEM_LIMIT = 56 * 2**20
MESH = pl.DeviceIdType.MESH
F32 = jnp.float32
BF16 = jnp.bfloat16


def _nt(a, b):
    return lax.dot_general(a, b, (((1,), (1,)), ((), ())), preferred_element_type=F32)


def _nn(a, b):
    return lax.dot_general(a, b, (((1,), (0,)), ((), ())), preferred_element_type=F32)


def _tn(a, b):
    return lax.dot_general(a, b, (((0,), (0,)), ((), ())), preferred_element_type=F32)


def _sum0(v):
    return jnp.sum(v, axis=0, keepdims=True)


def _rms(x):
    rinv = lax.rsqrt(jnp.mean(x * x, axis=-1, keepdims=True) + EPS)
    return x * rinv, rinv


def _rms_mod_bwd(dh, n, rinv, g, sc):
    dhs = dh * (1.0 + sc)
    dn = dhs * g
    dx = rinv * (dn - n * jnp.mean(dn * n, axis=-1, keepdims=True))
    return dx, _sum0(dh), _sum0(dh * (n * g)), _sum0(dhs * n)


def _silu_grad(z, sg):
    return sg * (1.0 + z * (1.0 - sg))


def _params(*sem):
    return pltpu.CompilerParams(dimension_semantics=sem, vmem_limit_bytes=VMEM_LIMIT)


def _row(i, d):
    return pl.BlockSpec((i, d), lambda *_: (0, 0))


def _weight_spec(rows, d, idx):
    return pl.BlockSpec((NDEV, rows, d), lambda *_: (0, idx, 0), pipeline_mode=pl.Buffered(1))


def _my_index():
    return 4 * lax.axis_index("x") + 2 * lax.axis_index("y") + lax.axis_index("c")


def _peer(k):
    x, y, c = lax.axis_index("x"), lax.axis_index("y"), lax.axis_index("c")
    px = 1 - x if k & 4 else x
    py = 1 - y if k & 2 else y
    pc = 1 - c if k & 1 else c
    return (px, py, pc), 4 * px + 2 * py + pc


def _gather_sems():
    return [pltpu.SemaphoreType.DMA((NDEV - 1,)), pltpu.SemaphoreType.DMA((NDEV - 1,)), pltpu.SemaphoreType.DMA((1,))]


class _Gather:
    def __init__(self, srcs, dsts, sems):
        self.src, self.dst = srcs[0], dsts[0]
        self.send, self.recv, self.local = sems
        x, y, c = lax.axis_index("x"), lax.axis_index("y"), lax.axis_index("c")
        self.me, self.sibling, self.core = (x, y, c), (x, y, 1 - c), c
        self.chips = [(1 - x, y), (x, 1 - y), (1 - x, 1 - y)]

    def _copy(self, k, block, to, from_input=False):
        slot = self.dst.at[4 * block[0] + 2 * block[1] + block[2]]
        return pltpu.make_async_remote_copy(
            src_ref=self.src if from_input else slot, dst_ref=slot, send_sem=self.send.at[k], recv_sem=self.recv.at[k],
            device_id=to, device_id_type=MESH)

    def _own(self):
        return pltpu.make_async_copy(self.src, self.dst.at[_my_index()], self.local.at[0])

    def _first(self):
        return [self._copy(0, self.me, self.sibling, True)] + [
            self._copy(1 + j, self.me, (*chip, self.core), True) for j, chip in enumerate(self.chips)]

    def start(self):
        self._own().start()
        for cp in self._first():
            cp.start()

    def forward(self):
        for j, chip in enumerate(self.chips):
            self._copy(1 + j, (*chip, self.core), self.me).wait_recv()
            self._copy(4 + j, (*chip, self.core), self.sibling).start()

    def finish(self):
        self._copy(0, self.sibling, self.me).wait_recv()
        for j, chip in enumerate(self.chips):
            self._copy(4 + j, (*chip, 1 - self.core), self.me).wait_recv()
        for cp in self._first():
            cp.wait_send()
        for j, chip in enumerate(self.chips):
            self._copy(4 + j, (*chip, self.core), self.sibling).wait_send()
        self._own().wait()


def _scatter_sems(n):
    return [pltpu.SemaphoreType.DMA((7 * n,)), pltpu.SemaphoreType.DMA((7 * n,)), pltpu.SemaphoreType.DMA((n,))]


class _Scatter:
    def __init__(self, srcs, dsts, sems):
        send_sems, recv_sems, local_sems = sems
        me = _my_index()
        self.copies = []
        for a, (src, dst) in enumerate(zip(srcs, dsts)):
            r = dst.shape[2]
            self.copies.append(pltpu.make_async_copy(src.at[:, pl.ds(me * r, r), :], dst.at[me], local_sems.at[a]))
            for k in range(1, NDEV):
                dev, p = _peer(k)
                self.copies.append(pltpu.make_async_remote_copy(
                    src_ref=src.at[:, pl.ds(p * r, r), :], dst_ref=dst.at[me],
                    send_sem=send_sems.at[a * 7 + k - 1], recv_sem=recv_sems.at[a * 7 + k - 1],
                    device_id=dev, device_id_type=MESH))

    def start(self):
        for cp in self.copies:
            cp.start()

    def forward(self):
        pass

    def finish(self):
        for cp in self.copies:
            cp.wait()


def _land_shape(part):
    a, r, c = part.shape
    return jax.ShapeDtypeStruct((NDEV, a, r // NDEV, c), part.dtype)


ANY = pl.BlockSpec(memory_space=pl.ANY)


class _Ride:
    def __init__(self, kind, srcs):
        self.kind, self.srcs = kind, list(srcs)
        if kind == "gather":
            self.out_shape = [jax.ShapeDtypeStruct((NDEV,) + a.shape, a.dtype) for a in self.srcs]
            self.sems = _gather_sems()
        else:
            self.out_shape = [_land_shape(a) for a in self.srcs]
            self.sems = _scatter_sems(len(self.srcs))

    def exchange(self, ins, outs, sems):
        return (_Gather if self.kind == "gather" else _Scatter)(ins, outs, sems)


def _small_exchange(small, ride, name):
    r, c = small.shape
    nr = len(ride.srcs)

    def body(v_ref, *refs):
        rin, refs = refs[:nr], refs[nr:]
        out_ref, refs = refs[0], refs[1:]
        rout, refs = refs[:nr], refs[nr:]
        send_sems, recv_sems, rsems = refs[0], refs[1], refs[2:]
        big = ride.exchange(rin, rout, rsems)
        big.start()
        copies = _small_pushes(v_ref, out_ref, send_sems, recv_sems)
        for cp in copies:
            cp.start()
        for cp in copies:
            cp.wait()
        big.forward()
        big.finish()

    res = pl.pallas_call(body, name=name,
        out_shape=[jax.ShapeDtypeStruct((NDEV, r, c), small.dtype)] + ride.out_shape,
        in_specs=[pl.BlockSpec(memory_space=pltpu.VMEM)] + [ANY] * nr,
        out_specs=[pl.BlockSpec(memory_space=pltpu.VMEM)] + [ANY] * nr,
        scratch_shapes=[pltpu.SemaphoreType.DMA((NDEV - 1,)), pltpu.SemaphoreType.DMA((NDEV - 1,))] + ride.sems,
    )(small, *ride.srcs)
    return res[0], res[1:]


def _small_pushes(src_ref, out_ref, send_sems, recv_sems):
    me = _my_index()
    out_ref[me] = src_ref[...]
    copies = []
    for k in range(1, NDEV):
        dev, _ = _peer(k)
        copies.append(pltpu.make_async_remote_copy(
            src_ref=src_ref, dst_ref=out_ref.at[me], send_sem=send_sems.at[k - 1], recv_sem=recv_sems.at[k - 1],
            device_id=dev, device_id_type=MESH))
    return copies


def _prologue(small, ada_w, bias, ride):
    r, c = small.shape
    nl, d, ncol = ada_w.shape
    nr = len(ride.srcs)

    def body(v_ref, w_ref, b_ref, *refs):
        rin, refs = refs[:nr], refs[nr:]
        out_ref, mod_ref, refs = refs[0], refs[1], refs[2:]
        rout, refs = refs[:nr], refs[nr:]
        cols, send1, recv1, send2, recv2, rsems = refs[0], refs[1], refs[2], refs[3], refs[4], refs[5:]
        big = ride.exchange(rin, rout, rsems)
        big.start()
        first = _small_pushes(v_ref, out_ref, send1, recv1)
        for cp in first:
            cp.start()
        for cp in first:
            cp.wait()
        for layer in range(nl):
            acc = jnp.zeros((NDEV, ncol), F32) + b_ref[layer]
            for j in range(d // c):
                cj = out_ref[:, j, :]
                acc = acc + jnp.dot(cj * jax.nn.sigmoid(cj), w_ref[layer, pl.ds(j * c, c), :],
                                    preferred_element_type=F32, precision=lax.Precision.HIGHEST)
            cols[pl.ds(layer * NDEV, NDEV), :] = acc
        second = _small_pushes(cols, mod_ref, send2, recv2)
        for cp in second:
            cp.start()
        for cp in second:
            cp.wait()
        big.forward()
        big.finish()

    vmem = pl.BlockSpec(memory_space=pltpu.VMEM)
    sem = pltpu.SemaphoreType.DMA((NDEV - 1,))
    res = pl.pallas_call(body, name="prologue",
        out_shape=[jax.ShapeDtypeStruct((NDEV, r, c), F32), jax.ShapeDtypeStruct((NDEV, nl * NDEV, ncol), F32)] + ride.out_shape,
        in_specs=[vmem, vmem, vmem] + [ANY] * nr, out_specs=[vmem, vmem] + [ANY] * nr,
        scratch_shapes=[pltpu.VMEM((nl * NDEV, ncol), F32), sem, sem, sem, sem] + ride.sems,
        compiler_params=pltpu.CompilerParams(vmem_limit_bytes=VMEM_LIMIT),
    )(small, ada_w, bias, *ride.srcs)
    return res[0], res[1], res[2:]


def _call(kernel_body, *, name, grid, in_specs, out_specs, out_shape, args, scratch_shapes=(), ride=None):
    n_in, n_out, n_sc = len(in_specs), len(out_specs), len(scratch_shapes)
    nr = len(ride.srcs) if ride else 0
    in_specs, out_specs, out_shape = list(in_specs), list(out_specs), list(out_shape)
    scratch_shapes, args = list(scratch_shapes), list(args)
    if ride is not None:
        in_specs += [ANY] * nr
        out_specs += [ANY] * nr
        out_shape += ride.out_shape
        args += ride.srcs
        scratch_shapes += ride.sems

    def body(*refs):
        ins, refs = refs[:n_in], refs[n_in:]
        rin, refs = refs[:nr], refs[nr:]
        outs, refs = refs[:n_out], refs[n_out:]
        rout, refs = refs[:nr], refs[nr:]
        scratch, rsems = refs[:n_sc], refs[n_sc:]
        first, last = True, True
        for axis, extent in enumerate(grid):
            first &= pl.program_id(axis) == 0
            last &= pl.program_id(axis) == extent - 1
        if ride is not None:
            middle = last if len(grid) > 1 else pl.program_id(0) == (3 * grid[0]) // 4
            exchange = ride.exchange(rin, rout, rsems)
            pl.when(first)(exchange.start)
            pl.when(middle)(exchange.forward)
        kernel_body(*ins, *outs, *scratch)
        if ride is not None:
            pl.when(last)(exchange.finish)

    res = pl.pallas_call(body, name=name, grid=grid, in_specs=in_specs, out_specs=out_specs, out_shape=out_shape,
                         scratch_shapes=scratch_shapes, compiler_params=_params(*(("arbitrary",) * len(grid))))(*args)
    return res[:n_out], res[n_out:]


def _ada_wgrad(c_all, dmod):
    nl, _, ncol = dmod.shape
    d = c_all.shape[1]
    bd = min(d, 256)

    def body(c_ref, dm_ref, o_ref):
        cv = c_ref[...]
        ca = cv * jax.nn.sigmoid(cv)
        o_ref[0] = lax.dot_general(ca, dm_ref[0], (((0,), (0,)), ((), ())), preferred_element_type=F32,
                                   precision=lax.Precision.HIGHEST)

    return pl.pallas_call(body, name="ada_wgrad", grid=(nl, d // bd),
        in_specs=[pl.BlockSpec((NDEV, bd), lambda i, j: (0, j)), pl.BlockSpec((1, NDEV, ncol), lambda i, j: (i, 0, 0))],
        out_specs=pl.BlockSpec((1, bd, ncol), lambda i, j: (i, j, 0)),
        out_shape=jax.ShapeDtypeStruct((nl, d, ncol), F32),
        compiler_params=_params("arbitrary", "arbitrary"),
    )(c_all, dmod)


def _row_block(r, c, bytes_per_row_elem=4, budget=2 * 2**20):
    if r * c * bytes_per_row_elem <= budget or r % 8:
        return r
    best = 8
    for b in range(8, r + 1, 8):
        if r % b == 0 and b * c * bytes_per_row_elem <= budget:
            best = b
    return best


def _sum_slots(land, name):
    _, r, c = land.shape
    br = _row_block(r, c, 8 * land.dtype.itemsize)

    def body(l_ref, o_ref):
        acc = l_ref[0].astype(F32)
        for s in range(1, NDEV):
            acc = acc + l_ref[s].astype(F32)
        o_ref[...] = acc

    return pl.pallas_call(body, name=name, grid=(r // br,),
        in_specs=[pl.BlockSpec((NDEV, br, c), lambda i: (0, i, 0))],
        out_specs=pl.BlockSpec((br, c), lambda i: (i, 0)),
        out_shape=jax.ShapeDtypeStruct((r, c), F32),
        compiler_params=_params("arbitrary"),
    )(land)


def _adamw(w, g, m, v, name):
    r, c = w.shape
    br = _row_block(r, c)

    def body(w_ref, g_ref, m_ref, v_ref, d_ref, mo_ref, vo_ref):
        d_ref[...], mo_ref[...], vo_ref[...] = _adam_math(w_ref[...], g_ref[...], m_ref[...], v_ref[...])

    spec = pl.BlockSpec((br, c), lambda i: (i, 0))
    return pl.pallas_call(body, name=name, grid=(r // br,),
        in_specs=[spec] * 4, out_specs=[spec] * 3,
        out_shape=[jax.ShapeDtypeStruct((r, c), F32)] * 3,
        compiler_params=_params("arbitrary"),
    )(w, g, m, v)


def _adam_math(w, g, m, v):
    m2 = ADAM_B1 * m + (1.0 - ADAM_B1) * g
    v2 = ADAM_B2 * v + (1.0 - ADAM_B2) * (g * g)
    m_hat = m2 / (1.0 - ADAM_B1 ** ADAM_STEP)
    v_hat = v2 / (1.0 - ADAM_B2 ** ADAM_STEP)
    return -ADAM_LR * (m_hat / (jnp.sqrt(v_hat) + ADAM_EPS) + ADAM_WD * w), m2, v2


def _slot_sum(land_ref, *lead):
    acc = land_ref[(0,) + lead].astype(F32)
    for s in range(1, NDEV):
        acc = acc + land_ref[(s,) + lead].astype(F32)
    return acc


def _finalize(land, w, m, v, transposed, name):
    _, r, c = land.shape
    cb = 256 if (transposed and c % 256 == 0) else c
    wblk = pl.BlockSpec((cb, r), lambda i: (i, 0)) if transposed else pl.BlockSpec((r, cb), lambda i: (0, i))

    def body(l_ref, w_ref, m_ref, v_ref, g_ref, d_ref, mo_ref, vo_ref):
        g = _slot_sum(l_ref)
        g = g.T if transposed else g
        g_ref[...] = g
        d_ref[...], mo_ref[...], vo_ref[...] = _adam_math(w_ref[...], g, m_ref[...], v_ref[...])

    return pl.pallas_call(body, name=name, grid=(c // cb,),
        in_specs=[pl.BlockSpec((NDEV, r, cb), lambda i: (0, 0, i)), wblk, wblk, wblk], out_specs=[wblk] * 4,
        out_shape=[jax.ShapeDtypeStruct(w.shape, F32)] * 4,
        compiler_params=_params("arbitrary"),
    )(land, w, m, v)


def _finalize_ffn(land0, land1, wg, wu, wd, mg, mu, md, vg, vu, vd):
    nl, fs, d = wg.shape
    db = min(256, d)

    def kernel_body(l0_ref, l1_ref, wg_ref, wu_ref, wd_ref, mg_ref, mu_ref, md_ref, vg_ref, vu_ref, vd_ref, *outs):
        layer = pl.program_id(0)
        triples = [(wg_ref, mg_ref, vg_ref), (wu_ref, mu_ref, vu_ref), (wd_ref, md_ref, vd_ref)]

        def run(land_ref):
            for j, (w_ref, m_ref, v_ref) in enumerate(triples):
                g = _slot_sum(land_ref, j)
                delta, m2, v2 = _adam_math(w_ref[0], g, m_ref[0], v_ref[0])
                for o_ref, val in zip(outs[j::3], (g, delta, m2, v2)):
                    o_ref[0] = val

        @pl.when(layer == 0)
        def _():
            run(l0_ref)

        @pl.when(layer == 1)
        def _():
            run(l1_ref)

    blk = pl.BlockSpec((1, fs, db), lambda l, i: (l, 0, i))
    lblk = [pl.BlockSpec((NDEV, 3, fs, db), lambda l, i: (0, 0, 0, i * (1 - l))),
            pl.BlockSpec((NDEV, 3, fs, db), lambda l, i: (0, 0, 0, i * l))]
    outs, _ = _call(kernel_body, name="finalize_ffn", grid=(nl, d // db),
        in_specs=lblk + [blk] * 9, out_specs=[blk] * 12, out_shape=[jax.ShapeDtypeStruct(wg.shape, F32)] * 12,
        args=(land0, land1, wg, wu, wd, mg, mu, md, vg, vu, vd))
    return outs


CONV_ROWS = 64
LANES = 128


def _shifted_copies(buf, sh, n):
    sh[0] = buf[...]
    for r in range(1, 8):
        sh[r, pl.ds(0, n - 8), :] = buf[pl.ds(r, n - 8), :]


def _window(sh, o, rows, cols):
    return sh[o % 8, pl.ds(o - o % 8, rows), cols]


def _conv_in(x, mod, gmix, gw, w1_rows, w1_idx, b1, tm, ride):
    s, d = x.shape

    def kernel_body(x_ref, mod_ref, g_ref, w_ref, b_ref, h_ref, u_ref, glu_ref):
        n, _ = _rms(x_ref[...])
        h = (n * g_ref[...]) * (1.0 + mod_ref[1:2, :]) + mod_ref[0:1, :]
        hb = h.astype(BF16)
        h_ref[...] = hb
        u = _nt(hb, w_ref[...].reshape(NDEV * w1_rows, d)) + b_ref[...]
        u_ref[...] = u
        glu_ref[...] = u[:, :d] * jax.nn.sigmoid(u[:, d:])

    tile = pl.BlockSpec((tm, d), lambda i: (i, 0))
    return _call(kernel_body, name="conv_in", grid=(s // tm,),
        in_specs=[tile, _row(8, d), _row(1, d), _weight_spec(w1_rows, d, w1_idx), _row(1, 2 * d)],
        out_specs=[tile, pl.BlockSpec((tm, 2 * d), lambda i: (i, 0)), tile],
        out_shape=[jax.ShapeDtypeStruct((s, d), BF16), jax.ShapeDtypeStruct((s, 2 * d), F32),
                   jax.ShapeDtypeStruct((s, d), F32)],
        args=(x, mod, gmix, gw, b1), ride=ride)


def _conv_mid(glu, wdw, bdw, ln_g, ln_b, gw, w2_rows, w2_idx, b2, x, mod, tm, ride):
    s, d = x.shape
    off = CONV_HALO - (CONV_WIDTH - 1)
    rc = min(CONV_ROWS, tm)
    nsub = 2 if s % (2 * tm) == 0 else 1
    tb = nsub * tm

    def kernel_body(glu_ref, halo_ref, wdw_ref, bdw_ref, lng_ref, lnb_ref, w_ref, b2_ref, x_ref, mod_ref,
                    dwc_ref, s_ref, y_ref, x1_ref, buf, sh):
        i = pl.program_id(0)
        w2 = w_ref[...].reshape(NDEV * w2_rows, d)
        for sub in range(nsub):
            rows = pl.ds(sub * tm, tm)
            if sub == 0:
                buf[pl.ds(0, CONV_HALO), :] = jnp.where(i > 0, halo_ref[...], 0.0)
            else:
                buf[pl.ds(0, CONV_HALO), :] = glu_ref[pl.ds(sub * tm - CONV_HALO, CONV_HALO), :]
            buf[pl.ds(CONV_HALO, tm), :] = glu_ref[rows, :]
            _shifted_copies(buf, sh, CONV_HALO + tm)
            for cb in range(d // LANES):
                cols = pl.ds(cb * LANES, LANES)
                taps = wdw_ref[:, cols]
                for r in range(tm // rc):
                    part = jnp.zeros((rc, LANES), F32) + bdw_ref[:, cols]
                    for k in range(CONV_WIDTH):
                        part = part + _window(sh, r * rc + off + k, rc, cols) * taps[k:k + 1, :]
                    dwc_ref[pl.ds(sub * tm + r * rc, rc), cols] = part
            acc = dwc_ref[rows, :]
            mu = jnp.mean(acc, axis=-1, keepdims=True)
            xc = acc - mu
            rstd = lax.rsqrt(jnp.mean(xc * xc, axis=-1, keepdims=True) + EPS)
            ln = (xc * rstd) * lng_ref[...] + lnb_ref[...]
            sb = (ln * jax.nn.sigmoid(ln)).astype(BF16)
            s_ref[rows, :] = sb
            y = _nn(sb, w2) + b2_ref[...]
            y_ref[rows, :] = y.astype(BF16)
            x1_ref[rows, :] = x_ref[rows, :] + (1.0 + mod_ref[2:3, :]) * y

    tile = pl.BlockSpec((tb, d), lambda i: (i, 0))
    halo = pl.BlockSpec((CONV_HALO, d), lambda i: (jnp.maximum(i * (tb // CONV_HALO) - 1, 0), 0))
    return _call(kernel_body, name="conv_mid", grid=(s // tb,),
        in_specs=[tile, halo, _row(CONV_WIDTH, d), _row(1, d), _row(1, d), _row(1, d),
                  _weight_spec(w2_rows, d, w2_idx), _row(1, d), tile, _row(8, d)],
        out_specs=[tile, tile, tile, tile],
        out_shape=[jax.ShapeDtypeStruct((s, d), F32), jax.ShapeDtypeStruct((s, d), BF16),
                   jax.ShapeDtypeStruct((s, d), BF16), jax.ShapeDtypeStruct((s, d), F32)],
        scratch_shapes=[pltpu.VMEM((CONV_HALO + tm, d), F32), pltpu.VMEM((8, CONV_HALO + tm, d), F32)],
        args=(glu, glu, wdw, bdw, ln_g, ln_b, gw, b2, x, mod), ride=ride)


def _ffn_fwd(x, mod, gffn, weights, fs, f, tm, name, ride=None, loss=None):
    s, d = x.shape

    last = s // tm - 1

    def kernel_body(x_ref, mod_ref, g_ref, wg_ref, wu_ref, wd_ref, *rest):
        xv = x_ref[...]
        n, _ = _rms(xv)
        hb = ((n * g_ref[...]) * (1.0 + mod_ref[4:5, :]) + mod_ref[3:4, :]).astype(BF16)
        gg = _nt(hb, wg_ref[...].reshape(f, d))
        uu = _nt(hb, wu_ref[...].reshape(f, d))
        ab = ((gg * jax.nn.sigmoid(gg)) * uu).astype(BF16)
        y = _nn(ab, wd_ref[...].reshape(f, d))
        xo = xv + (1.0 + mod_ref[5:6, :]) * y
        if loss is None:
            h_ref, gg_ref, uu_ref, y_ref, xo_ref = rest
            xo_ref[...] = xo
        else:
            t_ref, gfin_ref, h_ref, gg_ref, uu_ref, y_ref, dx_ref, st_ref = rest
            dx_ref[...] = _loss_tile(xo, t_ref[...], gfin_ref[...], st_ref, pl.program_id(0), last)
        h_ref[...] = hb
        gg_ref[...] = gg.astype(BF16)
        uu_ref[...] = uu.astype(BF16)
        y_ref[...] = y.astype(BF16)

    tile = pl.BlockSpec((tm, d), lambda i: (i, 0))
    wide = pl.BlockSpec((tm, f), lambda i: (i, 0))
    extra_in = [] if loss is None else [tile, _row(1, d)]
    extra_out = [] if loss is None else [_row(8, d)]
    return _call(kernel_body, name=name, grid=(s // tm,),
        in_specs=[tile, _row(8, d), _row(1, d)] + [_weight_spec(fs, d, idx) for _, idx in weights] + extra_in,
        out_specs=[tile, wide, wide, tile, tile] + extra_out,
        out_shape=[jax.ShapeDtypeStruct((s, d), BF16), jax.ShapeDtypeStruct((s, f), BF16),
                   jax.ShapeDtypeStruct((s, f), BF16), jax.ShapeDtypeStruct((s, d), BF16),
                   jax.ShapeDtypeStruct((s, d), F32)] + [jax.ShapeDtypeStruct((8, d), F32)] * len(extra_out),
        args=(x, mod, gffn) + tuple(a for a, _ in weights) + (() if loss is None else tuple(loss)), ride=ride)


def _loss_tile(x, target, g, st_ref, i, last):
    d = x.shape[1]

    @pl.when(i == 0)
    def _():
        st_ref[...] = jnp.zeros_like(st_ref)

    n, rinv = _rms(x)
    err = n * g - target
    dy = err * (1.0 / d)
    st_ref[0:1, :] += _sum0(dy * n)
    st_ref[1:2, :] += _sum0(err * err) * (0.5 / d)

    @pl.when(i == last)
    def _():
        st_ref[2:3, :] = jnp.zeros((1, d), F32) + jnp.sum(st_ref[1:2, :], axis=-1, keepdims=True)

    dn = dy * g
    return rinv * (dn - n * jnp.mean(dn * n, axis=-1, keepdims=True))


def _pool_fwd(x, mod, gmix, pw, ls, tm, ride):
    s, d = x.shape
    dg = d // len(POOL_WINDOWS)

    def body(x_ref, halo_ref, mod_ref, g_ref, pw_ref, ls_ref, mixed_ref, yp_ref, xo_ref, buf):
        i = pl.program_id(0)

        def hfun(xv):
            n, _ = _rms(xv)
            return (n * g_ref[...]) * (1.0 + mod_ref[1:2, :]) + mod_ref[0:1, :]

        xv = x_ref[...]
        h = hfun(xv)
        buf[pl.ds(0, POOL_HALO), :] = jnp.where(i > 0, hfun(halo_ref[...]), 0.0)
        buf[pl.ds(POOL_HALO, tm), :] = h
        t = i * tm + lax.broadcasted_iota(jnp.int32, (tm, 1), 0)
        gate = 1.0 + mod_ref[2:3, :]
        for gi, w in enumerate(POOL_WINDOWS):
            cols = pl.ds(gi * dg, dg)
            ws = buf[pl.ds(POOL_HALO, tm), cols]
            for j in range(1, w):
                ws = ws + buf[pl.ds(POOL_HALO - j, tm), cols]
            inv = 1.0 / jnp.minimum(t + 1, w).astype(F32)
            mb = (ws * inv - h[:, gi * dg:(gi + 1) * dg]).astype(BF16)
            mixed_ref[:, cols] = mb
            yp = _nn(mb, pw_ref[gi])
            yp_ref[:, cols] = yp.astype(BF16)
            xo_ref[:, cols] = xv[:, gi * dg:(gi + 1) * dg] + gate[:, gi * dg:(gi + 1) * dg] * (yp * ls_ref[:, cols])

    tile = pl.BlockSpec((tm, d), lambda i: (i, 0))
    halo = pl.BlockSpec((POOL_HALO, d), lambda i: (jnp.maximum(i * (tm // POOL_HALO) - 1, 0), 0))
    return _call(body, name="pool_fwd", grid=(s // tm,),
        in_specs=[tile, halo, _row(8, d), _row(1, d), pl.BlockSpec((len(POOL_WINDOWS), dg, dg), lambda i: (0, 0, 0)),
                  _row(1, d)],
        out_specs=[tile, tile, tile],
        out_shape=[jax.ShapeDtypeStruct((s, d), BF16), jax.ShapeDtypeStruct((s, d), BF16),
                   jax.ShapeDtypeStruct((s, d), F32)],
        scratch_shapes=[pltpu.VMEM((POOL_HALO + tm, d), F32)],
        args=(x, x, mod, gmix, pw, ls), ride=ride)


def _ffn_bwd(dxo, x, gg, uu, y, mod, gffn, weights, fs, f, tm, name, ride=None):
    s, d = x.shape

    def kernel_body(dxo_ref, x_ref, gg_ref, uu_ref, y_ref, mod_ref, g_ref, wg_ref, wu_ref, wd_ref,
                    dg_ref, du_ref, a_ref, dy_ref, dxi_ref, st_ref):
        @pl.when(pl.program_id(0) == 0)
        def _():
            st_ref[...] = jnp.zeros_like(st_ref)

        dxo_v = dxo_ref[...]
        dyb = (dxo_v * (1.0 + mod_ref[5:6, :])).astype(BF16)
        dy_ref[...] = dyb
        da = _nt(dyb, wd_ref[...].reshape(f, d))
        ggv, uuv = gg_ref[...].astype(F32), uu_ref[...].astype(F32)
        sg = jax.nn.sigmoid(ggv)
        silu = ggv * sg
        a_ref[...] = (silu * uuv).astype(BF16)
        dub = (da * silu).astype(BF16)
        dgb = (da * uuv * _silu_grad(ggv, sg)).astype(BF16)
        du_ref[...] = dub
        dg_ref[...] = dgb
        dh = _nn(dgb, wg_ref[...].reshape(f, d)) + _nn(dub, wu_ref[...].reshape(f, d))
        n, rinv = _rms(x_ref[...])
        dx, dsh, dsc, dgain = _rms_mod_bwd(dh, n, rinv, g_ref[...], mod_ref[4:5, :])
        dxi_ref[...] = dxo_v + dx
        st_ref[0:1, :] += dsh
        st_ref[1:2, :] += dsc
        st_ref[2:3, :] += _sum0(dxo_v * y_ref[...].astype(F32))
        st_ref[3:4, :] += dgain

    tile = pl.BlockSpec((tm, d), lambda i: (i, 0))
    wide = pl.BlockSpec((tm, f), lambda i: (i, 0))
    return _call(kernel_body, name=name, grid=(s // tm,),
        in_specs=[tile, tile, wide, wide, tile, _row(8, d), _row(1, d)] + [_weight_spec(fs, d, idx) for _, idx in weights],
        out_specs=[wide, wide, wide, tile, tile, _row(8, d)],
        out_shape=[jax.ShapeDtypeStruct((s, f), BF16)] * 3 + [jax.ShapeDtypeStruct((s, d), BF16),
                   jax.ShapeDtypeStruct((s, d), F32), jax.ShapeDtypeStruct((8, d), F32)],
        args=(dxo, x, gg, uu, y, mod, gffn) + tuple(a for a, _ in weights), ride=ride)


def _ffn_wgrad(dgb, dub, ab, h, dyb, fb, ts, name):
    s, f = dgb.shape
    d = h.shape[1]
    last = s // ts - 1

    def body(dg_ref, du_ref, a_ref, h_ref, dy_ref, o_ref, acc):
        t = pl.program_id(1)

        @pl.when(t == 0)
        def _():
            acc[...] = jnp.zeros_like(acc)

        hv = h_ref[...]
        acc[0] += _tn(dg_ref[...], hv)
        acc[1] += _tn(du_ref[...], hv)
        acc[2] += _tn(a_ref[...], dy_ref[...])

        @pl.when(t == last)
        def _():
            o_ref[...] = acc[...].astype(BF16)

    wide = pl.BlockSpec((ts, fb), lambda j, t: (t, j))
    tile = pl.BlockSpec((ts, d), lambda j, t: (t, 0))
    return pl.pallas_call(body, name=name, grid=(f // fb, s // ts),
        in_specs=[wide, wide, wide, tile, tile],
        out_specs=pl.BlockSpec((3, fb, d), lambda j, t: (0, j, 0)),
        out_shape=jax.ShapeDtypeStruct((3, f, d), BF16),
        scratch_shapes=[pltpu.VMEM((3, fb, d), F32)],
        compiler_params=_params("arbitrary", "arbitrary"),
    )(dgb, dub, ab, h, dyb)


def _pool_bwd(dxo, x, yp, mixed, mod, gmix, pw, ls, tm):
    s, d = x.shape
    ng = len(POOL_WINDOWS)
    dg = d // ng
    last = s // tm - 1

    def body(dxo_ref, dxh_ref, x_ref, yp_ref, mixed_ref, mod_ref, g_ref, pw_ref, ls_ref,
             dxi_ref, dpw_ref, st_ref, bufy, bufq, bufh, acc):
        i = pl.program_id(0)

        @pl.when(i == 0)
        def _():
            st_ref[...] = jnp.zeros_like(st_ref)
            acc[...] = jnp.zeros_like(acc)

        gate = 1.0 + mod_ref[2:3, :]
        lsv = ls_ref[...]
        dxo_v = dxo_ref[...]
        st_ref[2:3, :] += _sum0(dxo_v * yp_ref[...].astype(F32))
        bufy[pl.ds(0, tm), :] = (dxo_v * (gate * lsv)).astype(BF16)
        bufy[pl.ds(tm, POOL_HALO), :] = jnp.where(i < last, dxh_ref[...] * (gate * lsv), 0.0).astype(BF16)
        t = i * tm + lax.broadcasted_iota(jnp.int32, (tm + POOL_HALO, 1), 0)
        for gi, w in enumerate(POOL_WINDOWS):
            cols = pl.ds(gi * dg, dg)
            dm = _nt(bufy[:, cols], pw_ref[gi])
            bufq[:, cols] = dm * (1.0 / jnp.minimum(t + 1, w).astype(F32))
            dh = bufq[pl.ds(0, tm), cols] - dm[0:tm, :]
            for j in range(1, w):
                dh = dh + bufq[pl.ds(j, tm), cols]
            bufh[:, cols] = dh
            acc[gi] += _tn(mixed_ref[:, cols], bufy[pl.ds(0, tm), cols])
        n, rinv = _rms(x_ref[...])
        dx, dsh, dsc, dgain = _rms_mod_bwd(bufh[...], n, rinv, g_ref[...], mod_ref[1:2, :])
        dxi_ref[...] = dxo_v + dx
        st_ref[0:1, :] += dsh
        st_ref[1:2, :] += dsc
        st_ref[3:4, :] += dgain

        @pl.when(i == last)
        def _():
            r = st_ref[2:3, :]
            st_ref[4:5, :] = r * lsv
            st_ref[5:6, :] = r * gate
            dpw_ref[...] = acc[...].astype(BF16)

    tile = pl.BlockSpec((tm, d), lambda i: (i, 0))
    nxt = pl.BlockSpec((POOL_HALO, d), lambda i: (jnp.minimum((i + 1) * (tm // POOL_HALO), s // POOL_HALO - 1), 0))
    pws = pl.BlockSpec((ng, dg, dg), lambda i: (0, 0, 0))
    return _call(body, name="pool_bwd", grid=(s // tm,),
        in_specs=[tile, nxt, tile, tile, tile, _row(8, d), _row(1, d), pws, _row(1, d)],
        out_specs=[tile, pws, _row(8, d)],
        out_shape=[jax.ShapeDtypeStruct((s, d), F32), jax.ShapeDtypeStruct((ng, dg, dg), BF16),
                   jax.ShapeDtypeStruct((8, d), F32)],
        scratch_shapes=[pltpu.VMEM((tm + POOL_HALO, d), BF16), pltpu.VMEM((tm + POOL_HALO, d), F32),
                        pltpu.VMEM((tm, d), F32), pltpu.VMEM((ng, dg, dg), F32)],
        args=(dxo, dxo, x, yp, mixed, mod, gmix, pw, ls))[0]


def _conv_bwd_mid(dxo, y, dwc, sb, mod, ln_g, ln_b, gw, w2_rows, w2_idx, tm):
    s, d = dwc.shape
    last = s // tm - 1

    def body(dxo_ref, y_ref, dwc_ref, s_ref, mod_ref, lng_ref, lnb_ref, w_ref, dd_ref, dw_ref, st_ref, acc):
        i = pl.program_id(0)

        @pl.when(i == 0)
        def _():
            st_ref[...] = jnp.zeros_like(st_ref)
            acc[...] = jnp.zeros_like(acc)

        dxo_v = dxo_ref[...]
        st_ref[0:1, :] += _sum0(dxo_v * y_ref[...].astype(F32))
        dy = dxo_v * (1.0 + mod_ref[2:3, :])
        st_ref[1:2, :] += _sum0(dy)
        dyb = dy.astype(BF16)
        ds = _nt(dyb, w_ref[...].reshape(NDEV * w2_rows, d))
        acc[...] += _tn(s_ref[...], dyb)
        v = dwc_ref[...]
        mu = jnp.mean(v, axis=-1, keepdims=True)
        xc = v - mu
        rstd = lax.rsqrt(jnp.mean(xc * xc, axis=-1, keepdims=True) + EPS)
        xhat = xc * rstd
        ln = xhat * lng_ref[...] + lnb_ref[...]
        dln = ds * _silu_grad(ln, jax.nn.sigmoid(ln))
        st_ref[2:3, :] += _sum0(dln * xhat)
        st_ref[3:4, :] += _sum0(dln)
        dxh = dln * lng_ref[...]
        dd = rstd * (dxh - jnp.mean(dxh, axis=-1, keepdims=True) - xhat * jnp.mean(dxh * xhat, axis=-1, keepdims=True))
        dd_ref[...] = dd
        st_ref[4:5, :] += _sum0(dd)

        @pl.when(i == last)
        def _():
            dw_ref[...] = acc[...].astype(BF16)

    tile = pl.BlockSpec((tm, d), lambda i: (i, 0))
    return _call(body, name="conv_bwd_mid", grid=(s // tm,),
        in_specs=[tile, tile, tile, tile, _row(8, d), _row(1, d), _row(1, d), _weight_spec(w2_rows, d, w2_idx)],
        out_specs=[tile, pl.BlockSpec((d, d), lambda i: (0, 0)), _row(8, d)],
        out_shape=[jax.ShapeDtypeStruct((s, d), F32), jax.ShapeDtypeStruct((d, d), BF16),
                   jax.ShapeDtypeStruct((8, d), F32)],
        scratch_shapes=[pltpu.VMEM((d, d), F32)],
        args=(dxo, y, dwc, sb, mod, ln_g, ln_b, gw))[0]


def _conv_bwd_in(dd, glu, u, hb, x, dxo, wdw, gw, w1_rows, w1_idx, mod, gmix, tm, ride):
    s, d = x.shape
    last = s // tm - 1
    off = CONV_HALO - (CONV_WIDTH - 1)
    rw = 32
    tap_group = 16

    def kernel_body(dd_ref, ddn_ref, glu_ref, glp_ref, u_ref, h_ref, x_ref, dxo_ref, wdw_ref, w_ref, mod_ref, g_ref,
                    dxi_ref, dw_ref, dwdw_ref, st_ref, bufd, bufg, shd, shg, dgl, accw, acc):
        i = pl.program_id(0)

        @pl.when(i == 0)
        def _():
            st_ref[...] = jnp.zeros_like(st_ref)
            accw[...] = jnp.zeros_like(accw)
            acc[...] = jnp.zeros_like(acc)

        bufd[pl.ds(0, tm), :] = dd_ref[...]
        bufd[pl.ds(tm, CONV_HALO), :] = jnp.where(i < last, ddn_ref[...], 0.0)
        bufg[pl.ds(0, CONV_HALO), :] = jnp.where(i > 0, glp_ref[...], 0.0)
        bufg[pl.ds(CONV_HALO, tm), :] = glu_ref[...]
        _shifted_copies(bufd, shd, tm + CONV_HALO)
        _shifted_copies(bufg, shg, CONV_HALO + tm)
        for cb in range(d // LANES):
            cols = pl.ds(cb * LANES, LANES)
            taps = wdw_ref[:, cols]
            for r in range(tm // rw):
                part = jnp.zeros((rw, LANES), F32)
                for k in range(CONV_WIDTH):
                    part = part + _window(shd, r * rw + CONV_WIDTH - 1 - k, rw, cols) * taps[k:k + 1, :]
                dgl[pl.ds(r * rw, rw), cols] = part
            for k0 in range(0, CONV_WIDTH, tap_group):
                group = range(k0, min(CONV_WIDTH, k0 + tap_group))
                sums = {k: jnp.zeros((8, LANES), F32) for k in group}
                for r in range(tm // rw):
                    ddc = bufd[pl.ds(r * rw, rw), cols]
                    for k in group:
                        p = _window(shg, r * rw + off + k, rw, cols) * ddc
                        for q in range(rw // 8):
                            sums[k] = sums[k] + p[q * 8:(q + 1) * 8, :]
                for k in group:
                    accw[k, :, cols] += sums[k]
        dglu = dgl[...]
        uv = u_ref[...]
        a, g = uv[:, :d], uv[:, d:]
        sg = jax.nn.sigmoid(g)
        da = dglu * sg
        dgt = dglu * a * (sg * (1.0 - sg))
        du = jnp.concatenate([da, dgt], axis=1)
        st_ref[0:1, :] += _sum0(du)
        dub = du.astype(BF16)
        w = w_ref[...].reshape(NDEV * w1_rows, d)
        dh = _nn(dub, w)
        acc[...] += _tn(dub, h_ref[...])
        n, rinv = _rms(x_ref[...])
        dx, dsh, dsc, dgain = _rms_mod_bwd(dh, n, rinv, g_ref[...], mod_ref[1:2, :])
        dxi_ref[...] = dxo_ref[...] + dx
        st_ref[1:2, 0:d] += dsh
        st_ref[2:3, 0:d] += dsc
        st_ref[3:4, 0:d] += dgain

        @pl.when(i == last)
        def _():
            dw_ref[...] = acc[...].astype(BF16)
            dwdw_ref[...] = jnp.sum(accw[...], axis=1)

    tile = pl.BlockSpec((tm, d), lambda i: (i, 0))
    prv = pl.BlockSpec((CONV_HALO, d), lambda i: (jnp.maximum(i * (tm // CONV_HALO) - 1, 0), 0))
    nxt = pl.BlockSpec((CONV_HALO, d), lambda i: (jnp.minimum((i + 1) * (tm // CONV_HALO), s // CONV_HALO - 1), 0))
    return _call(kernel_body, name="conv_bwd_in", grid=(s // tm,),
        in_specs=[tile, nxt, tile, prv, pl.BlockSpec((tm, 2 * d), lambda i: (i, 0)), tile, tile, tile,
                  _row(CONV_WIDTH, d), _weight_spec(w1_rows, d, w1_idx), _row(8, d), _row(1, d)],
        out_specs=[tile, pl.BlockSpec((2 * d, d), lambda i: (0, 0)), _row(CONV_HALO, d), _row(8, 2 * d)],
        out_shape=[jax.ShapeDtypeStruct((s, d), F32), jax.ShapeDtypeStruct((2 * d, d), BF16),
                   jax.ShapeDtypeStruct((CONV_HALO, d), F32), jax.ShapeDtypeStruct((8, 2 * d), F32)],
        scratch_shapes=[pltpu.VMEM((tm + CONV_HALO, d), F32), pltpu.VMEM((CONV_HALO + tm, d), F32),
                        pltpu.VMEM((8, tm + CONV_HALO, d), F32), pltpu.VMEM((8, CONV_HALO + tm, d), F32),
                        pltpu.VMEM((tm, d), F32), pltpu.VMEM((CONV_HALO, 8, d), F32), pltpu.VMEM((2 * d, d), F32)],
        args=(dd, dd, glu, glu, u, hb, x, dxo, wdw, gw, mod, gmix), ride=ride)


def kernel(x, c, ada_w, ada_b, norm_mix_g, norm_ffn_g, conv_w1, conv_b1, conv_wdw, conv_bdw, conv_ln_g, conv_ln_b, conv_w2, conv_b2, pool_w, pool_ls, ffn_w_gate, ffn_w_up, ffn_w_down, final_g, loss_target, m_ada_w, m_ada_b, m_norm_mix_g, m_norm_ffn_g, m_conv_w1, m_conv_b1, m_conv_wdw, m_conv_bdw, m_conv_ln_g, m_conv_ln_b, m_conv_w2, m_conv_b2, m_pool_w, m_pool_ls, m_ffn_w_gate, m_ffn_w_up, m_ffn_w_down, m_final_g, v_ada_w, v_ada_b, v_norm_mix_g, v_norm_ffn_g, v_conv_w1, v_conv_b1, v_conv_wdw, v_conv_bdw, v_conv_ln_g, v_conv_ln_b, v_conv_w2, v_conv_b2, v_pool_w, v_pool_ls, v_ffn_w_gate, v_ffn_w_up, v_ffn_w_down, v_final_g):
    _, s, d = x.shape
    f = ffn_w_down.shape[1] * NDEV
    fs = f // NDEV
    r1, r2 = 2 * d // NDEV, d // NDEV
    ng = len(POOL_WINDOWS)
    dg = d // ng
    pr = ng * (dg // NDEV) * dg // d
    ncol = ada_w.shape[2]
    dc = d // NDEV
    tm = min(256, s)
    me = _my_index()
    x0 = x.reshape(s, d)
    target = loss_target.reshape(s, d)

    small = jnp.concatenate([c.reshape(NDEV, dc), conv_wdw[0], pool_ls], axis=0)
    shard_a = conv_w1[0].T.astype(BF16)
    bias = lax.dynamic_slice_in_dim(ada_b, me * ncol, ncol, axis=1)[:, None, :]
    small_all, mod_all, (gwa,) = _prologue(small, ada_w, bias, _Ride("gather", [shard_a]))
    c_all = small_all[:, 0:NDEV, :].reshape(NDEV, d)
    wdw = small_all[:, NDEV:NDEV + CONV_WIDTH, :].transpose(1, 0, 2).reshape(CONV_WIDTH, d)
    ls = small_all[:, NDEV + CONV_WIDTH, :].reshape(1, d)
    mod_mine = lax.dynamic_index_in_dim(mod_all.reshape(NDEV, 2, NDEV, ncol), me, axis=2, keepdims=False)
    mod = mod_mine.transpose(1, 0, 2).reshape(2, 6, d)
    mod = jnp.concatenate([mod, jnp.zeros((2, 2, d), F32)], axis=1)

    shard_b1 = jnp.concatenate([ffn_w_gate[0].T, conv_w2[0]], axis=0).astype(BF16)
    shard_b2 = jnp.concatenate([ffn_w_up[0].T, ffn_w_down[0]], axis=0).astype(BF16)
    shard_c = jnp.concatenate([ffn_w_gate[1].T, ffn_w_up[1].T, pool_w.reshape(pr, d)], axis=0).astype(BF16)
    shard_d = ffn_w_down[1].astype(BF16)
    w1_at = (gwa, 0)

    (h0, u, glu), (gwb1,) = _conv_in(x0, mod[0], norm_mix_g[0:1], w1_at[0], r1, w1_at[1], conv_b1, tm,
                                    _Ride("gather", [shard_b1]))
    w2_at = (gwb1[:, fs:fs + r2, :], 0)
    (dwc, sb, y0, x1), (gwb2,) = _conv_mid(glu, wdw, conv_bdw, conv_ln_g, conv_ln_b, w2_at[0], r2, w2_at[1], conv_b2,
                                           x0, mod[0], tm, _Ride("gather", [shard_b2]))
    ffn0_w = [(gwb1, 0), (gwb2, 0), (gwb2, 1)]
    (h1, gg0, uu0, yf0, x2), (gwc,) = _ffn_fwd(x1, mod[0], norm_ffn_g[0:1], ffn0_w, fs, f, tm, "ffn_fwd0",
                                               _Ride("gather", [shard_c]))
    pw = gwc[:, 2 * fs:2 * fs + pr, :].reshape(NDEV, ng, dg // NDEV, dg).transpose(1, 0, 2, 3).reshape(ng, dg, dg)
    (mixed, yp, x3), (gwd,) = _pool_fwd(x2, mod[1], norm_mix_g[1:2], pw, ls, tm, _Ride("gather", [shard_d]))
    ffn1_w = [(gwc, 0), (gwc, 1), (gwd, 0)]
    (h3, gg1, uu1, yf1, dx4, st_loss), _ = _ffn_fwd(x3, mod[1], norm_ffn_g[1:2], ffn1_w, fs, f, tm, "ffn_fwd1",
                                                    loss=(target, final_g.reshape(1, d)))

    fb = f // 2 if (f // 2) % 128 == 0 else f
    ts = min(512, s)
    (dgb, dub, ab, dyb, dx3, st_f1), _ = _ffn_bwd(dx4, x3, gg1, uu1, yf1, mod[1], norm_ffn_g[1:2], ffn1_w, fs, f, tm,
                                                  "ffn_bwd1")
    gf1 = _ffn_wgrad(dgb, dub, ab, h3, dyb, fb, ts, "ffn_wgrad1")
    dx2, gpw, st_p = _pool_bwd(dx3, x2, yp, mixed, mod[1], norm_mix_g[1:2], pw, ls, tm)
    (dgb, dub, ab, dyb, dx1, st_f0), (land_f1,) = _ffn_bwd(dx2, x1, gg0, uu0, yf0, mod[0], norm_ffn_g[0:1], ffn0_w, fs, f,
                                                           tm, "ffn_bwd0", _Ride("scatter", [gf1]))
    gf0 = _ffn_wgrad(dgb, dub, ab, h1, dyb, fb, ts, "ffn_wgrad0")
    dd, gw2, st_m = _conv_bwd_mid(dx1, y0, dwc, sb, mod[0], conv_ln_g, conv_ln_b, w2_at[0], r2, w2_at[1], tm)
    (dx0, gw1, gwdw, st_c), (land_f0, land_pw, land_w2) = _conv_bwd_in(
        dd, glu, u, h0, x0, dx1, wdw, w1_at[0], r1, w1_at[1], mod[0], norm_mix_g[0:1], tm,
        _Ride("scatter", [gf0, gpw, gw2[None]]))

    prow = jnp.concatenate([
        st_c[1:3, 0:d], st_m[0:1], st_f0[0:3], st_p[0:2], st_p[4:5], st_f1[0:3],
        st_c[3:4, 0:d], st_p[3:4], st_f0[3:4], st_f1[3:4],
        st_c[0:1, 0:d], st_c[0:1, d:2 * d], st_m[4:5], st_m[2:4], st_m[1:2], st_loss[0:1],
        gwdw[0:CONV_WIDTH], st_p[5:6], st_loss[2:3]], axis=0)
    p_all, (land_w1,) = _small_exchange(prow, _Ride("scatter", [gw1[None]]), "allgather_stats")
    psum = _sum_slots(p_all, "sum_stats")
    loss = psum[prow.shape[0] - 1, 0]

    tr = lambda a: jnp.swapaxes(a, 1, 2)
    ffn_out = _finalize_ffn(land_f0, land_f1, tr(ffn_w_gate), tr(ffn_w_up), ffn_w_down,
                            tr(m_ffn_w_gate), tr(m_ffn_w_up), m_ffn_w_down, tr(v_ffn_w_gate), tr(v_ffn_w_up), v_ffn_w_down)
    fin_w1 = _finalize(land_w1.reshape(NDEV, r1, d), conv_w1[0], m_conv_w1[0], v_conv_w1[0], True, "finalize_w1")
    fin_w2 = _finalize(land_w2.reshape(NDEV, r2, d), conv_w2[0], m_conv_w2[0], v_conv_w2[0], False, "finalize_w2")
    pshape = (ng * (dg // NDEV), dg)
    fin_pw = _finalize(land_pw.reshape((NDEV,) + pshape), pool_w.reshape(pshape), m_pool_w.reshape(pshape),
                       v_pool_w.reshape(pshape), False, "finalize_pool_w")
    dmod_all = p_all[:, 0:12, :].reshape(NDEV, 2, 6 * d)
    dmod_cols = lax.dynamic_slice_in_dim(dmod_all, me * ncol, ncol, axis=2).transpose(1, 0, 2)
    g_ada_w = _ada_wgrad(c_all, dmod_cols)

    def adam(w, g, m, v, name):
        shp = w.shape
        w2d = (-1, shp[-1])
        dl, mo, vo = _adamw(w.reshape(w2d), g.reshape(w2d), m.reshape(w2d), v.reshape(w2d), name)
        return dl.reshape(shp), mo.reshape(shp), vo.reshape(shp)

    rep_names = ["ada_b", "norm_mix_g", "norm_ffn_g", "conv_b1", "conv_bdw", "conv_ln_g", "conv_ln_b", "conv_b2", "final_g"]
    rep_w = [ada_b, norm_mix_g, norm_ffn_g, conv_b1, conv_bdw, conv_ln_g, conv_ln_b, conv_b2, final_g]
    rep_m = [m_ada_b, m_norm_mix_g, m_norm_ffn_g, m_conv_b1, m_conv_bdw, m_conv_ln_g, m_conv_ln_b, m_conv_b2, m_final_g]
    rep_v = [v_ada_b, v_norm_mix_g, v_norm_ffn_g, v_conv_b1, v_conv_bdw, v_conv_ln_g, v_conv_ln_b, v_conv_b2, v_final_g]
    nrep = sum(w.size for w in rep_w) // d
    pad = jnp.zeros(((-nrep) % 8, d), F32)

    def pack(arrs, fill):
        return jnp.concatenate([a.reshape(-1, d) for a in arrs] + [pad + fill], axis=0)

    rep_g = jnp.concatenate([psum[0:nrep], pad], axis=0)
    rep_d, rep_mo, rep_vo = _adamw(pack(rep_w, 0.0), rep_g, pack(rep_m, 0.0), pack(rep_v, 1.0), "adamw_replicated")

    def unpack(packed):
        out, cur = [], 0
        for w in rep_w:
            k = w.size // d
            out.append(packed[cur:cur + k].reshape(w.shape))
            cur += k
        return out

    rep = dict(zip(rep_names, zip(unpack(psum), unpack(rep_d), unpack(rep_mo), unpack(rep_vo))))

    g_wdw_full = psum[nrep:nrep + CONV_WIDTH]
    g_wdw = lax.dynamic_slice_in_dim(g_wdw_full, me * dc, dc, axis=1)
    g_ls = lax.dynamic_slice_in_dim(psum[nrep + CONV_WIDTH:nrep + CONV_WIDTH + 1], me * dc, dc, axis=1)
    tiny = lambda a, b: jnp.concatenate([a.reshape(CONV_WIDTH, dc), b.reshape(1, dc)], axis=0)
    t_d, t_m, t_v = _adamw(tiny(conv_wdw, pool_ls), tiny(g_wdw, g_ls), tiny(m_conv_wdw, m_pool_ls),
                           tiny(v_conv_wdw, v_pool_ls), "adamw_taps")

    def taps(a):
        return a[0:CONV_WIDTH][None], a[CONV_WIDTH:CONV_WIDTH + 1]

    sharded = {
        "ada_w": (g_ada_w,) + adam(ada_w, g_ada_w, m_ada_w, v_ada_w, "adamw_ada_w"),
        "conv_w1": tuple(a[None] for a in fin_w1),
        "conv_w2": tuple(a[None] for a in fin_w2),
        "pool_w": tuple(a.reshape(pool_w.shape) for a in fin_pw),
        "ffn_w_gate": tuple(tr(a) for a in ffn_out[0::3]),
        "ffn_w_up": tuple(tr(a) for a in ffn_out[1::3]),
        "ffn_w_down": tuple(ffn_out[2::3]),
        "conv_wdw": (g_wdw[None], taps(t_d)[0], taps(t_m)[0], taps(t_v)[0]),
        "pool_ls": (g_ls, taps(t_d)[1], taps(t_m)[1], taps(t_v)[1]),
    }
    every = {**rep, **sharded}
    order = ["ada_w", "ada_b", "norm_mix_g", "norm_ffn_g", "conv_w1", "conv_b1", "conv_wdw", "conv_bdw", "conv_ln_g",
             "conv_ln_b", "conv_w2", "conv_b2", "pool_w", "pool_ls", "ffn_w_gate", "ffn_w_up", "ffn_w_down", "final_g"]
    grads = [every[n][0] for n in order]
    deltas = [every[n][1] for n in order]
    new_m = [every[n][2] for n in order]
    new_v = [every[n][3] for n in order]
    return (loss, dx0.reshape(1, s, d), *grads, *deltas, *new_m, *new_v)
```

```python
import jax
import jax.numpy as jnp
from jax import lax
from jax.experimental import pallas as pl
from jax.experimental.pallas import tpu as pltpu

NDEV = 8
EPS = 1e-6
CONV_WIDTH = 31
POOL_WINDOWS = (2, 4, 8, 16)
CONV_HALO = 32
POOL_HALO = 16
ADAM_LR = 0.001
ADAM_B1 = 0.9
ADAM_B2 = 0.999
ADAM_EPS = 1e-08
ADAM_WD = 0.01
ADAM_STEP = 10
VMEM_LIMIT = 56 * 2**20
MESH = pl.DeviceIdType.MESH
F32 = jnp.float32
BF16 = jnp.bfloat16


def _nt(a, b):
    return lax.dot_general(a, b, (((1,), (1,)), ((), ())), preferred_element_type=F32)


def _nn(a, b):
    return lax.dot_general(a, b, (((1,), (0,)), ((), ())), preferred_element_type=F32)


def _tn(a, b):
    return lax.dot_general(a, b, (((0,), (0,)), ((), ())), preferred_element_type=F32)


def _sum0(v):
    return jnp.sum(v, axis=0, keepdims=True)


def _rms(x):
    rinv = lax.rsqrt(jnp.mean(x * x, axis=-1, keepdims=True) + EPS)
    return x * rinv, rinv


def _rms_mod_bwd(dh, n, rinv, g, sc):
    dhs = dh * (1.0 + sc)
    dn = dhs * g
    dx = rinv * (dn - n * jnp.mean(dn * n, axis=-1, keepdims=True))
    return dx, _sum0(dh), _sum0(dh * (n * g)), _sum0(dhs * n)


def _silu_grad(z, sg):
    return sg * (1.0 + z * (1.0 - sg))


def _params(*sem):
    return pltpu.CompilerParams(dimension_semantics=sem, vmem_limit_bytes=VMEM_LIMIT)


def _row(i, d):
    return pl.BlockSpec((i, d), lambda *_: (0, 0))


def _weight_spec(rows, d, idx):
    return pl.BlockSpec((NDEV, rows, d), lambda *_: (0, idx, 0), pipeline_mode=pl.Buffered(1))


def _my_index():
    return 4 * lax.axis_index("x") + 2 * lax.axis_index("y") + lax.axis_index("c")


def _peer(k):
    x, y, c = lax.axis_index("x"), lax.axis_index("y"), lax.axis_index("c")
    px = 1 - x if k & 4 else x
    py = 1 - y if k & 2 else y
    pc = 1 - c if k & 1 else c
    return (px, py, pc), 4 * px + 2 * py + pc


def _gather_sems():
    return [pltpu.SemaphoreType.DMA((NDEV - 1,)), pltpu.SemaphoreType.DMA((NDEV - 1,)), pltpu.SemaphoreType.DMA((1,))]


class _Gather:
    def __init__(self, srcs, dsts, sems):
        self.src, self.dst = srcs[0], dsts[0]
        self.send, self.recv, self.local = sems
        x, y, c = lax.axis_index("x"), lax.axis_index("y"), lax.axis_index("c")
        self.me, self.sibling, self.core = (x, y, c), (x, y, 1 - c), c
        self.chips = [(1 - x, y), (x, 1 - y), (1 - x, 1 - y)]

    def _copy(self, k, block, to, from_input=False):
        slot = self.dst.at[4 * block[0] + 2 * block[1] + block[2]]
        return pltpu.make_async_remote_copy(
            src_ref=self.src if from_input else slot, dst_ref=slot, send_sem=self.send.at[k], recv_sem=self.recv.at[k],
            device_id=to, device_id_type=MESH)

    def _own(self):
        return pltpu.make_async_copy(self.src, self.dst.at[_my_index()], self.local.at[0])

    def _first(self):
        return [self._copy(0, self.me, self.sibling, True)] + [
            self._copy(1 + j, self.me, (*chip, self.core), True) for j, chip in enumerate(self.chips)]

    def start(self):
        self._own().start()
        for cp in self._first():
            cp.start()

    def forward(self):
        for j, chip in enumerate(self.chips):
            self._copy(1 + j, (*chip, self.core), self.me).wait_recv()
            self._copy(4 + j, (*chip, self.core), self.sibling).start()

    def finish(self):
        self._copy(0, self.sibling, self.me).wait_recv()
        for j, chip in enumerate(self.chips):
            self._copy(4 + j, (*chip, 1 - self.core), self.me).wait_recv()
        for cp in self._first():
            cp.wait_send()
        for j, chip in enumerate(self.chips):
            self._copy(4 + j, (*chip, self.core), self.sibling).wait_send()
        self._own().wait()


def _scatter_sems(n):
    return [pltpu.SemaphoreType.DMA((7 * n,)), pltpu.SemaphoreType.DMA((7 * n,)), pltpu.SemaphoreType.DMA((n,))]


class _Scatter:
    def __init__(self, srcs, dsts, sems):
        send_sems, recv_sems, local_sems = sems
        me = _my_index()
        self.copies = []
        for a, (src, dst) in enumerate(zip(srcs, dsts)):
            r = dst.shape[2]
            self.copies.append(pltpu.make_async_copy(src.at[:, pl.ds(me * r, r), :], dst.at[me], local_sems.at[a]))
            for k in range(1, NDEV):
                dev, p = _peer(k)
                self.copies.append(pltpu.make_async_remote_copy(
                    src_ref=src.at[:, pl.ds(p * r, r), :], dst_ref=dst.at[me],
                    send_sem=send_sems.at[a * 7 + k - 1], recv_sem=recv_sems.at[a * 7 + k - 1],
                    device_id=dev, device_id_type=MESH))

    def start(self):
        for cp in self.copies:
            cp.start()

    def forward(self):
        pass

    def finish(self):
        for cp in self.copies:
            cp.wait()


def _land_shape(part):
    a, r, c = part.shape
    return jax.ShapeDtypeStruct((NDEV, a, r // NDEV, c), part.dtype)


ANY = pl.BlockSpec(memory_space=pl.ANY)


class _Ride:
    def __init__(self, kind, srcs):
        self.kind, self.srcs = kind, list(srcs)
        if kind == "gather":
            self.out_shape = [jax.ShapeDtypeStruct((NDEV,) + a.shape, a.dtype) for a in self.srcs]
            self.sems = _gather_sems()
        else:
            self.out_shape = [_land_shape(a) for a in self.srcs]
            self.sems = _scatter_sems(len(self.srcs))

    def exchange(self, ins, outs, sems):
        return (_Gather if self.kind == "gather" else _Scatter)(ins, outs, sems)


def _small_exchange(small, ride, name):
    r, c = small.shape
    nr = len(ride.srcs)

    def body(v_ref, *refs):
        rin, refs = refs[:nr], refs[nr:]
        out_ref, refs = refs[0], refs[1:]
        rout, refs = refs[:nr], refs[nr:]
        send_sems, recv_sems, rsems = refs[0], refs[1], refs[2:]
        big = ride.exchange(rin, rout, rsems)
        big.start()
        copies = _small_pushes(v_ref, out_ref, send_sems, recv_sems)
        for cp in copies:
            cp.start()
        for cp in copies:
            cp.wait()
        big.forward()
        big.finish()

    res = pl.pallas_call(body, name=name,
        out_shape=[jax.ShapeDtypeStruct((NDEV, r, c), small.dtype)] + ride.out_shape,
        in_specs=[pl.BlockSpec(memory_space=pltpu.VMEM)] + [ANY] * nr,
        out_specs=[pl.BlockSpec(memory_space=pltpu.VMEM)] + [ANY] * nr,
        scratch_shapes=[pltpu.SemaphoreType.DMA((NDEV - 1,)), pltpu.SemaphoreType.DMA((NDEV - 1,))] + ride.sems,
    )(small, *ride.srcs)
    return res[0], res[1:]


def _small_pushes(src_ref, out_ref, send_sems, recv_sems):
    me = _my_index()
    out_ref[me] = src_ref[...]
    copies = []
    for k in range(1, NDEV):
        dev, _ = _peer(k)
        copies.append(pltpu.make_async_remote_copy(
            src_ref=src_ref, dst_ref=out_ref.at[me], send_sem=send_sems.at[k - 1], recv_sem=recv_sems.at[k - 1],
            device_id=dev, device_id_type=MESH))
    return copies


def _prologue(small, ada_w, bias, ride):
    r, c = small.shape
    nl, d, ncol = ada_w.shape
    nr = len(ride.srcs)

    def body(v_ref, w_ref, b_ref, *refs):
        rin, refs = refs[:nr], refs[nr:]
        out_ref, mod_ref, refs = refs[0], refs[1], refs[2:]
        rout, refs = refs[:nr], refs[nr:]
        cols, send1, recv1, send2, recv2, rsems = refs[0], refs[1], refs[2], refs[3], refs[4], refs[5:]
        big = ride.exchange(rin, rout, rsems)
        big.start()
        first = _small_pushes(v_ref, out_ref, send1, recv1)
        for cp in first:
            cp.start()
        for cp in first:
            cp.wait()
        for layer in range(nl):
            acc = jnp.zeros((NDEV, ncol), F32) + b_ref[layer]
            for j in range(d // c):
                cj = out_ref[:, j, :]
                acc = acc + jnp.dot(cj * jax.nn.sigmoid(cj), w_ref[layer, pl.ds(j * c, c), :],
                                    preferred_element_type=F32, precision=lax.Precision.HIGHEST)
            cols[pl.ds(layer * NDEV, NDEV), :] = acc
        second = _small_pushes(cols, mod_ref, send2, recv2)
        for cp in second:
            cp.start()
        for cp in second:
            cp.wait()
        big.forward()
        big.finish()

    vmem = pl.BlockSpec(memory_space=pltpu.VMEM)
    sem = pltpu.SemaphoreType.DMA((NDEV - 1,))
    res = pl.pallas_call(body, name="prologue",
        out_shape=[jax.ShapeDtypeStruct((NDEV, r, c), F32), jax.ShapeDtypeStruct((NDEV, nl * NDEV, ncol), F32)] + ride.out_shape,
        in_specs=[vmem, vmem, vmem] + [ANY] * nr, out_specs=[vmem, vmem] + [ANY] * nr,
        scratch_shapes=[pltpu.VMEM((nl * NDEV, ncol), F32), sem, sem, sem, sem] + ride.sems,
        compiler_params=pltpu.CompilerParams(vmem_limit_bytes=VMEM_LIMIT),
    )(small, ada_w, bias, *ride.srcs)
    return res[0], res[1], res[2:]


def _call(kernel_body, *, name, grid, in_specs, out_specs, out_shape, args, scratch_shapes=(), ride=None):
    n_in, n_out, n_sc = len(in_specs), len(out_specs), len(scratch_shapes)
    nr = len(ride.srcs) if ride else 0
    in_specs, out_specs, out_shape = list(in_specs), list(out_specs), list(out_shape)
    scratch_shapes, args = list(scratch_shapes), list(args)
    if ride is not None:
        in_specs += [ANY] * nr
        out_specs += [ANY] * nr
        out_shape += ride.out_shape
        args += ride.srcs
        scratch_shapes += ride.sems

    def body(*refs):
        ins, refs = refs[:n_in], refs[n_in:]
        rin, refs = refs[:nr], refs[nr:]
        outs, refs = refs[:n_out], refs[n_out:]
        rout, refs = refs[:nr], refs[nr:]
        scratch, rsems = refs[:n_sc], refs[n_sc:]
        first, last = True, True
        for axis, extent in enumerate(grid):
            first &= pl.program_id(axis) == 0
            last &= pl.program_id(axis) == extent - 1
        if ride is not None:
            middle = last if len(grid) > 1 else pl.program_id(0) == (3 * grid[0]) // 4
            exchange = ride.exchange(rin, rout, rsems)
            pl.when(first)(exchange.start)
            pl.when(middle)(exchange.forward)
        kernel_body(*ins, *outs, *scratch)
        if ride is not None:
            pl.when(last)(exchange.finish)

    res = pl.pallas_call(body, name=name, grid=grid, in_specs=in_specs, out_specs=out_specs, out_shape=out_shape,
                         scratch_shapes=scratch_shapes, compiler_params=_params(*(("arbitrary",) * len(grid))))(*args)
    return res[:n_out], res[n_out:]


def _ada_wgrad(c_all, dmod):
    nl, _, ncol = dmod.shape
    d = c_all.shape[1]
    bd = min(d, 256)

    def body(c_ref, dm_ref, o_ref):
        cv = c_ref[...]
        ca = cv * jax.nn.sigmoid(cv)
        o_ref[0] = lax.dot_general(ca, dm_ref[0], (((0,), (0,)), ((), ())), preferred_element_type=F32,
                                   precision=lax.Precision.HIGHEST)

    return pl.pallas_call(body, name="ada_wgrad", grid=(nl, d // bd),
        in_specs=[pl.BlockSpec((NDEV, bd), lambda i, j: (0, j)), pl.BlockSpec((1, NDEV, ncol), lambda i, j: (i, 0, 0))],
        out_specs=pl.BlockSpec((1, bd, ncol), lambda i, j: (i, j, 0)),
        out_shape=jax.ShapeDtypeStruct((nl, d, ncol), F32),
        compiler_params=_params("arbitrary", "arbitrary"),
    )(c_all, dmod)


def _row_block(r, c, bytes_per_row_elem=4, budget=2 * 2**20):
    if r * c * bytes_per_row_elem <= budget or r % 8:
        return r
    best = 8
    for b in range(8, r + 1, 8):
        if r % b == 0 and b * c * bytes_per_row_elem <= budget:
            best = b
    return best


def _sum_slots(land, name):
    _, r, c = land.shape
    br = _row_block(r, c, 8 * land.dtype.itemsize)

    def body(l_ref, o_ref):
        acc = l_ref[0].astype(F32)
        for s in range(1, NDEV):
            acc = acc + l_ref[s].astype(F32)
        o_ref[...] = acc

    return pl.pallas_call(body, name=name, grid=(r // br,),
        in_specs=[pl.BlockSpec((NDEV, br, c), lambda i: (0, i, 0))],
        out_specs=pl.BlockSpec((br, c), lambda i: (i, 0)),
        out_shape=jax.ShapeDtypeStruct((r, c), F32),
        compiler_params=_params("arbitrary"),
    )(land)


def _adamw(w, g, m, v, name):
    r, c = w.shape
    br = _row_block(r, c)

    def body(w_ref, g_ref, m_ref, v_ref, d_ref, mo_ref, vo_ref):
        d_ref[...], mo_ref[...], vo_ref[...] = _adam_math(w_ref[...], g_ref[...], m_ref[...], v_ref[...])

    spec = pl.BlockSpec((br, c), lambda i: (i, 0))
    return pl.pallas_call(body, name=name, grid=(r // br,),
        in_specs=[spec] * 4, out_specs=[spec] * 3,
        out_shape=[jax.ShapeDtypeStruct((r, c), F32)] * 3,
        compiler_params=_params("arbitrary"),
    )(w, g, m, v)


def _adam_math(w, g, m, v):
    m2 = ADAM_B1 * m + (1.0 - ADAM_B1) * g
    v2 = ADAM_B2 * v + (1.0 - ADAM_B2) * (g * g)
    m_hat = m2 / (1.0 - ADAM_B1 ** ADAM_STEP)
    v_hat = v2 / (1.0 - ADAM_B2 ** ADAM_STEP)
    return -ADAM_LR * (m_hat / (jnp.sqrt(v_hat) + ADAM_EPS) + ADAM_WD * w), m2, v2


def _slot_sum(land_ref, *lead):
    acc = land_ref[(0,) + lead].astype(F32)
    for s in range(1, NDEV):
        acc = acc + land_ref[(s,) + lead].astype(F32)
    return acc


def _finalize(land, w, m, v, transposed, name):
    _, r, c = land.shape
    cb = 256 if (transposed and c % 256 == 0) else c
    wblk = pl.BlockSpec((cb, r), lambda i: (i, 0)) if transposed else pl.BlockSpec((r, cb), lambda i: (0, i))

    def body(l_ref, w_ref, m_ref, v_ref, g_ref, d_ref, mo_ref, vo_ref):
        g = _slot_sum(l_ref)
        g = g.T if transposed else g
        g_ref[...] = g
        d_ref[...], mo_ref[...], vo_ref[...] = _adam_math(w_ref[...], g, m_ref[...], v_ref[...])

    return pl.pallas_call(body, name=name, grid=(c // cb,),
        in_specs=[pl.BlockSpec((NDEV, r, cb), lambda i: (0, 0, i)), wblk, wblk, wblk], out_specs=[wblk] * 4,
        out_shape=[jax.ShapeDtypeStruct(w.shape, F32)] * 4,
        compiler_params=_params("arbitrary"),
    )(land, w, m, v)


def _finalize_ffn(land0, land1, wg, wu, wd, mg, mu, md, vg, vu, vd):
    nl, fs, d = wg.shape
    db = min(256, d)

    def kernel_body(l0_ref, l1_ref, wg_ref, wu_ref, wd_ref, mg_ref, mu_ref, md_ref, vg_ref, vu_ref, vd_ref, *outs):
        layer = pl.program_id(0)
        triples = [(wg_ref, mg_ref, vg_ref), (wu_ref, mu_ref, vu_ref), (wd_ref, md_ref, vd_ref)]

        def run(land_ref):
            for j, (w_ref, m_ref, v_ref) in enumerate(triples):
                g = _slot_sum(land_ref, j)
                delta, m2, v2 = _adam_math(w_ref[0], g, m_ref[0], v_ref[0])
                for o_ref, val in zip(outs[j::3], (g, delta, m2, v2)):
                    o_ref[0] = val

        @pl.when(layer == 0)
        def _():
            run(l0_ref)

        @pl.when(layer == 1)
        def _():
            run(l1_ref)

    blk = pl.BlockSpec((1, fs, db), lambda l, i: (l, 0, i))
    lblk = [pl.BlockSpec((NDEV, 3, fs, db), lambda l, i: (0, 0, 0, i * (1 - l))),
            pl.BlockSpec((NDEV, 3, fs, db), lambda l, i: (0, 0, 0, i * l))]
    outs, _ = _call(kernel_body, name="finalize_ffn", grid=(nl, d // db),
        in_specs=lblk + [blk] * 9, out_specs=[blk] * 12, out_shape=[jax.ShapeDtypeStruct(wg.shape, F32)] * 12,
        args=(land0, land1, wg, wu, wd, mg, mu, md, vg, vu, vd))
    return outs


CONV_ROWS = 64
LANES = 128


def _shifted_copies(buf, sh, n):
    sh[0] = buf[...]
    for r in range(1, 8):
        sh[r, pl.ds(0, n - 8), :] = buf[pl.ds(r, n - 8), :]


def _window(sh, o, rows, cols):
    return sh[o % 8, pl.ds(o - o % 8, rows), cols]


def _conv_in(x, mod, gmix, gw, w1_rows, w1_idx, b1, tm, ride):
    s, d = x.shape

    def kernel_body(x_ref, mod_ref, g_ref, w_ref, b_ref, h_ref, u_ref, glu_ref):
        n, _ = _rms(x_ref[...])
        h = (n * g_ref[...]) * (1.0 + mod_ref[1:2, :]) + mod_ref[0:1, :]
        hb = h.astype(BF16)
        h_ref[...] = hb
        u = _nt(hb, w_ref[...].reshape(NDEV * w1_rows, d)) + b_ref[...]
        u_ref[...] = u.astype(BF16)
        glu_ref[...] = (u[:, :d] * jax.nn.sigmoid(u[:, d:])).astype(BF16)

    tile = pl.BlockSpec((tm, d), lambda i: (i, 0))
    return _call(kernel_body, name="conv_in", grid=(s // tm,),
        in_specs=[tile, _row(8, d), _row(1, d), _weight_spec(w1_rows, d, w1_idx), _row(1, 2 * d)],
        out_specs=[tile, pl.BlockSpec((tm, 2 * d), lambda i: (i, 0)), tile],
        out_shape=[jax.ShapeDtypeStruct((s, d), BF16), jax.ShapeDtypeStruct((s, 2 * d), BF16),
                   jax.ShapeDtypeStruct((s, d), BF16)],
        args=(x, mod, gmix, gw, b1), ride=ride)


def _conv_mid(glu, wdw, bdw, ln_g, ln_b, gw, w2_rows, w2_idx, b2, x, mod, tm, ride):
    s, d = x.shape
    off = CONV_HALO - (CONV_WIDTH - 1)
    rc = min(CONV_ROWS, tm)
    nsub = 2 if s % (2 * tm) == 0 else 1
    tb = nsub * tm

    def kernel_body(glu_ref, halo_ref, wdw_ref, bdw_ref, lng_ref, lnb_ref, w_ref, b2_ref, x_ref, mod_ref,
                    dwc_ref, s_ref, y_ref, x1_ref, buf, sh):
        i = pl.program_id(0)
        w2 = w_ref[...].reshape(NDEV * w2_rows, d)
        for sub in range(nsub):
            rows = pl.ds(sub * tm, tm)
            if sub == 0:
                buf[pl.ds(0, CONV_HALO), :] = jnp.where(i > 0, halo_ref[...].astype(F32), 0.0)
            else:
                buf[pl.ds(0, CONV_HALO), :] = glu_ref[pl.ds(sub * tm - CONV_HALO, CONV_HALO), :].astype(F32)
            buf[pl.ds(CONV_HALO, tm), :] = glu_ref[rows, :].astype(F32)
            _shifted_copies(buf, sh, CONV_HALO + tm)
            for cb in range(d // LANES):
                cols = pl.ds(cb * LANES, LANES)
                taps = wdw_ref[:, cols]
                for r in range(tm // rc):
                    part = jnp.zeros((rc, LANES), F32) + bdw_ref[:, cols]
                    for k in range(CONV_WIDTH):
                        part = part + _window(sh, r * rc + off + k, rc, cols) * taps[k:k + 1, :]
                    dwc_ref[pl.ds(sub * tm + r * rc, rc), cols] = part
            acc = dwc_ref[rows, :]
            mu = jnp.mean(acc, axis=-1, keepdims=True)
            xc = acc - mu
            rstd = lax.rsqrt(jnp.mean(xc * xc, axis=-1, keepdims=True) + EPS)
            ln = (xc * rstd) * lng_ref[...] + lnb_ref[...]
            sb = (ln * jax.nn.sigmoid(ln)).astype(BF16)
            s_ref[rows, :] = sb
            y = _nn(sb, w2) + b2_ref[...]
            y_ref[rows, :] = y.astype(BF16)
            x1_ref[rows, :] = x_ref[rows, :] + (1.0 + mod_ref[2:3, :]) * y

    tile = pl.BlockSpec((tb, d), lambda i: (i, 0))
    halo = pl.BlockSpec((CONV_HALO, d), lambda i: (jnp.maximum(i * (tb // CONV_HALO) - 1, 0), 0))
    return _call(kernel_body, name="conv_mid", grid=(s // tb,),
        in_specs=[tile, halo, _row(CONV_WIDTH, d), _row(1, d), _row(1, d), _row(1, d),
                  _weight_spec(w2_rows, d, w2_idx), _row(1, d), tile, _row(8, d)],
        out_specs=[tile, tile, tile, tile],
        out_shape=[jax.ShapeDtypeStruct((s, d), F32), jax.ShapeDtypeStruct((s, d), BF16),
                   jax.ShapeDtypeStruct((s, d), BF16), jax.ShapeDtypeStruct((s, d), F32)],
        scratch_shapes=[pltpu.VMEM((CONV_HALO + tm, d), F32), pltpu.VMEM((8, CONV_HALO + tm, d), F32)],
        args=(glu, glu, wdw, bdw, ln_g, ln_b, gw, b2, x, mod), ride=ride)


def _ffn_fwd(x, mod, gffn, weights, fs, f, tm, name, ride=None, loss=None):
    s, d = x.shape

    last = s // tm - 1

    def kernel_body(x_ref, mod_ref, g_ref, wg_ref, wu_ref, wd_ref, *rest):
        xv = x_ref[...]
        n, _ = _rms(xv)
        hb = ((n * g_ref[...]) * (1.0 + mod_ref[4:5, :]) + mod_ref[3:4, :]).astype(BF16)
        gg = _nt(hb, wg_ref[...].reshape(f, d))
        uu = _nt(hb, wu_ref[...].reshape(f, d))
        ab = ((gg * jax.nn.sigmoid(gg)) * uu).astype(BF16)
        y = _nn(ab, wd_ref[...].reshape(f, d))
        xo = xv + (1.0 + mod_ref[5:6, :]) * y
        if loss is None:
            h_ref, gg_ref, uu_ref, y_ref, xo_ref = rest
            xo_ref[...] = xo
        else:
            t_ref, gfin_ref, h_ref, gg_ref, uu_ref, y_ref, dx_ref, st_ref = rest
            dx_ref[...] = _loss_tile(xo, t_ref[...], gfin_ref[...], st_ref, pl.program_id(0), last)
        h_ref[...] = hb
        gg_ref[...] = gg.astype(BF16)
        uu_ref[...] = uu.astype(BF16)
        y_ref[...] = y.astype(BF16)

    tile = pl.BlockSpec((tm, d), lambda i: (i, 0))
    wide = pl.BlockSpec((tm, f), lambda i: (i, 0))
    extra_in = [] if loss is None else [tile, _row(1, d)]
    extra_out = [] if loss is None else [_row(8, d)]
    return _call(kernel_body, name=name, grid=(s // tm,),
        in_specs=[tile, _row(8, d), _row(1, d)] + [_weight_spec(fs, d, idx) for _, idx in weights] + extra_in,
        out_specs=[tile, wide, wide, tile, tile] + extra_out,
        out_shape=[jax.ShapeDtypeStruct((s, d), BF16), jax.ShapeDtypeStruct((s, f), BF16),
                   jax.ShapeDtypeStruct((s, f), BF16), jax.ShapeDtypeStruct((s, d), BF16),
                   jax.ShapeDtypeStruct((s, d), F32)] + [jax.ShapeDtypeStruct((8, d), F32)] * len(extra_out),
        args=(x, mod, gffn) + tuple(a for a, _ in weights) + (() if loss is None else tuple(loss)), ride=ride)


def _loss_tile(x, target, g, st_ref, i, last):
    d = x.shape[1]

    @pl.when(i == 0)
    def _():
        st_ref[...] = jnp.zeros_like(st_ref)

    n, rinv = _rms(x)
    err = n * g - target
    dy = err * (1.0 / d)
    st_ref[0:1, :] += _sum0(dy * n)
    st_ref[1:2, :] += _sum0(err * err) * (0.5 / d)

    @pl.when(i == last)
    def _():
        st_ref[2:3, :] = jnp.zeros((1, d), F32) + jnp.sum(st_ref[1:2, :], axis=-1, keepdims=True)

    dn = dy * g
    return rinv * (dn - n * jnp.mean(dn * n, axis=-1, keepdims=True))


def _pool_fwd(x, mod, gmix, pw, ls, tm, ride):
    s, d = x.shape
    dg = d // len(POOL_WINDOWS)

    def body(x_ref, halo_ref, mod_ref, g_ref, pw_ref, ls_ref, mixed_ref, yp_ref, xo_ref, buf):
        i = pl.program_id(0)

        def hfun(xv):
            n, _ = _rms(xv)
            return (n * g_ref[...]) * (1.0 + mod_ref[1:2, :]) + mod_ref[0:1, :]

        xv = x_ref[...]
        h = hfun(xv)
        buf[pl.ds(0, POOL_HALO), :] = jnp.where(i > 0, hfun(halo_ref[...]), 0.0)
        buf[pl.ds(POOL_HALO, tm), :] = h
        t = i * tm + lax.broadcasted_iota(jnp.int32, (tm, 1), 0)
        gate = 1.0 + mod_ref[2:3, :]
        for gi, w in enumerate(POOL_WINDOWS):
            cols = pl.ds(gi * dg, dg)
            ws = buf[pl.ds(POOL_HALO, tm), cols]
            for j in range(1, w):
                ws = ws + buf[pl.ds(POOL_HALO - j, tm), cols]
            inv = 1.0 / jnp.minimum(t + 1, w).astype(F32)
            mb = (ws * inv - h[:, gi * dg:(gi + 1) * dg]).astype(BF16)
            mixed_ref[:, cols] = mb
            yp = _nn(mb, pw_ref[gi])
            yp_ref[:, cols] = yp.astype(BF16)
            xo_ref[:, cols] = xv[:, gi * dg:(gi + 1) * dg] + gate[:, gi * dg:(gi + 1) * dg] * (yp * ls_ref[:, cols])

    tile = pl.BlockSpec((tm, d), lambda i: (i, 0))
    halo = pl.BlockSpec((POOL_HALO, d), lambda i: (jnp.maximum(i * (tm // POOL_HALO) - 1, 0), 0))
    return _call(body, name="pool_fwd", grid=(s // tm,),
        in_specs=[tile, halo, _row(8, d), _row(1, d), pl.BlockSpec((len(POOL_WINDOWS), dg, dg), lambda i: (0, 0, 0)),
                  _row(1, d)],
        out_specs=[tile, tile, tile],
        out_shape=[jax.ShapeDtypeStruct((s, d), BF16), jax.ShapeDtypeStruct((s, d), BF16),
                   jax.ShapeDtypeStruct((s, d), F32)],
        scratch_shapes=[pltpu.VMEM((POOL_HALO + tm, d), F32)],
        args=(x, x, mod, gmix, pw, ls), ride=ride)


def _ffn_bwd(dxo, x, gg, uu, y, mod, gffn, weights, fs, f, tm, name, ride=None):
    s, d = x.shape

    def kernel_body(dxo_ref, x_ref, gg_ref, uu_ref, y_ref, mod_ref, g_ref, wg_ref, wu_ref, wd_ref,
                    dg_ref, du_ref, a_ref, dy_ref, dxi_ref, st_ref):
        @pl.when(pl.program_id(0) == 0)
        def _():
            st_ref[...] = jnp.zeros_like(st_ref)

        dxo_v = dxo_ref[...]
        dyb = (dxo_v * (1.0 + mod_ref[5:6, :])).astype(BF16)
        dy_ref[...] = dyb
        da = _nt(dyb, wd_ref[...].reshape(f, d))
        ggv, uuv = gg_ref[...].astype(F32), uu_ref[...].astype(F32)
        sg = jax.nn.sigmoid(ggv)
        silu = ggv * sg
        a_ref[...] = (silu * uuv).astype(BF16)
        dub = (da * silu).astype(BF16)
        dgb = (da * uuv * _silu_grad(ggv, sg)).astype(BF16)
        du_ref[...] = dub
        dg_ref[...] = dgb
        dh = _nn(dgb, wg_ref[...].reshape(f, d)) + _nn(dub, wu_ref[...].reshape(f, d))
        n, rinv = _rms(x_ref[...])
        dx, dsh, dsc, dgain = _rms_mod_bwd(dh, n, rinv, g_ref[...], mod_ref[4:5, :])
        dxi_ref[...] = dxo_v + dx
        st_ref[0:1, :] += dsh
        st_ref[1:2, :] += dsc
        st_ref[2:3, :] += _sum0(dxo_v * y_ref[...].astype(F32))
        st_ref[3:4, :] += dgain

    tile = pl.BlockSpec((tm, d), lambda i: (i, 0))
    wide = pl.BlockSpec((tm, f), lambda i: (i, 0))
    return _call(kernel_body, name=name, grid=(s // tm,),
        in_specs=[tile, tile, wide, wide, tile, _row(8, d), _row(1, d)] + [_weight_spec(fs, d, idx) for _, idx in weights],
        out_specs=[wide, wide, wide, tile, tile, _row(8, d)],
        out_shape=[jax.ShapeDtypeStruct((s, f), BF16)] * 3 + [jax.ShapeDtypeStruct((s, d), BF16),
                   jax.ShapeDtypeStruct((s, d), F32), jax.ShapeDtypeStruct((8, d), F32)],
        args=(dxo, x, gg, uu, y, mod, gffn) + tuple(a for a, _ in weights), ride=ride)


def _ffn_wgrad(dgb, dub, ab, h, dyb, fb, ts, name):
    s, f = dgb.shape
    d = h.shape[1]
    last = s // ts - 1

    def body(dg_ref, du_ref, a_ref, h_ref, dy_ref, o_ref, acc):
        t = pl.program_id(1)

        @pl.when(t == 0)
        def _():
            acc[...] = jnp.zeros_like(acc)

        hv = h_ref[...]
        acc[0] += _tn(dg_ref[...], hv)
        acc[1] += _tn(du_ref[...], hv)
        acc[2] += _tn(a_ref[...], dy_ref[...])

        @pl.when(t == last)
        def _():
            o_ref[...] = acc[...].astype(BF16)

    wide = pl.BlockSpec((ts, fb), lambda j, t: (t, j))
    tile = pl.BlockSpec((ts, d), lambda j, t: (t, 0))
    return pl.pallas_call(body, name=name, grid=(f // fb, s // ts),
        in_specs=[wide, wide, wide, tile, tile],
        out_specs=pl.BlockSpec((3, fb, d), lambda j, t: (0, j, 0)),
        out_shape=jax.ShapeDtypeStruct((3, f, d), BF16),
        scratch_shapes=[pltpu.VMEM((3, fb, d), F32)],
        compiler_params=_params("arbitrary", "arbitrary"),
    )(dgb, dub, ab, h, dyb)


def _pool_bwd(dxo, x, yp, mixed, mod, gmix, pw, ls, tm):
    s, d = x.shape
    ng = len(POOL_WINDOWS)
    dg = d // ng
    last = s // tm - 1

    def body(dxo_ref, dxh_ref, x_ref, yp_ref, mixed_ref, mod_ref, g_ref, pw_ref, ls_ref,
             dxi_ref, dpw_ref, st_ref, bufy, bufq, bufh, acc):
        i = pl.program_id(0)

        @pl.when(i == 0)
        def _():
            st_ref[...] = jnp.zeros_like(st_ref)
            acc[...] = jnp.zeros_like(acc)

        gate = 1.0 + mod_ref[2:3, :]
        lsv = ls_ref[...]
        dxo_v = dxo_ref[...]
        st_ref[2:3, :] += _sum0(dxo_v * yp_ref[...].astype(F32))
        bufy[pl.ds(0, tm), :] = (dxo_v * (gate * lsv)).astype(BF16)
        bufy[pl.ds(tm, POOL_HALO), :] = jnp.where(i < last, dxh_ref[...] * (gate * lsv), 0.0).astype(BF16)
        t = i * tm + lax.broadcasted_iota(jnp.int32, (tm + POOL_HALO, 1), 0)
        for gi, w in enumerate(POOL_WINDOWS):
            cols = pl.ds(gi * dg, dg)
            dm = _nt(bufy[:, cols], pw_ref[gi])
            bufq[:, cols] = dm * (1.0 / jnp.minimum(t + 1, w).astype(F32))
            dh = bufq[pl.ds(0, tm), cols] - dm[0:tm, :]
            for j in range(1, w):
                dh = dh + bufq[pl.ds(j, tm), cols]
            bufh[:, cols] = dh
            acc[gi] += _tn(mixed_ref[:, cols], bufy[pl.ds(0, tm), cols])
        n, rinv = _rms(x_ref[...])
        dx, dsh, dsc, dgain = _rms_mod_bwd(bufh[...], n, rinv, g_ref[...], mod_ref[1:2, :])
        dxi_ref[...] = dxo_v + dx
        st_ref[0:1, :] += dsh
        st_ref[1:2, :] += dsc
        st_ref[3:4, :] += dgain

        @pl.when(i == last)
        def _():
            r = st_ref[2:3, :]
            st_ref[4:5, :] = r * lsv
            st_ref[5:6, :] = r * gate
            dpw_ref[...] = acc[...].astype(BF16)

    tile = pl.BlockSpec((tm, d), lambda i: (i, 0))
    nxt = pl.BlockSpec((POOL_HALO, d), lambda i: (jnp.minimum((i + 1) * (tm // POOL_HALO), s // POOL_HALO - 1), 0))
    pws = pl.BlockSpec((ng, dg, dg), lambda i: (0, 0, 0))
    return _call(body, name="pool_bwd", grid=(s // tm,),
        in_specs=[tile, nxt, tile, tile, tile, _row(8, d), _row(1, d), pws, _row(1, d)],
        out_specs=[tile, pws, _row(8, d)],
        out_shape=[jax.ShapeDtypeStruct((s, d), F32), jax.ShapeDtypeStruct((ng, dg, dg), BF16),
                   jax.ShapeDtypeStruct((8, d), F32)],
        scratch_shapes=[pltpu.VMEM((tm + POOL_HALO, d), BF16), pltpu.VMEM((tm + POOL_HALO, d), F32),
                        pltpu.VMEM((tm, d), F32), pltpu.VMEM((ng, dg, dg), F32)],
        args=(dxo, dxo, x, yp, mixed, mod, gmix, pw, ls))[0]


def _conv_bwd_mid(dxo, y, dwc, sb, mod, ln_g, ln_b, gw, w2_rows, w2_idx, tm):
    s, d = dwc.shape
    last = s // tm - 1

    def body(dxo_ref, y_ref, dwc_ref, s_ref, mod_ref, lng_ref, lnb_ref, w_ref, dd_ref, dw_ref, st_ref, acc):
        i = pl.program_id(0)

        @pl.when(i == 0)
        def _():
            st_ref[...] = jnp.zeros_like(st_ref)
            acc[...] = jnp.zeros_like(acc)

        dxo_v = dxo_ref[...]
        st_ref[0:1, :] += _sum0(dxo_v * y_ref[...].astype(F32))
        dy = dxo_v * (1.0 + mod_ref[2:3, :])
        st_ref[1:2, :] += _sum0(dy)
        dyb = dy.astype(BF16)
        ds = _nt(dyb, w_ref[...].reshape(NDEV * w2_rows, d))
        acc[...] += _tn(s_ref[...], dyb)
        v = dwc_ref[...]
        mu = jnp.mean(v, axis=-1, keepdims=True)
        xc = v - mu
        rstd = lax.rsqrt(jnp.mean(xc * xc, axis=-1, keepdims=True) + EPS)
        xhat = xc * rstd
        ln = xhat * lng_ref[...] + lnb_ref[...]
        dln = ds * _silu_grad(ln, jax.nn.sigmoid(ln))
        st_ref[2:3, :] += _sum0(dln * xhat)
        st_ref[3:4, :] += _sum0(dln)
        dxh = dln * lng_ref[...]
        dd = rstd * (dxh - jnp.mean(dxh, axis=-1, keepdims=True) - xhat * jnp.mean(dxh * xhat, axis=-1, keepdims=True))
        dd_ref[...] = dd
        st_ref[4:5, :] += _sum0(dd)

        @pl.when(i == last)
        def _():
            dw_ref[...] = acc[...].astype(BF16)

    tile = pl.BlockSpec((tm, d), lambda i: (i, 0))
    return _call(body, name="conv_bwd_mid", grid=(s // tm,),
        in_specs=[tile, tile, tile, tile, _row(8, d), _row(1, d), _row(1, d), _weight_spec(w2_rows, d, w2_idx)],
        out_specs=[tile, pl.BlockSpec((d, d), lambda i: (0, 0)), _row(8, d)],
        out_shape=[jax.ShapeDtypeStruct((s, d), F32), jax.ShapeDtypeStruct((d, d), BF16),
                   jax.ShapeDtypeStruct((8, d), F32)],
        scratch_shapes=[pltpu.VMEM((d, d), F32)],
        args=(dxo, y, dwc, sb, mod, ln_g, ln_b, gw))[0]


def _conv_bwd_in(dd, glu, u, hb, x, dxo, wdw, gw, w1_rows, w1_idx, mod, gmix, tm, nsub, ride):
    s, d = x.shape
    off = CONV_HALO - (CONV_WIDTH - 1)
    rw = 32
    tap_group = 16
    tb = nsub * tm
    last = s // tb - 1

    def kernel_body(dd_ref, ddn_ref, glu_ref, glp_ref, u_ref, h_ref, x_ref, dxo_ref, wdw_ref, w_ref, mod_ref, g_ref,
                    dxi_ref, dw_ref, dwdw_ref, st_ref, bufd, bufg, shd, shg, dub, accw, acc):
        i = pl.program_id(0)

        @pl.when(i == 0)
        def _():
            st_ref[...] = jnp.zeros_like(st_ref)
            accw[...] = jnp.zeros_like(accw)
            acc[...] = jnp.zeros_like(acc)

        w1t = w_ref[...].reshape(NDEV * w1_rows, d)
        for sub in range(nsub):
            base = sub * tm
            tile_rows = pl.ds(base, tm)
            bufd[pl.ds(0, tm), :] = dd_ref[tile_rows, :]
            if sub == nsub - 1:
                bufd[pl.ds(tm, CONV_HALO), :] = jnp.where(i < last, ddn_ref[...], 0.0)
            else:
                bufd[pl.ds(tm, CONV_HALO), :] = dd_ref[pl.ds(base + tm, CONV_HALO), :]
            if sub == 0:
                bufg[pl.ds(0, CONV_HALO), :] = jnp.where(i > 0, glp_ref[...].astype(F32), 0.0)
            else:
                bufg[pl.ds(0, CONV_HALO), :] = glu_ref[pl.ds(base - CONV_HALO, CONV_HALO), :].astype(F32)
            bufg[pl.ds(CONV_HALO, tm), :] = glu_ref[tile_rows, :].astype(F32)
            _shifted_copies(bufd, shd, tm + CONV_HALO)
            _shifted_copies(bufg, shg, CONV_HALO + tm)
            for cb in range(d // LANES):
                cols = pl.ds(cb * LANES, LANES)
                gcols = pl.ds(d + cb * LANES, LANES)
                taps = wdw_ref[:, cols]
                sum_a, sum_g = jnp.zeros((8, LANES), F32), jnp.zeros((8, LANES), F32)
                for r in range(tm // rw):
                    rows = pl.ds(base + r * rw, rw)
                    part = jnp.zeros((rw, LANES), F32)
                    for k in range(CONV_WIDTH):
                        part = part + _window(shd, r * rw + CONV_WIDTH - 1 - k, rw, cols) * taps[k:k + 1, :]
                    sg = jax.nn.sigmoid(u_ref[rows, gcols].astype(F32))
                    da = part * sg
                    dgt = part * u_ref[rows, cols].astype(F32) * (sg * (1.0 - sg))
                    dub[sub, pl.ds(r * rw, rw), cols] = da.astype(BF16)
                    dub[sub, pl.ds(r * rw, rw), gcols] = dgt.astype(BF16)
                    for q in range(rw // 8):
                        sum_a = sum_a + da[q * 8:(q + 1) * 8, :]
                        sum_g = sum_g + dgt[q * 8:(q + 1) * 8, :]
                st_ref[0:1, cols] += _sum0(sum_a)
                st_ref[0:1, gcols] += _sum0(sum_g)
                for k0 in range(0, CONV_WIDTH, tap_group):
                    group = range(k0, min(CONV_WIDTH, k0 + tap_group))
                    sums = {k: jnp.zeros((8, LANES), F32) for k in group}
                    for r in range(tm // rw):
                        ddc = bufd[pl.ds(r * rw, rw), cols]
                        for k in group:
                            p = _window(shg, r * rw + off + k, rw, cols) * ddc
                            for q in range(rw // 8):
                                sums[k] = sums[k] + p[q * 8:(q + 1) * 8, :]
                    for k in group:
                        accw[k, :, cols] += sums[k]
            dubv = dub[sub]
            dh = _nn(dubv, w1t)
            acc[...] += _tn(dubv, h_ref[tile_rows, :])
            n, rinv = _rms(x_ref[tile_rows, :])
            dx, dsh, dsc, dgain = _rms_mod_bwd(dh, n, rinv, g_ref[...], mod_ref[1:2, :])
            dxi_ref[tile_rows, :] = dxo_ref[tile_rows, :] + dx
            st_ref[1:2, 0:d] += dsh
            st_ref[2:3, 0:d] += dsc
            st_ref[3:4, 0:d] += dgain

        @pl.when(i == last)
        def _():
            dw_ref[...] = acc[...].astype(BF16)
            dwdw_ref[...] = jnp.sum(accw[...], axis=1)

    tile = pl.BlockSpec((tb, d), lambda i: (i, 0))
    prv = pl.BlockSpec((CONV_HALO, d), lambda i: (jnp.maximum(i * (tb // CONV_HALO) - 1, 0), 0))
    nxt = pl.BlockSpec((CONV_HALO, d), lambda i: (jnp.minimum((i + 1) * (tb // CONV_HALO), s // CONV_HALO - 1), 0))
    return _call(kernel_body, name="conv_bwd_in", grid=(s // tb,),
        in_specs=[tile, nxt, tile, prv, pl.BlockSpec((tb, 2 * d), lambda i: (i, 0)), tile, tile, tile,
                  _row(CONV_WIDTH, d), _weight_spec(w1_rows, d, w1_idx), _row(8, d), _row(1, d)],
        out_specs=[tile, pl.BlockSpec((2 * d, d), lambda i: (0, 0)), _row(CONV_HALO, d), _row(8, 2 * d)],
        out_shape=[jax.ShapeDtypeStruct((s, d), F32), jax.ShapeDtypeStruct((2 * d, d), BF16),
                   jax.ShapeDtypeStruct((CONV_HALO, d), F32), jax.ShapeDtypeStruct((8, 2 * d), F32)],
        scratch_shapes=[pltpu.VMEM((tm + CONV_HALO, d), F32), pltpu.VMEM((CONV_HALO + tm, d), F32),
                        pltpu.VMEM((8, tm + CONV_HALO, d), F32), pltpu.VMEM((8, CONV_HALO + tm, d), F32),
                        pltpu.VMEM((nsub, tm, 2 * d), BF16), pltpu.VMEM((CONV_HALO, 8, d), F32),
                        pltpu.VMEM((2 * d, d), F32)],
        args=(dd, dd, glu, glu, u, hb, x, dxo, wdw, gw, mod, gmix), ride=ride)


def kernel(x, c, ada_w, ada_b, norm_mix_g, norm_ffn_g, conv_w1, conv_b1, conv_wdw, conv_bdw, conv_ln_g, conv_ln_b, conv_w2, conv_b2, pool_w, pool_ls, ffn_w_gate, ffn_w_up, ffn_w_down, final_g, loss_target, m_ada_w, m_ada_b, m_norm_mix_g, m_norm_ffn_g, m_conv_w1, m_conv_b1, m_conv_wdw, m_conv_bdw, m_conv_ln_g, m_conv_ln_b, m_conv_w2, m_conv_b2, m_pool_w, m_pool_ls, m_ffn_w_gate, m_ffn_w_up, m_ffn_w_down, m_final_g, v_ada_w, v_ada_b, v_norm_mix_g, v_norm_ffn_g, v_conv_w1, v_conv_b1, v_conv_wdw, v_conv_bdw, v_conv_ln_g, v_conv_ln_b, v_conv_w2, v_conv_b2, v_pool_w, v_pool_ls, v_ffn_w_gate, v_ffn_w_up, v_ffn_w_down, v_final_g):
    _, s, d = x.shape
    f = ffn_w_down.shape[1] * NDEV
    fs = f // NDEV
    r1, r2 = 2 * d // NDEV, d // NDEV
    ng = len(POOL_WINDOWS)
    dg = d // ng
    pr = ng * (dg // NDEV) * dg // d
    ncol = ada_w.shape[2]
    dc = d // NDEV
    tm = min(256, s)
    me = _my_index()
    x0 = x.reshape(s, d)
    target = loss_target.reshape(s, d)

    small = jnp.concatenate([c.reshape(NDEV, dc), conv_wdw[0], pool_ls], axis=0)
    shard_a = conv_w1[0].T.astype(BF16)
    bias = lax.dynamic_slice_in_dim(ada_b, me * ncol, ncol, axis=1)[:, None, :]
    small_all, mod_all, (gwa,) = _prologue(small, ada_w, bias, _Ride("gather", [shard_a]))
    c_all = small_all[:, 0:NDEV, :].reshape(NDEV, d)
    wdw = small_all[:, NDEV:NDEV + CONV_WIDTH, :].transpose(1, 0, 2).reshape(CONV_WIDTH, d)
    ls = small_all[:, NDEV + CONV_WIDTH, :].reshape(1, d)
    mod_mine = lax.dynamic_index_in_dim(mod_all.reshape(NDEV, 2, NDEV, ncol), me, axis=2, keepdims=False)
    mod = mod_mine.transpose(1, 0, 2).reshape(2, 6, d)
    mod = jnp.concatenate([mod, jnp.zeros((2, 2, d), F32)], axis=1)

    shard_b1 = jnp.concatenate([ffn_w_gate[0].T, conv_w2[0]], axis=0).astype(BF16)
    shard_b2 = jnp.concatenate([ffn_w_up[0].T, ffn_w_down[0]], axis=0).astype(BF16)
    shard_c = jnp.concatenate([ffn_w_gate[1].T, ffn_w_up[1].T, pool_w.reshape(pr, d)], axis=0).astype(BF16)
    shard_d = ffn_w_down[1].astype(BF16)
    w1_at = (gwa, 0)

    (h0, u, glu), (gwb1,) = _conv_in(x0, mod[0], norm_mix_g[0:1], w1_at[0], r1, w1_at[1], conv_b1, tm,
                                    _Ride("gather", [shard_b1]))
    w2_at = (gwb1[:, fs:fs + r2, :], 0)
    (dwc, sb, y0, x1), (gwb2,) = _conv_mid(glu, wdw, conv_bdw, conv_ln_g, conv_ln_b, w2_at[0], r2, w2_at[1], conv_b2,
                                           x0, mod[0], tm, _Ride("gather", [shard_b2]))
    ffn0_w = [(gwb1, 0), (gwb2, 0), (gwb2, 1)]
    (h1, gg0, uu0, yf0, x2), (gwc,) = _ffn_fwd(x1, mod[0], norm_ffn_g[0:1], ffn0_w, fs, f, tm, "ffn_fwd0",
                                               _Ride("gather", [shard_c]))
    pw = gwc[:, 2 * fs:2 * fs + pr, :].reshape(NDEV, ng, dg // NDEV, dg).transpose(1, 0, 2, 3).reshape(ng, dg, dg)
    (mixed, yp, x3), (gwd,) = _pool_fwd(x2, mod[1], norm_mix_g[1:2], pw, ls, tm, _Ride("gather", [shard_d]))
    ffn1_w = [(gwc, 0), (gwc, 1), (gwd, 0)]
    (h3, gg1, uu1, yf1, dx4, st_loss), _ = _ffn_fwd(x3, mod[1], norm_ffn_g[1:2], ffn1_w, fs, f, tm, "ffn_fwd1",
                                                    loss=(target, final_g.reshape(1, d)))

    fb = f // 2 if (f // 2) % 128 == 0 else f
    ts = min(512, s)
    (dgb, dub, ab, dyb, dx3, st_f1), _ = _ffn_bwd(dx4, x3, gg1, uu1, yf1, mod[1], norm_ffn_g[1:2], ffn1_w, fs, f, tm,
                                                  "ffn_bwd1")
    gf1 = _ffn_wgrad(dgb, dub, ab, h3, dyb, fb, ts, "ffn_wgrad1")
    dx2, gpw, st_p = _pool_bwd(dx3, x2, yp, mixed, mod[1], norm_mix_g[1:2], pw, ls, tm)
    (dgb, dub, ab, dyb, dx1, st_f0), (land_f1,) = _ffn_bwd(dx2, x1, gg0, uu0, yf0, mod[0], norm_ffn_g[0:1], ffn0_w, fs, f,
                                                           tm, "ffn_bwd0", _Ride("scatter", [gf1]))
    gf0 = _ffn_wgrad(dgb, dub, ab, h1, dyb, fb, ts, "ffn_wgrad0")
    dd, gw2, st_m = _conv_bwd_mid(dx1, y0, dwc, sb, mod[0], conv_ln_g, conv_ln_b, w2_at[0], r2, w2_at[1], tm)
    (dx0, gw1, gwdw, st_c), (land_f0, land_pw, land_w2) = _conv_bwd_in(
        dd, glu, u, h0, x0, dx1, wdw, w1_at[0], r1, w1_at[1], mod[0], norm_mix_g[0:1], tm, 1,
        _Ride("scatter", [gf0, gpw, gw2[None]]))

    prow = jnp.concatenate([
        st_c[1:3, 0:d], st_m[0:1], st_f0[0:3], st_p[0:2], st_p[4:5], st_f1[0:3],
        st_c[3:4, 0:d], st_p[3:4], st_f0[3:4], st_f1[3:4],
        st_c[0:1, 0:d], st_c[0:1, d:2 * d], st_m[4:5], st_m[2:4], st_m[1:2], st_loss[0:1],
        gwdw[0:CONV_WIDTH], st_p[5:6], st_loss[2:3]], axis=0)
    p_all, (land_w1,) = _small_exchange(prow, _Ride("scatter", [gw1[None]]), "allgather_stats")
    psum = _sum_slots(p_all, "sum_stats")
    loss = psum[prow.shape[0] - 1, 0]

    tr = lambda a: jnp.swapaxes(a, 1, 2)
    ffn_out = _finalize_ffn(land_f0, land_f1, tr(ffn_w_gate), tr(ffn_w_up), ffn_w_down,
                            tr(m_ffn_w_gate), tr(m_ffn_w_up), m_ffn_w_down, tr(v_ffn_w_gate), tr(v_ffn_w_up), v_ffn_w_down)
    fin_w1 = _finalize(land_w1.reshape(NDEV, r1, d), conv_w1[0], m_conv_w1[0], v_conv_w1[0], True, "finalize_w1")
    fin_w2 = _finalize(land_w2.reshape(NDEV, r2, d), conv_w2[0], m_conv_w2[0], v_conv_w2[0], False, "finalize_w2")
    pshape = (ng * (dg // NDEV), dg)
    fin_pw = _finalize(land_pw.reshape((NDEV,) + pshape), pool_w.reshape(pshape), m_pool_w.reshape(pshape),
                       v_pool_w.reshape(pshape), False, "finalize_pool_w")
    dmod_all = p_all[:, 0:12, :].reshape(NDEV, 2, 6 * d)
    dmod_cols = lax.dynamic_slice_in_dim(dmod_all, me * ncol, ncol, axis=2).transpose(1, 0, 2)
    g_ada_w = _ada_wgrad(c_all, dmod_cols)

    def adam(w, g, m, v, name):
        shp = w.shape
        w2d = (-1, shp[-1])
        dl, mo, vo = _adamw(w.reshape(w2d), g.reshape(w2d), m.reshape(w2d), v.reshape(w2d), name)
        return dl.reshape(shp), mo.reshape(shp), vo.reshape(shp)

    rep_names = ["ada_b", "norm_mix_g", "norm_ffn_g", "conv_b1", "conv_bdw", "conv_ln_g", "conv_ln_b", "conv_b2", "final_g"]
    rep_w = [ada_b, norm_mix_g, norm_ffn_g, conv_b1, conv_bdw, conv_ln_g, conv_ln_b, conv_b2, final_g]
    rep_m = [m_ada_b, m_norm_mix_g, m_norm_ffn_g, m_conv_b1, m_conv_bdw, m_conv_ln_g, m_conv_ln_b, m_conv_b2, m_final_g]
    rep_v = [v_ada_b, v_norm_mix_g, v_norm_ffn_g, v_conv_b1, v_conv_bdw, v_conv_ln_g, v_conv_ln_b, v_conv_b2, v_final_g]
    nrep = sum(w.size for w in rep_w) // d
    pad = jnp.zeros(((-nrep) % 8, d), F32)

    def pack(arrs, fill):
        return jnp.concatenate([a.reshape(-1, d) for a in arrs] + [pad + fill], axis=0)

    rep_g = jnp.concatenate([psum[0:nrep], pad], axis=0)
    rep_d, rep_mo, rep_vo = _adamw(pack(rep_w, 0.0), rep_g, pack(rep_m, 0.0), pack(rep_v, 1.0), "adamw_replicated")

    def unpack(packed):
        out, cur = [], 0
        for w in rep_w:
            k = w.size // d
            out.append(packed[cur:cur + k].reshape(w.shape))
            cur += k
        return out

    rep = dict(zip(rep_names, zip(unpack(psum), unpack(rep_d), unpack(rep_mo), unpack(rep_vo))))

    g_wdw_full = psum[nrep:nrep + CONV_WIDTH]
    g_wdw = lax.dynamic_slice_in_dim(g_wdw_full, me * dc, dc, axis=1)
    g_ls = lax.dynamic_slice_in_dim(psum[nrep + CONV_WIDTH:nrep + CONV_WIDTH + 1], me * dc, dc, axis=1)
    tiny = lambda a, b: jnp.concatenate([a.reshape(CONV_WIDTH, dc), b.reshape(1, dc)], axis=0)
    t_d, t_m, t_v = _adamw(tiny(conv_wdw, pool_ls), tiny(g_wdw, g_ls), tiny(m_conv_wdw, m_pool_ls),
                           tiny(v_conv_wdw, v_pool_ls), "adamw_taps")

    def taps(a):
        return a[0:CONV_WIDTH][None], a[CONV_WIDTH:CONV_WIDTH + 1]

    sharded = {
        "ada_w": (g_ada_w,) + adam(ada_w, g_ada_w, m_ada_w, v_ada_w, "adamw_ada_w"),
        "conv_w1": tuple(a[None] for a in fin_w1),
        "conv_w2": tuple(a[None] for a in fin_w2),
        "pool_w": tuple(a.reshape(pool_w.shape) for a in fin_pw),
        "ffn_w_gate": tuple(tr(a) for a in ffn_out[0::3]),
        "ffn_w_up": tuple(tr(a) for a in ffn_out[1::3]),
        "ffn_w_down": tuple(ffn_out[2::3]),
        "conv_wdw": (g_wdw[None], taps(t_d)[0], taps(t_m)[0], taps(t_v)[0]),
        "pool_ls": (g_ls, taps(t_d)[1], taps(t_m)[1], taps(t_v)[1]),
    }
    every = {**rep, **sharded}
    order = ["ada_w", "ada_b", "norm_mix_g", "norm_ffn_g", "conv_w1", "conv_b1", "conv_wdw", "conv_bdw", "conv_ln_g",
             "conv_ln_b", "conv_w2", "conv_b2", "pool_w", "pool_ls", "ffn_w_gate", "ffn_w_up", "ffn_w_down", "final_g"]
    grads = [every[n][0] for n in order]
    deltas = [every[n][1] for n in order]
    new_m = [every[n][2] for n in order]
    new_v = [every[n][3] for n in order]
    return (loss, dx0.reshape(1, s, d), *grads, *deltas, *new_m, *new_v)
```

```python
import jax
import jax.numpy as jnp
from jax import lax
from jax.experimental import pallas as pl
from jax.experimental.pallas import tpu as pltpu

NDEV = 8
EPS = 1e-6
CONV_WIDTH = 31
POOL_WINDOWS = (2, 4, 8, 16)
CONV_HALO = 32
POOL_HALO = 16
ADAM_LR = 0.001
ADAM_B1 = 0.9
ADAM_B2 = 0.999
ADAM_EPS = 1e-08
ADAM_WD = 0.01
ADAM_STEP = 10
VMEM_LIMIT = 56 * 2**20
MESH = pl.DeviceIdType.MESH
F32 = jnp.float32
BF16 = jnp.bfloat16


def _nt(a, b):
    return lax.dot_general(a, b, (((1,), (1,)), ((), ())), preferred_element_type=F32)


def _nn(a, b):
    return lax.dot_general(a, b, (((1,), (0,)), ((), ())), preferred_element_type=F32)


def _tn(a, b):
    return lax.dot_general(a, b, (((0,), (0,)), ((), ())), preferred_element_type=F32)


def _sum0(v):
    return jnp.sum(v, axis=0, keepdims=True)


def _rms(x):
    rinv = lax.rsqrt(jnp.mean(x * x, axis=-1, keepdims=True) + EPS)
    return x * rinv, rinv


def _rms_mod_bwd(dh, n, rinv, g, sc):
    dhs = dh * (1.0 + sc)
    dn = dhs * g
    dx = rinv * (dn - n * jnp.mean(dn * n, axis=-1, keepdims=True))
    return dx, _sum0(dh), _sum0(dh * (n * g)), _sum0(dhs * n)


def _silu_grad(z, sg):
    return sg * (1.0 + z * (1.0 - sg))


def _params(*sem):
    return pltpu.CompilerParams(dimension_semantics=sem, vmem_limit_bytes=VMEM_LIMIT)


def _row(i, d):
    return pl.BlockSpec((i, d), lambda *_: (0, 0))


def _weight_spec(rows, d, idx):
    return pl.BlockSpec((NDEV, rows, d), lambda *_: (0, idx, 0), pipeline_mode=pl.Buffered(1))


def _my_index():
    return 4 * lax.axis_index("x") + 2 * lax.axis_index("y") + lax.axis_index("c")


def _peer(k):
    x, y, c = lax.axis_index("x"), lax.axis_index("y"), lax.axis_index("c")
    px = 1 - x if k & 4 else x
    py = 1 - y if k & 2 else y
    pc = 1 - c if k & 1 else c
    return (px, py, pc), 4 * px + 2 * py + pc


def _gather_sems():
    return [pltpu.SemaphoreType.DMA((NDEV - 1,)), pltpu.SemaphoreType.DMA((NDEV - 1,)), pltpu.SemaphoreType.DMA((1,))]


class _Gather:
    def __init__(self, srcs, dsts, sems):
        self.src, self.dst = srcs[0], dsts[0]
        self.send, self.recv, self.local = sems
        x, y, c = lax.axis_index("x"), lax.axis_index("y"), lax.axis_index("c")
        self.me, self.sibling, self.core = (x, y, c), (x, y, 1 - c), c
        self.chips = [(1 - x, y), (x, 1 - y), (1 - x, 1 - y)]

    def _copy(self, k, block, to, from_input=False):
        slot = self.dst.at[4 * block[0] + 2 * block[1] + block[2]]
        return pltpu.make_async_remote_copy(
            src_ref=self.src if from_input else slot, dst_ref=slot, send_sem=self.send.at[k], recv_sem=self.recv.at[k],
            device_id=to, device_id_type=MESH)

    def _own(self):
        return pltpu.make_async_copy(self.src, self.dst.at[_my_index()], self.local.at[0])

    def _first(self):
        return [self._copy(0, self.me, self.sibling, True)] + [
            self._copy(1 + j, self.me, (*chip, self.core), True) for j, chip in enumerate(self.chips)]

    def start(self):
        self._own().start()
        for cp in self._first():
            cp.start()

    def forward(self):
        for j, chip in enumerate(self.chips):
            self._copy(1 + j, (*chip, self.core), self.me).wait_recv()
            self._copy(4 + j, (*chip, self.core), self.sibling).start()

    def finish(self):
        self._copy(0, self.sibling, self.me).wait_recv()
        for j, chip in enumerate(self.chips):
            self._copy(4 + j, (*chip, 1 - self.core), self.me).wait_recv()
        for cp in self._first():
            cp.wait_send()
        for j, chip in enumerate(self.chips):
            self._copy(4 + j, (*chip, self.core), self.sibling).wait_send()
        self._own().wait()


def _scatter_sems(n):
    return [pltpu.SemaphoreType.DMA((7 * n,)), pltpu.SemaphoreType.DMA((7 * n,)), pltpu.SemaphoreType.DMA((n,))]


class _Scatter:
    def __init__(self, srcs, dsts, sems):
        send_sems, recv_sems, local_sems = sems
        me = _my_index()
        self.copies = []
        for a, (src, dst) in enumerate(zip(srcs, dsts)):
            r = dst.shape[2]
            self.copies.append(pltpu.make_async_copy(src.at[:, pl.ds(me * r, r), :], dst.at[me], local_sems.at[a]))
            for k in range(1, NDEV):
                dev, p = _peer(k)
                self.copies.append(pltpu.make_async_remote_copy(
                    src_ref=src.at[:, pl.ds(p * r, r), :], dst_ref=dst.at[me],
                    send_sem=send_sems.at[a * 7 + k - 1], recv_sem=recv_sems.at[a * 7 + k - 1],
                    device_id=dev, device_id_type=MESH))

    def start(self):
        for cp in self.copies:
            cp.start()

    def forward(self):
        pass

    def finish(self):
        for cp in self.copies:
            cp.wait()


def _land_shape(part):
    a, r, c = part.shape
    return jax.ShapeDtypeStruct((NDEV, a, r // NDEV, c), part.dtype)


ANY = pl.BlockSpec(memory_space=pl.ANY)


class _Ride:
    def __init__(self, kind, srcs):
        self.kind, self.srcs = kind, list(srcs)
        if kind == "gather":
            self.out_shape = [jax.ShapeDtypeStruct((NDEV,) + a.shape, a.dtype) for a in self.srcs]
            self.sems = _gather_sems()
        else:
            self.out_shape = [_land_shape(a) for a in self.srcs]
            self.sems = _scatter_sems(len(self.srcs))

    def exchange(self, ins, outs, sems):
        return (_Gather if self.kind == "gather" else _Scatter)(ins, outs, sems)


def _small_exchange(small, ride, name):
    r, c = small.shape
    nr = len(ride.srcs)

    def body(v_ref, *refs):
        rin, refs = refs[:nr], refs[nr:]
        out_ref, refs = refs[0], refs[1:]
        rout, refs = refs[:nr], refs[nr:]
        send_sems, recv_sems, rsems = refs[0], refs[1], refs[2:]
        big = ride.exchange(rin, rout, rsems)
        big.start()
        copies = _small_pushes(v_ref, out_ref, send_sems, recv_sems)
        for cp in copies:
            cp.start()
        for cp in copies:
            cp.wait()
        big.forward()
        big.finish()

    res = pl.pallas_call(body, name=name,
        out_shape=[jax.ShapeDtypeStruct((NDEV, r, c), small.dtype)] + ride.out_shape,
        in_specs=[pl.BlockSpec(memory_space=pltpu.VMEM)] + [ANY] * nr,
        out_specs=[pl.BlockSpec(memory_space=pltpu.VMEM)] + [ANY] * nr,
        scratch_shapes=[pltpu.SemaphoreType.DMA((NDEV - 1,)), pltpu.SemaphoreType.DMA((NDEV - 1,))] + ride.sems,
    )(small, *ride.srcs)
    return res[0], res[1:]


def _small_pushes(src_ref, out_ref, send_sems, recv_sems):
    me = _my_index()
    out_ref[me] = src_ref[...]
    copies = []
    for k in range(1, NDEV):
        dev, _ = _peer(k)
        copies.append(pltpu.make_async_remote_copy(
            src_ref=src_ref, dst_ref=out_ref.at[me], send_sem=send_sems.at[k - 1], recv_sem=recv_sems.at[k - 1],
            device_id=dev, device_id_type=MESH))
    return copies


def _prologue(small, ada_w, bias, ride):
    r, c = small.shape
    nl, d, ncol = ada_w.shape
    nr = len(ride.srcs)

    def body(v_ref, w_ref, b_ref, *refs):
        rin, refs = refs[:nr], refs[nr:]
        out_ref, mod_ref, refs = refs[0], refs[1], refs[2:]
        rout, refs = refs[:nr], refs[nr:]
        cols, send1, recv1, send2, recv2, rsems = refs[0], refs[1], refs[2], refs[3], refs[4], refs[5:]
        big = ride.exchange(rin, rout, rsems)
        big.start()
        first = _small_pushes(v_ref, out_ref, send1, recv1)
        for cp in first:
            cp.start()
        for cp in first:
            cp.wait()
        for layer in range(nl):
            acc = jnp.zeros((NDEV, ncol), F32) + b_ref[layer]
            for j in range(d // c):
                cj = out_ref[:, j, :]
                acc = acc + jnp.dot(cj * jax.nn.sigmoid(cj), w_ref[layer, pl.ds(j * c, c), :],
                                    preferred_element_type=F32, precision=lax.Precision.HIGHEST)
            cols[pl.ds(layer * NDEV, NDEV), :] = acc
        second = _small_pushes(cols, mod_ref, send2, recv2)
        for cp in second:
            cp.start()
        for cp in second:
            cp.wait()
        big.forward()
        big.finish()

    vmem = pl.BlockSpec(memory_space=pltpu.VMEM)
    sem = pltpu.SemaphoreType.DMA((NDEV - 1,))
    res = pl.pallas_call(body, name="prologue",
        out_shape=[jax.ShapeDtypeStruct((NDEV, r, c), F32), jax.ShapeDtypeStruct((NDEV, nl * NDEV, ncol), F32)] + ride.out_shape,
        in_specs=[vmem, vmem, vmem] + [ANY] * nr, out_specs=[vmem, vmem] + [ANY] * nr,
        scratch_shapes=[pltpu.VMEM((nl * NDEV, ncol), F32), sem, sem, sem, sem] + ride.sems,
        compiler_params=pltpu.CompilerParams(vmem_limit_bytes=VMEM_LIMIT),
    )(small, ada_w, bias, *ride.srcs)
    return res[0], res[1], res[2:]


def _call(kernel_body, *, name, grid, in_specs, out_specs, out_shape, args, scratch_shapes=(), ride=None):
    n_in, n_out, n_sc = len(in_specs), len(out_specs), len(scratch_shapes)
    nr = len(ride.srcs) if ride else 0
    in_specs, out_specs, out_shape = list(in_specs), list(out_specs), list(out_shape)
    scratch_shapes, args = list(scratch_shapes), list(args)
    if ride is not None:
        in_specs += [ANY] * nr
        out_specs += [ANY] * nr
        out_shape += ride.out_shape
        args += ride.srcs
        scratch_shapes += ride.sems

    def body(*refs):
        ins, refs = refs[:n_in], refs[n_in:]
        rin, refs = refs[:nr], refs[nr:]
        outs, refs = refs[:n_out], refs[n_out:]
        rout, refs = refs[:nr], refs[nr:]
        scratch, rsems = refs[:n_sc], refs[n_sc:]
        first, last = True, True
        for axis, extent in enumerate(grid):
            first &= pl.program_id(axis) == 0
            last &= pl.program_id(axis) == extent - 1
        if ride is not None:
            middle = last if len(grid) > 1 else pl.program_id(0) == (3 * grid[0]) // 4
            exchange = ride.exchange(rin, rout, rsems)
            pl.when(first)(exchange.start)
            pl.when(middle)(exchange.forward)
        kernel_body(*ins, *outs, *scratch)
        if ride is not None:
            pl.when(last)(exchange.finish)

    res = pl.pallas_call(body, name=name, grid=grid, in_specs=in_specs, out_specs=out_specs, out_shape=out_shape,
                         scratch_shapes=scratch_shapes, compiler_params=_params(*(("arbitrary",) * len(grid))))(*args)
    return res[:n_out], res[n_out:]


def _ada_wgrad(c_all, dmod):
    nl, _, ncol = dmod.shape
    d = c_all.shape[1]
    bd = min(d, 256)

    def body(c_ref, dm_ref, o_ref):
        cv = c_ref[...]
        ca = cv * jax.nn.sigmoid(cv)
        o_ref[0] = lax.dot_general(ca, dm_ref[0], (((0,), (0,)), ((), ())), preferred_element_type=F32,
                                   precision=lax.Precision.HIGHEST)

    return pl.pallas_call(body, name="ada_wgrad", grid=(nl, d // bd),
        in_specs=[pl.BlockSpec((NDEV, bd), lambda i, j: (0, j)), pl.BlockSpec((1, NDEV, ncol), lambda i, j: (i, 0, 0))],
        out_specs=pl.BlockSpec((1, bd, ncol), lambda i, j: (i, j, 0)),
        out_shape=jax.ShapeDtypeStruct((nl, d, ncol), F32),
        compiler_params=_params("arbitrary", "arbitrary"),
    )(c_all, dmod)


def _pack_stats(st_c, st_m, st_f0, st_p, st_f1, st_loss, gwdw):
    d = st_m.shape[1]

    def body(c_ref, m_ref, f0_ref, p_ref, f1_ref, l_ref, w_ref, o_ref):
        pieces = [
            c_ref[1:3, 0:d], m_ref[0:1, :], f0_ref[0:3, :], p_ref[0:2, :], p_ref[4:5, :], f1_ref[0:3, :],
            c_ref[3:4, 0:d], p_ref[3:4, :], f0_ref[3:4, :], f1_ref[3:4, :],
            c_ref[0:1, 0:d], c_ref[0:1, d:2 * d], m_ref[4:5, :], m_ref[2:4, :], m_ref[1:2, :], l_ref[0:1, :],
            w_ref[0:CONV_WIDTH, :], p_ref[5:6, :], l_ref[2:3, :]]
        row = 0
        for piece in pieces:
            o_ref[pl.ds(row, piece.shape[0]), :] = piece
            row += piece.shape[0]

    return pl.pallas_call(body, name="pack_stats",
                          out_shape=jax.ShapeDtypeStruct((12 + 4 + 7 + CONV_WIDTH + 2, d), F32),
                          )(st_c, st_m, st_f0, st_p, st_f1, st_loss, gwdw)


def _row_block(r, c, bytes_per_row_elem=4, budget=2 * 2**20):
    if r * c * bytes_per_row_elem <= budget or r % 8:
        return r
    best = 8
    for b in range(8, r + 1, 8):
        if r % b == 0 and b * c * bytes_per_row_elem <= budget:
            best = b
    return best


def _sum_slots(land, name):
    _, r, c = land.shape
    br = _row_block(r, c, 8 * land.dtype.itemsize)

    def body(l_ref, o_ref):
        acc = l_ref[0].astype(F32)
        for s in range(1, NDEV):
            acc = acc + l_ref[s].astype(F32)
        o_ref[...] = acc

    return pl.pallas_call(body, name=name, grid=(r // br,),
        in_specs=[pl.BlockSpec((NDEV, br, c), lambda i: (0, i, 0))],
        out_specs=pl.BlockSpec((br, c), lambda i: (i, 0)),
        out_shape=jax.ShapeDtypeStruct((r, c), F32),
        compiler_params=_params("arbitrary"),
    )(land)


def _adamw(w, g, m, v, name):
    r, c = w.shape
    br = _row_block(r, c)

    def body(w_ref, g_ref, m_ref, v_ref, d_ref, mo_ref, vo_ref):
        d_ref[...], mo_ref[...], vo_ref[...] = _adam_math(w_ref[...], g_ref[...], m_ref[...], v_ref[...])

    spec = pl.BlockSpec((br, c), lambda i: (i, 0))
    return pl.pallas_call(body, name=name, grid=(r // br,),
        in_specs=[spec] * 4, out_specs=[spec] * 3,
        out_shape=[jax.ShapeDtypeStruct((r, c), F32)] * 3,
        compiler_params=_params("arbitrary"),
    )(w, g, m, v)


def _adam_math(w, g, m, v):
    m2 = ADAM_B1 * m + (1.0 - ADAM_B1) * g
    v2 = ADAM_B2 * v + (1.0 - ADAM_B2) * (g * g)
    m_hat = m2 / (1.0 - ADAM_B1 ** ADAM_STEP)
    v_hat = v2 / (1.0 - ADAM_B2 ** ADAM_STEP)
    return -ADAM_LR * (m_hat / (jnp.sqrt(v_hat) + ADAM_EPS) + ADAM_WD * w), m2, v2


def _slot_sum(land_ref, *lead):
    acc = land_ref[(0,) + lead].astype(F32)
    for s in range(1, NDEV):
        acc = acc + land_ref[(s,) + lead].astype(F32)
    return acc


def _finalize(land, w, m, v, transposed, name):
    _, r, c = land.shape
    cb = 256 if (transposed and c % 256 == 0) else c
    wblk = pl.BlockSpec((cb, r), lambda i: (i, 0)) if transposed else pl.BlockSpec((r, cb), lambda i: (0, i))

    def body(l_ref, w_ref, m_ref, v_ref, g_ref, d_ref, mo_ref, vo_ref):
        g = _slot_sum(l_ref)
        g = g.T if transposed else g
        g_ref[...] = g
        d_ref[...], mo_ref[...], vo_ref[...] = _adam_math(w_ref[...], g, m_ref[...], v_ref[...])

    return pl.pallas_call(body, name=name, grid=(c // cb,),
        in_specs=[pl.BlockSpec((NDEV, r, cb), lambda i: (0, 0, i)), wblk, wblk, wblk], out_specs=[wblk] * 4,
        out_shape=[jax.ShapeDtypeStruct(w.shape, F32)] * 4,
        compiler_params=_params("arbitrary"),
    )(land, w, m, v)


def _finalize_ffn(land0, land1, wg, wu, wd, mg, mu, md, vg, vu, vd):
    nl, fs, d = wg.shape
    db = min(256, d)

    def kernel_body(l0_ref, l1_ref, wg_ref, wu_ref, wd_ref, mg_ref, mu_ref, md_ref, vg_ref, vu_ref, vd_ref, *outs):
        layer = pl.program_id(0)
        triples = [(wg_ref, mg_ref, vg_ref), (wu_ref, mu_ref, vu_ref), (wd_ref, md_ref, vd_ref)]

        def run(land_ref):
            for j, (w_ref, m_ref, v_ref) in enumerate(triples):
                g = _slot_sum(land_ref, j)
                delta, m2, v2 = _adam_math(w_ref[0], g, m_ref[0], v_ref[0])
                for o_ref, val in zip(outs[j::3], (g, delta, m2, v2)):
                    o_ref[0] = val

        @pl.when(layer == 0)
        def _():
            run(l0_ref)

        @pl.when(layer == 1)
        def _():
            run(l1_ref)

    blk = pl.BlockSpec((1, fs, db), lambda l, i: (l, 0, i))
    lblk = [pl.BlockSpec((NDEV, 3, fs, db), lambda l, i: (0, 0, 0, i * (1 - l))),
            pl.BlockSpec((NDEV, 3, fs, db), lambda l, i: (0, 0, 0, i * l))]
    outs, _ = _call(kernel_body, name="finalize_ffn", grid=(nl, d // db),
        in_specs=lblk + [blk] * 9, out_specs=[blk] * 12, out_shape=[jax.ShapeDtypeStruct(wg.shape, F32)] * 12,
        args=(land0, land1, wg, wu, wd, mg, mu, md, vg, vu, vd))
    return outs


CONV_ROWS = 64
LANES = 128


def _shifted_copies(buf, sh, n):
    sh[0] = buf[...]
    for r in range(1, 8):
        sh[r, pl.ds(0, n - 8), :] = buf[pl.ds(r, n - 8), :]


def _window(sh, o, rows, cols):
    return sh[o % 8, pl.ds(o - o % 8, rows), cols]


def _conv_in(x, mod, gmix, gw, w1_rows, w1_idx, b1, tm, ride):
    s, d = x.shape

    def kernel_body(x_ref, mod_ref, g_ref, w_ref, b_ref, h_ref, u_ref, glu_ref):
        n, _ = _rms(x_ref[...])
        h = (n * g_ref[...]) * (1.0 + mod_ref[1:2, :]) + mod_ref[0:1, :]
        hb = h.astype(BF16)
        h_ref[...] = hb
        u = _nt(hb, w_ref[...].reshape(NDEV * w1_rows, d)) + b_ref[...]
        u_ref[...] = u.astype(BF16)
        glu_ref[...] = (u[:, :d] * jax.nn.sigmoid(u[:, d:])).astype(BF16)

    tile = pl.BlockSpec((tm, d), lambda i: (i, 0))
    return _call(kernel_body, name="conv_in", grid=(s // tm,),
        in_specs=[tile, _row(8, d), _row(1, d), _weight_spec(w1_rows, d, w1_idx), _row(1, 2 * d)],
        out_specs=[tile, pl.BlockSpec((tm, 2 * d), lambda i: (i, 0)), tile],
        out_shape=[jax.ShapeDtypeStruct((s, d), BF16), jax.ShapeDtypeStruct((s, 2 * d), BF16),
                   jax.ShapeDtypeStruct((s, d), BF16)],
        args=(x, mod, gmix, gw, b1), ride=ride)


def _conv_mid(glu, wdw, bdw, ln_g, ln_b, gw, w2_rows, w2_idx, b2, x, mod, tm, ride):
    s, d = x.shape
    off = CONV_HALO - (CONV_WIDTH - 1)
    rc = min(CONV_ROWS, tm)
    nsub = 2 if s % (2 * tm) == 0 else 1
    tb = nsub * tm

    def kernel_body(glu_ref, halo_ref, wdw_ref, bdw_ref, lng_ref, lnb_ref, w_ref, b2_ref, x_ref, mod_ref,
                    dwc_ref, s_ref, y_ref, x1_ref, buf, sh):
        i = pl.program_id(0)
        w2 = w_ref[...].reshape(NDEV * w2_rows, d)
        for sub in range(nsub):
            rows = pl.ds(sub * tm, tm)
            if sub == 0:
                buf[pl.ds(0, CONV_HALO), :] = jnp.where(i > 0, halo_ref[...].astype(F32), 0.0)
            else:
                buf[pl.ds(0, CONV_HALO), :] = glu_ref[pl.ds(sub * tm - CONV_HALO, CONV_HALO), :].astype(F32)
            buf[pl.ds(CONV_HALO, tm), :] = glu_ref[rows, :].astype(F32)
            _shifted_copies(buf, sh, CONV_HALO + tm)
            for cb in range(d // LANES):
                cols = pl.ds(cb * LANES, LANES)
                taps = wdw_ref[:, cols]
                for r in range(tm // rc):
                    part = jnp.zeros((rc, LANES), F32) + bdw_ref[:, cols]
                    for k in range(CONV_WIDTH):
                        part = part + _window(sh, r * rc + off + k, rc, cols) * taps[k:k + 1, :]
                    dwc_ref[pl.ds(sub * tm + r * rc, rc), cols] = part
            acc = dwc_ref[rows, :]
            mu = jnp.mean(acc, axis=-1, keepdims=True)
            xc = acc - mu
            rstd = lax.rsqrt(jnp.mean(xc * xc, axis=-1, keepdims=True) + EPS)
            ln = (xc * rstd) * lng_ref[...] + lnb_ref[...]
            sb = (ln * jax.nn.sigmoid(ln)).astype(BF16)
            s_ref[rows, :] = sb
            y = _nn(sb, w2) + b2_ref[...]
            y_ref[rows, :] = y.astype(BF16)
            x1_ref[rows, :] = x_ref[rows, :] + (1.0 + mod_ref[2:3, :]) * y

    tile = pl.BlockSpec((tb, d), lambda i: (i, 0))
    halo = pl.BlockSpec((CONV_HALO, d), lambda i: (jnp.maximum(i * (tb // CONV_HALO) - 1, 0), 0))
    return _call(kernel_body, name="conv_mid", grid=(s // tb,),
        in_specs=[tile, halo, _row(CONV_WIDTH, d), _row(1, d), _row(1, d), _row(1, d),
                  _weight_spec(w2_rows, d, w2_idx), _row(1, d), tile, _row(8, d)],
        out_specs=[tile, tile, tile, tile],
        out_shape=[jax.ShapeDtypeStruct((s, d), F32), jax.ShapeDtypeStruct((s, d), BF16),
                   jax.ShapeDtypeStruct((s, d), BF16), jax.ShapeDtypeStruct((s, d), F32)],
        scratch_shapes=[pltpu.VMEM((CONV_HALO + tm, d), F32), pltpu.VMEM((8, CONV_HALO + tm, d), F32)],
        args=(glu, glu, wdw, bdw, ln_g, ln_b, gw, b2, x, mod), ride=ride)


def _ffn_fwd(x, mod, gffn, weights, fs, f, tm, name, ride=None, loss=None):
    s, d = x.shape

    last = s // tm - 1

    def kernel_body(x_ref, mod_ref, g_ref, wg_ref, wu_ref, wd_ref, *rest):
        xv = x_ref[...]
        n, _ = _rms(xv)
        hb = ((n * g_ref[...]) * (1.0 + mod_ref[4:5, :]) + mod_ref[3:4, :]).astype(BF16)
        gg = _nt(hb, wg_ref[...].reshape(f, d))
        uu = _nt(hb, wu_ref[...].reshape(f, d))
        ab = ((gg * jax.nn.sigmoid(gg)) * uu).astype(BF16)
        y = _nn(ab, wd_ref[...].reshape(f, d))
        xo = xv + (1.0 + mod_ref[5:6, :]) * y
        if loss is None:
            h_ref, gg_ref, uu_ref, y_ref, xo_ref = rest
            xo_ref[...] = xo
        else:
            t_ref, gfin_ref, h_ref, gg_ref, uu_ref, y_ref, dx_ref, st_ref = rest
            dx_ref[...] = _loss_tile(xo, t_ref[...], gfin_ref[...], st_ref, pl.program_id(0), last)
        h_ref[...] = hb
        gg_ref[...] = gg.astype(BF16)
        uu_ref[...] = uu.astype(BF16)
        y_ref[...] = y.astype(BF16)

    tile = pl.BlockSpec((tm, d), lambda i: (i, 0))
    wide = pl.BlockSpec((tm, f), lambda i: (i, 0))
    extra_in = [] if loss is None else [tile, _row(1, d)]
    extra_out = [] if loss is None else [_row(8, d)]
    return _call(kernel_body, name=name, grid=(s // tm,),
        in_specs=[tile, _row(8, d), _row(1, d)] + [_weight_spec(fs, d, idx) for _, idx in weights] + extra_in,
        out_specs=[tile, wide, wide, tile, tile] + extra_out,
        out_shape=[jax.ShapeDtypeStruct((s, d), BF16), jax.ShapeDtypeStruct((s, f), BF16),
                   jax.ShapeDtypeStruct((s, f), BF16), jax.ShapeDtypeStruct((s, d), BF16),
                   jax.ShapeDtypeStruct((s, d), F32)] + [jax.ShapeDtypeStruct((8, d), F32)] * len(extra_out),
        args=(x, mod, gffn) + tuple(a for a, _ in weights) + (() if loss is None else tuple(loss)), ride=ride)


def _loss_tile(x, target, g, st_ref, i, last):
    d = x.shape[1]

    @pl.when(i == 0)
    def _():
        st_ref[...] = jnp.zeros_like(st_ref)

    n, rinv = _rms(x)
    err = n * g - target
    dy = err * (1.0 / d)
    st_ref[0:1, :] += _sum0(dy * n)
    st_ref[1:2, :] += _sum0(err * err) * (0.5 / d)

    @pl.when(i == last)
    def _():
        st_ref[2:3, :] = jnp.zeros((1, d), F32) + jnp.sum(st_ref[1:2, :], axis=-1, keepdims=True)

    dn = dy * g
    return rinv * (dn - n * jnp.mean(dn * n, axis=-1, keepdims=True))


def _pool_fwd(x, mod, gmix, pw, ls, tm, ride):
    s, d = x.shape
    dg = d // len(POOL_WINDOWS)

    def body(x_ref, halo_ref, mod_ref, g_ref, pw_ref, ls_ref, mixed_ref, yp_ref, xo_ref, buf):
        i = pl.program_id(0)

        def hfun(xv):
            n, _ = _rms(xv)
            return (n * g_ref[...]) * (1.0 + mod_ref[1:2, :]) + mod_ref[0:1, :]

        xv = x_ref[...]
        h = hfun(xv)
        buf[pl.ds(0, POOL_HALO), :] = jnp.where(i > 0, hfun(halo_ref[...]), 0.0)
        buf[pl.ds(POOL_HALO, tm), :] = h
        t = i * tm + lax.broadcasted_iota(jnp.int32, (tm, 1), 0)
        gate = 1.0 + mod_ref[2:3, :]
        for gi, w in enumerate(POOL_WINDOWS):
            cols = pl.ds(gi * dg, dg)
            ws = buf[pl.ds(POOL_HALO, tm), cols]
            for j in range(1, w):
                ws = ws + buf[pl.ds(POOL_HALO - j, tm), cols]
            inv = 1.0 / jnp.minimum(t + 1, w).astype(F32)
            mb = (ws * inv - h[:, gi * dg:(gi + 1) * dg]).astype(BF16)
            mixed_ref[:, cols] = mb
            yp = _nn(mb, pw_ref[gi])
            yp_ref[:, cols] = yp.astype(BF16)
            xo_ref[:, cols] = xv[:, gi * dg:(gi + 1) * dg] + gate[:, gi * dg:(gi + 1) * dg] * (yp * ls_ref[:, cols])

    tile = pl.BlockSpec((tm, d), lambda i: (i, 0))
    halo = pl.BlockSpec((POOL_HALO, d), lambda i: (jnp.maximum(i * (tm // POOL_HALO) - 1, 0), 0))
    return _call(body, name="pool_fwd", grid=(s // tm,),
        in_specs=[tile, halo, _row(8, d), _row(1, d), pl.BlockSpec((len(POOL_WINDOWS), dg, dg), lambda i: (0, 0, 0)),
                  _row(1, d)],
        out_specs=[tile, tile, tile],
        out_shape=[jax.ShapeDtypeStruct((s, d), BF16), jax.ShapeDtypeStruct((s, d), BF16),
                   jax.ShapeDtypeStruct((s, d), F32)],
        scratch_shapes=[pltpu.VMEM((POOL_HALO + tm, d), F32)],
        args=(x, x, mod, gmix, pw, ls), ride=ride)


def _ffn_bwd(dxo, x, gg, uu, y, mod, gffn, weights, fs, f, tm, name, ride=None):
    s, d = x.shape

    def kernel_body(dxo_ref, x_ref, gg_ref, uu_ref, y_ref, mod_ref, g_ref, wg_ref, wu_ref, wd_ref,
                    dg_ref, du_ref, a_ref, dy_ref, dxi_ref, st_ref):
        @pl.when(pl.program_id(0) == 0)
        def _():
            st_ref[...] = jnp.zeros_like(st_ref)

        dxo_v = dxo_ref[...]
        dyb = (dxo_v * (1.0 + mod_ref[5:6, :])).astype(BF16)
        dy_ref[...] = dyb
        da = _nt(dyb, wd_ref[...].reshape(f, d))
        ggv, uuv = gg_ref[...].astype(F32), uu_ref[...].astype(F32)
        sg = jax.nn.sigmoid(ggv)
        silu = ggv * sg
        a_ref[...] = (silu * uuv).astype(BF16)
        dub = (da * silu).astype(BF16)
        dgb = (da * uuv * _silu_grad(ggv, sg)).astype(BF16)
        du_ref[...] = dub
        dg_ref[...] = dgb
        dh = _nn(dgb, wg_ref[...].reshape(f, d)) + _nn(dub, wu_ref[...].reshape(f, d))
        n, rinv = _rms(x_ref[...])
        dx, dsh, dsc, dgain = _rms_mod_bwd(dh, n, rinv, g_ref[...], mod_ref[4:5, :])
        dxi_ref[...] = dxo_v + dx
        st_ref[0:1, :] += dsh
        st_ref[1:2, :] += dsc
        st_ref[2:3, :] += _sum0(dxo_v * y_ref[...].astype(F32))
        st_ref[3:4, :] += dgain

    tile = pl.BlockSpec((tm, d), lambda i: (i, 0))
    wide = pl.BlockSpec((tm, f), lambda i: (i, 0))
    return _call(kernel_body, name=name, grid=(s // tm,),
        in_specs=[tile, tile, wide, wide, tile, _row(8, d), _row(1, d)] + [_weight_spec(fs, d, idx) for _, idx in weights],
        out_specs=[wide, wide, wide, tile, tile, _row(8, d)],
        out_shape=[jax.ShapeDtypeStruct((s, f), BF16)] * 3 + [jax.ShapeDtypeStruct((s, d), BF16),
                   jax.ShapeDtypeStruct((s, d), F32), jax.ShapeDtypeStruct((8, d), F32)],
        args=(dxo, x, gg, uu, y, mod, gffn) + tuple(a for a, _ in weights), ride=ride)


def _ffn_wgrad(dgb, dub, ab, h, dyb, fb, ts, name):
    s, f = dgb.shape
    d = h.shape[1]
    last = s // ts - 1

    def body(dg_ref, du_ref, a_ref, h_ref, dy_ref, o_ref, acc):
        t = pl.program_id(1)

        @pl.when(t == 0)
        def _():
            acc[...] = jnp.zeros_like(acc)

        hv = h_ref[...]
        acc[0] += _tn(dg_ref[...], hv)
        acc[1] += _tn(du_ref[...], hv)
        acc[2] += _tn(a_ref[...], dy_ref[...])

        @pl.when(t == last)
        def _():
            o_ref[...] = acc[...].astype(BF16)

    wide = pl.BlockSpec((ts, fb), lambda j, t: (t, j))
    tile = pl.BlockSpec((ts, d), lambda j, t: (t, 0))
    return pl.pallas_call(body, name=name, grid=(f // fb, s // ts),
        in_specs=[wide, wide, wide, tile, tile],
        out_specs=pl.BlockSpec((3, fb, d), lambda j, t: (0, j, 0)),
        out_shape=jax.ShapeDtypeStruct((3, f, d), BF16),
        scratch_shapes=[pltpu.VMEM((3, fb, d), F32)],
        compiler_params=_params("arbitrary", "arbitrary"),
    )(dgb, dub, ab, h, dyb)


def _pool_bwd(dxo, x, yp, mixed, mod, gmix, pw, ls, tm):
    s, d = x.shape
    ng = len(POOL_WINDOWS)
    dg = d // ng
    last = s // tm - 1

    def body(dxo_ref, dxh_ref, x_ref, yp_ref, mixed_ref, mod_ref, g_ref, pw_ref, ls_ref,
             dxi_ref, dpw_ref, st_ref, bufy, bufq, bufh, acc):
        i = pl.program_id(0)

        @pl.when(i == 0)
        def _():
            st_ref[...] = jnp.zeros_like(st_ref)
            acc[...] = jnp.zeros_like(acc)

        gate = 1.0 + mod_ref[2:3, :]
        lsv = ls_ref[...]
        dxo_v = dxo_ref[...]
        st_ref[2:3, :] += _sum0(dxo_v * yp_ref[...].astype(F32))
        bufy[pl.ds(0, tm), :] = (dxo_v * (gate * lsv)).astype(BF16)
        bufy[pl.ds(tm, POOL_HALO), :] = jnp.where(i < last, dxh_ref[...] * (gate * lsv), 0.0).astype(BF16)
        t = i * tm + lax.broadcasted_iota(jnp.int32, (tm + POOL_HALO, 1), 0)
        for gi, w in enumerate(POOL_WINDOWS):
            cols = pl.ds(gi * dg, dg)
            dm = _nt(bufy[:, cols], pw_ref[gi])
            bufq[:, cols] = dm * (1.0 / jnp.minimum(t + 1, w).astype(F32))
            dh = bufq[pl.ds(0, tm), cols] - dm[0:tm, :]
            for j in range(1, w):
                dh = dh + bufq[pl.ds(j, tm), cols]
            bufh[:, cols] = dh
            acc[gi] += _tn(mixed_ref[:, cols], bufy[pl.ds(0, tm), cols])
        n, rinv = _rms(x_ref[...])
        dx, dsh, dsc, dgain = _rms_mod_bwd(bufh[...], n, rinv, g_ref[...], mod_ref[1:2, :])
        dxi_ref[...] = dxo_v + dx
        st_ref[0:1, :] += dsh
        st_ref[1:2, :] += dsc
        st_ref[3:4, :] += dgain

        @pl.when(i == last)
        def _():
            r = st_ref[2:3, :]
            st_ref[4:5, :] = r * lsv
            st_ref[5:6, :] = r * gate
            dpw_ref[...] = acc[...].astype(BF16)

    tile = pl.BlockSpec((tm, d), lambda i: (i, 0))
    nxt = pl.BlockSpec((POOL_HALO, d), lambda i: (jnp.minimum((i + 1) * (tm // POOL_HALO), s // POOL_HALO - 1), 0))
    pws = pl.BlockSpec((ng, dg, dg), lambda i: (0, 0, 0))
    return _call(body, name="pool_bwd", grid=(s // tm,),
        in_specs=[tile, nxt, tile, tile, tile, _row(8, d), _row(1, d), pws, _row(1, d)],
        out_specs=[tile, pws, _row(8, d)],
        out_shape=[jax.ShapeDtypeStruct((s, d), F32), jax.ShapeDtypeStruct((ng, dg, dg), BF16),
                   jax.ShapeDtypeStruct((8, d), F32)],
        scratch_shapes=[pltpu.VMEM((tm + POOL_HALO, d), BF16), pltpu.VMEM((tm + POOL_HALO, d), F32),
                        pltpu.VMEM((tm, d), F32), pltpu.VMEM((ng, dg, dg), F32)],
        args=(dxo, dxo, x, yp, mixed, mod, gmix, pw, ls))[0]


def _conv_bwd_mid(dxo, y, dwc, sb, mod, ln_g, ln_b, gw, w2_rows, w2_idx, tm):
    s, d = dwc.shape
    last = s // tm - 1

    def body(dxo_ref, y_ref, dwc_ref, s_ref, mod_ref, lng_ref, lnb_ref, w_ref, dd_ref, dw_ref, st_ref, acc):
        i = pl.program_id(0)

        @pl.when(i == 0)
        def _():
            st_ref[...] = jnp.zeros_like(st_ref)
            acc[...] = jnp.zeros_like(acc)

        dxo_v = dxo_ref[...]
        st_ref[0:1, :] += _sum0(dxo_v * y_ref[...].astype(F32))
        dy = dxo_v * (1.0 + mod_ref[2:3, :])
        st_ref[1:2, :] += _sum0(dy)
        dyb = dy.astype(BF16)
        ds = _nt(dyb, w_ref[...].reshape(NDEV * w2_rows, d))
        acc[...] += _tn(s_ref[...], dyb)
        v = dwc_ref[...]
        mu = jnp.mean(v, axis=-1, keepdims=True)
        xc = v - mu
        rstd = lax.rsqrt(jnp.mean(xc * xc, axis=-1, keepdims=True) + EPS)
        xhat = xc * rstd
        ln = xhat * lng_ref[...] + lnb_ref[...]
        dln = ds * _silu_grad(ln, jax.nn.sigmoid(ln))
        st_ref[2:3, :] += _sum0(dln * xhat)
        st_ref[3:4, :] += _sum0(dln)
        dxh = dln * lng_ref[...]
        dd = rstd * (dxh - jnp.mean(dxh, axis=-1, keepdims=True) - xhat * jnp.mean(dxh * xhat, axis=-1, keepdims=True))
        dd_ref[...] = dd
        st_ref[4:5, :] += _sum0(dd)

        @pl.when(i == last)
        def _():
            dw_ref[...] = acc[...].astype(BF16)

    tile = pl.BlockSpec((tm, d), lambda i: (i, 0))
    return _call(body, name="conv_bwd_mid", grid=(s // tm,),
        in_specs=[tile, tile, tile, tile, _row(8, d), _row(1, d), _row(1, d), _weight_spec(w2_rows, d, w2_idx)],
        out_specs=[tile, pl.BlockSpec((d, d), lambda i: (0, 0)), _row(8, d)],
        out_shape=[jax.ShapeDtypeStruct((s, d), F32), jax.ShapeDtypeStruct((d, d), BF16),
                   jax.ShapeDtypeStruct((8, d), F32)],
        scratch_shapes=[pltpu.VMEM((d, d), F32)],
        args=(dxo, y, dwc, sb, mod, ln_g, ln_b, gw))[0]


def _conv_bwd_in(dd, glu, u, hb, x, dxo, wdw, gw, w1_rows, w1_idx, mod, gmix, tm, nsub, ride):
    s, d = x.shape
    off = CONV_HALO - (CONV_WIDTH - 1)
    rw = 32
    tap_group = 16
    tb = nsub * tm
    last = s // tb - 1

    def kernel_body(dd_ref, ddn_ref, glu_ref, glp_ref, u_ref, h_ref, x_ref, dxo_ref, wdw_ref, w_ref, mod_ref, g_ref,
                    dxi_ref, dw_ref, dwdw_ref, st_ref, bufd, bufg, shd, shg, dgl, accw, acc):
        i = pl.program_id(0)

        @pl.when(i == 0)
        def _():
            st_ref[...] = jnp.zeros_like(st_ref)
            accw[...] = jnp.zeros_like(accw)
            acc[...] = jnp.zeros_like(acc)

        w1t = w_ref[...].reshape(NDEV * w1_rows, d)
        for sub in range(nsub):
            base = sub * tm
            tile_rows = pl.ds(base, tm)
            bufd[pl.ds(0, tm), :] = dd_ref[tile_rows, :]
            if sub == nsub - 1:
                bufd[pl.ds(tm, CONV_HALO), :] = jnp.where(i < last, ddn_ref[...], 0.0)
            else:
                bufd[pl.ds(tm, CONV_HALO), :] = dd_ref[pl.ds(base + tm, CONV_HALO), :]
            if sub == 0:
                bufg[pl.ds(0, CONV_HALO), :] = jnp.where(i > 0, glp_ref[...].astype(F32), 0.0)
            else:
                bufg[pl.ds(0, CONV_HALO), :] = glu_ref[pl.ds(base - CONV_HALO, CONV_HALO), :].astype(F32)
            bufg[pl.ds(CONV_HALO, tm), :] = glu_ref[tile_rows, :].astype(F32)
            _shifted_copies(bufd, shd, tm + CONV_HALO)
            _shifted_copies(bufg, shg, CONV_HALO + tm)
            for cb in range(d // LANES):
                cols = pl.ds(cb * LANES, LANES)
                taps = wdw_ref[:, cols]
                for r in range(tm // rw):
                    part = jnp.zeros((rw, LANES), F32)
                    for k in range(CONV_WIDTH):
                        part = part + _window(shd, r * rw + CONV_WIDTH - 1 - k, rw, cols) * taps[k:k + 1, :]
                    dgl[sub, pl.ds(r * rw, rw), cols] = part
                for k0 in range(0, CONV_WIDTH, tap_group):
                    group = range(k0, min(CONV_WIDTH, k0 + tap_group))
                    sums = {k: jnp.zeros((8, LANES), F32) for k in group}
                    for r in range(tm // rw):
                        ddc = bufd[pl.ds(r * rw, rw), cols]
                        for k in group:
                            p = _window(shg, r * rw + off + k, rw, cols) * ddc
                            for q in range(rw // 8):
                                sums[k] = sums[k] + p[q * 8:(q + 1) * 8, :]
                    for k in group:
                        accw[k, :, cols] += sums[k]
            dglu = dgl[sub]
            uv = u_ref[tile_rows, :].astype(F32)
            a, g = uv[:, :d], uv[:, d:]
            sg = jax.nn.sigmoid(g)
            du = jnp.concatenate([dglu * sg, dglu * a * (sg * (1.0 - sg))], axis=1)
            st_ref[0:1, :] += _sum0(du)
            dub = du.astype(BF16)
            dh = _nn(dub, w1t)
            acc[...] += _tn(dub, h_ref[tile_rows, :])
            n, rinv = _rms(x_ref[tile_rows, :])
            dx, dsh, dsc, dgain = _rms_mod_bwd(dh, n, rinv, g_ref[...], mod_ref[1:2, :])
            dxi_ref[tile_rows, :] = dxo_ref[tile_rows, :] + dx
            st_ref[1:2, 0:d] += dsh
            st_ref[2:3, 0:d] += dsc
            st_ref[3:4, 0:d] += dgain

        @pl.when(i == last)
        def _():
            dw_ref[...] = acc[...].astype(BF16)
            dwdw_ref[...] = jnp.sum(accw[...], axis=1)

    tile = pl.BlockSpec((tb, d), lambda i: (i, 0))
    prv = pl.BlockSpec((CONV_HALO, d), lambda i: (jnp.maximum(i * (tb // CONV_HALO) - 1, 0), 0))
    nxt = pl.BlockSpec((CONV_HALO, d), lambda i: (jnp.minimum((i + 1) * (tb // CONV_HALO), s // CONV_HALO - 1), 0))
    return _call(kernel_body, name="conv_bwd_in", grid=(s // tb,),
        in_specs=[tile, nxt, tile, prv, pl.BlockSpec((tb, 2 * d), lambda i: (i, 0)), tile, tile, tile,
                  _row(CONV_WIDTH, d), _weight_spec(w1_rows, d, w1_idx), _row(8, d), _row(1, d)],
        out_specs=[tile, pl.BlockSpec((2 * d, d), lambda i: (0, 0)), _row(CONV_HALO, d), _row(8, 2 * d)],
        out_shape=[jax.ShapeDtypeStruct((s, d), F32), jax.ShapeDtypeStruct((2 * d, d), BF16),
                   jax.ShapeDtypeStruct((CONV_HALO, d), F32), jax.ShapeDtypeStruct((8, 2 * d), F32)],
        scratch_shapes=[pltpu.VMEM((tm + CONV_HALO, d), F32), pltpu.VMEM((CONV_HALO + tm, d), F32),
                        pltpu.VMEM((8, tm + CONV_HALO, d), F32), pltpu.VMEM((8, CONV_HALO + tm, d), F32),
                        pltpu.VMEM((nsub, tm, d), F32), pltpu.VMEM((CONV_HALO, 8, d), F32),
                        pltpu.VMEM((2 * d, d), F32)],
        args=(dd, dd, glu, glu, u, hb, x, dxo, wdw, gw, mod, gmix), ride=ride)


def kernel(x, c, ada_w, ada_b, norm_mix_g, norm_ffn_g, conv_w1, conv_b1, conv_wdw, conv_bdw, conv_ln_g, conv_ln_b, conv_w2, conv_b2, pool_w, pool_ls, ffn_w_gate, ffn_w_up, ffn_w_down, final_g, loss_target, m_ada_w, m_ada_b, m_norm_mix_g, m_norm_ffn_g, m_conv_w1, m_conv_b1, m_conv_wdw, m_conv_bdw, m_conv_ln_g, m_conv_ln_b, m_conv_w2, m_conv_b2, m_pool_w, m_pool_ls, m_ffn_w_gate, m_ffn_w_up, m_ffn_w_down, m_final_g, v_ada_w, v_ada_b, v_norm_mix_g, v_norm_ffn_g, v_conv_w1, v_conv_b1, v_conv_wdw, v_conv_bdw, v_conv_ln_g, v_conv_ln_b, v_conv_w2, v_conv_b2, v_pool_w, v_pool_ls, v_ffn_w_gate, v_ffn_w_up, v_ffn_w_down, v_final_g):
    _, s, d = x.shape
    f = ffn_w_down.shape[1] * NDEV
    fs = f // NDEV
    r1, r2 = 2 * d // NDEV, d // NDEV
    ng = len(POOL_WINDOWS)
    dg = d // ng
    pr = ng * (dg // NDEV) * dg // d
    ncol = ada_w.shape[2]
    dc = d // NDEV
    tm = min(256, s)
    me = _my_index()
    x0 = x.reshape(s, d)
    target = loss_target.reshape(s, d)

    small = jnp.concatenate([c.reshape(NDEV, dc), conv_wdw[0], pool_ls], axis=0)
    shard_a = conv_w1[0].T.astype(BF16)
    bias = lax.dynamic_slice_in_dim(ada_b, me * ncol, ncol, axis=1)[:, None, :]
    small_all, mod_all, (gwa,) = _prologue(small, ada_w, bias, _Ride("gather", [shard_a]))
    c_all = small_all[:, 0:NDEV, :].reshape(NDEV, d)
    wdw = small_all[:, NDEV:NDEV + CONV_WIDTH, :].transpose(1, 0, 2).reshape(CONV_WIDTH, d)
    ls = small_all[:, NDEV + CONV_WIDTH, :].reshape(1, d)
    mod_mine = lax.dynamic_index_in_dim(mod_all.reshape(NDEV, 2, NDEV, ncol), me, axis=2, keepdims=False)
    mod = mod_mine.transpose(1, 0, 2).reshape(2, 6, d)
    mod = jnp.concatenate([mod, jnp.zeros((2, 2, d), F32)], axis=1)

    shard_b1 = jnp.concatenate([ffn_w_gate[0].T, conv_w2[0]], axis=0).astype(BF16)
    shard_b2 = jnp.concatenate([ffn_w_up[0].T, ffn_w_down[0]], axis=0).astype(BF16)
    shard_c = jnp.concatenate([ffn_w_gate[1].T, ffn_w_up[1].T, pool_w.reshape(pr, d)], axis=0).astype(BF16)
    shard_d = ffn_w_down[1].astype(BF16)
    w1_at = (gwa, 0)

    (h0, u, glu), (gwb1,) = _conv_in(x0, mod[0], norm_mix_g[0:1], w1_at[0], r1, w1_at[1], conv_b1, tm,
                                    _Ride("gather", [shard_b1]))
    w2_at = (gwb1[:, fs:fs + r2, :], 0)
    (dwc, sb, y0, x1), (gwb2,) = _conv_mid(glu, wdw, conv_bdw, conv_ln_g, conv_ln_b, w2_at[0], r2, w2_at[1], conv_b2,
                                           x0, mod[0], tm, _Ride("gather", [shard_b2]))
    ffn0_w = [(gwb1, 0), (gwb2, 0), (gwb2, 1)]
    (h1, gg0, uu0, yf0, x2), (gwc,) = _ffn_fwd(x1, mod[0], norm_ffn_g[0:1], ffn0_w, fs, f, tm, "ffn_fwd0",
                                               _Ride("gather", [shard_c]))
    pw = gwc[:, 2 * fs:2 * fs + pr, :].reshape(NDEV, ng, dg // NDEV, dg).transpose(1, 0, 2, 3).reshape(ng, dg, dg)
    (mixed, yp, x3), (gwd,) = _pool_fwd(x2, mod[1], norm_mix_g[1:2], pw, ls, tm, _Ride("gather", [shard_d]))
    ffn1_w = [(gwc, 0), (gwc, 1), (gwd, 0)]
    (h3, gg1, uu1, yf1, dx4, st_loss), _ = _ffn_fwd(x3, mod[1], norm_ffn_g[1:2], ffn1_w, fs, f, tm, "ffn_fwd1",
                                                    loss=(target, final_g.reshape(1, d)))

    fb = f // 2 if (f // 2) % 128 == 0 else f
    ts = min(512, s)
    (dgb, dub, ab, dyb, dx3, st_f1), _ = _ffn_bwd(dx4, x3, gg1, uu1, yf1, mod[1], norm_ffn_g[1:2], ffn1_w, fs, f, tm,
                                                  "ffn_bwd1")
    gf1 = _ffn_wgrad(dgb, dub, ab, h3, dyb, fb, ts, "ffn_wgrad1")
    dx2, gpw, st_p = _pool_bwd(dx3, x2, yp, mixed, mod[1], norm_mix_g[1:2], pw, ls, tm)
    (dgb, dub, ab, dyb, dx1, st_f0), (land_f1,) = _ffn_bwd(dx2, x1, gg0, uu0, yf0, mod[0], norm_ffn_g[0:1], ffn0_w, fs, f,
                                                           tm, "ffn_bwd0", _Ride("scatter", [gf1]))
    gf0 = _ffn_wgrad(dgb, dub, ab, h1, dyb, fb, ts, "ffn_wgrad0")
    dd, gw2, st_m = _conv_bwd_mid(dx1, y0, dwc, sb, mod[0], conv_ln_g, conv_ln_b, w2_at[0], r2, w2_at[1], tm)
    (dx0, gw1, gwdw, st_c), (land_f0, land_pw, land_w2) = _conv_bwd_in(
        dd, glu, u, h0, x0, dx1, wdw, w1_at[0], r1, w1_at[1], mod[0], norm_mix_g[0:1], tm, 1,
        _Ride("scatter", [gf0, gpw, gw2[None]]))

    prow = _pack_stats(st_c, st_m, st_f0, st_p, st_f1, st_loss, gwdw)
    p_all, (land_w1,) = _small_exchange(prow, _Ride("scatter", [gw1[None]]), "allgather_stats")
    psum = _sum_slots(p_all, "sum_stats")
    loss = psum[prow.shape[0] - 1, 0]

    tr = lambda a: jnp.swapaxes(a, 1, 2)
    ffn_out = _finalize_ffn(land_f0, land_f1, tr(ffn_w_gate), tr(ffn_w_up), ffn_w_down,
                            tr(m_ffn_w_gate), tr(m_ffn_w_up), m_ffn_w_down, tr(v_ffn_w_gate), tr(v_ffn_w_up), v_ffn_w_down)
    fin_w1 = _finalize(land_w1.reshape(NDEV, r1, d), conv_w1[0], m_conv_w1[0], v_conv_w1[0], True, "finalize_w1")
    fin_w2 = _finalize(land_w2.reshape(NDEV, r2, d), conv_w2[0], m_conv_w2[0], v_conv_w2[0], False, "finalize_w2")
    pshape = (ng * (dg // NDEV), dg)
    fin_pw = _finalize(land_pw.reshape((NDEV,) + pshape), pool_w.reshape(pshape), m_pool_w.reshape(pshape),
                       v_pool_w.reshape(pshape), False, "finalize_pool_w")
    dmod_all = p_all[:, 0:12, :].reshape(NDEV, 2, 6 * d)
    dmod_cols = lax.dynamic_slice_in_dim(dmod_all, me * ncol, ncol, axis=2).transpose(1, 0, 2)
    g_ada_w = _ada_wgrad(c_all, dmod_cols)

    def adam(w, g, m, v, name):
        shp = w.shape
        w2d = (-1, shp[-1])
        dl, mo, vo = _adamw(w.reshape(w2d), g.reshape(w2d), m.reshape(w2d), v.reshape(w2d), name)
        return dl.reshape(shp), mo.reshape(shp), vo.reshape(shp)

    rep_names = ["ada_b", "norm_mix_g", "norm_ffn_g", "conv_b1", "conv_bdw", "conv_ln_g", "conv_ln_b", "conv_b2", "final_g"]
    rep_w = [ada_b, norm_mix_g, norm_ffn_g, conv_b1, conv_bdw, conv_ln_g, conv_ln_b, conv_b2, final_g]
    rep_m = [m_ada_b, m_norm_mix_g, m_norm_ffn_g, m_conv_b1, m_conv_bdw, m_conv_ln_g, m_conv_ln_b, m_conv_b2, m_final_g]
    rep_v = [v_ada_b, v_norm_mix_g, v_norm_ffn_g, v_conv_b1, v_conv_bdw, v_conv_ln_g, v_conv_ln_b, v_conv_b2, v_final_g]
    nrep = sum(w.size for w in rep_w) // d
    pad = jnp.zeros(((-nrep) % 8, d), F32)

    def pack(arrs, fill):
        return jnp.concatenate([a.reshape(-1, d) for a in arrs] + [pad + fill], axis=0)

    rep_g = jnp.concatenate([psum[0:nrep], pad], axis=0)
    rep_d, rep_mo, rep_vo = _adamw(pack(rep_w, 0.0), rep_g, pack(rep_m, 0.0), pack(rep_v, 1.0), "adamw_replicated")

    def unpack(packed):
        out, cur = [], 0
        for w in rep_w:
            k = w.size // d
            out.append(packed[cur:cur + k].reshape(w.shape))
            cur += k
        return out

    rep = dict(zip(rep_names, zip(unpack(psum), unpack(rep_d), unpack(rep_mo), unpack(rep_vo))))

    g_wdw_full = psum[nrep:nrep + CONV_WIDTH]
    g_wdw = lax.dynamic_slice_in_dim(g_wdw_full, me * dc, dc, axis=1)
    g_ls = lax.dynamic_slice_in_dim(psum[nrep + CONV_WIDTH:nrep + CONV_WIDTH + 1], me * dc, dc, axis=1)
    tiny = lambda a, b: jnp.concatenate([a.reshape(CONV_WIDTH, dc), b.reshape(1, dc)], axis=0)
    t_d, t_m, t_v = _adamw(tiny(conv_wdw, pool_ls), tiny(g_wdw, g_ls), tiny(m_conv_wdw, m_pool_ls),
                           tiny(v_conv_wdw, v_pool_ls), "adamw_taps")

    def taps(a):
        return a[0:CONV_WIDTH][None], a[CONV_WIDTH:CONV_WIDTH + 1]

    sharded = {
        "ada_w": (g_ada_w,) + adam(ada_w, g_ada_w, m_ada_w, v_ada_w, "adamw_ada_w"),
        "conv_w1": tuple(a[None] for a in fin_w1),
        "conv_w2": tuple(a[None] for a in fin_w2),
        "pool_w": tuple(a.reshape(pool_w.shape) for a in fin_pw),
        "ffn_w_gate": tuple(tr(a) for a in ffn_out[0::3]),
        "ffn_w_up": tuple(tr(a) for a in ffn_out[1::3]),
        "ffn_w_down": tuple(ffn_out[2::3]),
        "conv_wdw": (g_wdw[None], taps(t_d)[0], taps(t_m)[0], taps(t_v)[0]),
        "pool_ls": (g_ls, taps(t_d)[1], taps(t_m)[1], taps(t_v)[1]),
    }
    every = {**rep, **sharded}
    order = ["ada_w", "ada_b", "norm_mix_g", "norm_ffn_g", "conv_w1", "conv_b1", "conv_wdw", "conv_bdw", "conv_ln_g",
             "conv_ln_b", "conv_w2", "conv_b2", "pool_w", "pool_ls", "ffn_w_gate", "ffn_w_up", "ffn_w_down", "final_g"]
    grads = [every[n][0] for n in order]
    deltas = [every[n][1] for n in order]
    new_m = [every[n][2] for n in order]
    new_v = [every[n][3] for n in order]
    return (loss, dx0.reshape(1, s, d), *grads, *deltas, *new_m, *new_v)
```

```python
import jax
import jax.numpy as jnp
from jax import lax
from jax.experimental import pallas as pl
from jax.experimental.pallas import tpu as pltpu

NDEV = 8
EPS = 1e-6
CONV_WIDTH = 31
POOL_WINDOWS = (2, 4, 8, 16)
CONV_HALO = 32
POOL_HALO = 16
ADAM_LR = 0.001
ADAM_B1 = 0.9
ADAM_B2 = 0.999
ADAM_EPS = 1e-08
ADAM_WD = 0.01
ADAM_STEP = 10
VMEM_LIMIT = 56 * 2**20
MESH = pl.DeviceIdType.MESH
F32 = jnp.float32
BF16 = jnp.bfloat16


def _nt(a, b):
    return lax.dot_general(a, b, (((1,), (1,)), ((), ())), preferred_element_type=F32)


def _nn(a, b):
    return lax.dot_general(a, b, (((1,), (0,)), ((), ())), preferred_element_type=F32)


def _tn(a, b):
    return lax.dot_general(a, b, (((0,), (0,)), ((), ())), preferred_element_type=F32)


def _sum0(v):
    return jnp.sum(v, axis=0, keepdims=True)


def _rms(x):
    rinv = lax.rsqrt(jnp.mean(x * x, axis=-1, keepdims=True) + EPS)
    return x * rinv, rinv


def _rms_mod_bwd(dh, n, rinv, g, sc):
    dhs = dh * (1.0 + sc)
    dn = dhs * g
    dx = rinv * (dn - n * jnp.mean(dn * n, axis=-1, keepdims=True))
    return dx, _sum0(dh), _sum0(dh * (n * g)), _sum0(dhs * n)


def _silu_grad(z, sg):
    return sg * (1.0 + z * (1.0 - sg))


def _params(*sem):
    return pltpu.CompilerParams(dimension_semantics=sem, vmem_limit_bytes=VMEM_LIMIT)


def _row(i, d):
    return pl.BlockSpec((i, d), lambda *_: (0, 0))


def _weight_spec(rows, d, idx):
    return pl.BlockSpec((NDEV, rows, d), lambda *_: (0, idx, 0), pipeline_mode=pl.Buffered(1))


def _my_index():
    return 4 * lax.axis_index("x") + 2 * lax.axis_index("y") + lax.axis_index("c")


def _peer(k):
    x, y, c = lax.axis_index("x"), lax.axis_index("y"), lax.axis_index("c")
    px = 1 - x if k & 4 else x
    py = 1 - y if k & 2 else y
    pc = 1 - c if k & 1 else c
    return (px, py, pc), 4 * px + 2 * py + pc


def _gather_sems():
    return [pltpu.SemaphoreType.DMA((NDEV - 1,)), pltpu.SemaphoreType.DMA((NDEV - 1,)), pltpu.SemaphoreType.DMA((1,))]


class _Gather:
    def __init__(self, srcs, dsts, sems):
        self.src, self.dst = srcs[0], dsts[0]
        self.send, self.recv, self.local = sems
        x, y, c = lax.axis_index("x"), lax.axis_index("y"), lax.axis_index("c")
        self.me, self.sibling, self.core = (x, y, c), (x, y, 1 - c), c
        self.chips = [(1 - x, y), (x, 1 - y), (1 - x, 1 - y)]

    def _copy(self, k, block, to, from_input=False):
        slot = self.dst.at[4 * block[0] + 2 * block[1] + block[2]]
        return pltpu.make_async_remote_copy(
            src_ref=self.src if from_input else slot, dst_ref=slot, send_sem=self.send.at[k], recv_sem=self.recv.at[k],
            device_id=to, device_id_type=MESH)

    def _own(self):
        return pltpu.make_async_copy(self.src, self.dst.at[_my_index()], self.local.at[0])

    def _first(self):
        return [self._copy(0, self.me, self.sibling, True)] + [
            self._copy(1 + j, self.me, (*chip, self.core), True) for j, chip in enumerate(self.chips)]

    def start(self):
        self._own().start()
        for cp in self._first():
            cp.start()

    def forward(self):
        for j, chip in enumerate(self.chips):
            self._copy(1 + j, (*chip, self.core), self.me).wait_recv()
            self._copy(4 + j, (*chip, self.core), self.sibling).start()

    def finish(self):
        self._copy(0, self.sibling, self.me).wait_recv()
        for j, chip in enumerate(self.chips):
            self._copy(4 + j, (*chip, 1 - self.core), self.me).wait_recv()
        for cp in self._first():
            cp.wait_send()
        for j, chip in enumerate(self.chips):
            self._copy(4 + j, (*chip, self.core), self.sibling).wait_send()
        self._own().wait()


def _scatter_sems(n):
    return [pltpu.SemaphoreType.DMA((7 * n,)), pltpu.SemaphoreType.DMA((7 * n,)), pltpu.SemaphoreType.DMA((n,))]


class _Scatter:
    def __init__(self, srcs, dsts, sems):
        send_sems, recv_sems, local_sems = sems
        me = _my_index()
        self.copies = []
        for a, (src, dst) in enumerate(zip(srcs, dsts)):
            r = dst.shape[2]
            self.copies.append(pltpu.make_async_copy(src.at[:, pl.ds(me * r, r), :], dst.at[me], local_sems.at[a]))
            for k in range(1, NDEV):
                dev, p = _peer(k)
                self.copies.append(pltpu.make_async_remote_copy(
                    src_ref=src.at[:, pl.ds(p * r, r), :], dst_ref=dst.at[me],
                    send_sem=send_sems.at[a * 7 + k - 1], recv_sem=recv_sems.at[a * 7 + k - 1],
                    device_id=dev, device_id_type=MESH))

    def start(self):
        for cp in self.copies:
            cp.start()

    def forward(self):
        pass

    def finish(self):
        for cp in self.copies:
            cp.wait()


def _land_shape(part):
    a, r, c = part.shape
    return jax.ShapeDtypeStruct((NDEV, a, r // NDEV, c), part.dtype)


ANY = pl.BlockSpec(memory_space=pl.ANY)


class _Ride:
    def __init__(self, kind, srcs):
        self.kind, self.srcs = kind, list(srcs)
        if kind == "gather":
            self.out_shape = [jax.ShapeDtypeStruct((NDEV,) + a.shape, a.dtype) for a in self.srcs]
            self.sems = _gather_sems()
        else:
            self.out_shape = [_land_shape(a) for a in self.srcs]
            self.sems = _scatter_sems(len(self.srcs))

    def exchange(self, ins, outs, sems):
        return (_Gather if self.kind == "gather" else _Scatter)(ins, outs, sems)


def _small_exchange(small, ride, name):
    r, c = small.shape
    nr = len(ride.srcs)

    def body(v_ref, *refs):
        rin, refs = refs[:nr], refs[nr:]
        out_ref, refs = refs[0], refs[1:]
        rout, refs = refs[:nr], refs[nr:]
        send_sems, recv_sems, rsems = refs[0], refs[1], refs[2:]
        big = ride.exchange(rin, rout, rsems)
        big.start()
        copies = _small_pushes(v_ref, out_ref, send_sems, recv_sems)
        for cp in copies:
            cp.start()
        for cp in copies:
            cp.wait()
        big.forward()
        big.finish()

    res = pl.pallas_call(body, name=name,
        out_shape=[jax.ShapeDtypeStruct((NDEV, r, c), small.dtype)] + ride.out_shape,
        in_specs=[pl.BlockSpec(memory_space=pltpu.VMEM)] + [ANY] * nr,
        out_specs=[pl.BlockSpec(memory_space=pltpu.VMEM)] + [ANY] * nr,
        scratch_shapes=[pltpu.SemaphoreType.DMA((NDEV - 1,)), pltpu.SemaphoreType.DMA((NDEV - 1,))] + ride.sems,
    )(small, *ride.srcs)
    return res[0], res[1:]


def _small_pushes(src_ref, out_ref, send_sems, recv_sems):
    me = _my_index()
    out_ref[me] = src_ref[...]
    copies = []
    for k in range(1, NDEV):
        dev, _ = _peer(k)
        copies.append(pltpu.make_async_remote_copy(
            src_ref=src_ref, dst_ref=out_ref.at[me], send_sem=send_sems.at[k - 1], recv_sem=recv_sems.at[k - 1],
            device_id=dev, device_id_type=MESH))
    return copies


def _prologue(small, ada_w, bias, ride):
    r, c = small.shape
    nl, d, ncol = ada_w.shape
    nr = len(ride.srcs)

    def body(v_ref, w_ref, b_ref, *refs):
        rin, refs = refs[:nr], refs[nr:]
        out_ref, mod_ref, refs = refs[0], refs[1], refs[2:]
        rout, refs = refs[:nr], refs[nr:]
        cols, send1, recv1, send2, recv2, rsems = refs[0], refs[1], refs[2], refs[3], refs[4], refs[5:]
        big = ride.exchange(rin, rout, rsems)
        big.start()
        first = _small_pushes(v_ref, out_ref, send1, recv1)
        for cp in first:
            cp.start()
        for cp in first:
            cp.wait()
        for layer in range(nl):
            acc = jnp.zeros((NDEV, ncol), F32) + b_ref[layer]
            for j in range(d // c):
                cj = out_ref[:, j, :]
                acc = acc + jnp.dot(cj * jax.nn.sigmoid(cj), w_ref[layer, pl.ds(j * c, c), :],
                                    preferred_element_type=F32, precision=lax.Precision.HIGHEST)
            cols[pl.ds(layer * NDEV, NDEV), :] = acc
        second = _small_pushes(cols, mod_ref, send2, recv2)
        for cp in second:
            cp.start()
        for cp in second:
            cp.wait()
        big.forward()
        big.finish()

    vmem = pl.BlockSpec(memory_space=pltpu.VMEM)
    sem = pltpu.SemaphoreType.DMA((NDEV - 1,))
    res = pl.pallas_call(body, name="prologue",
        out_shape=[jax.ShapeDtypeStruct((NDEV, r, c), F32), jax.ShapeDtypeStruct((NDEV, nl * NDEV, ncol), F32)] + ride.out_shape,
        in_specs=[vmem, vmem, vmem] + [ANY] * nr, out_specs=[vmem, vmem] + [ANY] * nr,
        scratch_shapes=[pltpu.VMEM((nl * NDEV, ncol), F32), sem, sem, sem, sem] + ride.sems,
        compiler_params=pltpu.CompilerParams(vmem_limit_bytes=VMEM_LIMIT),
    )(small, ada_w, bias, *ride.srcs)
    return res[0], res[1], res[2:]


def _call(kernel_body, *, name, grid, in_specs, out_specs, out_shape, args, scratch_shapes=(), ride=None):
    n_in, n_out, n_sc = len(in_specs), len(out_specs), len(scratch_shapes)
    nr = len(ride.srcs) if ride else 0
    in_specs, out_specs, out_shape = list(in_specs), list(out_specs), list(out_shape)
    scratch_shapes, args = list(scratch_shapes), list(args)
    if ride is not None:
        in_specs += [ANY] * nr
        out_specs += [ANY] * nr
        out_shape += ride.out_shape
        args += ride.srcs
        scratch_shapes += ride.sems

    def body(*refs):
        ins, refs = refs[:n_in], refs[n_in:]
        rin, refs = refs[:nr], refs[nr:]
        outs, refs = refs[:n_out], refs[n_out:]
        rout, refs = refs[:nr], refs[nr:]
        scratch, rsems = refs[:n_sc], refs[n_sc:]
        first, last = True, True
        for axis, extent in enumerate(grid):
            first &= pl.program_id(axis) == 0
            last &= pl.program_id(axis) == extent - 1
        if ride is not None:
            middle = last if len(grid) > 1 else pl.program_id(0) == (3 * grid[0]) // 4
            exchange = ride.exchange(rin, rout, rsems)
            pl.when(first)(exchange.start)
            pl.when(middle)(exchange.forward)
        kernel_body(*ins, *outs, *scratch)
        if ride is not None:
            pl.when(last)(exchange.finish)

    res = pl.pallas_call(body, name=name, grid=grid, in_specs=in_specs, out_specs=out_specs, out_shape=out_shape,
                         scratch_shapes=scratch_shapes, compiler_params=_params(*(("arbitrary",) * len(grid))))(*args)
    return res[:n_out], res[n_out:]


def _ada_update(c_all, dmod, w, m, v):
    nl, d, ncol = w.shape
    bd = min(d, 256)

    def body(c_ref, dm_ref, w_ref, m_ref, v_ref, g_ref, d_ref, mo_ref, vo_ref):
        cv = c_ref[...]
        g = lax.dot_general(cv * jax.nn.sigmoid(cv), dm_ref[0], (((0,), (0,)), ((), ())), preferred_element_type=F32,
                            precision=lax.Precision.HIGHEST)
        g_ref[0] = g
        d_ref[0], mo_ref[0], vo_ref[0] = _adam_math(w_ref[0], g, m_ref[0], v_ref[0])

    blk = pl.BlockSpec((1, bd, ncol), lambda i, j: (i, j, 0))
    return pl.pallas_call(body, name="ada_update", grid=(nl, d // bd),
        in_specs=[pl.BlockSpec((NDEV, bd), lambda i, j: (0, j)), pl.BlockSpec((1, NDEV, ncol), lambda i, j: (i, 0, 0)),
                  blk, blk, blk],
        out_specs=[blk] * 4, out_shape=[jax.ShapeDtypeStruct(w.shape, F32)] * 4,
        compiler_params=_params("arbitrary", "arbitrary"),
    )(c_all, dmod, w, m, v)


def _pack_stats(st_c, st_m, st_f0, st_p, st_f1, st_loss, gwdw):
    d = st_m.shape[1]

    def body(c_ref, m_ref, f0_ref, p_ref, f1_ref, l_ref, w_ref, o_ref):
        pieces = [
            c_ref[1:3, 0:d], m_ref[0:1, :], f0_ref[0:3, :], p_ref[0:2, :], p_ref[4:5, :], f1_ref[0:3, :],
            c_ref[3:4, 0:d], p_ref[3:4, :], f0_ref[3:4, :], f1_ref[3:4, :],
            c_ref[0:1, 0:d], c_ref[0:1, d:2 * d], m_ref[4:5, :], m_ref[2:4, :], m_ref[1:2, :], l_ref[0:1, :],
            w_ref[0:CONV_WIDTH, :], p_ref[5:6, :], l_ref[2:3, :]]
        row = 0
        for piece in pieces:
            o_ref[pl.ds(row, piece.shape[0]), :] = piece
            row += piece.shape[0]

    return pl.pallas_call(body, name="pack_stats",
                          out_shape=jax.ShapeDtypeStruct((12 + 4 + 7 + CONV_WIDTH + 2, d), F32),
                          )(st_c, st_m, st_f0, st_p, st_f1, st_loss, gwdw)


def _row_block(r, c, bytes_per_row_elem=4, budget=2 * 2**20):
    if r * c * bytes_per_row_elem <= budget or r % 8:
        return r
    best = 8
    for b in range(8, r + 1, 8):
        if r % b == 0 and b * c * bytes_per_row_elem <= budget:
            best = b
    return best


def _sum_slots(land, name):
    _, r, c = land.shape
    br = _row_block(r, c, 8 * land.dtype.itemsize)

    def body(l_ref, o_ref):
        acc = l_ref[0].astype(F32)
        for s in range(1, NDEV):
            acc = acc + l_ref[s].astype(F32)
        o_ref[...] = acc

    return pl.pallas_call(body, name=name, grid=(r // br,),
        in_specs=[pl.BlockSpec((NDEV, br, c), lambda i: (0, i, 0))],
        out_specs=pl.BlockSpec((br, c), lambda i: (i, 0)),
        out_shape=jax.ShapeDtypeStruct((r, c), F32),
        compiler_params=_params("arbitrary"),
    )(land)


def _adamw(w, g, m, v, name):
    r, c = w.shape
    br = _row_block(r, c)

    def body(w_ref, g_ref, m_ref, v_ref, d_ref, mo_ref, vo_ref):
        d_ref[...], mo_ref[...], vo_ref[...] = _adam_math(w_ref[...], g_ref[...], m_ref[...], v_ref[...])

    spec = pl.BlockSpec((br, c), lambda i: (i, 0))
    return pl.pallas_call(body, name=name, grid=(r // br,),
        in_specs=[spec] * 4, out_specs=[spec] * 3,
        out_shape=[jax.ShapeDtypeStruct((r, c), F32)] * 3,
        compiler_params=_params("arbitrary"),
    )(w, g, m, v)


def _adam_math(w, g, m, v):
    m2 = ADAM_B1 * m + (1.0 - ADAM_B1) * g
    v2 = ADAM_B2 * v + (1.0 - ADAM_B2) * (g * g)
    m_hat = m2 / (1.0 - ADAM_B1 ** ADAM_STEP)
    v_hat = v2 / (1.0 - ADAM_B2 ** ADAM_STEP)
    return -ADAM_LR * (m_hat / (jnp.sqrt(v_hat) + ADAM_EPS) + ADAM_WD * w), m2, v2


def _slot_sum(land_ref, *lead):
    acc = land_ref[(0,) + lead].astype(F32)
    for s in range(1, NDEV):
        acc = acc + land_ref[(s,) + lead].astype(F32)
    return acc


def _finalize(land, w, m, v, transposed, name):
    _, r, c = land.shape
    cb = 256 if (transposed and c % 256 == 0) else c
    wblk = pl.BlockSpec((cb, r), lambda i: (i, 0)) if transposed else pl.BlockSpec((r, cb), lambda i: (0, i))

    def body(l_ref, w_ref, m_ref, v_ref, g_ref, d_ref, mo_ref, vo_ref):
        g = _slot_sum(l_ref)
        g = g.T if transposed else g
        g_ref[...] = g
        d_ref[...], mo_ref[...], vo_ref[...] = _adam_math(w_ref[...], g, m_ref[...], v_ref[...])

    return pl.pallas_call(body, name=name, grid=(c // cb,),
        in_specs=[pl.BlockSpec((NDEV, r, cb), lambda i: (0, 0, i)), wblk, wblk, wblk], out_specs=[wblk] * 4,
        out_shape=[jax.ShapeDtypeStruct(w.shape, F32)] * 4,
        compiler_params=_params("arbitrary"),
    )(land, w, m, v)


def _finalize_ffn(land0, land1, wg, wu, wd, mg, mu, md, vg, vu, vd):
    nl, fs, d = wg.shape
    db = min(256, d)

    def kernel_body(l0_ref, l1_ref, wg_ref, wu_ref, wd_ref, mg_ref, mu_ref, md_ref, vg_ref, vu_ref, vd_ref, *outs):
        layer = pl.program_id(0)
        triples = [(wg_ref, mg_ref, vg_ref), (wu_ref, mu_ref, vu_ref), (wd_ref, md_ref, vd_ref)]

        def run(land_ref):
            for j, (w_ref, m_ref, v_ref) in enumerate(triples):
                g = _slot_sum(land_ref, j)
                delta, m2, v2 = _adam_math(w_ref[0], g, m_ref[0], v_ref[0])
                for o_ref, val in zip(outs[j::3], (g, delta, m2, v2)):
                    o_ref[0] = val

        @pl.when(layer == 0)
        def _():
            run(l0_ref)

        @pl.when(layer == 1)
        def _():
            run(l1_ref)

    blk = pl.BlockSpec((1, fs, db), lambda l, i: (l, 0, i))
    lblk = [pl.BlockSpec((NDEV, 3, fs, db), lambda l, i: (0, 0, 0, i * (1 - l))),
            pl.BlockSpec((NDEV, 3, fs, db), lambda l, i: (0, 0, 0, i * l))]
    outs, _ = _call(kernel_body, name="finalize_ffn", grid=(nl, d // db),
        in_specs=lblk + [blk] * 9, out_specs=[blk] * 12, out_shape=[jax.ShapeDtypeStruct(wg.shape, F32)] * 12,
        args=(land0, land1, wg, wu, wd, mg, mu, md, vg, vu, vd))
    return outs


CONV_ROWS = 64
LANES = 128


def _shifted_copies(buf, sh, n):
    sh[0] = buf[...]
    for r in range(1, 8):
        sh[r, pl.ds(0, n - 8), :] = buf[pl.ds(r, n - 8), :]


def _window(sh, o, rows, cols):
    return sh[o % 8, pl.ds(o - o % 8, rows), cols]


def _conv_in(x, mod, gmix, gw, w1_rows, w1_idx, b1, tm, ride):
    s, d = x.shape

    def kernel_body(x_ref, mod_ref, g_ref, w_ref, b_ref, h_ref, u_ref, glu_ref):
        n, _ = _rms(x_ref[...])
        h = (n * g_ref[...]) * (1.0 + mod_ref[1:2, :]) + mod_ref[0:1, :]
        hb = h.astype(BF16)
        h_ref[...] = hb
        u = _nt(hb, w_ref[...].reshape(NDEV * w1_rows, d)) + b_ref[...]
        u_ref[...] = u.astype(BF16)
        glu_ref[...] = (u[:, :d] * jax.nn.sigmoid(u[:, d:])).astype(BF16)

    tile = pl.BlockSpec((tm, d), lambda i: (i, 0))
    return _call(kernel_body, name="conv_in", grid=(s // tm,),
        in_specs=[tile, _row(8, d), _row(1, d), _weight_spec(w1_rows, d, w1_idx), _row(1, 2 * d)],
        out_specs=[tile, pl.BlockSpec((tm, 2 * d), lambda i: (i, 0)), tile],
        out_shape=[jax.ShapeDtypeStruct((s, d), BF16), jax.ShapeDtypeStruct((s, 2 * d), BF16),
                   jax.ShapeDtypeStruct((s, d), BF16)],
        args=(x, mod, gmix, gw, b1), ride=ride)


def _conv_mid(glu, wdw, bdw, ln_g, ln_b, gw, w2_rows, w2_idx, b2, x, mod, tm, ride):
    s, d = x.shape
    off = CONV_HALO - (CONV_WIDTH - 1)
    rc = min(CONV_ROWS, tm)
    nsub = 2 if s % (2 * tm) == 0 else 1
    tb = nsub * tm

    def kernel_body(glu_ref, halo_ref, wdw_ref, bdw_ref, lng_ref, lnb_ref, w_ref, b2_ref, x_ref, mod_ref,
                    dwc_ref, s_ref, y_ref, x1_ref, buf, sh):
        i = pl.program_id(0)
        w2 = w_ref[...].reshape(NDEV * w2_rows, d)
        for sub in range(nsub):
            rows = pl.ds(sub * tm, tm)
            if sub == 0:
                buf[pl.ds(0, CONV_HALO), :] = jnp.where(i > 0, halo_ref[...].astype(F32), 0.0)
            else:
                buf[pl.ds(0, CONV_HALO), :] = glu_ref[pl.ds(sub * tm - CONV_HALO, CONV_HALO), :].astype(F32)
            buf[pl.ds(CONV_HALO, tm), :] = glu_ref[rows, :].astype(F32)
            _shifted_copies(buf, sh, CONV_HALO + tm)
            for cb in range(d // LANES):
                cols = pl.ds(cb * LANES, LANES)
                taps = wdw_ref[:, cols]
                for r in range(tm // rc):
                    part = jnp.zeros((rc, LANES), F32) + bdw_ref[:, cols]
                    for k in range(CONV_WIDTH):
                        part = part + _window(sh, r * rc + off + k, rc, cols) * taps[k:k + 1, :]
                    dwc_ref[pl.ds(sub * tm + r * rc, rc), cols] = part
            acc = dwc_ref[rows, :]
            mu = jnp.mean(acc, axis=-1, keepdims=True)
            xc = acc - mu
            rstd = lax.rsqrt(jnp.mean(xc * xc, axis=-1, keepdims=True) + EPS)
            ln = (xc * rstd) * lng_ref[...] + lnb_ref[...]
            sb = (ln * jax.nn.sigmoid(ln)).astype(BF16)
            s_ref[rows, :] = sb
            y = _nn(sb, w2) + b2_ref[...]
            y_ref[rows, :] = y.astype(BF16)
            x1_ref[rows, :] = x_ref[rows, :] + (1.0 + mod_ref[2:3, :]) * y

    tile = pl.BlockSpec((tb, d), lambda i: (i, 0))
    halo = pl.BlockSpec((CONV_HALO, d), lambda i: (jnp.maximum(i * (tb // CONV_HALO) - 1, 0), 0))
    return _call(kernel_body, name="conv_mid", grid=(s // tb,),
        in_specs=[tile, halo, _row(CONV_WIDTH, d), _row(1, d), _row(1, d), _row(1, d),
                  _weight_spec(w2_rows, d, w2_idx), _row(1, d), tile, _row(8, d)],
        out_specs=[tile, tile, tile, tile],
        out_shape=[jax.ShapeDtypeStruct((s, d), F32), jax.ShapeDtypeStruct((s, d), BF16),
                   jax.ShapeDtypeStruct((s, d), BF16), jax.ShapeDtypeStruct((s, d), F32)],
        scratch_shapes=[pltpu.VMEM((CONV_HALO + tm, d), F32), pltpu.VMEM((8, CONV_HALO + tm, d), F32)],
        args=(glu, glu, wdw, bdw, ln_g, ln_b, gw, b2, x, mod), ride=ride)


def _ffn_fwd(x, mod, gffn, weights, fs, f, tm, name, ride=None, loss=None):
    s, d = x.shape

    last = s // tm - 1

    def kernel_body(x_ref, mod_ref, g_ref, wg_ref, wu_ref, wd_ref, *rest):
        xv = x_ref[...]
        n, _ = _rms(xv)
        hb = ((n * g_ref[...]) * (1.0 + mod_ref[4:5, :]) + mod_ref[3:4, :]).astype(BF16)
        gg = _nt(hb, wg_ref[...].reshape(f, d))
        uu = _nt(hb, wu_ref[...].reshape(f, d))
        ab = ((gg * jax.nn.sigmoid(gg)) * uu).astype(BF16)
        y = _nn(ab, wd_ref[...].reshape(f, d))
        xo = xv + (1.0 + mod_ref[5:6, :]) * y
        if loss is None:
            h_ref, gg_ref, uu_ref, y_ref, xo_ref = rest
            xo_ref[...] = xo
        else:
            t_ref, gfin_ref, h_ref, gg_ref, uu_ref, y_ref, dx_ref, st_ref = rest
            dx_ref[...] = _loss_tile(xo, t_ref[...], gfin_ref[...], st_ref, pl.program_id(0), last)
        h_ref[...] = hb
        gg_ref[...] = gg.astype(BF16)
        uu_ref[...] = uu.astype(BF16)
        y_ref[...] = y.astype(BF16)

    tile = pl.BlockSpec((tm, d), lambda i: (i, 0))
    wide = pl.BlockSpec((tm, f), lambda i: (i, 0))
    extra_in = [] if loss is None else [tile, _row(1, d)]
    extra_out = [] if loss is None else [_row(8, d)]
    return _call(kernel_body, name=name, grid=(s // tm,),
        in_specs=[tile, _row(8, d), _row(1, d)] + [_weight_spec(fs, d, idx) for _, idx in weights] + extra_in,
        out_specs=[tile, wide, wide, tile, tile] + extra_out,
        out_shape=[jax.ShapeDtypeStruct((s, d), BF16), jax.ShapeDtypeStruct((s, f), BF16),
                   jax.ShapeDtypeStruct((s, f), BF16), jax.ShapeDtypeStruct((s, d), BF16),
                   jax.ShapeDtypeStruct((s, d), F32)] + [jax.ShapeDtypeStruct((8, d), F32)] * len(extra_out),
        args=(x, mod, gffn) + tuple(a for a, _ in weights) + (() if loss is None else tuple(loss)), ride=ride)


def _loss_tile(x, target, g, st_ref, i, last):
    d = x.shape[1]

    @pl.when(i == 0)
    def _():
        st_ref[...] = jnp.zeros_like(st_ref)

    n, rinv = _rms(x)
    err = n * g - target
    dy = err * (1.0 / d)
    st_ref[0:1, :] += _sum0(dy * n)
    st_ref[1:2, :] += _sum0(err * err) * (0.5 / d)

    @pl.when(i == last)
    def _():
        st_ref[2:3, :] = jnp.zeros((1, d), F32) + jnp.sum(st_ref[1:2, :], axis=-1, keepdims=True)

    dn = dy * g
    return rinv * (dn - n * jnp.mean(dn * n, axis=-1, keepdims=True))


def _pool_fwd(x, mod, gmix, pw, ls, tm, ride):
    s, d = x.shape
    dg = d // len(POOL_WINDOWS)

    def body(x_ref, halo_ref, mod_ref, g_ref, pw_ref, ls_ref, mixed_ref, yp_ref, xo_ref, buf):
        i = pl.program_id(0)

        def hfun(xv):
            n, _ = _rms(xv)
            return (n * g_ref[...]) * (1.0 + mod_ref[1:2, :]) + mod_ref[0:1, :]

        xv = x_ref[...]
        h = hfun(xv)
        buf[pl.ds(0, POOL_HALO), :] = jnp.where(i > 0, hfun(halo_ref[...]), 0.0)
        buf[pl.ds(POOL_HALO, tm), :] = h
        t = i * tm + lax.broadcasted_iota(jnp.int32, (tm, 1), 0)
        gate = 1.0 + mod_ref[2:3, :]
        for gi, w in enumerate(POOL_WINDOWS):
            cols = pl.ds(gi * dg, dg)
            ws = buf[pl.ds(POOL_HALO, tm), cols]
            for j in range(1, w):
                ws = ws + buf[pl.ds(POOL_HALO - j, tm), cols]
            inv = 1.0 / jnp.minimum(t + 1, w).astype(F32)
            mb = (ws * inv - h[:, gi * dg:(gi + 1) * dg]).astype(BF16)
            mixed_ref[:, cols] = mb
            yp = _nn(mb, pw_ref[gi])
            yp_ref[:, cols] = yp.astype(BF16)
            xo_ref[:, cols] = xv[:, gi * dg:(gi + 1) * dg] + gate[:, gi * dg:(gi + 1) * dg] * (yp * ls_ref[:, cols])

    tile = pl.BlockSpec((tm, d), lambda i: (i, 0))
    halo = pl.BlockSpec((POOL_HALO, d), lambda i: (jnp.maximum(i * (tm // POOL_HALO) - 1, 0), 0))
    return _call(body, name="pool_fwd", grid=(s // tm,),
        in_specs=[tile, halo, _row(8, d), _row(1, d), pl.BlockSpec((len(POOL_WINDOWS), dg, dg), lambda i: (0, 0, 0)),
                  _row(1, d)],
        out_specs=[tile, tile, tile],
        out_shape=[jax.ShapeDtypeStruct((s, d), BF16), jax.ShapeDtypeStruct((s, d), BF16),
                   jax.ShapeDtypeStruct((s, d), F32)],
        scratch_shapes=[pltpu.VMEM((POOL_HALO + tm, d), F32)],
        args=(x, x, mod, gmix, pw, ls), ride=ride)


def _ffn_bwd(dxo, x, gg, uu, y, mod, gffn, weights, fs, f, tm, name, ride=None):
    s, d = x.shape

    def kernel_body(dxo_ref, x_ref, gg_ref, uu_ref, y_ref, mod_ref, g_ref, wg_ref, wu_ref, wd_ref,
                    dg_ref, du_ref, a_ref, dy_ref, dxi_ref, st_ref):
        @pl.when(pl.program_id(0) == 0)
        def _():
            st_ref[...] = jnp.zeros_like(st_ref)

        dxo_v = dxo_ref[...]
        dyb = (dxo_v * (1.0 + mod_ref[5:6, :])).astype(BF16)
        dy_ref[...] = dyb
        da = _nt(dyb, wd_ref[...].reshape(f, d))
        ggv, uuv = gg_ref[...].astype(F32), uu_ref[...].astype(F32)
        sg = jax.nn.sigmoid(ggv)
        silu = ggv * sg
        a_ref[...] = (silu * uuv).astype(BF16)
        dub = (da * silu).astype(BF16)
        dgb = (da * uuv * _silu_grad(ggv, sg)).astype(BF16)
        du_ref[...] = dub
        dg_ref[...] = dgb
        dh = _nn(dgb, wg_ref[...].reshape(f, d)) + _nn(dub, wu_ref[...].reshape(f, d))
        n, rinv = _rms(x_ref[...])
        dx, dsh, dsc, dgain = _rms_mod_bwd(dh, n, rinv, g_ref[...], mod_ref[4:5, :])
        dxi_ref[...] = dxo_v + dx
        st_ref[0:1, :] += dsh
        st_ref[1:2, :] += dsc
        st_ref[2:3, :] += _sum0(dxo_v * y_ref[...].astype(F32))
        st_ref[3:4, :] += dgain

    tile = pl.BlockSpec((tm, d), lambda i: (i, 0))
    wide = pl.BlockSpec((tm, f), lambda i: (i, 0))
    return _call(kernel_body, name=name, grid=(s // tm,),
        in_specs=[tile, tile, wide, wide, tile, _row(8, d), _row(1, d)] + [_weight_spec(fs, d, idx) for _, idx in weights],
        out_specs=[wide, wide, wide, tile, tile, _row(8, d)],
        out_shape=[jax.ShapeDtypeStruct((s, f), BF16)] * 3 + [jax.ShapeDtypeStruct((s, d), BF16),
                   jax.ShapeDtypeStruct((s, d), F32), jax.ShapeDtypeStruct((8, d), F32)],
        args=(dxo, x, gg, uu, y, mod, gffn) + tuple(a for a, _ in weights), ride=ride)


def _ffn_wgrad(dgb, dub, ab, h, dyb, fb, ts, name):
    s, f = dgb.shape
    d = h.shape[1]
    last = s // ts - 1

    def body(dg_ref, du_ref, a_ref, h_ref, dy_ref, o_ref, acc):
        t = pl.program_id(1)

        @pl.when(t == 0)
        def _():
            acc[...] = jnp.zeros_like(acc)

        hv = h_ref[...]
        acc[0] += _tn(dg_ref[...], hv)
        acc[1] += _tn(du_ref[...], hv)
        acc[2] += _tn(a_ref[...], dy_ref[...])

        @pl.when(t == last)
        def _():
            o_ref[...] = acc[...].astype(BF16)

    wide = pl.BlockSpec((ts, fb), lambda j, t: (t, j))
    tile = pl.BlockSpec((ts, d), lambda j, t: (t, 0))
    return pl.pallas_call(body, name=name, grid=(f // fb, s // ts),
        in_specs=[wide, wide, wide, tile, tile],
        out_specs=pl.BlockSpec((3, fb, d), lambda j, t: (0, j, 0)),
        out_shape=jax.ShapeDtypeStruct((3, f, d), BF16),
        scratch_shapes=[pltpu.VMEM((3, fb, d), F32)],
        compiler_params=_params("arbitrary", "arbitrary"),
    )(dgb, dub, ab, h, dyb)


def _pool_bwd(dxo, x, yp, mixed, mod, gmix, pw, ls, tm):
    s, d = x.shape
    ng = len(POOL_WINDOWS)
    dg = d // ng
    last = s // tm - 1

    def body(dxo_ref, dxh_ref, x_ref, yp_ref, mixed_ref, mod_ref, g_ref, pw_ref, ls_ref,
             dxi_ref, dpw_ref, st_ref, bufy, bufq, bufh, acc):
        i = pl.program_id(0)

        @pl.when(i == 0)
        def _():
            st_ref[...] = jnp.zeros_like(st_ref)
            acc[...] = jnp.zeros_like(acc)

        gate = 1.0 + mod_ref[2:3, :]
        lsv = ls_ref[...]
        dxo_v = dxo_ref[...]
        st_ref[2:3, :] += _sum0(dxo_v * yp_ref[...].astype(F32))
        bufy[pl.ds(0, tm), :] = (dxo_v * (gate * lsv)).astype(BF16)
        bufy[pl.ds(tm, POOL_HALO), :] = jnp.where(i < last, dxh_ref[...] * (gate * lsv), 0.0).astype(BF16)
        t = i * tm + lax.broadcasted_iota(jnp.int32, (tm + POOL_HALO, 1), 0)
        for gi, w in enumerate(POOL_WINDOWS):
            cols = pl.ds(gi * dg, dg)
            dm = _nt(bufy[:, cols], pw_ref[gi])
            bufq[:, cols] = dm * (1.0 / jnp.minimum(t + 1, w).astype(F32))
            dh = bufq[pl.ds(0, tm), cols] - dm[0:tm, :]
            for j in range(1, w):
                dh = dh + bufq[pl.ds(j, tm), cols]
            bufh[:, cols] = dh
            acc[gi] += _tn(mixed_ref[:, cols], bufy[pl.ds(0, tm), cols])
        n, rinv = _rms(x_ref[...])
        dx, dsh, dsc, dgain = _rms_mod_bwd(bufh[...], n, rinv, g_ref[...], mod_ref[1:2, :])
        dxi_ref[...] = dxo_v + dx
        st_ref[0:1, :] += dsh
        st_ref[1:2, :] += dsc
        st_ref[3:4, :] += dgain

        @pl.when(i == last)
        def _():
            r = st_ref[2:3, :]
            st_ref[4:5, :] = r * lsv
            st_ref[5:6, :] = r * gate
            dpw_ref[...] = acc[...].astype(BF16)

    tile = pl.BlockSpec((tm, d), lambda i: (i, 0))
    nxt = pl.BlockSpec((POOL_HALO, d), lambda i: (jnp.minimum((i + 1) * (tm // POOL_HALO), s // POOL_HALO - 1), 0))
    pws = pl.BlockSpec((ng, dg, dg), lambda i: (0, 0, 0))
    return _call(body, name="pool_bwd", grid=(s // tm,),
        in_specs=[tile, nxt, tile, tile, tile, _row(8, d), _row(1, d), pws, _row(1, d)],
        out_specs=[tile, pws, _row(8, d)],
        out_shape=[jax.ShapeDtypeStruct((s, d), F32), jax.ShapeDtypeStruct((ng, dg, dg), BF16),
                   jax.ShapeDtypeStruct((8, d), F32)],
        scratch_shapes=[pltpu.VMEM((tm + POOL_HALO, d), BF16), pltpu.VMEM((tm + POOL_HALO, d), F32),
                        pltpu.VMEM((tm, d), F32), pltpu.VMEM((ng, dg, dg), F32)],
        args=(dxo, dxo, x, yp, mixed, mod, gmix, pw, ls))[0]


def _conv_bwd_mid(dxo, y, dwc, sb, mod, ln_g, ln_b, gw, w2_rows, w2_idx, tm):
    s, d = dwc.shape
    last = s // tm - 1

    def body(dxo_ref, y_ref, dwc_ref, s_ref, mod_ref, lng_ref, lnb_ref, w_ref, dd_ref, dw_ref, st_ref, acc):
        i = pl.program_id(0)

        @pl.when(i == 0)
        def _():
            st_ref[...] = jnp.zeros_like(st_ref)
            acc[...] = jnp.zeros_like(acc)

        dxo_v = dxo_ref[...]
        st_ref[0:1, :] += _sum0(dxo_v * y_ref[...].astype(F32))
        dy = dxo_v * (1.0 + mod_ref[2:3, :])
        st_ref[1:2, :] += _sum0(dy)
        dyb = dy.astype(BF16)
        ds = _nt(dyb, w_ref[...].reshape(NDEV * w2_rows, d))
        acc[...] += _tn(s_ref[...], dyb)
        v = dwc_ref[...]
        mu = jnp.mean(v, axis=-1, keepdims=True)
        xc = v - mu
        rstd = lax.rsqrt(jnp.mean(xc * xc, axis=-1, keepdims=True) + EPS)
        xhat = xc * rstd
        ln = xhat * lng_ref[...] + lnb_ref[...]
        dln = ds * _silu_grad(ln, jax.nn.sigmoid(ln))
        st_ref[2:3, :] += _sum0(dln * xhat)
        st_ref[3:4, :] += _sum0(dln)
        dxh = dln * lng_ref[...]
        dd = rstd * (dxh - jnp.mean(dxh, axis=-1, keepdims=True) - xhat * jnp.mean(dxh * xhat, axis=-1, keepdims=True))
        dd_ref[...] = dd
        st_ref[4:5, :] += _sum0(dd)

        @pl.when(i == last)
        def _():
            dw_ref[...] = acc[...].astype(BF16)

    tile = pl.BlockSpec((tm, d), lambda i: (i, 0))
    return _call(body, name="conv_bwd_mid", grid=(s // tm,),
        in_specs=[tile, tile, tile, tile, _row(8, d), _row(1, d), _row(1, d), _weight_spec(w2_rows, d, w2_idx)],
        out_specs=[tile, pl.BlockSpec((d, d), lambda i: (0, 0)), _row(8, d)],
        out_shape=[jax.ShapeDtypeStruct((s, d), F32), jax.ShapeDtypeStruct((d, d), BF16),
                   jax.ShapeDtypeStruct((8, d), F32)],
        scratch_shapes=[pltpu.VMEM((d, d), F32)],
        args=(dxo, y, dwc, sb, mod, ln_g, ln_b, gw))[0]


def _conv_bwd_in(dd, glu, u, hb, x, dxo, wdw, gw, w1_rows, w1_idx, mod, gmix, tm, nsub, ride):
    s, d = x.shape
    off = CONV_HALO - (CONV_WIDTH - 1)
    rw = 32
    tap_group = 16
    tb = nsub * tm
    last = s // tb - 1

    def kernel_body(dd_ref, ddn_ref, glu_ref, glp_ref, u_ref, h_ref, x_ref, dxo_ref, wdw_ref, w_ref, mod_ref, g_ref,
                    dxi_ref, dw_ref, dwdw_ref, st_ref, bufd, bufg, shd, shg, dgl, accw, acc):
        i = pl.program_id(0)

        @pl.when(i == 0)
        def _():
            st_ref[...] = jnp.zeros_like(st_ref)
            accw[...] = jnp.zeros_like(accw)
            acc[...] = jnp.zeros_like(acc)

        w1t = w_ref[...].reshape(NDEV * w1_rows, d)
        for sub in range(nsub):
            base = sub * tm
            tile_rows = pl.ds(base, tm)
            bufd[pl.ds(0, tm), :] = dd_ref[tile_rows, :]
            if sub == nsub - 1:
                bufd[pl.ds(tm, CONV_HALO), :] = jnp.where(i < last, ddn_ref[...], 0.0)
            else:
                bufd[pl.ds(tm, CONV_HALO), :] = dd_ref[pl.ds(base + tm, CONV_HALO), :]
            if sub == 0:
                bufg[pl.ds(0, CONV_HALO), :] = jnp.where(i > 0, glp_ref[...].astype(F32), 0.0)
            else:
                bufg[pl.ds(0, CONV_HALO), :] = glu_ref[pl.ds(base - CONV_HALO, CONV_HALO), :].astype(F32)
            bufg[pl.ds(CONV_HALO, tm), :] = glu_ref[tile_rows, :].astype(F32)
            _shifted_copies(bufd, shd, tm + CONV_HALO)
            _shifted_copies(bufg, shg, CONV_HALO + tm)
            for cb in range(d // LANES):
                cols = pl.ds(cb * LANES, LANES)
                taps = wdw_ref[:, cols]
                for r in range(tm // rw):
                    part = jnp.zeros((rw, LANES), F32)
                    for k in range(CONV_WIDTH):
                        part = part + _window(shd, r * rw + CONV_WIDTH - 1 - k, rw, cols) * taps[k:k + 1, :]
                    dgl[sub, pl.ds(r * rw, rw), cols] = part
                for k0 in range(0, CONV_WIDTH, tap_group):
                    group = range(k0, min(CONV_WIDTH, k0 + tap_group))
                    sums = {k: jnp.zeros((8, LANES), F32) for k in group}
                    for r in range(tm // rw):
                        ddc = bufd[pl.ds(r * rw, rw), cols]
                        for k in group:
                            p = _window(shg, r * rw + off + k, rw, cols) * ddc
                            for q in range(rw // 8):
                                sums[k] = sums[k] + p[q * 8:(q + 1) * 8, :]
                    for k in group:
                        accw[k, :, cols] += sums[k]
            dglu = dgl[sub]
            uv = u_ref[tile_rows, :].astype(F32)
            a, g = uv[:, :d], uv[:, d:]
            sg = jax.nn.sigmoid(g)
            du = jnp.concatenate([dglu * sg, dglu * a * (sg * (1.0 - sg))], axis=1)
            st_ref[0:1, :] += _sum0(du)
            dub = du.astype(BF16)
            dh = _nn(dub, w1t)
            acc[...] += _tn(dub, h_ref[tile_rows, :])
            n, rinv = _rms(x_ref[tile_rows, :])
            dx, dsh, dsc, dgain = _rms_mod_bwd(dh, n, rinv, g_ref[...], mod_ref[1:2, :])
            dxi_ref[tile_rows, :] = dxo_ref[tile_rows, :] + dx
            st_ref[1:2, 0:d] += dsh
            st_ref[2:3, 0:d] += dsc
            st_ref[3:4, 0:d] += dgain

        @pl.when(i == last)
        def _():
            dw_ref[...] = acc[...].astype(BF16)
            dwdw_ref[...] = jnp.sum(accw[...], axis=1)

    tile = pl.BlockSpec((tb, d), lambda i: (i, 0))
    prv = pl.BlockSpec((CONV_HALO, d), lambda i: (jnp.maximum(i * (tb // CONV_HALO) - 1, 0), 0))
    nxt = pl.BlockSpec((CONV_HALO, d), lambda i: (jnp.minimum((i + 1) * (tb // CONV_HALO), s // CONV_HALO - 1), 0))
    return _call(kernel_body, name="conv_bwd_in", grid=(s // tb,),
        in_specs=[tile, nxt, tile, prv, pl.BlockSpec((tb, 2 * d), lambda i: (i, 0)), tile, tile, tile,
                  _row(CONV_WIDTH, d), _weight_spec(w1_rows, d, w1_idx), _row(8, d), _row(1, d)],
        out_specs=[tile, pl.BlockSpec((2 * d, d), lambda i: (0, 0)), _row(CONV_HALO, d), _row(8, 2 * d)],
        out_shape=[jax.ShapeDtypeStruct((s, d), F32), jax.ShapeDtypeStruct((2 * d, d), BF16),
                   jax.ShapeDtypeStruct((CONV_HALO, d), F32), jax.ShapeDtypeStruct((8, 2 * d), F32)],
        scratch_shapes=[pltpu.VMEM((tm + CONV_HALO, d), F32), pltpu.VMEM((CONV_HALO + tm, d), F32),
                        pltpu.VMEM((8, tm + CONV_HALO, d), F32), pltpu.VMEM((8, CONV_HALO + tm, d), F32),
                        pltpu.VMEM((nsub, tm, d), F32), pltpu.VMEM((CONV_HALO, 8, d), F32),
                        pltpu.VMEM((2 * d, d), F32)],
        args=(dd, dd, glu, glu, u, hb, x, dxo, wdw, gw, mod, gmix), ride=ride)


def kernel(x, c, ada_w, ada_b, norm_mix_g, norm_ffn_g, conv_w1, conv_b1, conv_wdw, conv_bdw, conv_ln_g, conv_ln_b, conv_w2, conv_b2, pool_w, pool_ls, ffn_w_gate, ffn_w_up, ffn_w_down, final_g, loss_target, m_ada_w, m_ada_b, m_norm_mix_g, m_norm_ffn_g, m_conv_w1, m_conv_b1, m_conv_wdw, m_conv_bdw, m_conv_ln_g, m_conv_ln_b, m_conv_w2, m_conv_b2, m_pool_w, m_pool_ls, m_ffn_w_gate, m_ffn_w_up, m_ffn_w_down, m_final_g, v_ada_w, v_ada_b, v_norm_mix_g, v_norm_ffn_g, v_conv_w1, v_conv_b1, v_conv_wdw, v_conv_bdw, v_conv_ln_g, v_conv_ln_b, v_conv_w2, v_conv_b2, v_pool_w, v_pool_ls, v_ffn_w_gate, v_ffn_w_up, v_ffn_w_down, v_final_g):
    _, s, d = x.shape
    f = ffn_w_down.shape[1] * NDEV
    fs = f // NDEV
    r1, r2 = 2 * d // NDEV, d // NDEV
    ng = len(POOL_WINDOWS)
    dg = d // ng
    pr = ng * (dg // NDEV) * dg // d
    ncol = ada_w.shape[2]
    dc = d // NDEV
    tm = min(256, s)
    me = _my_index()
    x0 = x.reshape(s, d)
    target = loss_target.reshape(s, d)

    small = jnp.concatenate([c.reshape(NDEV, dc), conv_wdw[0], pool_ls], axis=0)
    shard_a = conv_w1[0].T.astype(BF16)
    bias = lax.dynamic_slice_in_dim(ada_b, me * ncol, ncol, axis=1)[:, None, :]
    small_all, mod_all, (gwa,) = _prologue(small, ada_w, bias, _Ride("gather", [shard_a]))
    c_all = small_all[:, 0:NDEV, :].reshape(NDEV, d)
    wdw = small_all[:, NDEV:NDEV + CONV_WIDTH, :].transpose(1, 0, 2).reshape(CONV_WIDTH, d)
    ls = small_all[:, NDEV + CONV_WIDTH, :].reshape(1, d)
    mod_mine = lax.dynamic_index_in_dim(mod_all.reshape(NDEV, 2, NDEV, ncol), me, axis=2, keepdims=False)
    mod = mod_mine.transpose(1, 0, 2).reshape(2, 6, d)
    mod = jnp.concatenate([mod, jnp.zeros((2, 2, d), F32)], axis=1)

    shard_b1 = jnp.concatenate([ffn_w_gate[0].T, conv_w2[0]], axis=0).astype(BF16)
    shard_b2 = jnp.concatenate([ffn_w_up[0].T, ffn_w_down[0]], axis=0).astype(BF16)
    shard_c = jnp.concatenate([ffn_w_gate[1].T, ffn_w_up[1].T, ffn_w_down[1], pool_w.reshape(pr, d)], axis=0).astype(BF16)
    w1_at = (gwa, 0)

    (h0, u, glu), (gwb1,) = _conv_in(x0, mod[0], norm_mix_g[0:1], w1_at[0], r1, w1_at[1], conv_b1, tm,
                                    _Ride("gather", [shard_b1]))
    w2_at = (gwb1[:, fs:fs + r2, :], 0)
    (dwc, sb, y0, x1), (gwb2,) = _conv_mid(glu, wdw, conv_bdw, conv_ln_g, conv_ln_b, w2_at[0], r2, w2_at[1], conv_b2,
                                           x0, mod[0], tm, _Ride("gather", [shard_b2]))
    ffn0_w = [(gwb1, 0), (gwb2, 0), (gwb2, 1)]
    (h1, gg0, uu0, yf0, x2), (gwc,) = _ffn_fwd(x1, mod[0], norm_ffn_g[0:1], ffn0_w, fs, f, tm, "ffn_fwd0",
                                               _Ride("gather", [shard_c]))
    pw = gwc[:, 3 * fs:3 * fs + pr, :].reshape(NDEV, ng, dg // NDEV, dg).transpose(1, 0, 2, 3).reshape(ng, dg, dg)
    (mixed, yp, x3), _ = _pool_fwd(x2, mod[1], norm_mix_g[1:2], pw, ls, tm, None)
    ffn1_w = [(gwc, 0), (gwc, 1), (gwc, 2)]
    (h3, gg1, uu1, yf1, dx4, st_loss), _ = _ffn_fwd(x3, mod[1], norm_ffn_g[1:2], ffn1_w, fs, f, tm, "ffn_fwd1",
                                                    loss=(target, final_g.reshape(1, d)))

    fb = f // 2 if (f // 2) % 128 == 0 else f
    ts = min(512, s)
    (dgb, dub, ab, dyb, dx3, st_f1), _ = _ffn_bwd(dx4, x3, gg1, uu1, yf1, mod[1], norm_ffn_g[1:2], ffn1_w, fs, f, tm,
                                                  "ffn_bwd1")
    gf1 = _ffn_wgrad(dgb, dub, ab, h3, dyb, fb, ts, "ffn_wgrad1")
    dx2, gpw, st_p = _pool_bwd(dx3, x2, yp, mixed, mod[1], norm_mix_g[1:2], pw, ls, tm)
    (dgb, dub, ab, dyb, dx1, st_f0), (land_f1,) = _ffn_bwd(dx2, x1, gg0, uu0, yf0, mod[0], norm_ffn_g[0:1], ffn0_w, fs, f,
                                                           tm, "ffn_bwd0", _Ride("scatter", [gf1]))
    gf0 = _ffn_wgrad(dgb, dub, ab, h1, dyb, fb, ts, "ffn_wgrad0")
    dd, gw2, st_m = _conv_bwd_mid(dx1, y0, dwc, sb, mod[0], conv_ln_g, conv_ln_b, w2_at[0], r2, w2_at[1], tm)
    (dx0, gw1, gwdw, st_c), (land_f0, land_pw, land_w2) = _conv_bwd_in(
        dd, glu, u, h0, x0, dx1, wdw, w1_at[0], r1, w1_at[1], mod[0], norm_mix_g[0:1], tm, 1,
        _Ride("scatter", [gf0, gpw, gw2[None]]))

    prow = _pack_stats(st_c, st_m, st_f0, st_p, st_f1, st_loss, gwdw)
    p_all, (land_w1,) = _small_exchange(prow, _Ride("scatter", [gw1[None]]), "allgather_stats")
    psum = _sum_slots(p_all, "sum_stats")
    loss = psum[prow.shape[0] - 1, 0]

    tr = lambda a: jnp.swapaxes(a, 1, 2)
    ffn_out = _finalize_ffn(land_f0, land_f1, tr(ffn_w_gate), tr(ffn_w_up), ffn_w_down,
                            tr(m_ffn_w_gate), tr(m_ffn_w_up), m_ffn_w_down, tr(v_ffn_w_gate), tr(v_ffn_w_up), v_ffn_w_down)
    fin_w1 = _finalize(land_w1.reshape(NDEV, r1, d), conv_w1[0], m_conv_w1[0], v_conv_w1[0], True, "finalize_w1")
    fin_w2 = _finalize(land_w2.reshape(NDEV, r2, d), conv_w2[0], m_conv_w2[0], v_conv_w2[0], False, "finalize_w2")
    pshape = (ng * (dg // NDEV), dg)
    fin_pw = _finalize(land_pw.reshape((NDEV,) + pshape), pool_w.reshape(pshape), m_pool_w.reshape(pshape),
                       v_pool_w.reshape(pshape), False, "finalize_pool_w")
    dmod_all = p_all[:, 0:12, :].reshape(NDEV, 2, 6 * d)
    dmod_cols = lax.dynamic_slice_in_dim(dmod_all, me * ncol, ncol, axis=2).transpose(1, 0, 2)
    fin_ada = _ada_update(c_all, dmod_cols, ada_w, m_ada_w, v_ada_w)

    rep_names = ["ada_b", "norm_mix_g", "norm_ffn_g", "conv_b1", "conv_bdw", "conv_ln_g", "conv_ln_b", "conv_b2", "final_g"]
    rep_w = [ada_b, norm_mix_g, norm_ffn_g, conv_b1, conv_bdw, conv_ln_g, conv_ln_b, conv_b2, final_g]
    rep_m = [m_ada_b, m_norm_mix_g, m_norm_ffn_g, m_conv_b1, m_conv_bdw, m_conv_ln_g, m_conv_ln_b, m_conv_b2, m_final_g]
    rep_v = [v_ada_b, v_norm_mix_g, v_norm_ffn_g, v_conv_b1, v_conv_bdw, v_conv_ln_g, v_conv_ln_b, v_conv_b2, v_final_g]
    nrep = sum(w.size for w in rep_w) // d
    pad = jnp.zeros(((-nrep) % 8, d), F32)

    def pack(arrs, fill):
        return jnp.concatenate([a.reshape(-1, d) for a in arrs] + [pad + fill], axis=0)

    rep_g = jnp.concatenate([psum[0:nrep], pad], axis=0)
    rep_d, rep_mo, rep_vo = _adamw(pack(rep_w, 0.0), rep_g, pack(rep_m, 0.0), pack(rep_v, 1.0), "adamw_replicated")

    def unpack(packed):
        out, cur = [], 0
        for w in rep_w:
            k = w.size // d
            out.append(packed[cur:cur + k].reshape(w.shape))
            cur += k
        return out

    rep = dict(zip(rep_names, zip(unpack(psum), unpack(rep_d), unpack(rep_mo), unpack(rep_vo))))

    g_wdw_full = psum[nrep:nrep + CONV_WIDTH]
    g_wdw = lax.dynamic_slice_in_dim(g_wdw_full, me * dc, dc, axis=1)
    g_ls = lax.dynamic_slice_in_dim(psum[nrep + CONV_WIDTH:nrep + CONV_WIDTH + 1], me * dc, dc, axis=1)
    tiny = lambda a, b: jnp.concatenate([a.reshape(CONV_WIDTH, dc), b.reshape(1, dc)], axis=0)
    t_d, t_m, t_v = _adamw(tiny(conv_wdw, pool_ls), tiny(g_wdw, g_ls), tiny(m_conv_wdw, m_pool_ls),
                           tiny(v_conv_wdw, v_pool_ls), "adamw_taps")

    def taps(a):
        return a[0:CONV_WIDTH][None], a[CONV_WIDTH:CONV_WIDTH + 1]

    sharded = {
        "ada_w": tuple(fin_ada),
        "conv_w1": tuple(a[None] for a in fin_w1),
        "conv_w2": tuple(a[None] for a in fin_w2),
        "pool_w": tuple(a.reshape(pool_w.shape) for a in fin_pw),
        "ffn_w_gate": tuple(tr(a) for a in ffn_out[0::3]),
        "ffn_w_up": tuple(tr(a) for a in ffn_out[1::3]),
        "ffn_w_down": tuple(ffn_out[2::3]),
        "conv_wdw": (g_wdw[None], taps(t_d)[0], taps(t_m)[0], taps(t_v)[0]),
        "pool_ls": (g_ls, taps(t_d)[1], taps(t_m)[1], taps(t_v)[1]),
    }
    every = {**rep, **sharded}
    order = ["ada_w", "ada_b", "norm_mix_g", "norm_ffn_g", "conv_w1", "conv_b1", "conv_wdw", "conv_bdw", "conv_ln_g",
             "conv_ln_b", "conv_w2", "conv_b2", "pool_w", "pool_ls", "ffn_w_gate", "ffn_w_up", "ffn_w_down", "final_g"]
    grads = [every[n][0] for n in order]
    deltas = [every[n][1] for n in order]
    new_m = [every[n][2] for n in order]
    new_v = [every[n][3] for n in order]
    return (loss, dx0.reshape(1, s, d), *grads, *deltas, *new_m, *new_v)
```

```python
import jax
import jax.numpy as jnp
from jax import lax
from jax.experimental import pallas as pl
from jax.experimental.pallas import tpu as pltpu

NDEV = 8
EPS = 1e-6
CONV_WIDTH = 31
POOL_WINDOWS = (2, 4, 8, 16)
CONV_HALO = 32
POOL_HALO = 16
ADAM_LR = 0.001
ADAM_B1 = 0.9
ADAM_B2 = 0.999
ADAM_EPS = 1e-08
ADAM_WD = 0.01
ADAM_STEP = 10
VMEM_LIMIT = 56 * 2**20
MESH = pl.DeviceIdType.MESH
F32 = jnp.float32
BF16 = jnp.bfloat16


def _nt(a, b):
    return lax.dot_general(a, b, (((1,), (1,)), ((), ())), preferred_element_type=F32)


def _nn(a, b):
    return lax.dot_general(a, b, (((1,), (0,)), ((), ())), preferred_element_type=F32)


def _tn(a, b):
    return lax.dot_general(a, b, (((0,), (0,)), ((), ())), preferred_element_type=F32)


def _sum0(v):
    return jnp.sum(v, axis=0, keepdims=True)


def _rms(x):
    rinv = lax.rsqrt(jnp.mean(x * x, axis=-1, keepdims=True) + EPS)
    return x * rinv, rinv


def _rms_mod_bwd(dh, n, rinv, g, sc):
    dhs = dh * (1.0 + sc)
    dn = dhs * g
    dx = rinv * (dn - n * jnp.mean(dn * n, axis=-1, keepdims=True))
    return dx, _sum0(dh), _sum0(dh * (n * g)), _sum0(dhs * n)


def _silu_grad(z, sg):
    return sg * (1.0 + z * (1.0 - sg))


def _params(*sem):
    return pltpu.CompilerParams(dimension_semantics=sem, vmem_limit_bytes=VMEM_LIMIT)


def _row(i, d):
    return pl.BlockSpec((i, d), lambda *_: (0, 0))


def _weight_spec(rows, d, idx):
    return pl.BlockSpec((NDEV, rows, d), lambda *_: (0, idx, 0), pipeline_mode=pl.Buffered(1))


def _my_index():
    return 4 * lax.axis_index("x") + 2 * lax.axis_index("y") + lax.axis_index("c")


def _peer(k):
    x, y, c = lax.axis_index("x"), lax.axis_index("y"), lax.axis_index("c")
    px = 1 - x if k & 4 else x
    py = 1 - y if k & 2 else y
    pc = 1 - c if k & 1 else c
    return (px, py, pc), 4 * px + 2 * py + pc


def _gather_sems():
    return [pltpu.SemaphoreType.DMA((NDEV - 1,)), pltpu.SemaphoreType.DMA((NDEV - 1,)), pltpu.SemaphoreType.DMA((1,))]


class _Gather:
    def __init__(self, srcs, dsts, sems):
        self.src, self.dst = srcs[0], dsts[0]
        self.send, self.recv, self.local = sems
        x, y, c = lax.axis_index("x"), lax.axis_index("y"), lax.axis_index("c")
        self.me, self.sibling, self.core = (x, y, c), (x, y, 1 - c), c
        self.chips = [(1 - x, y), (x, 1 - y), (1 - x, 1 - y)]

    def _copy(self, k, block, to, from_input=False):
        slot = self.dst.at[4 * block[0] + 2 * block[1] + block[2]]
        return pltpu.make_async_remote_copy(
            src_ref=self.src if from_input else slot, dst_ref=slot, send_sem=self.send.at[k], recv_sem=self.recv.at[k],
            device_id=to, device_id_type=MESH)

    def _own(self):
        return pltpu.make_async_copy(self.src, self.dst.at[_my_index()], self.local.at[0])

    def _first(self):
        return [self._copy(0, self.me, self.sibling, True)] + [
            self._copy(1 + j, self.me, (*chip, self.core), True) for j, chip in enumerate(self.chips)]

    def start(self):
        self._own().start()
        for cp in self._first():
            cp.start()

    def forward(self):
        for j, chip in enumerate(self.chips):
            self._copy(1 + j, (*chip, self.core), self.me).wait_recv()
            self._copy(4 + j, (*chip, self.core), self.sibling).start()

    def finish(self):
        self._copy(0, self.sibling, self.me).wait_recv()
        for j, chip in enumerate(self.chips):
            self._copy(4 + j, (*chip, 1 - self.core), self.me).wait_recv()
        for cp in self._first():
            cp.wait_send()
        for j, chip in enumerate(self.chips):
            self._copy(4 + j, (*chip, self.core), self.sibling).wait_send()
        self._own().wait()


def _scatter_sems(n):
    return [pltpu.SemaphoreType.DMA((7 * n,)), pltpu.SemaphoreType.DMA((7 * n,)), pltpu.SemaphoreType.DMA((n,))]


class _Scatter:
    def __init__(self, srcs, dsts, sems):
        send_sems, recv_sems, local_sems = sems
        me = _my_index()
        self.copies = []
        for a, (src, dst) in enumerate(zip(srcs, dsts)):
            r = dst.shape[2]
            self.copies.append(pltpu.make_async_copy(src.at[:, pl.ds(me * r, r), :], dst.at[me], local_sems.at[a]))
            for k in range(1, NDEV):
                dev, p = _peer(k)
                self.copies.append(pltpu.make_async_remote_copy(
                    src_ref=src.at[:, pl.ds(p * r, r), :], dst_ref=dst.at[me],
                    send_sem=send_sems.at[a * 7 + k - 1], recv_sem=recv_sems.at[a * 7 + k - 1],
                    device_id=dev, device_id_type=MESH))

    def start(self):
        for cp in self.copies:
            cp.start()

    def forward(self):
        pass

    def finish(self):
        for cp in self.copies:
            cp.wait()


def _land_shape(part):
    a, r, c = part.shape
    return jax.ShapeDtypeStruct((NDEV, a, r // NDEV, c), part.dtype)


ANY = pl.BlockSpec(memory_space=pl.ANY)


class _Ride:
    def __init__(self, kind, srcs):
        self.kind, self.srcs = kind, list(srcs)
        if kind == "gather":
            self.out_shape = [jax.ShapeDtypeStruct((NDEV,) + a.shape, a.dtype) for a in self.srcs]
            self.sems = _gather_sems()
        else:
            self.out_shape = [_land_shape(a) for a in self.srcs]
            self.sems = _scatter_sems(len(self.srcs))

    def exchange(self, ins, outs, sems):
        return (_Gather if self.kind == "gather" else _Scatter)(ins, outs, sems)


def _small_exchange(small, ride, name):
    r, c = small.shape
    nr = len(ride.srcs)

    def body(v_ref, *refs):
        rin, refs = refs[:nr], refs[nr:]
        out_ref, refs = refs[0], refs[1:]
        rout, refs = refs[:nr], refs[nr:]
        send_sems, recv_sems, rsems = refs[0], refs[1], refs[2:]
        big = ride.exchange(rin, rout, rsems)
        big.start()
        copies = _small_pushes(v_ref, out_ref, send_sems, recv_sems)
        for cp in copies:
            cp.start()
        for cp in copies:
            cp.wait()
        big.forward()
        big.finish()

    res = pl.pallas_call(body, name=name,
        out_shape=[jax.ShapeDtypeStruct((NDEV, r, c), small.dtype)] + ride.out_shape,
        in_specs=[pl.BlockSpec(memory_space=pltpu.VMEM)] + [ANY] * nr,
        out_specs=[pl.BlockSpec(memory_space=pltpu.VMEM)] + [ANY] * nr,
        scratch_shapes=[pltpu.SemaphoreType.DMA((NDEV - 1,)), pltpu.SemaphoreType.DMA((NDEV - 1,))] + ride.sems,
    )(small, *ride.srcs)
    return res[0], res[1:]


def _small_pushes(src_ref, out_ref, send_sems, recv_sems):
    me = _my_index()
    out_ref[me] = src_ref[...]
    copies = []
    for k in range(1, NDEV):
        dev, _ = _peer(k)
        copies.append(pltpu.make_async_remote_copy(
            src_ref=src_ref, dst_ref=out_ref.at[me], send_sem=send_sems.at[k - 1], recv_sem=recv_sems.at[k - 1],
            device_id=dev, device_id_type=MESH))
    return copies


def _prologue(small, ada_w, bias, ride):
    r, c = small.shape
    nl, d, ncol = ada_w.shape
    nr = len(ride.srcs)

    def body(v_ref, w_ref, b_ref, *refs):
        rin, refs = refs[:nr], refs[nr:]
        out_ref, mod_ref, refs = refs[0], refs[1], refs[2:]
        rout, refs = refs[:nr], refs[nr:]
        cols, send1, recv1, send2, recv2, rsems = refs[0], refs[1], refs[2], refs[3], refs[4], refs[5:]
        big = ride.exchange(rin, rout, rsems)
        big.start()
        first = _small_pushes(v_ref, out_ref, send1, recv1)
        for cp in first:
            cp.start()
        for cp in first:
            cp.wait()
        for layer in range(nl):
            acc = jnp.zeros((NDEV, ncol), F32) + b_ref[layer]
            for j in range(d // c):
                cj = out_ref[:, j, :]
                acc = acc + jnp.dot(cj * jax.nn.sigmoid(cj), w_ref[layer, pl.ds(j * c, c), :],
                                    preferred_element_type=F32, precision=lax.Precision.HIGHEST)
            cols[pl.ds(layer * NDEV, NDEV), :] = acc
        second = _small_pushes(cols, mod_ref, send2, recv2)
        for cp in second:
            cp.start()
        for cp in second:
            cp.wait()
        big.forward()
        big.finish()

    vmem = pl.BlockSpec(memory_space=pltpu.VMEM)
    sem = pltpu.SemaphoreType.DMA((NDEV - 1,))
    res = pl.pallas_call(body, name="prologue",
        out_shape=[jax.ShapeDtypeStruct((NDEV, r, c), F32), jax.ShapeDtypeStruct((NDEV, nl * NDEV, ncol), F32)] + ride.out_shape,
        in_specs=[vmem, vmem, vmem] + [ANY] * nr, out_specs=[vmem, vmem] + [ANY] * nr,
        scratch_shapes=[pltpu.VMEM((nl * NDEV, ncol), F32), sem, sem, sem, sem] + ride.sems,
        compiler_params=pltpu.CompilerParams(vmem_limit_bytes=VMEM_LIMIT),
    )(small, ada_w, bias, *ride.srcs)
    return res[0], res[1], res[2:]


def _call(kernel_body, *, name, grid, in_specs, out_specs, out_shape, args, scratch_shapes=(), ride=None):
    n_in, n_out, n_sc = len(in_specs), len(out_specs), len(scratch_shapes)
    nr = len(ride.srcs) if ride else 0
    in_specs, out_specs, out_shape = list(in_specs), list(out_specs), list(out_shape)
    scratch_shapes, args = list(scratch_shapes), list(args)
    if ride is not None:
        in_specs += [ANY] * nr
        out_specs += [ANY] * nr
        out_shape += ride.out_shape
        args += ride.srcs
        scratch_shapes += ride.sems

    def body(*refs):
        ins, refs = refs[:n_in], refs[n_in:]
        rin, refs = refs[:nr], refs[nr:]
        outs, refs = refs[:n_out], refs[n_out:]
        rout, refs = refs[:nr], refs[nr:]
        scratch, rsems = refs[:n_sc], refs[n_sc:]
        first, last = True, True
        for axis, extent in enumerate(grid):
            first &= pl.program_id(axis) == 0
            last &= pl.program_id(axis) == extent - 1
        if ride is not None:
            middle = last if len(grid) > 1 else pl.program_id(0) == (3 * grid[0]) // 4
            exchange = ride.exchange(rin, rout, rsems)
            pl.when(first)(exchange.start)
            pl.when(middle)(exchange.forward)
        kernel_body(*ins, *outs, *scratch)
        if ride is not None:
            pl.when(last)(exchange.finish)

    res = pl.pallas_call(body, name=name, grid=grid, in_specs=in_specs, out_specs=out_specs, out_shape=out_shape,
                         scratch_shapes=scratch_shapes, compiler_params=_params(*(("arbitrary",) * len(grid))))(*args)
    return res[:n_out], res[n_out:]


def _ada_update(c_all, dmod, w, m, v):
    nl, d, ncol = w.shape
    bd = min(d, 256)

    def body(c_ref, dm_ref, w_ref, m_ref, v_ref, g_ref, d_ref, mo_ref, vo_ref):
        cv = c_ref[...]
        g = lax.dot_general(cv * jax.nn.sigmoid(cv), dm_ref[0], (((0,), (0,)), ((), ())), preferred_element_type=F32,
                            precision=lax.Precision.HIGHEST)
        g_ref[0] = g
        d_ref[0], mo_ref[0], vo_ref[0] = _adam_math(w_ref[0], g, m_ref[0], v_ref[0])

    blk = pl.BlockSpec((1, bd, ncol), lambda i, j: (i, j, 0))
    return pl.pallas_call(body, name="ada_update", grid=(nl, d // bd),
        in_specs=[pl.BlockSpec((NDEV, bd), lambda i, j: (0, j)), pl.BlockSpec((1, NDEV, ncol), lambda i, j: (i, 0, 0)),
                  blk, blk, blk],
        out_specs=[blk] * 4, out_shape=[jax.ShapeDtypeStruct(w.shape, F32)] * 4,
        compiler_params=_params("arbitrary", "arbitrary"),
    )(c_all, dmod, w, m, v)


def _pack_stats(st_c, st_m, st_f0, st_p, st_f1, st_loss, gwdw):
    d = st_m.shape[1]

    def body(c_ref, m_ref, f0_ref, p_ref, f1_ref, l_ref, w_ref, o_ref):
        pieces = [
            c_ref[1:3, 0:d], m_ref[0:1, :], f0_ref[0:3, :], p_ref[0:2, :], p_ref[4:5, :], f1_ref[0:3, :],
            c_ref[3:4, 0:d], p_ref[3:4, :], f0_ref[3:4, :], f1_ref[3:4, :],
            c_ref[0:1, 0:d], c_ref[0:1, d:2 * d], m_ref[4:5, :], m_ref[2:4, :], m_ref[1:2, :], l_ref[0:1, :],
            w_ref[0:CONV_WIDTH, :], p_ref[5:6, :], l_ref[2:3, :]]
        row = 0
        for piece in pieces:
            o_ref[pl.ds(row, piece.shape[0]), :] = piece
            row += piece.shape[0]

    return pl.pallas_call(body, name="pack_stats",
                          out_shape=jax.ShapeDtypeStruct((12 + 4 + 7 + CONV_WIDTH + 2, d), F32),
                          )(st_c, st_m, st_f0, st_p, st_f1, st_loss, gwdw)


def _row_block(r, c, bytes_per_row_elem=4, budget=2 * 2**20):
    if r * c * bytes_per_row_elem <= budget or r % 8:
        return r
    best = 8
    for b in range(8, r + 1, 8):
        if r % b == 0 and b * c * bytes_per_row_elem <= budget:
            best = b
    return best


def _sum_slots(land, name):
    _, r, c = land.shape
    br = _row_block(r, c, 8 * land.dtype.itemsize)

    def body(l_ref, o_ref):
        acc = l_ref[0].astype(F32)
        for s in range(1, NDEV):
            acc = acc + l_ref[s].astype(F32)
        o_ref[...] = acc

    return pl.pallas_call(body, name=name, grid=(r // br,),
        in_specs=[pl.BlockSpec((NDEV, br, c), lambda i: (0, i, 0))],
        out_specs=pl.BlockSpec((br, c), lambda i: (i, 0)),
        out_shape=jax.ShapeDtypeStruct((r, c), F32),
        compiler_params=_params("arbitrary"),
    )(land)


def _adamw(w, g, m, v, name):
    r, c = w.shape
    br = _row_block(r, c)

    def body(w_ref, g_ref, m_ref, v_ref, d_ref, mo_ref, vo_ref):
        d_ref[...], mo_ref[...], vo_ref[...] = _adam_math(w_ref[...], g_ref[...], m_ref[...], v_ref[...])

    spec = pl.BlockSpec((br, c), lambda i: (i, 0))
    return pl.pallas_call(body, name=name, grid=(r // br,),
        in_specs=[spec] * 4, out_specs=[spec] * 3,
        out_shape=[jax.ShapeDtypeStruct((r, c), F32)] * 3,
        compiler_params=_params("arbitrary"),
    )(w, g, m, v)


def _adam_math(w, g, m, v):
    m2 = ADAM_B1 * m + (1.0 - ADAM_B1) * g
    v2 = ADAM_B2 * v + (1.0 - ADAM_B2) * (g * g)
    m_hat = m2 / (1.0 - ADAM_B1 ** ADAM_STEP)
    v_hat = v2 / (1.0 - ADAM_B2 ** ADAM_STEP)
    return -ADAM_LR * (m_hat / (jnp.sqrt(v_hat) + ADAM_EPS) + ADAM_WD * w), m2, v2


def _slot_sum(land_ref, *lead):
    acc = land_ref[(0,) + lead].astype(F32)
    for s in range(1, NDEV):
        acc = acc + land_ref[(s,) + lead].astype(F32)
    return acc


def _finalize(land, w, m, v, transposed, name):
    _, r, c = land.shape
    cb = 256 if (transposed and c % 256 == 0) else c
    wblk = pl.BlockSpec((cb, r), lambda i: (i, 0)) if transposed else pl.BlockSpec((r, cb), lambda i: (0, i))

    def body(l_ref, w_ref, m_ref, v_ref, g_ref, d_ref, mo_ref, vo_ref):
        g = _slot_sum(l_ref)
        g = g.T if transposed else g
        g_ref[...] = g
        d_ref[...], mo_ref[...], vo_ref[...] = _adam_math(w_ref[...], g, m_ref[...], v_ref[...])

    return pl.pallas_call(body, name=name, grid=(c // cb,),
        in_specs=[pl.BlockSpec((NDEV, r, cb), lambda i: (0, 0, i)), wblk, wblk, wblk], out_specs=[wblk] * 4,
        out_shape=[jax.ShapeDtypeStruct(w.shape, F32)] * 4,
        compiler_params=_params("arbitrary"),
    )(land, w, m, v)


def _finalize_ffn(land0, land1, wg, wu, wd, mg, mu, md, vg, vu, vd):
    nl, fs, d = wg.shape
    db = min(256, d)

    def kernel_body(l0_ref, l1_ref, wg_ref, wu_ref, wd_ref, mg_ref, mu_ref, md_ref, vg_ref, vu_ref, vd_ref, *outs):
        layer = pl.program_id(0)
        triples = [(wg_ref, mg_ref, vg_ref), (wu_ref, mu_ref, vu_ref), (wd_ref, md_ref, vd_ref)]

        def run(land_ref):
            for j, (w_ref, m_ref, v_ref) in enumerate(triples):
                g = _slot_sum(land_ref, j)
                delta, m2, v2 = _adam_math(w_ref[0], g, m_ref[0], v_ref[0])
                for o_ref, val in zip(outs[j::3], (g, delta, m2, v2)):
                    o_ref[0] = val

        @pl.when(layer == 0)
        def _():
            run(l0_ref)

        @pl.when(layer == 1)
        def _():
            run(l1_ref)

    blk = pl.BlockSpec((1, fs, db), lambda l, i: (l, 0, i))
    lblk = [pl.BlockSpec((NDEV, 3, fs, db), lambda l, i: (0, 0, 0, i * (1 - l))),
            pl.BlockSpec((NDEV, 3, fs, db), lambda l, i: (0, 0, 0, i * l))]
    outs, _ = _call(kernel_body, name="finalize_ffn", grid=(nl, d // db),
        in_specs=lblk + [blk] * 9, out_specs=[blk] * 12, out_shape=[jax.ShapeDtypeStruct(wg.shape, F32)] * 12,
        args=(land0, land1, wg, wu, wd, mg, mu, md, vg, vu, vd))
    return outs


CONV_ROWS = 64
LANES = 128


def _shifted_copies(buf, sh, n):
    sh[0] = buf[...]
    for r in range(1, 8):
        sh[r, pl.ds(0, n - 8), :] = buf[pl.ds(r, n - 8), :]


def _window(sh, o, rows, cols):
    return sh[o % 8, pl.ds(o - o % 8, rows), cols]


def _conv_in(x, mod, gmix, gw, w1_rows, w1_idx, b1, tm, ride):
    s, d = x.shape

    def kernel_body(x_ref, mod_ref, g_ref, w_ref, b_ref, h_ref, u_ref, glu_ref):
        n, _ = _rms(x_ref[...])
        h = (n * g_ref[...]) * (1.0 + mod_ref[1:2, :]) + mod_ref[0:1, :]
        hb = h.astype(BF16)
        h_ref[...] = hb
        u = _nt(hb, w_ref[...].reshape(NDEV * w1_rows, d)) + b_ref[...]
        u_ref[...] = u.astype(BF16)
        glu_ref[...] = (u[:, :d] * jax.nn.sigmoid(u[:, d:])).astype(BF16)

    tile = pl.BlockSpec((tm, d), lambda i: (i, 0))
    return _call(kernel_body, name="conv_in", grid=(s // tm,),
        in_specs=[tile, _row(8, d), _row(1, d), _weight_spec(w1_rows, d, w1_idx), _row(1, 2 * d)],
        out_specs=[tile, pl.BlockSpec((tm, 2 * d), lambda i: (i, 0)), tile],
        out_shape=[jax.ShapeDtypeStruct((s, d), BF16), jax.ShapeDtypeStruct((s, 2 * d), BF16),
                   jax.ShapeDtypeStruct((s, d), BF16)],
        args=(x, mod, gmix, gw, b1), ride=ride)


def _conv_mid(glu, wdw, bdw, ln_g, ln_b, gw, w2_rows, w2_idx, b2, x, mod, tm, ride):
    s, d = x.shape
    off = CONV_HALO - (CONV_WIDTH - 1)
    rc = min(CONV_ROWS, tm)
    nsub = 2 if s % (2 * tm) == 0 else 1
    tb = nsub * tm

    def kernel_body(glu_ref, halo_ref, wdw_ref, bdw_ref, lng_ref, lnb_ref, w_ref, b2_ref, x_ref, mod_ref,
                    dwc_ref, s_ref, y_ref, x1_ref, buf, sh):
        i = pl.program_id(0)
        w2 = w_ref[...].reshape(NDEV * w2_rows, d)
        for sub in range(nsub):
            rows = pl.ds(sub * tm, tm)
            if sub == 0:
                buf[pl.ds(0, CONV_HALO), :] = jnp.where(i > 0, halo_ref[...].astype(F32), 0.0)
            else:
                buf[pl.ds(0, CONV_HALO), :] = glu_ref[pl.ds(sub * tm - CONV_HALO, CONV_HALO), :].astype(F32)
            buf[pl.ds(CONV_HALO, tm), :] = glu_ref[rows, :].astype(F32)
            _shifted_copies(buf, sh, CONV_HALO + tm)
            for cb in range(d // LANES):
                cols = pl.ds(cb * LANES, LANES)
                taps = wdw_ref[:, cols]
                for r in range(tm // rc):
                    part = jnp.zeros((rc, LANES), F32) + bdw_ref[:, cols]
                    for k in range(CONV_WIDTH):
                        part = part + _window(sh, r * rc + off + k, rc, cols) * taps[k:k + 1, :]
                    dwc_ref[pl.ds(sub * tm + r * rc, rc), cols] = part
            acc = dwc_ref[rows, :]
            mu = jnp.mean(acc, axis=-1, keepdims=True)
            xc = acc - mu
            rstd = lax.rsqrt(jnp.mean(xc * xc, axis=-1, keepdims=True) + EPS)
            ln = (xc * rstd) * lng_ref[...] + lnb_ref[...]
            sb = (ln * jax.nn.sigmoid(ln)).astype(BF16)
            s_ref[rows, :] = sb
            y = _nn(sb, w2) + b2_ref[...]
            y_ref[rows, :] = y.astype(BF16)
            x1_ref[rows, :] = x_ref[rows, :] + (1.0 + mod_ref[2:3, :]) * y

    tile = pl.BlockSpec((tb, d), lambda i: (i, 0))
    halo = pl.BlockSpec((CONV_HALO, d), lambda i: (jnp.maximum(i * (tb // CONV_HALO) - 1, 0), 0))
    return _call(kernel_body, name="conv_mid", grid=(s // tb,),
        in_specs=[tile, halo, _row(CONV_WIDTH, d), _row(1, d), _row(1, d), _row(1, d),
                  _weight_spec(w2_rows, d, w2_idx), _row(1, d), tile, _row(8, d)],
        out_specs=[tile, tile, tile, tile],
        out_shape=[jax.ShapeDtypeStruct((s, d), F32), jax.ShapeDtypeStruct((s, d), BF16),
                   jax.ShapeDtypeStruct((s, d), BF16), jax.ShapeDtypeStruct((s, d), F32)],
        scratch_shapes=[pltpu.VMEM((CONV_HALO + tm, d), F32), pltpu.VMEM((8, CONV_HALO + tm, d), F32)],
        args=(glu, glu, wdw, bdw, ln_g, ln_b, gw, b2, x, mod), ride=ride)


def _ffn_fwd(x, mod, gffn, weights, fs, f, tm, name, ride=None, loss=None):
    s, d = x.shape

    last = s // tm - 1

    def kernel_body(x_ref, mod_ref, g_ref, wg_ref, wu_ref, wd_ref, *rest):
        xv = x_ref[...]
        n, _ = _rms(xv)
        hb = ((n * g_ref[...]) * (1.0 + mod_ref[4:5, :]) + mod_ref[3:4, :]).astype(BF16)
        gg = _nt(hb, wg_ref[...].reshape(f, d))
        uu = _nt(hb, wu_ref[...].reshape(f, d))
        ab = ((gg * jax.nn.sigmoid(gg)) * uu).astype(BF16)
        y = _nn(ab, wd_ref[...].reshape(f, d))
        xo = xv + (1.0 + mod_ref[5:6, :]) * y
        if loss is None:
            h_ref, gg_ref, uu_ref, y_ref, xo_ref = rest
            xo_ref[...] = xo
        else:
            t_ref, gfin_ref, h_ref, gg_ref, uu_ref, y_ref, dx_ref, st_ref = rest
            dx_ref[...] = _loss_tile(xo, t_ref[...], gfin_ref[...], st_ref, pl.program_id(0), last)
        h_ref[...] = hb
        gg_ref[...] = gg.astype(BF16)
        uu_ref[...] = uu.astype(BF16)
        y_ref[...] = y.astype(BF16)

    tile = pl.BlockSpec((tm, d), lambda i: (i, 0))
    wide = pl.BlockSpec((tm, f), lambda i: (i, 0))
    extra_in = [] if loss is None else [tile, _row(1, d)]
    extra_out = [] if loss is None else [_row(8, d)]
    return _call(kernel_body, name=name, grid=(s // tm,),
        in_specs=[tile, _row(8, d), _row(1, d)] + [_weight_spec(fs, d, idx) for _, idx in weights] + extra_in,
        out_specs=[tile, wide, wide, tile, tile] + extra_out,
        out_shape=[jax.ShapeDtypeStruct((s, d), BF16), jax.ShapeDtypeStruct((s, f), BF16),
                   jax.ShapeDtypeStruct((s, f), BF16), jax.ShapeDtypeStruct((s, d), BF16),
                   jax.ShapeDtypeStruct((s, d), F32)] + [jax.ShapeDtypeStruct((8, d), F32)] * len(extra_out),
        args=(x, mod, gffn) + tuple(a for a, _ in weights) + (() if loss is None else tuple(loss)), ride=ride)


def _loss_tile(x, target, g, st_ref, i, last):
    d = x.shape[1]

    @pl.when(i == 0)
    def _():
        st_ref[...] = jnp.zeros_like(st_ref)

    n, rinv = _rms(x)
    err = n * g - target
    dy = err * (1.0 / d)
    st_ref[0:1, :] += _sum0(dy * n)
    st_ref[1:2, :] += _sum0(err * err) * (0.5 / d)

    @pl.when(i == last)
    def _():
        st_ref[2:3, :] = jnp.zeros((1, d), F32) + jnp.sum(st_ref[1:2, :], axis=-1, keepdims=True)

    dn = dy * g
    return rinv * (dn - n * jnp.mean(dn * n, axis=-1, keepdims=True))


def _pool_fwd(x, mod, gmix, pw, ls, tm, ride):
    s, d = x.shape
    dg = d // len(POOL_WINDOWS)

    def body(x_ref, halo_ref, mod_ref, g_ref, pw_ref, ls_ref, mixed_ref, yp_ref, xo_ref, buf):
        i = pl.program_id(0)

        def hfun(xv):
            n, _ = _rms(xv)
            return (n * g_ref[...]) * (1.0 + mod_ref[1:2, :]) + mod_ref[0:1, :]

        xv = x_ref[...]
        h = hfun(xv)
        buf[pl.ds(0, POOL_HALO), :] = jnp.where(i > 0, hfun(halo_ref[...]), 0.0)
        buf[pl.ds(POOL_HALO, tm), :] = h
        t = i * tm + lax.broadcasted_iota(jnp.int32, (tm, 1), 0)
        gate = 1.0 + mod_ref[2:3, :]
        for gi, w in enumerate(POOL_WINDOWS):
            cols = pl.ds(gi * dg, dg)
            ws = buf[pl.ds(POOL_HALO, tm), cols]
            for j in range(1, w):
                ws = ws + buf[pl.ds(POOL_HALO - j, tm), cols]
            inv = 1.0 / jnp.minimum(t + 1, w).astype(F32)
            mb = (ws * inv - h[:, gi * dg:(gi + 1) * dg]).astype(BF16)
            mixed_ref[:, cols] = mb
            yp = _nn(mb, pw_ref[gi])
            yp_ref[:, cols] = yp.astype(BF16)
            xo_ref[:, cols] = xv[:, gi * dg:(gi + 1) * dg] + gate[:, gi * dg:(gi + 1) * dg] * (yp * ls_ref[:, cols])

    tile = pl.BlockSpec((tm, d), lambda i: (i, 0))
    halo = pl.BlockSpec((POOL_HALO, d), lambda i: (jnp.maximum(i * (tm // POOL_HALO) - 1, 0), 0))
    return _call(body, name="pool_fwd", grid=(s // tm,),
        in_specs=[tile, halo, _row(8, d), _row(1, d), pl.BlockSpec((len(POOL_WINDOWS), dg, dg), lambda i: (0, 0, 0)),
                  _row(1, d)],
        out_specs=[tile, tile, tile],
        out_shape=[jax.ShapeDtypeStruct((s, d), BF16), jax.ShapeDtypeStruct((s, d), BF16),
                   jax.ShapeDtypeStruct((s, d), F32)],
        scratch_shapes=[pltpu.VMEM((POOL_HALO + tm, d), F32)],
        args=(x, x, mod, gmix, pw, ls), ride=ride)


def _ffn_bwd(dxo, x, gg, uu, y, mod, gffn, weights, fs, f, tm, name, ride=None):
    s, d = x.shape

    def kernel_body(dxo_ref, x_ref, gg_ref, uu_ref, y_ref, mod_ref, g_ref, wg_ref, wu_ref, wd_ref,
                    dg_ref, du_ref, a_ref, dy_ref, dxi_ref, st_ref):
        @pl.when(pl.program_id(0) == 0)
        def _():
            st_ref[...] = jnp.zeros_like(st_ref)

        dxo_v = dxo_ref[...]
        dyb = (dxo_v * (1.0 + mod_ref[5:6, :])).astype(BF16)
        dy_ref[...] = dyb
        da = _nt(dyb, wd_ref[...].reshape(f, d))
        ggv, uuv = gg_ref[...].astype(F32), uu_ref[...].astype(F32)
        sg = jax.nn.sigmoid(ggv)
        silu = ggv * sg
        a_ref[...] = (silu * uuv).astype(BF16)
        dub = (da * silu).astype(BF16)
        dgb = (da * uuv * _silu_grad(ggv, sg)).astype(BF16)
        du_ref[...] = dub
        dg_ref[...] = dgb
        dh = _nn(dgb, wg_ref[...].reshape(f, d)) + _nn(dub, wu_ref[...].reshape(f, d))
        n, rinv = _rms(x_ref[...])
        dx, dsh, dsc, dgain = _rms_mod_bwd(dh, n, rinv, g_ref[...], mod_ref[4:5, :])
        dxi_ref[...] = dxo_v + dx
        st_ref[0:1, :] += dsh
        st_ref[1:2, :] += dsc
        st_ref[2:3, :] += _sum0(dxo_v * y_ref[...].astype(F32))
        st_ref[3:4, :] += dgain

    tile = pl.BlockSpec((tm, d), lambda i: (i, 0))
    wide = pl.BlockSpec((tm, f), lambda i: (i, 0))
    return _call(kernel_body, name=name, grid=(s // tm,),
        in_specs=[tile, tile, wide, wide, tile, _row(8, d), _row(1, d)] + [_weight_spec(fs, d, idx) for _, idx in weights],
        out_specs=[wide, wide, wide, tile, tile, _row(8, d)],
        out_shape=[jax.ShapeDtypeStruct((s, f), BF16)] * 3 + [jax.ShapeDtypeStruct((s, d), BF16),
                   jax.ShapeDtypeStruct((s, d), F32), jax.ShapeDtypeStruct((8, d), F32)],
        args=(dxo, x, gg, uu, y, mod, gffn) + tuple(a for a, _ in weights), ride=ride)


def _ffn_wgrad(dgb, dub, ab, h, dyb, fb, ts, name):
    s, f = dgb.shape
    d = h.shape[1]
    last = s // ts - 1

    def body(dg_ref, du_ref, a_ref, h_ref, dy_ref, o_ref, acc):
        t = pl.program_id(1)

        @pl.when(t == 0)
        def _():
            acc[...] = jnp.zeros_like(acc)

        hv = h_ref[...]
        acc[0] += _tn(dg_ref[...], hv)
        acc[1] += _tn(du_ref[...], hv)
        acc[2] += _tn(a_ref[...], dy_ref[...])

        @pl.when(t == last)
        def _():
            o_ref[...] = acc[...].astype(BF16)

    wide = pl.BlockSpec((ts, fb), lambda j, t: (t, j))
    tile = pl.BlockSpec((ts, d), lambda j, t: (t, 0))
    return pl.pallas_call(body, name=name, grid=(f // fb, s // ts),
        in_specs=[wide, wide, wide, tile, tile],
        out_specs=pl.BlockSpec((3, fb, d), lambda j, t: (0, j, 0)),
        out_shape=jax.ShapeDtypeStruct((3, f, d), BF16),
        scratch_shapes=[pltpu.VMEM((3, fb, d), F32)],
        compiler_params=_params("arbitrary", "arbitrary"),
    )(dgb, dub, ab, h, dyb)


def _pool_bwd(dxo, x, yp, mixed, mod, gmix, pw, ls, tm):
    s, d = x.shape
    ng = len(POOL_WINDOWS)
    dg = d // ng
    last = s // tm - 1

    def body(dxo_ref, dxh_ref, x_ref, yp_ref, mixed_ref, mod_ref, g_ref, pw_ref, ls_ref,
             dxi_ref, dpw_ref, st_ref, bufy, bufq, bufh, acc):
        i = pl.program_id(0)

        @pl.when(i == 0)
        def _():
            st_ref[...] = jnp.zeros_like(st_ref)
            acc[...] = jnp.zeros_like(acc)

        gate = 1.0 + mod_ref[2:3, :]
        lsv = ls_ref[...]
        dxo_v = dxo_ref[...]
        st_ref[2:3, :] += _sum0(dxo_v * yp_ref[...].astype(F32))
        bufy[pl.ds(0, tm), :] = (dxo_v * (gate * lsv)).astype(BF16)
        bufy[pl.ds(tm, POOL_HALO), :] = jnp.where(i < last, dxh_ref[...] * (gate * lsv), 0.0).astype(BF16)
        t = i * tm + lax.broadcasted_iota(jnp.int32, (tm + POOL_HALO, 1), 0)
        for gi, w in enumerate(POOL_WINDOWS):
            cols = pl.ds(gi * dg, dg)
            dm = _nt(bufy[:, cols], pw_ref[gi])
            bufq[:, cols] = dm * (1.0 / jnp.minimum(t + 1, w).astype(F32))
            dh = bufq[pl.ds(0, tm), cols] - dm[0:tm, :]
            for j in range(1, w):
                dh = dh + bufq[pl.ds(j, tm), cols]
            bufh[:, cols] = dh
            acc[gi] += _tn(mixed_ref[:, cols], bufy[pl.ds(0, tm), cols])
        n, rinv = _rms(x_ref[...])
        dx, dsh, dsc, dgain = _rms_mod_bwd(bufh[...], n, rinv, g_ref[...], mod_ref[1:2, :])
        dxi_ref[...] = dxo_v + dx
        st_ref[0:1, :] += dsh
        st_ref[1:2, :] += dsc
        st_ref[3:4, :] += dgain

        @pl.when(i == last)
        def _():
            r = st_ref[2:3, :]
            st_ref[4:5, :] = r * lsv
            st_ref[5:6, :] = r * gate
            dpw_ref[...] = acc[...].astype(BF16)

    tile = pl.BlockSpec((tm, d), lambda i: (i, 0))
    nxt = pl.BlockSpec((POOL_HALO, d), lambda i: (jnp.minimum((i + 1) * (tm // POOL_HALO), s // POOL_HALO - 1), 0))
    pws = pl.BlockSpec((ng, dg, dg), lambda i: (0, 0, 0))
    return _call(body, name="pool_bwd", grid=(s // tm,),
        in_specs=[tile, nxt, tile, tile, tile, _row(8, d), _row(1, d), pws, _row(1, d)],
        out_specs=[tile, pws, _row(8, d)],
        out_shape=[jax.ShapeDtypeStruct((s, d), F32), jax.ShapeDtypeStruct((ng, dg, dg), BF16),
                   jax.ShapeDtypeStruct((8, d), F32)],
        scratch_shapes=[pltpu.VMEM((tm + POOL_HALO, d), BF16), pltpu.VMEM((tm + POOL_HALO, d), F32),
                        pltpu.VMEM((tm, d), F32), pltpu.VMEM((ng, dg, dg), F32)],
        args=(dxo, dxo, x, yp, mixed, mod, gmix, pw, ls))[0]


def _conv_bwd_mid(dxo, y, dwc, sb, mod, ln_g, ln_b, gw, w2_rows, w2_idx, tm):
    s, d = dwc.shape
    last = s // tm - 1

    def body(dxo_ref, y_ref, dwc_ref, s_ref, mod_ref, lng_ref, lnb_ref, w_ref, dd_ref, dw_ref, st_ref, acc):
        i = pl.program_id(0)

        @pl.when(i == 0)
        def _():
            st_ref[...] = jnp.zeros_like(st_ref)
            acc[...] = jnp.zeros_like(acc)

        dxo_v = dxo_ref[...]
        st_ref[0:1, :] += _sum0(dxo_v * y_ref[...].astype(F32))
        dy = dxo_v * (1.0 + mod_ref[2:3, :])
        st_ref[1:2, :] += _sum0(dy)
        dyb = dy.astype(BF16)
        ds = _nt(dyb, w_ref[...].reshape(NDEV * w2_rows, d))
        acc[...] += _tn(s_ref[...], dyb)
        v = dwc_ref[...]
        mu = jnp.mean(v, axis=-1, keepdims=True)
        xc = v - mu
        rstd = lax.rsqrt(jnp.mean(xc * xc, axis=-1, keepdims=True) + EPS)
        xhat = xc * rstd
        ln = xhat * lng_ref[...] + lnb_ref[...]
        dln = ds * _silu_grad(ln, jax.nn.sigmoid(ln))
        st_ref[2:3, :] += _sum0(dln * xhat)
        st_ref[3:4, :] += _sum0(dln)
        dxh = dln * lng_ref[...]
        dd = rstd * (dxh - jnp.mean(dxh, axis=-1, keepdims=True) - xhat * jnp.mean(dxh * xhat, axis=-1, keepdims=True))
        dd_ref[...] = dd
        st_ref[4:5, :] += _sum0(dd)

        @pl.when(i == last)
        def _():
            dw_ref[...] = acc[...].astype(BF16)

    tile = pl.BlockSpec((tm, d), lambda i: (i, 0))
    return _call(body, name="conv_bwd_mid", grid=(s // tm,),
        in_specs=[tile, tile, tile, tile, _row(8, d), _row(1, d), _row(1, d), _weight_spec(w2_rows, d, w2_idx)],
        out_specs=[tile, pl.BlockSpec((d, d), lambda i: (0, 0)), _row(8, d)],
        out_shape=[jax.ShapeDtypeStruct((s, d), F32), jax.ShapeDtypeStruct((d, d), BF16),
                   jax.ShapeDtypeStruct((8, d), F32)],
        scratch_shapes=[pltpu.VMEM((d, d), F32)],
        args=(dxo, y, dwc, sb, mod, ln_g, ln_b, gw))[0]


def _conv_bwd_in(dd, glu, u, hb, x, dxo, wdw, gw, w1_rows, w1_idx, mod, gmix, tm, nsub, ride):
    s, d = x.shape
    rw = 32
    tap_group = 16
    tb = nsub * tm
    last = s // tb - 1

    def kernel_body(dd_ref, ddn_ref, glu_ref, u_ref, h_ref, x_ref, dxo_ref, wdw_ref, w_ref, mod_ref, g_ref,
                    dxi_ref, dw_ref, dwdw_ref, st_ref, bufd, shd, dgl, accw, acc):
        i = pl.program_id(0)

        @pl.when(i == 0)
        def _():
            st_ref[...] = jnp.zeros_like(st_ref)
            accw[...] = jnp.zeros_like(accw)
            acc[...] = jnp.zeros_like(acc)

        w1t = w_ref[...].reshape(NDEV * w1_rows, d)
        for sub in range(nsub):
            base = sub * tm
            tile_rows = pl.ds(base, tm)
            bufd[pl.ds(0, tm), :] = dd_ref[tile_rows, :]
            if sub == nsub - 1:
                bufd[pl.ds(tm, CONV_HALO), :] = jnp.where(i < last, ddn_ref[...], 0.0)
            else:
                bufd[pl.ds(tm, CONV_HALO), :] = dd_ref[pl.ds(base + tm, CONV_HALO), :]
            _shifted_copies(bufd, shd, tm + CONV_HALO)
            for cb in range(d // LANES):
                cols = pl.ds(cb * LANES, LANES)
                taps = wdw_ref[:, cols]
                for r in range(tm // rw):
                    part = jnp.zeros((rw, LANES), F32)
                    for k in range(CONV_WIDTH):
                        part = part + _window(shd, r * rw + CONV_WIDTH - 1 - k, rw, cols) * taps[k:k + 1, :]
                    dgl[sub, pl.ds(r * rw, rw), cols] = part
                for k0 in range(0, CONV_WIDTH, tap_group):
                    group = range(k0, min(CONV_WIDTH, k0 + tap_group))
                    sums = {k: jnp.zeros((8, LANES), F32) for k in group}
                    for r in range(tm // rw):
                        gch = glu_ref[pl.ds(base + r * rw, rw), cols].astype(F32)
                        for k in group:
                            p = _window(shd, r * rw + CONV_WIDTH - 1 - k, rw, cols) * gch
                            for q in range(rw // 8):
                                sums[k] = sums[k] + p[q * 8:(q + 1) * 8, :]
                    for k in group:
                        accw[k, :, cols] += sums[k]
            dglu = dgl[sub]
            uv = u_ref[tile_rows, :].astype(F32)
            a, g = uv[:, :d], uv[:, d:]
            sg = jax.nn.sigmoid(g)
            du = jnp.concatenate([dglu * sg, dglu * a * (sg * (1.0 - sg))], axis=1)
            st_ref[0:1, :] += _sum0(du)
            dub = du.astype(BF16)
            dh = _nn(dub, w1t)
            acc[...] += _tn(dub, h_ref[tile_rows, :])
            n, rinv = _rms(x_ref[tile_rows, :])
            dx, dsh, dsc, dgain = _rms_mod_bwd(dh, n, rinv, g_ref[...], mod_ref[1:2, :])
            dxi_ref[tile_rows, :] = dxo_ref[tile_rows, :] + dx
            st_ref[1:2, 0:d] += dsh
            st_ref[2:3, 0:d] += dsc
            st_ref[3:4, 0:d] += dgain

        @pl.when(i == last)
        def _():
            dw_ref[...] = acc[...].astype(BF16)
            dwdw_ref[...] = jnp.sum(accw[...], axis=1)

    tile = pl.BlockSpec((tb, d), lambda i: (i, 0))
    nxt = pl.BlockSpec((CONV_HALO, d), lambda i: (jnp.minimum((i + 1) * (tb // CONV_HALO), s // CONV_HALO - 1), 0))
    return _call(kernel_body, name="conv_bwd_in", grid=(s // tb,),
        in_specs=[tile, nxt, tile, pl.BlockSpec((tb, 2 * d), lambda i: (i, 0)), tile, tile, tile,
                  _row(CONV_WIDTH, d), _weight_spec(w1_rows, d, w1_idx), _row(8, d), _row(1, d)],
        out_specs=[tile, pl.BlockSpec((2 * d, d), lambda i: (0, 0)), _row(CONV_HALO, d), _row(8, 2 * d)],
        out_shape=[jax.ShapeDtypeStruct((s, d), F32), jax.ShapeDtypeStruct((2 * d, d), BF16),
                   jax.ShapeDtypeStruct((CONV_HALO, d), F32), jax.ShapeDtypeStruct((8, 2 * d), F32)],
        scratch_shapes=[pltpu.VMEM((tm + CONV_HALO, d), F32), pltpu.VMEM((8, tm + CONV_HALO, d), F32),
                        pltpu.VMEM((nsub, tm, d), F32), pltpu.VMEM((CONV_HALO, 8, d), F32),
                        pltpu.VMEM((2 * d, d), F32)],
        args=(dd, dd, glu, u, hb, x, dxo, wdw, gw, mod, gmix), ride=ride)


def kernel(x, c, ada_w, ada_b, norm_mix_g, norm_ffn_g, conv_w1, conv_b1, conv_wdw, conv_bdw, conv_ln_g, conv_ln_b, conv_w2, conv_b2, pool_w, pool_ls, ffn_w_gate, ffn_w_up, ffn_w_down, final_g, loss_target, m_ada_w, m_ada_b, m_norm_mix_g, m_norm_ffn_g, m_conv_w1, m_conv_b1, m_conv_wdw, m_conv_bdw, m_conv_ln_g, m_conv_ln_b, m_conv_w2, m_conv_b2, m_pool_w, m_pool_ls, m_ffn_w_gate, m_ffn_w_up, m_ffn_w_down, m_final_g, v_ada_w, v_ada_b, v_norm_mix_g, v_norm_ffn_g, v_conv_w1, v_conv_b1, v_conv_wdw, v_conv_bdw, v_conv_ln_g, v_conv_ln_b, v_conv_w2, v_conv_b2, v_pool_w, v_pool_ls, v_ffn_w_gate, v_ffn_w_up, v_ffn_w_down, v_final_g):
    _, s, d = x.shape
    f = ffn_w_down.shape[1] * NDEV
    fs = f // NDEV
    r1, r2 = 2 * d // NDEV, d // NDEV
    ng = len(POOL_WINDOWS)
    dg = d // ng
    pr = ng * (dg // NDEV) * dg // d
    ncol = ada_w.shape[2]
    dc = d // NDEV
    tm = min(256, s)
    me = _my_index()
    x0 = x.reshape(s, d)
    target = loss_target.reshape(s, d)

    small = jnp.concatenate([c.reshape(NDEV, dc), conv_wdw[0], pool_ls], axis=0)
    shard_a = conv_w1[0].T.astype(BF16)
    bias = lax.dynamic_slice_in_dim(ada_b, me * ncol, ncol, axis=1)[:, None, :]
    small_all, mod_all, (gwa,) = _prologue(small, ada_w, bias, _Ride("gather", [shard_a]))
    c_all = small_all[:, 0:NDEV, :].reshape(NDEV, d)
    wdw = small_all[:, NDEV:NDEV + CONV_WIDTH, :].transpose(1, 0, 2).reshape(CONV_WIDTH, d)
    ls = small_all[:, NDEV + CONV_WIDTH, :].reshape(1, d)
    mod_mine = lax.dynamic_index_in_dim(mod_all.reshape(NDEV, 2, NDEV, ncol), me, axis=2, keepdims=False)
    mod = mod_mine.transpose(1, 0, 2).reshape(2, 6, d)
    mod = jnp.concatenate([mod, jnp.zeros((2, 2, d), F32)], axis=1)

    shard_b1 = jnp.concatenate([ffn_w_gate[0].T, conv_w2[0]], axis=0).astype(BF16)
    shard_b2 = jnp.concatenate([ffn_w_up[0].T, ffn_w_down[0]], axis=0).astype(BF16)
    shard_c = jnp.concatenate([ffn_w_gate[1].T, ffn_w_up[1].T, ffn_w_down[1], pool_w.reshape(pr, d)], axis=0).astype(BF16)
    w1_at = (gwa, 0)

    (h0, u, glu), (gwb1,) = _conv_in(x0, mod[0], norm_mix_g[0:1], w1_at[0], r1, w1_at[1], conv_b1, tm,
                                    _Ride("gather", [shard_b1]))
    w2_at = (gwb1[:, fs:fs + r2, :], 0)
    (dwc, sb, y0, x1), (gwb2,) = _conv_mid(glu, wdw, conv_bdw, conv_ln_g, conv_ln_b, w2_at[0], r2, w2_at[1], conv_b2,
                                           x0, mod[0], tm, _Ride("gather", [shard_b2]))
    ffn0_w = [(gwb1, 0), (gwb2, 0), (gwb2, 1)]
    (h1, gg0, uu0, yf0, x2), (gwc,) = _ffn_fwd(x1, mod[0], norm_ffn_g[0:1], ffn0_w, fs, f, tm, "ffn_fwd0",
                                               _Ride("gather", [shard_c]))
    pw = gwc[:, 3 * fs:3 * fs + pr, :].reshape(NDEV, ng, dg // NDEV, dg).transpose(1, 0, 2, 3).reshape(ng, dg, dg)
    (mixed, yp, x3), _ = _pool_fwd(x2, mod[1], norm_mix_g[1:2], pw, ls, tm, None)
    ffn1_w = [(gwc, 0), (gwc, 1), (gwc, 2)]
    (h3, gg1, uu1, yf1, dx4, st_loss), _ = _ffn_fwd(x3, mod[1], norm_ffn_g[1:2], ffn1_w, fs, f, tm, "ffn_fwd1",
                                                    loss=(target, final_g.reshape(1, d)))

    fb = f // 2 if (f // 2) % 128 == 0 else f
    ts = min(512, s)
    (dgb, dub, ab, dyb, dx3, st_f1), _ = _ffn_bwd(dx4, x3, gg1, uu1, yf1, mod[1], norm_ffn_g[1:2], ffn1_w, fs, f, tm,
                                                  "ffn_bwd1")
    gf1 = _ffn_wgrad(dgb, dub, ab, h3, dyb, fb, ts, "ffn_wgrad1")
    dx2, gpw, st_p = _pool_bwd(dx3, x2, yp, mixed, mod[1], norm_mix_g[1:2], pw, ls, tm)
    (dgb, dub, ab, dyb, dx1, st_f0), (land_f1,) = _ffn_bwd(dx2, x1, gg0, uu0, yf0, mod[0], norm_ffn_g[0:1], ffn0_w, fs, f,
                                                           tm, "ffn_bwd0", _Ride("scatter", [gf1]))
    gf0 = _ffn_wgrad(dgb, dub, ab, h1, dyb, fb, ts, "ffn_wgrad0")
    dd, gw2, st_m = _conv_bwd_mid(dx1, y0, dwc, sb, mod[0], conv_ln_g, conv_ln_b, w2_at[0], r2, w2_at[1], tm)
    (dx0, gw1, gwdw, st_c), (land_f0, land_pw, land_w2) = _conv_bwd_in(
        dd, glu, u, h0, x0, dx1, wdw, w1_at[0], r1, w1_at[1], mod[0], norm_mix_g[0:1], tm, 1,
        _Ride("scatter", [gf0, gpw, gw2[None]]))

    prow = _pack_stats(st_c, st_m, st_f0, st_p, st_f1, st_loss, gwdw)
    p_all, (land_w1,) = _small_exchange(prow, _Ride("scatter", [gw1[None]]), "allgather_stats")
    psum = _sum_slots(p_all, "sum_stats")
    loss = psum[prow.shape[0] - 1, 0]

    tr = lambda a: jnp.swapaxes(a, 1, 2)
    ffn_out = _finalize_ffn(land_f0, land_f1, tr(ffn_w_gate), tr(ffn_w_up), ffn_w_down,
                            tr(m_ffn_w_gate), tr(m_ffn_w_up), m_ffn_w_down, tr(v_ffn_w_gate), tr(v_ffn_w_up), v_ffn_w_down)
    fin_w1 = _finalize(land_w1.reshape(NDEV, r1, d), conv_w1[0], m_conv_w1[0], v_conv_w1[0], True, "finalize_w1")
    fin_w2 = _finalize(land_w2.reshape(NDEV, r2, d), conv_w2[0], m_conv_w2[0], v_conv_w2[0], False, "finalize_w2")
    pshape = (ng * (dg // NDEV), dg)
    fin_pw = _finalize(land_pw.reshape((NDEV,) + pshape), pool_w.reshape(pshape), m_pool_w.reshape(pshape),
                       v_pool_w.reshape(pshape), False, "finalize_pool_w")
    dmod_all = p_all[:, 0:12, :].reshape(NDEV, 2, 6 * d)
    dmod_cols = lax.dynamic_slice_in_dim(dmod_all, me * ncol, ncol, axis=2).transpose(1, 0, 2)
    fin_ada = _ada_update(c_all, dmod_cols, ada_w, m_ada_w, v_ada_w)

    rep_names = ["ada_b", "norm_mix_g", "norm_ffn_g", "conv_b1", "conv_bdw", "conv_ln_g", "conv_ln_b", "conv_b2", "final_g"]
    rep_w = [ada_b, norm_mix_g, norm_ffn_g, conv_b1, conv_bdw, conv_ln_g, conv_ln_b, conv_b2, final_g]
    rep_m = [m_ada_b, m_norm_mix_g, m_norm_ffn_g, m_conv_b1, m_conv_bdw, m_conv_ln_g, m_conv_ln_b, m_conv_b2, m_final_g]
    rep_v = [v_ada_b, v_norm_mix_g, v_norm_ffn_g, v_conv_b1, v_conv_bdw, v_conv_ln_g, v_conv_ln_b, v_conv_b2, v_final_g]
    nrep = sum(w.size for w in rep_w) // d
    pad = jnp.zeros(((-nrep) % 8, d), F32)

    def pack(arrs, fill):
        return jnp.concatenate([a.reshape(-1, d) for a in arrs] + [pad + fill], axis=0)

    rep_g = jnp.concatenate([psum[0:nrep], pad], axis=0)
    rep_d, rep_mo, rep_vo = _adamw(pack(rep_w, 0.0), rep_g, pack(rep_m, 0.0), pack(rep_v, 1.0), "adamw_replicated")

    def unpack(packed):
        out, cur = [], 0
        for w in rep_w:
            k = w.size // d
            out.append(packed[cur:cur + k].reshape(w.shape))
            cur += k
        return out

    rep = dict(zip(rep_names, zip(unpack(psum), unpack(rep_d), unpack(rep_mo), unpack(rep_vo))))

    g_wdw_full = psum[nrep:nrep + CONV_WIDTH]
    g_wdw = lax.dynamic_slice_in_dim(g_wdw_full, me * dc, dc, axis=1)
    g_ls = lax.dynamic_slice_in_dim(psum[nrep + CONV_WIDTH:nrep + CONV_WIDTH + 1], me * dc, dc, axis=1)
    tiny = lambda a, b: jnp.concatenate([a.reshape(CONV_WIDTH, dc), b.reshape(1, dc)], axis=0)
    t_d, t_m, t_v = _adamw(tiny(conv_wdw, pool_ls), tiny(g_wdw, g_ls), tiny(m_conv_wdw, m_pool_ls),
                           tiny(v_conv_wdw, v_pool_ls), "adamw_taps")

    def taps(a):
        return a[0:CONV_WIDTH][None], a[CONV_WIDTH:CONV_WIDTH + 1]

    sharded = {
        "ada_w": tuple(fin_ada),
        "conv_w1": tuple(a[None] for a in fin_w1),
        "conv_w2": tuple(a[None] for a in fin_w2),
        "pool_w": tuple(a.reshape(pool_w.shape) for a in fin_pw),
        "ffn_w_gate": tuple(tr(a) for a in ffn_out[0::3]),
        "ffn_w_up": tuple(tr(a) for a in ffn_out[1::3]),
        "ffn_w_down": tuple(ffn_out[2::3]),
        "conv_wdw": (g_wdw[None], taps(t_d)[0], taps(t_m)[0], taps(t_v)[0]),
        "pool_ls": (g_ls, taps(t_d)[1], taps(t_m)[1], taps(t_v)[1]),
    }
    every = {**rep, **sharded}
    order = ["ada_w", "ada_b", "norm_mix_g", "norm_ffn_g", "conv_w1", "conv_b1", "conv_wdw", "conv_bdw", "conv_ln_g",
             "conv_ln_b", "conv_w2", "conv_b2", "pool_w", "pool_ls", "ffn_w_gate", "ffn_w_up", "ffn_w_down", "final_g"]
    grads = [every[n][0] for n in order]
    deltas = [every[n][1] for n in order]
    new_m = [every[n][2] for n in order]
    new_v = [every[n][3] for n in order]
    return (loss, dx0.reshape(1, s, d), *grads, *deltas, *new_m, *new_v)
```

```python
import jax
import jax.numpy as jnp
from jax import lax
from jax.experimental import pallas as pl
from jax.experimental.pallas import tpu as pltpu

NDEV = 8
EPS = 1e-6
CONV_WIDTH = 31
POOL_WINDOWS = (2, 4, 8, 16)
CONV_HALO = 32
POOL_HALO = 16
ADAM_LR = 0.001
ADAM_B1 = 0.9
ADAM_B2 = 0.999
ADAM_EPS = 1e-08
ADAM_WD = 0.01
ADAM_STEP = 10
VMEM_LIMIT = 56 * 2**20
MESH = pl.DeviceIdType.MESH
F32 = jnp.float32
BF16 = jnp.bfloat16


def _nt(a, b):
    return lax.dot_general(a, b, (((1,), (1,)), ((), ())), preferred_element_type=F32)


def _nn(a, b):
    return lax.dot_general(a, b, (((1,), (0,)), ((), ())), preferred_element_type=F32)


def _tn(a, b):
    return lax.dot_general(a, b, (((0,), (0,)), ((), ())), preferred_element_type=F32)


def _sum0(v):
    return jnp.sum(v, axis=0, keepdims=True)


def _rms(x):
    rinv = lax.rsqrt(jnp.mean(x * x, axis=-1, keepdims=True) + EPS)
    return x * rinv, rinv


def _rms_mod_bwd(dh, n, rinv, g, sc):
    dhs = dh * (1.0 + sc)
    dn = dhs * g
    dx = rinv * (dn - n * jnp.mean(dn * n, axis=-1, keepdims=True))
    return dx, _sum0(dh), _sum0(dh * (n * g)), _sum0(dhs * n)


def _silu_grad(z, sg):
    return sg * (1.0 + z * (1.0 - sg))


def _params(*sem):
    return pltpu.CompilerParams(dimension_semantics=sem, vmem_limit_bytes=VMEM_LIMIT)


def _row(i, d):
    return pl.BlockSpec((i, d), lambda *_: (0, 0))


def _weight_spec(rows, d, idx):
    return pl.BlockSpec((NDEV, rows, d), lambda *_: (0, idx, 0), pipeline_mode=pl.Buffered(1))


def _my_index():
    return 4 * lax.axis_index("x") + 2 * lax.axis_index("y") + lax.axis_index("c")


def _peer(k):
    x, y, c = lax.axis_index("x"), lax.axis_index("y"), lax.axis_index("c")
    px = 1 - x if k & 4 else x
    py = 1 - y if k & 2 else y
    pc = 1 - c if k & 1 else c
    return (px, py, pc), 4 * px + 2 * py + pc


def _gather_sems():
    return [pltpu.SemaphoreType.DMA((NDEV - 1,)), pltpu.SemaphoreType.DMA((NDEV - 1,)), pltpu.SemaphoreType.DMA((1,))]


class _Gather:
    def __init__(self, srcs, dsts, sems):
        self.src, self.dst = srcs[0], dsts[0]
        self.send, self.recv, self.local = sems
        x, y, c = lax.axis_index("x"), lax.axis_index("y"), lax.axis_index("c")
        self.me, self.sibling, self.core = (x, y, c), (x, y, 1 - c), c
        self.chips = [(1 - x, y), (x, 1 - y), (1 - x, 1 - y)]

    def _copy(self, k, block, to, from_input=False):
        slot = self.dst.at[4 * block[0] + 2 * block[1] + block[2]]
        return pltpu.make_async_remote_copy(
            src_ref=self.src if from_input else slot, dst_ref=slot, send_sem=self.send.at[k], recv_sem=self.recv.at[k],
            device_id=to, device_id_type=MESH)

    def _own(self):
        return pltpu.make_async_copy(self.src, self.dst.at[_my_index()], self.local.at[0])

    def _first(self):
        return [self._copy(0, self.me, self.sibling, True)] + [
            self._copy(1 + j, self.me, (*chip, self.core), True) for j, chip in enumerate(self.chips)]

    def start(self):
        self._own().start()
        for cp in self._first():
            cp.start()

    def forward(self):
        for j, chip in enumerate(self.chips):
            self._copy(1 + j, (*chip, self.core), self.me).wait_recv()
            self._copy(4 + j, (*chip, self.core), self.sibling).start()

    def finish(self):
        self._copy(0, self.sibling, self.me).wait_recv()
        for j, chip in enumerate(self.chips):
            self._copy(4 + j, (*chip, 1 - self.core), self.me).wait_recv()
        for cp in self._first():
            cp.wait_send()
        for j, chip in enumerate(self.chips):
            self._copy(4 + j, (*chip, self.core), self.sibling).wait_send()
        self._own().wait()


def _scatter_sems(n):
    return [pltpu.SemaphoreType.DMA((7 * n,)), pltpu.SemaphoreType.DMA((7 * n,)), pltpu.SemaphoreType.DMA((n,))]


class _Scatter:
    def __init__(self, srcs, dsts, sems):
        send_sems, recv_sems, local_sems = sems
        me = _my_index()
        self.copies = []
        for a, (src, dst) in enumerate(zip(srcs, dsts)):
            r = dst.shape[2]
            self.copies.append(pltpu.make_async_copy(src.at[:, pl.ds(me * r, r), :], dst.at[me], local_sems.at[a]))
            for k in range(1, NDEV):
                dev, p = _peer(k)
                self.copies.append(pltpu.make_async_remote_copy(
                    src_ref=src.at[:, pl.ds(p * r, r), :], dst_ref=dst.at[me],
                    send_sem=send_sems.at[a * 7 + k - 1], recv_sem=recv_sems.at[a * 7 + k - 1],
                    device_id=dev, device_id_type=MESH))

    def start(self):
        for cp in self.copies:
            cp.start()

    def forward(self):
        pass

    def finish(self):
        for cp in self.copies:
            cp.wait()


def _land_shape(part):
    a, r, c = part.shape
    return jax.ShapeDtypeStruct((NDEV, a, r // NDEV, c), part.dtype)


ANY = pl.BlockSpec(memory_space=pl.ANY)


class _Ride:
    def __init__(self, kind, srcs):
        self.kind, self.srcs = kind, list(srcs)
        if kind == "gather":
            self.out_shape = [jax.ShapeDtypeStruct((NDEV,) + a.shape, a.dtype) for a in self.srcs]
            self.sems = _gather_sems()
        else:
            self.out_shape = [_land_shape(a) for a in self.srcs]
            self.sems = _scatter_sems(len(self.srcs))

    def exchange(self, ins, outs, sems):
        return (_Gather if self.kind == "gather" else _Scatter)(ins, outs, sems)


def _small_exchange(small, ride, name):
    r, c = small.shape
    nr = len(ride.srcs)

    def body(v_ref, *refs):
        rin, refs = refs[:nr], refs[nr:]
        out_ref, refs = refs[0], refs[1:]
        rout, refs = refs[:nr], refs[nr:]
        send_sems, recv_sems, rsems = refs[0], refs[1], refs[2:]
        big = ride.exchange(rin, rout, rsems)
        big.start()
        copies = _small_pushes(v_ref, out_ref, send_sems, recv_sems)
        for cp in copies:
            cp.start()
        for cp in copies:
            cp.wait()
        big.forward()
        big.finish()

    res = pl.pallas_call(body, name=name,
        out_shape=[jax.ShapeDtypeStruct((NDEV, r, c), small.dtype)] + ride.out_shape,
        in_specs=[pl.BlockSpec(memory_space=pltpu.VMEM)] + [ANY] * nr,
        out_specs=[pl.BlockSpec(memory_space=pltpu.VMEM)] + [ANY] * nr,
        scratch_shapes=[pltpu.SemaphoreType.DMA((NDEV - 1,)), pltpu.SemaphoreType.DMA((NDEV - 1,))] + ride.sems,
    )(small, *ride.srcs)
    return res[0], res[1:]


def _small_pushes(src_ref, out_ref, send_sems, recv_sems):
    me = _my_index()
    out_ref[me] = src_ref[...]
    copies = []
    for k in range(1, NDEV):
        dev, _ = _peer(k)
        copies.append(pltpu.make_async_remote_copy(
            src_ref=src_ref, dst_ref=out_ref.at[me], send_sem=send_sems.at[k - 1], recv_sem=recv_sems.at[k - 1],
            device_id=dev, device_id_type=MESH))
    return copies


def _prologue(small, ada_w, bias, ride):
    r, c = small.shape
    nl, d, ncol = ada_w.shape
    nr = len(ride.srcs)

    def body(v_ref, w_ref, b_ref, *refs):
        rin, refs = refs[:nr], refs[nr:]
        out_ref, mod_ref, refs = refs[0], refs[1], refs[2:]
        rout, refs = refs[:nr], refs[nr:]
        cols, send1, recv1, send2, recv2, rsems = refs[0], refs[1], refs[2], refs[3], refs[4], refs[5:]
        big = ride.exchange(rin, rout, rsems)
        big.start()
        first = _small_pushes(v_ref, out_ref, send1, recv1)
        for cp in first:
            cp.start()
        for cp in first:
            cp.wait()
        for layer in range(nl):
            acc = jnp.zeros((NDEV, ncol), F32) + b_ref[layer]
            for j in range(d // c):
                cj = out_ref[:, j, :]
                acc = acc + jnp.dot(cj * jax.nn.sigmoid(cj), w_ref[layer, pl.ds(j * c, c), :],
                                    preferred_element_type=F32, precision=lax.Precision.HIGHEST)
            cols[pl.ds(layer * NDEV, NDEV), :] = acc
        second = _small_pushes(cols, mod_ref, send2, recv2)
        for cp in second:
            cp.start()
        for cp in second:
            cp.wait()
        big.forward()
        big.finish()

    vmem = pl.BlockSpec(memory_space=pltpu.VMEM)
    sem = pltpu.SemaphoreType.DMA((NDEV - 1,))
    res = pl.pallas_call(body, name="prologue",
        out_shape=[jax.ShapeDtypeStruct((NDEV, r, c), F32), jax.ShapeDtypeStruct((NDEV, nl * NDEV, ncol), F32)] + ride.out_shape,
        in_specs=[vmem, vmem, vmem] + [ANY] * nr, out_specs=[vmem, vmem] + [ANY] * nr,
        scratch_shapes=[pltpu.VMEM((nl * NDEV, ncol), F32), sem, sem, sem, sem] + ride.sems,
        compiler_params=pltpu.CompilerParams(vmem_limit_bytes=VMEM_LIMIT),
    )(small, ada_w, bias, *ride.srcs)
    return res[0], res[1], res[2:]


def _call(kernel_body, *, name, grid, in_specs, out_specs, out_shape, args, scratch_shapes=(), ride=None):
    n_in, n_out, n_sc = len(in_specs), len(out_specs), len(scratch_shapes)
    nr = len(ride.srcs) if ride else 0
    in_specs, out_specs, out_shape = list(in_specs), list(out_specs), list(out_shape)
    scratch_shapes, args = list(scratch_shapes), list(args)
    if ride is not None:
        in_specs += [ANY] * nr
        out_specs += [ANY] * nr
        out_shape += ride.out_shape
        args += ride.srcs
        scratch_shapes += ride.sems

    def body(*refs):
        ins, refs = refs[:n_in], refs[n_in:]
        rin, refs = refs[:nr], refs[nr:]
        outs, refs = refs[:n_out], refs[n_out:]
        rout, refs = refs[:nr], refs[nr:]
        scratch, rsems = refs[:n_sc], refs[n_sc:]
        first, last = True, True
        for axis, extent in enumerate(grid):
            first &= pl.program_id(axis) == 0
            last &= pl.program_id(axis) == extent - 1
        if ride is not None:
            middle = last if len(grid) > 1 else pl.program_id(0) == (3 * grid[0]) // 4
            exchange = ride.exchange(rin, rout, rsems)
            pl.when(first)(exchange.start)
            pl.when(middle)(exchange.forward)
        kernel_body(*ins, *outs, *scratch)
        if ride is not None:
            pl.when(last)(exchange.finish)

    res = pl.pallas_call(body, name=name, grid=grid, in_specs=in_specs, out_specs=out_specs, out_shape=out_shape,
                         scratch_shapes=scratch_shapes, compiler_params=_params(*(("arbitrary",) * len(grid))))(*args)
    return res[:n_out], res[n_out:]


def _ada_update(c_all, dmod, w, m, v):
    nl, d, ncol = w.shape
    bd = min(d, 256)

    def body(c_ref, dm_ref, w_ref, m_ref, v_ref, g_ref, d_ref, mo_ref, vo_ref):
        cv = c_ref[...]
        g = lax.dot_general(cv * jax.nn.sigmoid(cv), dm_ref[0], (((0,), (0,)), ((), ())), preferred_element_type=F32,
                            precision=lax.Precision.HIGHEST)
        g_ref[0] = g
        d_ref[0], mo_ref[0], vo_ref[0] = _adam_math(w_ref[0], g, m_ref[0], v_ref[0])

    blk = pl.BlockSpec((1, bd, ncol), lambda i, j: (i, j, 0))
    return pl.pallas_call(body, name="ada_update", grid=(nl, d // bd),
        in_specs=[pl.BlockSpec((NDEV, bd), lambda i, j: (0, j)), pl.BlockSpec((1, NDEV, ncol), lambda i, j: (i, 0, 0)),
                  blk, blk, blk],
        out_specs=[blk] * 4, out_shape=[jax.ShapeDtypeStruct(w.shape, F32)] * 4,
        compiler_params=_params("arbitrary", "arbitrary"),
    )(c_all, dmod, w, m, v)


def _pack_stats(st_c, st_m, st_f0, st_p, st_f1, st_loss, gwdw):
    d = st_m.shape[1]

    def body(c_ref, m_ref, f0_ref, p_ref, f1_ref, l_ref, w_ref, o_ref):
        pieces = [
            c_ref[1:3, 0:d], m_ref[0:1, :], f0_ref[0:3, :], p_ref[0:2, :], p_ref[4:5, :], f1_ref[0:3, :],
            c_ref[3:4, 0:d], p_ref[3:4, :], f0_ref[3:4, :], f1_ref[3:4, :],
            c_ref[0:1, 0:d], c_ref[0:1, d:2 * d], m_ref[4:5, :], m_ref[2:4, :], m_ref[1:2, :], l_ref[0:1, :],
            w_ref[0:CONV_WIDTH, :], p_ref[5:6, :], l_ref[2:3, :]]
        row = 0
        for piece in pieces:
            o_ref[pl.ds(row, piece.shape[0]), :] = piece
            row += piece.shape[0]

    return pl.pallas_call(body, name="pack_stats",
                          out_shape=jax.ShapeDtypeStruct((12 + 4 + 7 + CONV_WIDTH + 2, d), F32),
                          )(st_c, st_m, st_f0, st_p, st_f1, st_loss, gwdw)


def _row_block(r, c, bytes_per_row_elem=4, budget=2 * 2**20):
    if r * c * bytes_per_row_elem <= budget or r % 8:
        return r
    best = 8
    for b in range(8, r + 1, 8):
        if r % b == 0 and b * c * bytes_per_row_elem <= budget:
            best = b
    return best


def _sum_slots(land, name):
    _, r, c = land.shape
    br = _row_block(r, c, 8 * land.dtype.itemsize)

    def body(l_ref, o_ref):
        acc = l_ref[0].astype(F32)
        for s in range(1, NDEV):
            acc = acc + l_ref[s].astype(F32)
        o_ref[...] = acc

    return pl.pallas_call(body, name=name, grid=(r // br,),
        in_specs=[pl.BlockSpec((NDEV, br, c), lambda i: (0, i, 0))],
        out_specs=pl.BlockSpec((br, c), lambda i: (i, 0)),
        out_shape=jax.ShapeDtypeStruct((r, c), F32),
        compiler_params=_params("arbitrary"),
    )(land)


def _adamw(w, g, m, v, name):
    r, c = w.shape
    br = _row_block(r, c)

    def body(w_ref, g_ref, m_ref, v_ref, d_ref, mo_ref, vo_ref):
        d_ref[...], mo_ref[...], vo_ref[...] = _adam_math(w_ref[...], g_ref[...], m_ref[...], v_ref[...])

    spec = pl.BlockSpec((br, c), lambda i: (i, 0))
    return pl.pallas_call(body, name=name, grid=(r // br,),
        in_specs=[spec] * 4, out_specs=[spec] * 3,
        out_shape=[jax.ShapeDtypeStruct((r, c), F32)] * 3,
        compiler_params=_params("arbitrary"),
    )(w, g, m, v)


def _adam_math(w, g, m, v):
    m2 = ADAM_B1 * m + (1.0 - ADAM_B1) * g
    v2 = ADAM_B2 * v + (1.0 - ADAM_B2) * (g * g)
    m_hat = m2 / (1.0 - ADAM_B1 ** ADAM_STEP)
    v_hat = v2 / (1.0 - ADAM_B2 ** ADAM_STEP)
    return -ADAM_LR * (m_hat / (jnp.sqrt(v_hat) + ADAM_EPS) + ADAM_WD * w), m2, v2


def _slot_sum(land_ref, *lead):
    acc = land_ref[(0,) + lead].astype(F32)
    for s in range(1, NDEV):
        acc = acc + land_ref[(s,) + lead].astype(F32)
    return acc


def _finalize(land, w, m, v, transposed, name):
    _, r, c = land.shape
    cb = 256 if (transposed and c % 256 == 0) else c
    wblk = pl.BlockSpec((cb, r), lambda i: (i, 0)) if transposed else pl.BlockSpec((r, cb), lambda i: (0, i))

    def body(l_ref, w_ref, m_ref, v_ref, g_ref, d_ref, mo_ref, vo_ref):
        g = _slot_sum(l_ref)
        g = g.T if transposed else g
        g_ref[...] = g
        d_ref[...], mo_ref[...], vo_ref[...] = _adam_math(w_ref[...], g, m_ref[...], v_ref[...])

    return pl.pallas_call(body, name=name, grid=(c // cb,),
        in_specs=[pl.BlockSpec((NDEV, r, cb), lambda i: (0, 0, i)), wblk, wblk, wblk], out_specs=[wblk] * 4,
        out_shape=[jax.ShapeDtypeStruct(w.shape, F32)] * 4,
        compiler_params=_params("arbitrary"),
    )(land, w, m, v)


def _finalize_ffn(land0, land1, wg, wu, wd, mg, mu, md, vg, vu, vd):
    nl, fs, d = wg.shape
    db = min(256, d)

    def kernel_body(l0_ref, l1_ref, wg_ref, wu_ref, wd_ref, mg_ref, mu_ref, md_ref, vg_ref, vu_ref, vd_ref, *outs):
        layer = pl.program_id(0)
        triples = [(wg_ref, mg_ref, vg_ref), (wu_ref, mu_ref, vu_ref), (wd_ref, md_ref, vd_ref)]

        def run(land_ref):
            for j, (w_ref, m_ref, v_ref) in enumerate(triples):
                g = _slot_sum(land_ref, j)
                delta, m2, v2 = _adam_math(w_ref[0], g, m_ref[0], v_ref[0])
                for o_ref, val in zip(outs[j::3], (g, delta, m2, v2)):
                    o_ref[0] = val

        @pl.when(layer == 0)
        def _():
            run(l0_ref)

        @pl.when(layer == 1)
        def _():
            run(l1_ref)

    blk = pl.BlockSpec((1, fs, db), lambda l, i: (l, 0, i))
    lblk = [pl.BlockSpec((NDEV, 3, fs, db), lambda l, i: (0, 0, 0, i * (1 - l))),
            pl.BlockSpec((NDEV, 3, fs, db), lambda l, i: (0, 0, 0, i * l))]
    outs, _ = _call(kernel_body, name="finalize_ffn", grid=(nl, d // db),
        in_specs=lblk + [blk] * 9, out_specs=[blk] * 12, out_shape=[jax.ShapeDtypeStruct(wg.shape, F32)] * 12,
        args=(land0, land1, wg, wu, wd, mg, mu, md, vg, vu, vd))
    return outs


CONV_ROWS = 32
LANES = 128


def _shifted_copies(buf, sh, n):
    sh[0] = buf[...]
    for r in range(1, 8):
        sh[r, pl.ds(0, n - 8), :] = buf[pl.ds(r, n - 8), :]


def _window(sh, o, rows, cols):
    return sh[o % 8, pl.ds(o - o % 8, rows), cols]


def _conv_in(x, mod, gmix, gw, w1_rows, w1_idx, b1, tm, ride):
    s, d = x.shape

    def kernel_body(x_ref, mod_ref, g_ref, w_ref, b_ref, h_ref, u_ref, glu_ref):
        n, _ = _rms(x_ref[...])
        h = (n * g_ref[...]) * (1.0 + mod_ref[1:2, :]) + mod_ref[0:1, :]
        hb = h.astype(BF16)
        h_ref[...] = hb
        u = _nt(hb, w_ref[...].reshape(NDEV * w1_rows, d)) + b_ref[...]
        u_ref[...] = u.astype(BF16)
        glu_ref[...] = (u[:, :d] * jax.nn.sigmoid(u[:, d:])).astype(BF16)

    tile = pl.BlockSpec((tm, d), lambda i: (i, 0))
    return _call(kernel_body, name="conv_in", grid=(s // tm,),
        in_specs=[tile, _row(8, d), _row(1, d), _weight_spec(w1_rows, d, w1_idx), _row(1, 2 * d)],
        out_specs=[tile, pl.BlockSpec((tm, 2 * d), lambda i: (i, 0)), tile],
        out_shape=[jax.ShapeDtypeStruct((s, d), BF16), jax.ShapeDtypeStruct((s, 2 * d), BF16),
                   jax.ShapeDtypeStruct((s, d), BF16)],
        args=(x, mod, gmix, gw, b1), ride=ride)


def _conv_mid(glu, wdw, bdw, ln_g, ln_b, gw, w2_rows, w2_idx, b2, x, mod, tm, ride):
    s, d = x.shape
    off = CONV_HALO - (CONV_WIDTH - 1)
    rc = min(CONV_ROWS, tm)
    nsub = 2 if s % (2 * tm) == 0 else 1
    tb = nsub * tm

    def kernel_body(glu_ref, halo_ref, wdw_ref, bdw_ref, lng_ref, lnb_ref, w_ref, b2_ref, x_ref, mod_ref,
                    dwc_ref, s_ref, y_ref, x1_ref, buf, sh):
        i = pl.program_id(0)
        w2 = w_ref[...].reshape(NDEV * w2_rows, d)
        for sub in range(nsub):
            rows = pl.ds(sub * tm, tm)
            if sub == 0:
                buf[pl.ds(0, CONV_HALO), :] = jnp.where(i > 0, halo_ref[...].astype(F32), 0.0)
            else:
                buf[pl.ds(0, CONV_HALO), :] = glu_ref[pl.ds(sub * tm - CONV_HALO, CONV_HALO), :].astype(F32)
            buf[pl.ds(CONV_HALO, tm), :] = glu_ref[rows, :].astype(F32)
            _shifted_copies(buf, sh, CONV_HALO + tm)
            for cb in range(d // LANES):
                cols = pl.ds(cb * LANES, LANES)
                taps = wdw_ref[:, cols]
                for r in range(tm // rc):
                    part = jnp.zeros((rc, LANES), F32) + bdw_ref[:, cols]
                    for k in range(CONV_WIDTH):
                        part = part + _window(sh, r * rc + off + k, rc, cols) * taps[k:k + 1, :]
                    dwc_ref[pl.ds(sub * tm + r * rc, rc), cols] = part
            acc = dwc_ref[rows, :]
            mu = jnp.mean(acc, axis=-1, keepdims=True)
            xc = acc - mu
            rstd = lax.rsqrt(jnp.mean(xc * xc, axis=-1, keepdims=True) + EPS)
            ln = (xc * rstd) * lng_ref[...] + lnb_ref[...]
            sb = (ln * jax.nn.sigmoid(ln)).astype(BF16)
            s_ref[rows, :] = sb
            y = _nn(sb, w2) + b2_ref[...]
            y_ref[rows, :] = y.astype(BF16)
            x1_ref[rows, :] = x_ref[rows, :] + (1.0 + mod_ref[2:3, :]) * y

    tile = pl.BlockSpec((tb, d), lambda i: (i, 0))
    halo = pl.BlockSpec((CONV_HALO, d), lambda i: (jnp.maximum(i * (tb // CONV_HALO) - 1, 0), 0))
    return _call(kernel_body, name="conv_mid", grid=(s // tb,),
        in_specs=[tile, halo, _row(CONV_WIDTH, d), _row(1, d), _row(1, d), _row(1, d),
                  _weight_spec(w2_rows, d, w2_idx), _row(1, d), tile, _row(8, d)],
        out_specs=[tile, tile, tile, tile],
        out_shape=[jax.ShapeDtypeStruct((s, d), F32), jax.ShapeDtypeStruct((s, d), BF16),
                   jax.ShapeDtypeStruct((s, d), BF16), jax.ShapeDtypeStruct((s, d), F32)],
        scratch_shapes=[pltpu.VMEM((CONV_HALO + tm, d), F32), pltpu.VMEM((8, CONV_HALO + tm, d), F32)],
        args=(glu, glu, wdw, bdw, ln_g, ln_b, gw, b2, x, mod), ride=ride)


def _ffn_fwd(x, mod, gffn, weights, fs, f, tm, name, ride=None, loss=None):
    s, d = x.shape

    last = s // tm - 1

    def kernel_body(x_ref, mod_ref, g_ref, wg_ref, wu_ref, wd_ref, *rest):
        xv = x_ref[...]
        n, _ = _rms(xv)
        hb = ((n * g_ref[...]) * (1.0 + mod_ref[4:5, :]) + mod_ref[3:4, :]).astype(BF16)
        gg = _nt(hb, wg_ref[...].reshape(f, d))
        uu = _nt(hb, wu_ref[...].reshape(f, d))
        ab = ((gg * jax.nn.sigmoid(gg)) * uu).astype(BF16)
        y = _nn(ab, wd_ref[...].reshape(f, d))
        xo = xv + (1.0 + mod_ref[5:6, :]) * y
        if loss is None:
            h_ref, gg_ref, uu_ref, y_ref, xo_ref = rest
            xo_ref[...] = xo
        else:
            t_ref, gfin_ref, h_ref, gg_ref, uu_ref, y_ref, dx_ref, st_ref = rest
            dx_ref[...] = _loss_tile(xo, t_ref[...], gfin_ref[...], st_ref, pl.program_id(0), last)
        h_ref[...] = hb
        gg_ref[...] = gg.astype(BF16)
        uu_ref[...] = uu.astype(BF16)
        y_ref[...] = y.astype(BF16)

    tile = pl.BlockSpec((tm, d), lambda i: (i, 0))
    wide = pl.BlockSpec((tm, f), lambda i: (i, 0))
    extra_in = [] if loss is None else [tile, _row(1, d)]
    extra_out = [] if loss is None else [_row(8, d)]
    return _call(kernel_body, name=name, grid=(s // tm,),
        in_specs=[tile, _row(8, d), _row(1, d)] + [_weight_spec(fs, d, idx) for _, idx in weights] + extra_in,
        out_specs=[tile, wide, wide, tile, tile] + extra_out,
        out_shape=[jax.ShapeDtypeStruct((s, d), BF16), jax.ShapeDtypeStruct((s, f), BF16),
                   jax.ShapeDtypeStruct((s, f), BF16), jax.ShapeDtypeStruct((s, d), BF16),
                   jax.ShapeDtypeStruct((s, d), F32)] + [jax.ShapeDtypeStruct((8, d), F32)] * len(extra_out),
        args=(x, mod, gffn) + tuple(a for a, _ in weights) + (() if loss is None else tuple(loss)), ride=ride)


def _loss_tile(x, target, g, st_ref, i, last):
    d = x.shape[1]

    @pl.when(i == 0)
    def _():
        st_ref[...] = jnp.zeros_like(st_ref)

    n, rinv = _rms(x)
    err = n * g - target
    dy = err * (1.0 / d)
    st_ref[0:1, :] += _sum0(dy * n)
    st_ref[1:2, :] += _sum0(err * err) * (0.5 / d)

    @pl.when(i == last)
    def _():
        st_ref[2:3, :] = jnp.zeros((1, d), F32) + jnp.sum(st_ref[1:2, :], axis=-1, keepdims=True)

    dn = dy * g
    return rinv * (dn - n * jnp.mean(dn * n, axis=-1, keepdims=True))


def _pool_fwd(x, mod, gmix, pw, ls, tm, ride):
    s, d = x.shape
    dg = d // len(POOL_WINDOWS)

    def body(x_ref, halo_ref, mod_ref, g_ref, pw_ref, ls_ref, mixed_ref, yp_ref, xo_ref, buf):
        i = pl.program_id(0)

        def hfun(xv):
            n, _ = _rms(xv)
            return (n * g_ref[...]) * (1.0 + mod_ref[1:2, :]) + mod_ref[0:1, :]

        xv = x_ref[...]
        h = hfun(xv)
        buf[pl.ds(0, POOL_HALO), :] = jnp.where(i > 0, hfun(halo_ref[...]), 0.0)
        buf[pl.ds(POOL_HALO, tm), :] = h
        t = i * tm + lax.broadcasted_iota(jnp.int32, (tm, 1), 0)
        gate = 1.0 + mod_ref[2:3, :]
        for gi, w in enumerate(POOL_WINDOWS):
            cols = pl.ds(gi * dg, dg)
            ws = buf[pl.ds(POOL_HALO, tm), cols]
            for j in range(1, w):
                ws = ws + buf[pl.ds(POOL_HALO - j, tm), cols]
            inv = 1.0 / jnp.minimum(t + 1, w).astype(F32)
            mb = (ws * inv - h[:, gi * dg:(gi + 1) * dg]).astype(BF16)
            mixed_ref[:, cols] = mb
            yp = _nn(mb, pw_ref[gi])
            yp_ref[:, cols] = yp.astype(BF16)
            xo_ref[:, cols] = xv[:, gi * dg:(gi + 1) * dg] + gate[:, gi * dg:(gi + 1) * dg] * (yp * ls_ref[:, cols])

    tile = pl.BlockSpec((tm, d), lambda i: (i, 0))
    halo = pl.BlockSpec((POOL_HALO, d), lambda i: (jnp.maximum(i * (tm // POOL_HALO) - 1, 0), 0))
    return _call(body, name="pool_fwd", grid=(s // tm,),
        in_specs=[tile, halo, _row(8, d), _row(1, d), pl.BlockSpec((len(POOL_WINDOWS), dg, dg), lambda i: (0, 0, 0)),
                  _row(1, d)],
        out_specs=[tile, tile, tile],
        out_shape=[jax.ShapeDtypeStruct((s, d), BF16), jax.ShapeDtypeStruct((s, d), BF16),
                   jax.ShapeDtypeStruct((s, d), F32)],
        scratch_shapes=[pltpu.VMEM((POOL_HALO + tm, d), F32)],
        args=(x, x, mod, gmix, pw, ls), ride=ride)


def _ffn_bwd(dxo, x, gg, uu, y, mod, gffn, weights, fs, f, tm, name, ride=None):
    s, d = x.shape

    def kernel_body(dxo_ref, x_ref, gg_ref, uu_ref, y_ref, mod_ref, g_ref, wg_ref, wu_ref, wd_ref,
                    dg_ref, du_ref, a_ref, dy_ref, dxi_ref, st_ref):
        @pl.when(pl.program_id(0) == 0)
        def _():
            st_ref[...] = jnp.zeros_like(st_ref)

        dxo_v = dxo_ref[...]
        dyb = (dxo_v * (1.0 + mod_ref[5:6, :])).astype(BF16)
        dy_ref[...] = dyb
        da = _nt(dyb, wd_ref[...].reshape(f, d))
        ggv, uuv = gg_ref[...].astype(F32), uu_ref[...].astype(F32)
        sg = jax.nn.sigmoid(ggv)
        silu = ggv * sg
        a_ref[...] = (silu * uuv).astype(BF16)
        dub = (da * silu).astype(BF16)
        dgb = (da * uuv * _silu_grad(ggv, sg)).astype(BF16)
        du_ref[...] = dub
        dg_ref[...] = dgb
        dh = _nn(dgb, wg_ref[...].reshape(f, d)) + _nn(dub, wu_ref[...].reshape(f, d))
        n, rinv = _rms(x_ref[...])
        dx, dsh, dsc, dgain = _rms_mod_bwd(dh, n, rinv, g_ref[...], mod_ref[4:5, :])
        dxi_ref[...] = dxo_v + dx
        st_ref[0:1, :] += dsh
        st_ref[1:2, :] += dsc
        st_ref[2:3, :] += _sum0(dxo_v * y_ref[...].astype(F32))
        st_ref[3:4, :] += dgain

    tile = pl.BlockSpec((tm, d), lambda i: (i, 0))
    wide = pl.BlockSpec((tm, f), lambda i: (i, 0))
    return _call(kernel_body, name=name, grid=(s // tm,),
        in_specs=[tile, tile, wide, wide, tile, _row(8, d), _row(1, d)] + [_weight_spec(fs, d, idx) for _, idx in weights],
        out_specs=[wide, wide, wide, tile, tile, _row(8, d)],
        out_shape=[jax.ShapeDtypeStruct((s, f), BF16)] * 3 + [jax.ShapeDtypeStruct((s, d), BF16),
                   jax.ShapeDtypeStruct((s, d), F32), jax.ShapeDtypeStruct((8, d), F32)],
        args=(dxo, x, gg, uu, y, mod, gffn) + tuple(a for a, _ in weights), ride=ride)


def _ffn_wgrad(dgb, dub, ab, h, dyb, fb, ts, name):
    s, f = dgb.shape
    d = h.shape[1]
    last = s // ts - 1

    def body(dg_ref, du_ref, a_ref, h_ref, dy_ref, o_ref, acc):
        t = pl.program_id(1)

        @pl.when(t == 0)
        def _():
            acc[...] = jnp.zeros_like(acc)

        hv = h_ref[...]
        acc[0] += _tn(dg_ref[...], hv)
        acc[1] += _tn(du_ref[...], hv)
        acc[2] += _tn(a_ref[...], dy_ref[...])

        @pl.when(t == last)
        def _():
            o_ref[...] = acc[...].astype(BF16)

    wide = pl.BlockSpec((ts, fb), lambda j, t: (t, j))
    tile = pl.BlockSpec((ts, d), lambda j, t: (t, 0))
    return pl.pallas_call(body, name=name, grid=(f // fb, s // ts),
        in_specs=[wide, wide, wide, tile, tile],
        out_specs=pl.BlockSpec((3, fb, d), lambda j, t: (0, j, 0)),
        out_shape=jax.ShapeDtypeStruct((3, f, d), BF16),
        scratch_shapes=[pltpu.VMEM((3, fb, d), F32)],
        compiler_params=_params("arbitrary", "arbitrary"),
    )(dgb, dub, ab, h, dyb)


def _pool_bwd(dxo, x, yp, mixed, mod, gmix, pw, ls, tm):
    s, d = x.shape
    ng = len(POOL_WINDOWS)
    dg = d // ng
    last = s // tm - 1

    def body(dxo_ref, dxh_ref, x_ref, yp_ref, mixed_ref, mod_ref, g_ref, pw_ref, ls_ref,
             dxi_ref, dpw_ref, st_ref, bufy, bufq, bufh, acc):
        i = pl.program_id(0)

        @pl.when(i == 0)
        def _():
            st_ref[...] = jnp.zeros_like(st_ref)
            acc[...] = jnp.zeros_like(acc)

        gate = 1.0 + mod_ref[2:3, :]
        lsv = ls_ref[...]
        dxo_v = dxo_ref[...]
        st_ref[2:3, :] += _sum0(dxo_v * yp_ref[...].astype(F32))
        bufy[pl.ds(0, tm), :] = (dxo_v * (gate * lsv)).astype(BF16)
        bufy[pl.ds(tm, POOL_HALO), :] = jnp.where(i < last, dxh_ref[...] * (gate * lsv), 0.0).astype(BF16)
        t = i * tm + lax.broadcasted_iota(jnp.int32, (tm + POOL_HALO, 1), 0)
        for gi, w in enumerate(POOL_WINDOWS):
            cols = pl.ds(gi * dg, dg)
            dm = _nt(bufy[:, cols], pw_ref[gi])
            bufq[:, cols] = dm * (1.0 / jnp.minimum(t + 1, w).astype(F32))
            dh = bufq[pl.ds(0, tm), cols] - dm[0:tm, :]
            for j in range(1, w):
                dh = dh + bufq[pl.ds(j, tm), cols]
            bufh[:, cols] = dh
            acc[gi] += _tn(mixed_ref[:, cols], bufy[pl.ds(0, tm), cols])
        n, rinv = _rms(x_ref[...])
        dx, dsh, dsc, dgain = _rms_mod_bwd(bufh[...], n, rinv, g_ref[...], mod_ref[1:2, :])
        dxi_ref[...] = dxo_v + dx
        st_ref[0:1, :] += dsh
        st_ref[1:2, :] += dsc
        st_ref[3:4, :] += dgain

        @pl.when(i == last)
        def _():
            r = st_ref[2:3, :]
            st_ref[4:5, :] = r * lsv
            st_ref[5:6, :] = r * gate
            dpw_ref[...] = acc[...].astype(BF16)

    tile = pl.BlockSpec((tm, d), lambda i: (i, 0))
    nxt = pl.BlockSpec((POOL_HALO, d), lambda i: (jnp.minimum((i + 1) * (tm // POOL_HALO), s // POOL_HALO - 1), 0))
    pws = pl.BlockSpec((ng, dg, dg), lambda i: (0, 0, 0))
    return _call(body, name="pool_bwd", grid=(s // tm,),
        in_specs=[tile, nxt, tile, tile, tile, _row(8, d), _row(1, d), pws, _row(1, d)],
        out_specs=[tile, pws, _row(8, d)],
        out_shape=[jax.ShapeDtypeStruct((s, d), F32), jax.ShapeDtypeStruct((ng, dg, dg), BF16),
                   jax.ShapeDtypeStruct((8, d), F32)],
        scratch_shapes=[pltpu.VMEM((tm + POOL_HALO, d), BF16), pltpu.VMEM((tm + POOL_HALO, d), F32),
                        pltpu.VMEM((tm, d), F32), pltpu.VMEM((ng, dg, dg), F32)],
        args=(dxo, dxo, x, yp, mixed, mod, gmix, pw, ls))[0]


def _conv_bwd_mid(dxo, y, dwc, sb, mod, ln_g, ln_b, gw, w2_rows, w2_idx, tm):
    s, d = dwc.shape
    last = s // tm - 1

    def body(dxo_ref, y_ref, dwc_ref, s_ref, mod_ref, lng_ref, lnb_ref, w_ref, dd_ref, dw_ref, st_ref, acc):
        i = pl.program_id(0)

        @pl.when(i == 0)
        def _():
            st_ref[...] = jnp.zeros_like(st_ref)
            acc[...] = jnp.zeros_like(acc)

        dxo_v = dxo_ref[...]
        st_ref[0:1, :] += _sum0(dxo_v * y_ref[...].astype(F32))
        dy = dxo_v * (1.0 + mod_ref[2:3, :])
        st_ref[1:2, :] += _sum0(dy)
        dyb = dy.astype(BF16)
        ds = _nt(dyb, w_ref[...].reshape(NDEV * w2_rows, d))
        acc[...] += _tn(s_ref[...], dyb)
        v = dwc_ref[...]
        mu = jnp.mean(v, axis=-1, keepdims=True)
        xc = v - mu
        rstd = lax.rsqrt(jnp.mean(xc * xc, axis=-1, keepdims=True) + EPS)
        xhat = xc * rstd
        ln = xhat * lng_ref[...] + lnb_ref[...]
        dln = ds * _silu_grad(ln, jax.nn.sigmoid(ln))
        st_ref[2:3, :] += _sum0(dln * xhat)
        st_ref[3:4, :] += _sum0(dln)
        dxh = dln * lng_ref[...]
        dd = rstd * (dxh - jnp.mean(dxh, axis=-1, keepdims=True) - xhat * jnp.mean(dxh * xhat, axis=-1, keepdims=True))
        dd_ref[...] = dd
        st_ref[4:5, :] += _sum0(dd)

        @pl.when(i == last)
        def _():
            dw_ref[...] = acc[...].astype(BF16)

    tile = pl.BlockSpec((tm, d), lambda i: (i, 0))
    return _call(body, name="conv_bwd_mid", grid=(s // tm,),
        in_specs=[tile, tile, tile, tile, _row(8, d), _row(1, d), _row(1, d), _weight_spec(w2_rows, d, w2_idx)],
        out_specs=[tile, pl.BlockSpec((d, d), lambda i: (0, 0)), _row(8, d)],
        out_shape=[jax.ShapeDtypeStruct((s, d), F32), jax.ShapeDtypeStruct((d, d), BF16),
                   jax.ShapeDtypeStruct((8, d), F32)],
        scratch_shapes=[pltpu.VMEM((d, d), F32)],
        args=(dxo, y, dwc, sb, mod, ln_g, ln_b, gw))[0]


def _conv_bwd_in(dd, glu, u, hb, x, dxo, wdw, gw, w1_rows, w1_idx, mod, gmix, tm, nsub, ride):
    s, d = x.shape
    rw = 32
    tap_group = 16
    tb = nsub * tm
    last = s // tb - 1

    def kernel_body(dd_ref, ddn_ref, glu_ref, u_ref, h_ref, x_ref, dxo_ref, wdw_ref, w_ref, mod_ref, g_ref,
                    dxi_ref, dw_ref, dwdw_ref, st_ref, bufd, shd, dgl, accw, acc):
        i = pl.program_id(0)

        @pl.when(i == 0)
        def _():
            st_ref[...] = jnp.zeros_like(st_ref)
            accw[...] = jnp.zeros_like(accw)
            acc[...] = jnp.zeros_like(acc)

        w1t = w_ref[...].reshape(NDEV * w1_rows, d)
        for sub in range(nsub):
            base = sub * tm
            tile_rows = pl.ds(base, tm)
            bufd[pl.ds(0, tm), :] = dd_ref[tile_rows, :]
            if sub == nsub - 1:
                bufd[pl.ds(tm, CONV_HALO), :] = jnp.where(i < last, ddn_ref[...], 0.0)
            else:
                bufd[pl.ds(tm, CONV_HALO), :] = dd_ref[pl.ds(base + tm, CONV_HALO), :]
            _shifted_copies(bufd, shd, tm + CONV_HALO)
            for cb in range(d // LANES):
                cols = pl.ds(cb * LANES, LANES)
                taps = wdw_ref[:, cols]
                for r in range(tm // rw):
                    part = jnp.zeros((rw, LANES), F32)
                    for k in range(CONV_WIDTH):
                        part = part + _window(shd, r * rw + CONV_WIDTH - 1 - k, rw, cols) * taps[k:k + 1, :]
                    dgl[sub, pl.ds(r * rw, rw), cols] = part
                for k0 in range(0, CONV_WIDTH, tap_group):
                    group = range(k0, min(CONV_WIDTH, k0 + tap_group))
                    sums = {k: jnp.zeros((8, LANES), F32) for k in group}
                    for r in range(tm // rw):
                        gch = glu_ref[pl.ds(base + r * rw, rw), cols].astype(F32)
                        for k in group:
                            p = _window(shd, r * rw + CONV_WIDTH - 1 - k, rw, cols) * gch
                            for q in range(rw // 8):
                                sums[k] = sums[k] + p[q * 8:(q + 1) * 8, :]
                    for k in group:
                        accw[k, :, cols] += sums[k]
            dglu = dgl[sub]
            uv = u_ref[tile_rows, :].astype(F32)
            a, g = uv[:, :d], uv[:, d:]
            sg = jax.nn.sigmoid(g)
            du = jnp.concatenate([dglu * sg, dglu * a * (sg * (1.0 - sg))], axis=1)
            st_ref[0:1, :] += _sum0(du)
            dub = du.astype(BF16)
            dh = _nn(dub, w1t)
            acc[...] += _tn(dub, h_ref[tile_rows, :])
            n, rinv = _rms(x_ref[tile_rows, :])
            dx, dsh, dsc, dgain = _rms_mod_bwd(dh, n, rinv, g_ref[...], mod_ref[1:2, :])
            dxi_ref[tile_rows, :] = dxo_ref[tile_rows, :] + dx
            st_ref[1:2, 0:d] += dsh
            st_ref[2:3, 0:d] += dsc
            st_ref[3:4, 0:d] += dgain

        @pl.when(i == last)
        def _():
            dw_ref[...] = acc[...].astype(BF16)
            dwdw_ref[...] = jnp.sum(accw[...], axis=1)

    tile = pl.BlockSpec((tb, d), lambda i: (i, 0))
    nxt = pl.BlockSpec((CONV_HALO, d), lambda i: (jnp.minimum((i + 1) * (tb // CONV_HALO), s // CONV_HALO - 1), 0))
    return _call(kernel_body, name="conv_bwd_in", grid=(s // tb,),
        in_specs=[tile, nxt, tile, pl.BlockSpec((tb, 2 * d), lambda i: (i, 0)), tile, tile, tile,
                  _row(CONV_WIDTH, d), _weight_spec(w1_rows, d, w1_idx), _row(8, d), _row(1, d)],
        out_specs=[tile, pl.BlockSpec((2 * d, d), lambda i: (0, 0)), _row(CONV_HALO, d), _row(8, 2 * d)],
        out_shape=[jax.ShapeDtypeStruct((s, d), F32), jax.ShapeDtypeStruct((2 * d, d), BF16),
                   jax.ShapeDtypeStruct((CONV_HALO, d), F32), jax.ShapeDtypeStruct((8, 2 * d), F32)],
        scratch_shapes=[pltpu.VMEM((tm + CONV_HALO, d), F32), pltpu.VMEM((8, tm + CONV_HALO, d), F32),
                        pltpu.VMEM((nsub, tm, d), F32), pltpu.VMEM((CONV_HALO, 8, d), F32),
                        pltpu.VMEM((2 * d, d), F32)],
        args=(dd, dd, glu, u, hb, x, dxo, wdw, gw, mod, gmix), ride=ride)


def kernel(x, c, ada_w, ada_b, norm_mix_g, norm_ffn_g, conv_w1, conv_b1, conv_wdw, conv_bdw, conv_ln_g, conv_ln_b, conv_w2, conv_b2, pool_w, pool_ls, ffn_w_gate, ffn_w_up, ffn_w_down, final_g, loss_target, m_ada_w, m_ada_b, m_norm_mix_g, m_norm_ffn_g, m_conv_w1, m_conv_b1, m_conv_wdw, m_conv_bdw, m_conv_ln_g, m_conv_ln_b, m_conv_w2, m_conv_b2, m_pool_w, m_pool_ls, m_ffn_w_gate, m_ffn_w_up, m_ffn_w_down, m_final_g, v_ada_w, v_ada_b, v_norm_mix_g, v_norm_ffn_g, v_conv_w1, v_conv_b1, v_conv_wdw, v_conv_bdw, v_conv_ln_g, v_conv_ln_b, v_conv_w2, v_conv_b2, v_pool_w, v_pool_ls, v_ffn_w_gate, v_ffn_w_up, v_ffn_w_down, v_final_g):
    _, s, d = x.shape
    f = ffn_w_down.shape[1] * NDEV
    fs = f // NDEV
    r1, r2 = 2 * d // NDEV, d // NDEV
    ng = len(POOL_WINDOWS)
    dg = d // ng
    pr = ng * (dg // NDEV) * dg // d
    ncol = ada_w.shape[2]
    dc = d // NDEV
    tm = min(256, s)
    me = _my_index()
    x0 = x.reshape(s, d)
    target = loss_target.reshape(s, d)

    small = jnp.concatenate([c.reshape(NDEV, dc), conv_wdw[0], pool_ls], axis=0)
    shard_a = conv_w1[0].T.astype(BF16)
    bias = lax.dynamic_slice_in_dim(ada_b, me * ncol, ncol, axis=1)[:, None, :]
    small_all, mod_all, (gwa,) = _prologue(small, ada_w, bias, _Ride("gather", [shard_a]))
    c_all = small_all[:, 0:NDEV, :].reshape(NDEV, d)
    wdw = small_all[:, NDEV:NDEV + CONV_WIDTH, :].transpose(1, 0, 2).reshape(CONV_WIDTH, d)
    ls = small_all[:, NDEV + CONV_WIDTH, :].reshape(1, d)
    mod_mine = lax.dynamic_index_in_dim(mod_all.reshape(NDEV, 2, NDEV, ncol), me, axis=2, keepdims=False)
    mod = mod_mine.transpose(1, 0, 2).reshape(2, 6, d)
    mod = jnp.concatenate([mod, jnp.zeros((2, 2, d), F32)], axis=1)

    shard_b1 = jnp.concatenate([ffn_w_gate[0].T, conv_w2[0]], axis=0).astype(BF16)
    shard_b2 = jnp.concatenate([ffn_w_up[0].T, ffn_w_down[0]], axis=0).astype(BF16)
    shard_c = jnp.concatenate([ffn_w_gate[1].T, ffn_w_up[1].T, ffn_w_down[1], pool_w.reshape(pr, d)], axis=0).astype(BF16)
    w1_at = (gwa, 0)

    (h0, u, glu), (gwb1,) = _conv_in(x0, mod[0], norm_mix_g[0:1], w1_at[0], r1, w1_at[1], conv_b1, tm,
                                    _Ride("gather", [shard_b1]))
    w2_at = (gwb1[:, fs:fs + r2, :], 0)
    (dwc, sb, y0, x1), (gwb2,) = _conv_mid(glu, wdw, conv_bdw, conv_ln_g, conv_ln_b, w2_at[0], r2, w2_at[1], conv_b2,
                                           x0, mod[0], tm, _Ride("gather", [shard_b2]))
    ffn0_w = [(gwb1, 0), (gwb2, 0), (gwb2, 1)]
    (h1, gg0, uu0, yf0, x2), (gwc,) = _ffn_fwd(x1, mod[0], norm_ffn_g[0:1], ffn0_w, fs, f, tm, "ffn_fwd0",
                                               _Ride("gather", [shard_c]))
    pw = gwc[:, 3 * fs:3 * fs + pr, :].reshape(NDEV, ng, dg // NDEV, dg).transpose(1, 0, 2, 3).reshape(ng, dg, dg)
    (mixed, yp, x3), _ = _pool_fwd(x2, mod[1], norm_mix_g[1:2], pw, ls, tm, None)
    ffn1_w = [(gwc, 0), (gwc, 1), (gwc, 2)]
    (h3, gg1, uu1, yf1, dx4, st_loss), _ = _ffn_fwd(x3, mod[1], norm_ffn_g[1:2], ffn1_w, fs, f, tm, "ffn_fwd1",
                                                    loss=(target, final_g.reshape(1, d)))

    fb = f // 2 if (f // 2) % 128 == 0 else f
    ts = min(512, s)
    (dgb, dub, ab, dyb, dx3, st_f1), _ = _ffn_bwd(dx4, x3, gg1, uu1, yf1, mod[1], norm_ffn_g[1:2], ffn1_w, fs, f, tm,
                                                  "ffn_bwd1")
    gf1 = _ffn_wgrad(dgb, dub, ab, h3, dyb, fb, ts, "ffn_wgrad1")
    dx2, gpw, st_p = _pool_bwd(dx3, x2, yp, mixed, mod[1], norm_mix_g[1:2], pw, ls, tm)
    (dgb, dub, ab, dyb, dx1, st_f0), (land_f1,) = _ffn_bwd(dx2, x1, gg0, uu0, yf0, mod[0], norm_ffn_g[0:1], ffn0_w, fs, f,
                                                           tm, "ffn_bwd0", _Ride("scatter", [gf1]))
    gf0 = _ffn_wgrad(dgb, dub, ab, h1, dyb, fb, ts, "ffn_wgrad0")
    dd, gw2, st_m = _conv_bwd_mid(dx1, y0, dwc, sb, mod[0], conv_ln_g, conv_ln_b, w2_at[0], r2, w2_at[1], tm)
    (dx0, gw1, gwdw, st_c), (land_f0, land_pw, land_w2) = _conv_bwd_in(
        dd, glu, u, h0, x0, dx1, wdw, w1_at[0], r1, w1_at[1], mod[0], norm_mix_g[0:1], tm, 1,
        _Ride("scatter", [gf0, gpw, gw2[None]]))

    prow = _pack_stats(st_c, st_m, st_f0, st_p, st_f1, st_loss, gwdw)
    p_all, (land_w1,) = _small_exchange(prow, _Ride("scatter", [gw1[None]]), "allgather_stats")
    psum = _sum_slots(p_all, "sum_stats")
    loss = psum[prow.shape[0] - 1, 0]

    tr = lambda a: jnp.swapaxes(a, 1, 2)
    ffn_out = _finalize_ffn(land_f0, land_f1, tr(ffn_w_gate), tr(ffn_w_up), ffn_w_down,
                            tr(m_ffn_w_gate), tr(m_ffn_w_up), m_ffn_w_down, tr(v_ffn_w_gate), tr(v_ffn_w_up), v_ffn_w_down)
    fin_w1 = _finalize(land_w1.reshape(NDEV, r1, d), conv_w1[0], m_conv_w1[0], v_conv_w1[0], True, "finalize_w1")
    fin_w2 = _finalize(land_w2.reshape(NDEV, r2, d), conv_w2[0], m_conv_w2[0], v_conv_w2[0], False, "finalize_w2")
    pshape = (ng * (dg // NDEV), dg)
    fin_pw = _finalize(land_pw.reshape((NDEV,) + pshape), pool_w.reshape(pshape), m_pool_w.reshape(pshape),
                       v_pool_w.reshape(pshape), False, "finalize_pool_w")
    dmod_all = p_all[:, 0:12, :].reshape(NDEV, 2, 6 * d)
    dmod_cols = lax.dynamic_slice_in_dim(dmod_all, me * ncol, ncol, axis=2).transpose(1, 0, 2)
    fin_ada = _ada_update(c_all, dmod_cols, ada_w, m_ada_w, v_ada_w)

    rep_names = ["ada_b", "norm_mix_g", "norm_ffn_g", "conv_b1", "conv_bdw", "conv_ln_g", "conv_ln_b", "conv_b2", "final_g"]
    rep_w = [ada_b, norm_mix_g, norm_ffn_g, conv_b1, conv_bdw, conv_ln_g, conv_ln_b, conv_b2, final_g]
    rep_m = [m_ada_b, m_norm_mix_g, m_norm_ffn_g, m_conv_b1, m_conv_bdw, m_conv_ln_g, m_conv_ln_b, m_conv_b2, m_final_g]
    rep_v = [v_ada_b, v_norm_mix_g, v_norm_ffn_g, v_conv_b1, v_conv_bdw, v_conv_ln_g, v_conv_ln_b, v_conv_b2, v_final_g]
    nrep = sum(w.size for w in rep_w) // d
    pad = jnp.zeros(((-nrep) % 8, d), F32)

    def pack(arrs, fill):
        return jnp.concatenate([a.reshape(-1, d) for a in arrs] + [pad + fill], axis=0)

    rep_g = jnp.concatenate([psum[0:nrep], pad], axis=0)
    rep_d, rep_mo, rep_vo = _adamw(pack(rep_w, 0.0), rep_g, pack(rep_m, 0.0), pack(rep_v, 1.0), "adamw_replicated")

    def unpack(packed):
        out, cur = [], 0
        for w in rep_w:
            k = w.size // d
            out.append(packed[cur:cur + k].reshape(w.shape))
            cur += k
        return out

    rep = dict(zip(rep_names, zip(unpack(psum), unpack(rep_d), unpack(rep_mo), unpack(rep_vo))))

    g_wdw_full = psum[nrep:nrep + CONV_WIDTH]
    g_wdw = lax.dynamic_slice_in_dim(g_wdw_full, me * dc, dc, axis=1)
    g_ls = lax.dynamic_slice_in_dim(psum[nrep + CONV_WIDTH:nrep + CONV_WIDTH + 1], me * dc, dc, axis=1)
    tiny = lambda a, b: jnp.concatenate([a.reshape(CONV_WIDTH, dc), b.reshape(1, dc)], axis=0)
    t_d, t_m, t_v = _adamw(tiny(conv_wdw, pool_ls), tiny(g_wdw, g_ls), tiny(m_conv_wdw, m_pool_ls),
                           tiny(v_conv_wdw, v_pool_ls), "adamw_taps")

    def taps(a):
        return a[0:CONV_WIDTH][None], a[CONV_WIDTH:CONV_WIDTH + 1]

    sharded = {
        "ada_w": tuple(fin_ada),
        "conv_w1": tuple(a[None] for a in fin_w1),
        "conv_w2": tuple(a[None] for a in fin_w2),
        "pool_w": tuple(a.reshape(pool_w.shape) for a in fin_pw),
        "ffn_w_gate": tuple(tr(a) for a in ffn_out[0::3]),
        "ffn_w_up": tuple(tr(a) for a in ffn_out[1::3]),
        "ffn_w_down": tuple(ffn_out[2::3]),
        "conv_wdw": (g_wdw[None], taps(t_d)[0], taps(t_m)[0], taps(t_v)[0]),
        "pool_ls": (g_ls, taps(t_d)[1], taps(t_m)[1], taps(t_v)[1]),
    }
    every = {**rep, **sharded}
    order = ["ada_w", "ada_b", "norm_mix_g", "norm_ffn_g", "conv_w1", "conv_b1", "conv_wdw", "conv_bdw", "conv_ln_g",
             "conv_ln_b", "conv_w2", "conv_b2", "pool_w", "pool_ls", "ffn_w_gate", "ffn_w_up", "ffn_w_down", "final_g"]
    grads = [every[n][0] for n in order]
    deltas = [every[n][1] for n in order]
    new_m = [every[n][2] for n in order]
    new_v = [every[n][3] for n in order]
    return (loss, dx0.reshape(1, s, d), *grads, *deltas, *new_m, *new_v)
```

```python
import jax
import jax.numpy as jnp
from jax import lax
from jax.experimental import pallas as pl
from jax.experimental.pallas import tpu as pltpu

NDEV = 8
EPS = 1e-6
CONV_WIDTH = 31
POOL_WINDOWS = (2, 4, 8, 16)
CONV_HALO = 32
POOL_HALO = 16
ADAM_LR = 0.001
ADAM_B1 = 0.9
ADAM_B2 = 0.999
ADAM_EPS = 1e-08
ADAM_WD = 0.01
ADAM_STEP = 10
VMEM_LIMIT = 56 * 2**20
MESH = pl.DeviceIdType.MESH
F32 = jnp.float32
BF16 = jnp.bfloat16


def _nt(a, b):
    return lax.dot_general(a, b, (((1,), (1,)), ((), ())), preferred_element_type=F32)


def _nn(a, b):
    return lax.dot_general(a, b, (((1,), (0,)), ((), ())), preferred_element_type=F32)


def _tn(a, b):
    return lax.dot_general(a, b, (((0,), (0,)), ((), ())), preferred_element_type=F32)


def _sum0(v):
    return jnp.sum(v, axis=0, keepdims=True)


def _rms(x):
    rinv = lax.rsqrt(jnp.mean(x * x, axis=-1, keepdims=True) + EPS)
    return x * rinv, rinv


def _rms_mod_bwd(dh, n, rinv, g, sc):
    dhs = dh * (1.0 + sc)
    dn = dhs * g
    dx = rinv * (dn - n * jnp.mean(dn * n, axis=-1, keepdims=True))
    return dx, _sum0(dh), _sum0(dh * (n * g)), _sum0(dhs * n)


def _silu_grad(z, sg):
    return sg * (1.0 + z * (1.0 - sg))


def _params(*sem):
    return pltpu.CompilerParams(dimension_semantics=sem, vmem_limit_bytes=VMEM_LIMIT)


def _row(i, d):
    return pl.BlockSpec((i, d), lambda *_: (0, 0))


def _weight_spec(rows, d, idx):
    return pl.BlockSpec((NDEV, rows, d), lambda *_: (0, idx, 0), pipeline_mode=pl.Buffered(1))


def _my_index():
    return 4 * lax.axis_index("x") + 2 * lax.axis_index("y") + lax.axis_index("c")


def _peer(k):
    x, y, c = lax.axis_index("x"), lax.axis_index("y"), lax.axis_index("c")
    px = 1 - x if k & 4 else x
    py = 1 - y if k & 2 else y
    pc = 1 - c if k & 1 else c
    return (px, py, pc), 4 * px + 2 * py + pc


def _gather_sems():
    return [pltpu.SemaphoreType.DMA((NDEV - 1,)), pltpu.SemaphoreType.DMA((NDEV - 1,)), pltpu.SemaphoreType.DMA((1,))]


class _Gather:
    def __init__(self, srcs, dsts, sems):
        self.src, self.dst = srcs[0], dsts[0]
        self.send, self.recv, self.local = sems
        x, y, c = lax.axis_index("x"), lax.axis_index("y"), lax.axis_index("c")
        self.me, self.sibling, self.core = (x, y, c), (x, y, 1 - c), c
        self.chips = [(1 - x, y), (x, 1 - y), (1 - x, 1 - y)]

    def _copy(self, k, block, to, from_input=False):
        slot = self.dst.at[4 * block[0] + 2 * block[1] + block[2]]
        return pltpu.make_async_remote_copy(
            src_ref=self.src if from_input else slot, dst_ref=slot, send_sem=self.send.at[k], recv_sem=self.recv.at[k],
            device_id=to, device_id_type=MESH)

    def _own(self):
        return pltpu.make_async_copy(self.src, self.dst.at[_my_index()], self.local.at[0])

    def _first(self):
        return [self._copy(0, self.me, self.sibling, True)] + [
            self._copy(1 + j, self.me, (*chip, self.core), True) for j, chip in enumerate(self.chips)]

    def start(self):
        self._own().start()
        for cp in self._first():
            cp.start()

    def forward(self):
        for j, chip in enumerate(self.chips):
            self._copy(1 + j, (*chip, self.core), self.me).wait_recv()
            self._copy(4 + j, (*chip, self.core), self.sibling).start()

    def finish(self):
        self._copy(0, self.sibling, self.me).wait_recv()
        for j, chip in enumerate(self.chips):
            self._copy(4 + j, (*chip, 1 - self.core), self.me).wait_recv()
        for cp in self._first():
            cp.wait_send()
        for j, chip in enumerate(self.chips):
            self._copy(4 + j, (*chip, self.core), self.sibling).wait_send()
        self._own().wait()


def _scatter_sems(n):
    return [pltpu.SemaphoreType.DMA((7 * n,)), pltpu.SemaphoreType.DMA((7 * n,)), pltpu.SemaphoreType.DMA((n,))]


class _Scatter:
    def __init__(self, srcs, dsts, sems):
        send_sems, recv_sems, local_sems = sems
        me = _my_index()
        self.copies = []
        for a, (src, dst) in enumerate(zip(srcs, dsts)):
            r = dst.shape[2]
            self.copies.append(pltpu.make_async_copy(src.at[:, pl.ds(me * r, r), :], dst.at[me], local_sems.at[a]))
            for k in range(1, NDEV):
                dev, p = _peer(k)
                self.copies.append(pltpu.make_async_remote_copy(
                    src_ref=src.at[:, pl.ds(p * r, r), :], dst_ref=dst.at[me],
                    send_sem=send_sems.at[a * 7 + k - 1], recv_sem=recv_sems.at[a * 7 + k - 1],
                    device_id=dev, device_id_type=MESH))

    def start(self):
        for cp in self.copies:
            cp.start()

    def forward(self):
        pass

    def finish(self):
        for cp in self.copies:
            cp.wait()


def _land_shape(part):
    a, r, c = part.shape
    return jax.ShapeDtypeStruct((NDEV, a, r // NDEV, c), part.dtype)


ANY = pl.BlockSpec(memory_space=pl.ANY)


class _Ride:
    def __init__(self, kind, srcs):
        self.kind, self.srcs = kind, list(srcs)
        if kind == "gather":
            self.out_shape = [jax.ShapeDtypeStruct((NDEV,) + a.shape, a.dtype) for a in self.srcs]
            self.sems = _gather_sems()
        else:
            self.out_shape = [_land_shape(a) for a in self.srcs]
            self.sems = _scatter_sems(len(self.srcs))

    def exchange(self, ins, outs, sems):
        return (_Gather if self.kind == "gather" else _Scatter)(ins, outs, sems)


def _small_exchange(small, ride, name):
    r, c = small.shape
    nr = len(ride.srcs)

    def body(v_ref, *refs):
        rin, refs = refs[:nr], refs[nr:]
        out_ref, refs = refs[0], refs[1:]
        rout, refs = refs[:nr], refs[nr:]
        send_sems, recv_sems, rsems = refs[0], refs[1], refs[2:]
        big = ride.exchange(rin, rout, rsems)
        big.start()
        copies = _small_pushes(v_ref, out_ref, send_sems, recv_sems)
        for cp in copies:
            cp.start()
        for cp in copies:
            cp.wait()
        big.forward()
        big.finish()

    res = pl.pallas_call(body, name=name,
        out_shape=[jax.ShapeDtypeStruct((NDEV, r, c), small.dtype)] + ride.out_shape,
        in_specs=[pl.BlockSpec(memory_space=pltpu.VMEM)] + [ANY] * nr,
        out_specs=[pl.BlockSpec(memory_space=pltpu.VMEM)] + [ANY] * nr,
        scratch_shapes=[pltpu.SemaphoreType.DMA((NDEV - 1,)), pltpu.SemaphoreType.DMA((NDEV - 1,))] + ride.sems,
    )(small, *ride.srcs)
    return res[0], res[1:]


def _small_pushes(src_ref, out_ref, send_sems, recv_sems):
    me = _my_index()
    out_ref[me] = src_ref[...]
    copies = []
    for k in range(1, NDEV):
        dev, _ = _peer(k)
        copies.append(pltpu.make_async_remote_copy(
            src_ref=src_ref, dst_ref=out_ref.at[me], send_sem=send_sems.at[k - 1], recv_sem=recv_sems.at[k - 1],
            device_id=dev, device_id_type=MESH))
    return copies


def _prologue(small, ada_w, bias, ride):
    r, c = small.shape
    nl, d, ncol = ada_w.shape
    nr = len(ride.srcs)

    def body(v_ref, w_ref, b_ref, *refs):
        rin, refs = refs[:nr], refs[nr:]
        out_ref, mod_ref, refs = refs[0], refs[1], refs[2:]
        rout, refs = refs[:nr], refs[nr:]
        cols, send1, recv1, send2, recv2, rsems = refs[0], refs[1], refs[2], refs[3], refs[4], refs[5:]
        big = ride.exchange(rin, rout, rsems)
        big.start()
        first = _small_pushes(v_ref, out_ref, send1, recv1)
        for cp in first:
            cp.start()
        for cp in first:
            cp.wait()
        for layer in range(nl):
            acc = jnp.zeros((NDEV, ncol), F32) + b_ref[layer]
            for j in range(d // c):
                cj = out_ref[:, j, :]
                acc = acc + jnp.dot(cj * jax.nn.sigmoid(cj), w_ref[layer, pl.ds(j * c, c), :],
                                    preferred_element_type=F32, precision=lax.Precision.HIGHEST)
            cols[pl.ds(layer * NDEV, NDEV), :] = acc
        second = _small_pushes(cols, mod_ref, send2, recv2)
        for cp in second:
            cp.start()
        for cp in second:
            cp.wait()
        big.forward()
        big.finish()

    vmem = pl.BlockSpec(memory_space=pltpu.VMEM)
    sem = pltpu.SemaphoreType.DMA((NDEV - 1,))
    res = pl.pallas_call(body, name="prologue",
        out_shape=[jax.ShapeDtypeStruct((NDEV, r, c), F32), jax.ShapeDtypeStruct((NDEV, nl * NDEV, ncol), F32)] + ride.out_shape,
        in_specs=[vmem, vmem, vmem] + [ANY] * nr, out_specs=[vmem, vmem] + [ANY] * nr,
        scratch_shapes=[pltpu.VMEM((nl * NDEV, ncol), F32), sem, sem, sem, sem] + ride.sems,
        compiler_params=pltpu.CompilerParams(vmem_limit_bytes=VMEM_LIMIT),
    )(small, ada_w, bias, *ride.srcs)
    return res[0], res[1], res[2:]


def _call(kernel_body, *, name, grid, in_specs, out_specs, out_shape, args, scratch_shapes=(), ride=None):
    n_in, n_out, n_sc = len(in_specs), len(out_specs), len(scratch_shapes)
    nr = len(ride.srcs) if ride else 0
    in_specs, out_specs, out_shape = list(in_specs), list(out_specs), list(out_shape)
    scratch_shapes, args = list(scratch_shapes), list(args)
    if ride is not None:
        in_specs += [ANY] * nr
        out_specs += [ANY] * nr
        out_shape += ride.out_shape
        args += ride.srcs
        scratch_shapes += ride.sems

    def body(*refs):
        ins, refs = refs[:n_in], refs[n_in:]
        rin, refs = refs[:nr], refs[nr:]
        outs, refs = refs[:n_out], refs[n_out:]
        rout, refs = refs[:nr], refs[nr:]
        scratch, rsems = refs[:n_sc], refs[n_sc:]
        first, last = True, True
        for axis, extent in enumerate(grid):
            first &= pl.program_id(axis) == 0
            last &= pl.program_id(axis) == extent - 1
        if ride is not None:
            middle = last if len(grid) > 1 else pl.program_id(0) == (3 * grid[0]) // 4
            exchange = ride.exchange(rin, rout, rsems)
            pl.when(first)(exchange.start)
            pl.when(middle)(exchange.forward)
        kernel_body(*ins, *outs, *scratch)
        if ride is not None:
            pl.when(last)(exchange.finish)

    res = pl.pallas_call(body, name=name, grid=grid, in_specs=in_specs, out_specs=out_specs, out_shape=out_shape,
                         scratch_shapes=scratch_shapes, compiler_params=_params(*(("arbitrary",) * len(grid))))(*args)
    return res[:n_out], res[n_out:]


def _ada_update(c_all, dmod, w, m, v):
    nl, d, ncol = w.shape
    bd = min(d, 256)

    def body(c_ref, dm_ref, w_ref, m_ref, v_ref, g_ref, d_ref, mo_ref, vo_ref):
        cv = c_ref[...]
        g = lax.dot_general(cv * jax.nn.sigmoid(cv), dm_ref[0], (((0,), (0,)), ((), ())), preferred_element_type=F32,
                            precision=lax.Precision.HIGHEST)
        g_ref[0] = g
        d_ref[0], mo_ref[0], vo_ref[0] = _adam_math(w_ref[0], g, m_ref[0], v_ref[0])

    blk = pl.BlockSpec((1, bd, ncol), lambda i, j: (i, j, 0))
    return pl.pallas_call(body, name="ada_update", grid=(nl, d // bd),
        in_specs=[pl.BlockSpec((NDEV, bd), lambda i, j: (0, j)), pl.BlockSpec((1, NDEV, ncol), lambda i, j: (i, 0, 0)),
                  blk, blk, blk],
        out_specs=[blk] * 4, out_shape=[jax.ShapeDtypeStruct(w.shape, F32)] * 4,
        compiler_params=_params("arbitrary", "arbitrary"),
    )(c_all, dmod, w, m, v)


def _pack_stats(st_c, st_m, st_f0, st_p, st_f1, st_loss, gwdw):
    d = st_m.shape[1]

    def body(c_ref, m_ref, f0_ref, p_ref, f1_ref, l_ref, w_ref, o_ref):
        pieces = [
            c_ref[1:3, 0:d], m_ref[0:1, :], f0_ref[0:3, :], p_ref[0:2, :], p_ref[4:5, :], f1_ref[0:3, :],
            c_ref[3:4, 0:d], p_ref[3:4, :], f0_ref[3:4, :], f1_ref[3:4, :],
            c_ref[0:1, 0:d], c_ref[0:1, d:2 * d], m_ref[4:5, :], m_ref[2:4, :], m_ref[1:2, :], l_ref[0:1, :],
            w_ref[0:CONV_WIDTH, :], p_ref[5:6, :], l_ref[2:3, :]]
        row = 0
        for piece in pieces:
            o_ref[pl.ds(row, piece.shape[0]), :] = piece
            row += piece.shape[0]

    return pl.pallas_call(body, name="pack_stats",
                          out_shape=jax.ShapeDtypeStruct((12 + 4 + 7 + CONV_WIDTH + 2, d), F32),
                          )(st_c, st_m, st_f0, st_p, st_f1, st_loss, gwdw)


def _row_block(r, c, bytes_per_row_elem=4, budget=2 * 2**20):
    if r * c * bytes_per_row_elem <= budget or r % 8:
        return r
    best = 8
    for b in range(8, r + 1, 8):
        if r % b == 0 and b * c * bytes_per_row_elem <= budget:
            best = b
    return best


def _sum_slots(land, name):
    _, r, c = land.shape
    br = _row_block(r, c, 8 * land.dtype.itemsize)

    def body(l_ref, o_ref):
        acc = l_ref[0].astype(F32)
        for s in range(1, NDEV):
            acc = acc + l_ref[s].astype(F32)
        o_ref[...] = acc

    return pl.pallas_call(body, name=name, grid=(r // br,),
        in_specs=[pl.BlockSpec((NDEV, br, c), lambda i: (0, i, 0))],
        out_specs=pl.BlockSpec((br, c), lambda i: (i, 0)),
        out_shape=jax.ShapeDtypeStruct((r, c), F32),
        compiler_params=_params("arbitrary"),
    )(land)


def _adamw(w, g, m, v, name):
    r, c = w.shape
    br = _row_block(r, c)

    def body(w_ref, g_ref, m_ref, v_ref, d_ref, mo_ref, vo_ref):
        d_ref[...], mo_ref[...], vo_ref[...] = _adam_math(w_ref[...], g_ref[...], m_ref[...], v_ref[...])

    spec = pl.BlockSpec((br, c), lambda i: (i, 0))
    return pl.pallas_call(body, name=name, grid=(r // br,),
        in_specs=[spec] * 4, out_specs=[spec] * 3,
        out_shape=[jax.ShapeDtypeStruct((r, c), F32)] * 3,
        compiler_params=_params("arbitrary"),
    )(w, g, m, v)


def _adam_math(w, g, m, v):
    m2 = ADAM_B1 * m + (1.0 - ADAM_B1) * g
    v2 = ADAM_B2 * v + (1.0 - ADAM_B2) * (g * g)
    m_hat = m2 / (1.0 - ADAM_B1 ** ADAM_STEP)
    v_hat = v2 / (1.0 - ADAM_B2 ** ADAM_STEP)
    return -ADAM_LR * (m_hat / (jnp.sqrt(v_hat) + ADAM_EPS) + ADAM_WD * w), m2, v2


def _slot_sum(land_ref, *lead):
    acc = land_ref[(0,) + lead].astype(F32)
    for s in range(1, NDEV):
        acc = acc + land_ref[(s,) + lead].astype(F32)
    return acc


def _finalize(land, w, m, v, transposed, name):
    _, r, c = land.shape
    cb = 256 if (transposed and c % 256 == 0) else c
    wblk = pl.BlockSpec((cb, r), lambda i: (i, 0)) if transposed else pl.BlockSpec((r, cb), lambda i: (0, i))

    def body(l_ref, w_ref, m_ref, v_ref, g_ref, d_ref, mo_ref, vo_ref):
        g = _slot_sum(l_ref)
        g = g.T if transposed else g
        g_ref[...] = g
        d_ref[...], mo_ref[...], vo_ref[...] = _adam_math(w_ref[...], g, m_ref[...], v_ref[...])

    return pl.pallas_call(body, name=name, grid=(c // cb,),
        in_specs=[pl.BlockSpec((NDEV, r, cb), lambda i: (0, 0, i)), wblk, wblk, wblk], out_specs=[wblk] * 4,
        out_shape=[jax.ShapeDtypeStruct(w.shape, F32)] * 4,
        compiler_params=_params("arbitrary"),
    )(land, w, m, v)


def _finalize_ffn(land0, land1, wg, wu, wd, mg, mu, md, vg, vu, vd):
    nl, fs, d = wg.shape
    db = min(256, d)

    def kernel_body(l0_ref, l1_ref, wg_ref, wu_ref, wd_ref, mg_ref, mu_ref, md_ref, vg_ref, vu_ref, vd_ref, *outs):
        layer = pl.program_id(0)
        triples = [(wg_ref, mg_ref, vg_ref), (wu_ref, mu_ref, vu_ref), (wd_ref, md_ref, vd_ref)]

        def run(land_ref):
            for j, (w_ref, m_ref, v_ref) in enumerate(triples):
                g = _slot_sum(land_ref, j)
                delta, m2, v2 = _adam_math(w_ref[0], g, m_ref[0], v_ref[0])
                for o_ref, val in zip(outs[j::3], (g, delta, m2, v2)):
                    o_ref[0] = val

        @pl.when(layer == 0)
        def _():
            run(l0_ref)

        @pl.when(layer == 1)
        def _():
            run(l1_ref)

    blk = pl.BlockSpec((1, fs, db), lambda l, i: (l, 0, i))
    lblk = [pl.BlockSpec((NDEV, 3, fs, db), lambda l, i: (0, 0, 0, i * (1 - l))),
            pl.BlockSpec((NDEV, 3, fs, db), lambda l, i: (0, 0, 0, i * l))]
    outs, _ = _call(kernel_body, name="finalize_ffn", grid=(nl, d // db),
        in_specs=lblk + [blk] * 9, out_specs=[blk] * 12, out_shape=[jax.ShapeDtypeStruct(wg.shape, F32)] * 12,
        args=(land0, land1, wg, wu, wd, mg, mu, md, vg, vu, vd))
    return outs


CONV_ROWS = 32
LANES = 128


def _shifted_copies(buf, sh, n):
    sh[0] = buf[...]
    for r in range(1, 8):
        sh[r, pl.ds(0, n - 8), :] = buf[pl.ds(r, n - 8), :]


def _window(sh, o, rows, cols):
    return sh[o % 8, pl.ds(o - o % 8, rows), cols]


def _conv_in(x, mod, gmix, gw, w1_rows, w1_idx, b1, tm, ride):
    s, d = x.shape

    def kernel_body(x_ref, mod_ref, g_ref, w_ref, b_ref, h_ref, u_ref, glu_ref):
        n, _ = _rms(x_ref[...])
        h = (n * g_ref[...]) * (1.0 + mod_ref[1:2, :]) + mod_ref[0:1, :]
        hb = h.astype(BF16)
        h_ref[...] = hb
        u = _nt(hb, w_ref[...].reshape(NDEV * w1_rows, d)) + b_ref[...]
        u_ref[...] = u.astype(BF16)
        glu_ref[...] = (u[:, :d] * jax.nn.sigmoid(u[:, d:])).astype(BF16)

    tile = pl.BlockSpec((tm, d), lambda i: (i, 0))
    return _call(kernel_body, name="conv_in", grid=(s // tm,),
        in_specs=[tile, _row(8, d), _row(1, d), _weight_spec(w1_rows, d, w1_idx), _row(1, 2 * d)],
        out_specs=[tile, pl.BlockSpec((tm, 2 * d), lambda i: (i, 0)), tile],
        out_shape=[jax.ShapeDtypeStruct((s, d), BF16), jax.ShapeDtypeStruct((s, 2 * d), BF16),
                   jax.ShapeDtypeStruct((s, d), BF16)],
        args=(x, mod, gmix, gw, b1), ride=ride)


def _conv_mid(glu, wdw, bdw, ln_g, ln_b, gw, w2_rows, w2_idx, b2, x, mod, tm, ride):
    s, d = x.shape
    off = CONV_HALO - (CONV_WIDTH - 1)
    rc = min(CONV_ROWS, tm)
    nsub = 2 if s % (2 * tm) == 0 else 1
    tb = nsub * tm

    def kernel_body(glu_ref, halo_ref, wdw_ref, bdw_ref, lng_ref, lnb_ref, w_ref, b2_ref, x_ref, mod_ref,
                    dwc_ref, s_ref, y_ref, x1_ref, buf, sh):
        i = pl.program_id(0)
        w2 = w_ref[...].reshape(NDEV * w2_rows, d)
        for sub in range(nsub):
            rows = pl.ds(sub * tm, tm)
            if sub == 0:
                buf[pl.ds(0, CONV_HALO), :] = jnp.where(i > 0, halo_ref[...].astype(F32), 0.0)
            else:
                buf[pl.ds(0, CONV_HALO), :] = glu_ref[pl.ds(sub * tm - CONV_HALO, CONV_HALO), :].astype(F32)
            buf[pl.ds(CONV_HALO, tm), :] = glu_ref[rows, :].astype(F32)
            _shifted_copies(buf, sh, CONV_HALO + tm)
            for cb in range(d // LANES):
                cols = pl.ds(cb * LANES, LANES)
                taps = wdw_ref[:, cols]
                for r in range(tm // rc):
                    part = jnp.zeros((rc, LANES), F32) + bdw_ref[:, cols]
                    for k in range(CONV_WIDTH):
                        part = part + _window(sh, r * rc + off + k, rc, cols) * taps[k:k + 1, :]
                    dwc_ref[pl.ds(sub * tm + r * rc, rc), cols] = part
            acc = dwc_ref[rows, :]
            mu = jnp.mean(acc, axis=-1, keepdims=True)
            xc = acc - mu
            rstd = lax.rsqrt(jnp.mean(xc * xc, axis=-1, keepdims=True) + EPS)
            ln = (xc * rstd) * lng_ref[...] + lnb_ref[...]
            sb = (ln * jax.nn.sigmoid(ln)).astype(BF16)
            s_ref[rows, :] = sb
            y = _nn(sb, w2) + b2_ref[...]
            y_ref[rows, :] = y.astype(BF16)
            x1_ref[rows, :] = x_ref[rows, :] + (1.0 + mod_ref[2:3, :]) * y

    tile = pl.BlockSpec((tb, d), lambda i: (i, 0))
    halo = pl.BlockSpec((CONV_HALO, d), lambda i: (jnp.maximum(i * (tb // CONV_HALO) - 1, 0), 0))
    return _call(kernel_body, name="conv_mid", grid=(s // tb,),
        in_specs=[tile, halo, _row(CONV_WIDTH, d), _row(1, d), _row(1, d), _row(1, d),
                  _weight_spec(w2_rows, d, w2_idx), _row(1, d), tile, _row(8, d)],
        out_specs=[tile, tile, tile, tile],
        out_shape=[jax.ShapeDtypeStruct((s, d), F32), jax.ShapeDtypeStruct((s, d), BF16),
                   jax.ShapeDtypeStruct((s, d), BF16), jax.ShapeDtypeStruct((s, d), F32)],
        scratch_shapes=[pltpu.VMEM((CONV_HALO + tm, d), F32), pltpu.VMEM((8, CONV_HALO + tm, d), F32)],
        args=(glu, glu, wdw, bdw, ln_g, ln_b, gw, b2, x, mod), ride=ride)


def _ffn_fwd(x, mod, gffn, weights, fs, f, tm, name, ride=None, loss=None):
    s, d = x.shape

    last = s // tm - 1

    def kernel_body(x_ref, mod_ref, g_ref, wg_ref, wu_ref, wd_ref, *rest):
        xv = x_ref[...]
        n, _ = _rms(xv)
        hb = ((n * g_ref[...]) * (1.0 + mod_ref[4:5, :]) + mod_ref[3:4, :]).astype(BF16)
        gg = _nt(hb, wg_ref[...].reshape(f, d))
        uu = _nt(hb, wu_ref[...].reshape(f, d))
        ab = ((gg * jax.nn.sigmoid(gg)) * uu).astype(BF16)
        y = _nn(ab, wd_ref[...].reshape(f, d))
        xo = xv + (1.0 + mod_ref[5:6, :]) * y
        if loss is None:
            h_ref, gg_ref, uu_ref, y_ref, xo_ref = rest
            xo_ref[...] = xo
        else:
            t_ref, gfin_ref, h_ref, gg_ref, uu_ref, y_ref, dx_ref, st_ref = rest
            dx_ref[...] = _loss_tile(xo, t_ref[...], gfin_ref[...], st_ref, pl.program_id(0), last)
        h_ref[...] = hb
        gg_ref[...] = gg.astype(BF16)
        uu_ref[...] = uu.astype(BF16)
        y_ref[...] = y.astype(BF16)

    tile = pl.BlockSpec((tm, d), lambda i: (i, 0))
    wide = pl.BlockSpec((tm, f), lambda i: (i, 0))
    extra_in = [] if loss is None else [tile, _row(1, d)]
    extra_out = [] if loss is None else [_row(8, d)]
    return _call(kernel_body, name=name, grid=(s // tm,),
        in_specs=[tile, _row(8, d), _row(1, d)] + [_weight_spec(fs, d, idx) for _, idx in weights] + extra_in,
        out_specs=[tile, wide, wide, tile, tile] + extra_out,
        out_shape=[jax.ShapeDtypeStruct((s, d), BF16), jax.ShapeDtypeStruct((s, f), BF16),
                   jax.ShapeDtypeStruct((s, f), BF16), jax.ShapeDtypeStruct((s, d), BF16),
                   jax.ShapeDtypeStruct((s, d), F32)] + [jax.ShapeDtypeStruct((8, d), F32)] * len(extra_out),
        args=(x, mod, gffn) + tuple(a for a, _ in weights) + (() if loss is None else tuple(loss)), ride=ride)


def _loss_tile(x, target, g, st_ref, i, last):
    d = x.shape[1]

    @pl.when(i == 0)
    def _():
        st_ref[...] = jnp.zeros_like(st_ref)

    n, rinv = _rms(x)
    err = n * g - target
    dy = err * (1.0 / d)
    st_ref[0:1, :] += _sum0(dy * n)
    st_ref[1:2, :] += _sum0(err * err) * (0.5 / d)

    @pl.when(i == last)
    def _():
        st_ref[2:3, :] = jnp.zeros((1, d), F32) + jnp.sum(st_ref[1:2, :], axis=-1, keepdims=True)

    dn = dy * g
    return rinv * (dn - n * jnp.mean(dn * n, axis=-1, keepdims=True))


def _pool_fwd(x, mod, gmix, pw, ls, tm, ride):
    s, d = x.shape
    dg = d // len(POOL_WINDOWS)

    def body(x_ref, halo_ref, mod_ref, g_ref, pw_ref, ls_ref, mixed_ref, yp_ref, xo_ref, buf):
        i = pl.program_id(0)

        def hfun(xv):
            n, _ = _rms(xv)
            return (n * g_ref[...]) * (1.0 + mod_ref[1:2, :]) + mod_ref[0:1, :]

        xv = x_ref[...]
        h = hfun(xv)
        buf[pl.ds(0, POOL_HALO), :] = jnp.where(i > 0, hfun(halo_ref[...]), 0.0)
        buf[pl.ds(POOL_HALO, tm), :] = h
        t = i * tm + lax.broadcasted_iota(jnp.int32, (tm, 1), 0)
        gate = 1.0 + mod_ref[2:3, :]
        for gi, w in enumerate(POOL_WINDOWS):
            cols = pl.ds(gi * dg, dg)
            ws = buf[pl.ds(POOL_HALO, tm), cols]
            for j in range(1, w):
                ws = ws + buf[pl.ds(POOL_HALO - j, tm), cols]
            inv = 1.0 / jnp.minimum(t + 1, w).astype(F32)
            mb = (ws * inv - h[:, gi * dg:(gi + 1) * dg]).astype(BF16)
            mixed_ref[:, cols] = mb
            yp = _nn(mb, pw_ref[gi])
            yp_ref[:, cols] = yp.astype(BF16)
            xo_ref[:, cols] = xv[:, gi * dg:(gi + 1) * dg] + gate[:, gi * dg:(gi + 1) * dg] * (yp * ls_ref[:, cols])

    tile = pl.BlockSpec((tm, d), lambda i: (i, 0))
    halo = pl.BlockSpec((POOL_HALO, d), lambda i: (jnp.maximum(i * (tm // POOL_HALO) - 1, 0), 0))
    return _call(body, name="pool_fwd", grid=(s // tm,),
        in_specs=[tile, halo, _row(8, d), _row(1, d), pl.BlockSpec((len(POOL_WINDOWS), dg, dg), lambda i: (0, 0, 0)),
                  _row(1, d)],
        out_specs=[tile, tile, tile],
        out_shape=[jax.ShapeDtypeStruct((s, d), BF16), jax.ShapeDtypeStruct((s, d), BF16),
                   jax.ShapeDtypeStruct((s, d), F32)],
        scratch_shapes=[pltpu.VMEM((POOL_HALO + tm, d), F32)],
        args=(x, x, mod, gmix, pw, ls), ride=ride)


def _ffn_bwd(dxo, x, gg, uu, y, mod, gffn, weights, fs, f, tm, name, ride=None):
    s, d = x.shape

    steps = s // tm

    def kernel_body(dxo_ref, x_ref, gg_hbm, uu_hbm, y_ref, mod_ref, g_ref, wg_ref, wu_ref, wd_ref,
                    dg_ref, du_ref, a_ref, dy_ref, dxi_ref, st_ref, gbuf, ubuf, ring_sems):
        i = pl.program_id(0)

        def fetch(step, slot):
            rows = pl.ds(pl.multiple_of(step * tm, tm), tm)
            return [pltpu.make_async_copy(src.at[rows, :], buf.at[slot], ring_sems.at[k, slot])
                    for k, (src, buf) in enumerate(((gg_hbm, gbuf), (uu_hbm, ubuf)))]

        @pl.when(i == 0)
        def _():
            st_ref[...] = jnp.zeros_like(st_ref)
            for ahead in range(min(2, steps)):
                for cp in fetch(ahead, ahead):
                    cp.start()

        @pl.when(i + 2 < steps)
        def _():
            for cp in fetch(i + 2, (i + 2) % 3):
                cp.start()

        dxo_v = dxo_ref[...]
        dyb = (dxo_v * (1.0 + mod_ref[5:6, :])).astype(BF16)
        dy_ref[...] = dyb
        da = _nt(dyb, wd_ref[...].reshape(f, d))
        slot = i % 3
        for cp in fetch(i, slot):
            cp.wait()
        ggv, uuv = gbuf[slot].astype(F32), ubuf[slot].astype(F32)
        sg = jax.nn.sigmoid(ggv)
        silu = ggv * sg
        a_ref[...] = (silu * uuv).astype(BF16)
        dub = (da * silu).astype(BF16)
        dgb = (da * uuv * _silu_grad(ggv, sg)).astype(BF16)
        du_ref[...] = dub
        dg_ref[...] = dgb
        dh = _nn(dgb, wg_ref[...].reshape(f, d)) + _nn(dub, wu_ref[...].reshape(f, d))
        n, rinv = _rms(x_ref[...])
        dx, dsh, dsc, dgain = _rms_mod_bwd(dh, n, rinv, g_ref[...], mod_ref[4:5, :])
        dxi_ref[...] = dxo_v + dx
        st_ref[0:1, :] += dsh
        st_ref[1:2, :] += dsc
        st_ref[2:3, :] += _sum0(dxo_v * y_ref[...].astype(F32))
        st_ref[3:4, :] += dgain

    tile = pl.BlockSpec((tm, d), lambda i: (i, 0))
    wide = pl.BlockSpec((tm, f), lambda i: (i, 0))
    return _call(kernel_body, name=name, grid=(s // tm,),
        in_specs=[tile, tile, ANY, ANY, tile, _row(8, d), _row(1, d)] + [_weight_spec(fs, d, idx) for _, idx in weights],
        out_specs=[wide, wide, wide, tile, tile, _row(8, d)],
        out_shape=[jax.ShapeDtypeStruct((s, f), BF16)] * 3 + [jax.ShapeDtypeStruct((s, d), BF16),
                   jax.ShapeDtypeStruct((s, d), F32), jax.ShapeDtypeStruct((8, d), F32)],
        scratch_shapes=[pltpu.VMEM((3, tm, f), BF16), pltpu.VMEM((3, tm, f), BF16), pltpu.SemaphoreType.DMA((2, 3))],
        args=(dxo, x, gg, uu, y, mod, gffn) + tuple(a for a, _ in weights), ride=ride)


def _ffn_wgrad(dgb, dub, ab, h, dyb, fb, ts, name):
    s, f = dgb.shape
    d = h.shape[1]
    last = s // ts - 1

    def body(dg_ref, du_ref, a_ref, h_ref, dy_ref, o_ref, acc):
        t = pl.program_id(1)

        @pl.when(t == 0)
        def _():
            acc[...] = jnp.zeros_like(acc)

        hv = h_ref[...]
        acc[0] += _tn(dg_ref[...], hv)
        acc[1] += _tn(du_ref[...], hv)
        acc[2] += _tn(a_ref[...], dy_ref[...])

        @pl.when(t == last)
        def _():
            o_ref[...] = acc[...].astype(BF16)

    wide = pl.BlockSpec((ts, fb), lambda j, t: (t, j))
    tile = pl.BlockSpec((ts, d), lambda j, t: (t, 0))
    return pl.pallas_call(body, name=name, grid=(f // fb, s // ts),
        in_specs=[wide, wide, wide, tile, tile],
        out_specs=pl.BlockSpec((3, fb, d), lambda j, t: (0, j, 0)),
        out_shape=jax.ShapeDtypeStruct((3, f, d), BF16),
        scratch_shapes=[pltpu.VMEM((3, fb, d), F32)],
        compiler_params=_params("arbitrary", "arbitrary"),
    )(dgb, dub, ab, h, dyb)


def _pool_bwd(dxo, x, yp, mixed, mod, gmix, pw, ls, tm):
    s, d = x.shape
    ng = len(POOL_WINDOWS)
    dg = d // ng
    last = s // tm - 1

    def body(dxo_ref, dxh_ref, x_ref, yp_ref, mixed_ref, mod_ref, g_ref, pw_ref, ls_ref,
             dxi_ref, dpw_ref, st_ref, bufy, bufq, bufh, acc):
        i = pl.program_id(0)

        @pl.when(i == 0)
        def _():
            st_ref[...] = jnp.zeros_like(st_ref)
            acc[...] = jnp.zeros_like(acc)

        gate = 1.0 + mod_ref[2:3, :]
        lsv = ls_ref[...]
        dxo_v = dxo_ref[...]
        st_ref[2:3, :] += _sum0(dxo_v * yp_ref[...].astype(F32))
        bufy[pl.ds(0, tm), :] = (dxo_v * (gate * lsv)).astype(BF16)
        bufy[pl.ds(tm, POOL_HALO), :] = jnp.where(i < last, dxh_ref[...] * (gate * lsv), 0.0).astype(BF16)
        t = i * tm + lax.broadcasted_iota(jnp.int32, (tm + POOL_HALO, 1), 0)
        for gi, w in enumerate(POOL_WINDOWS):
            cols = pl.ds(gi * dg, dg)
            dm = _nt(bufy[:, cols], pw_ref[gi])
            bufq[:, cols] = dm * (1.0 / jnp.minimum(t + 1, w).astype(F32))
            dh = bufq[pl.ds(0, tm), cols] - dm[0:tm, :]
            for j in range(1, w):
                dh = dh + bufq[pl.ds(j, tm), cols]
            bufh[:, cols] = dh
            acc[gi] += _tn(mixed_ref[:, cols], bufy[pl.ds(0, tm), cols])
        n, rinv = _rms(x_ref[...])
        dx, dsh, dsc, dgain = _rms_mod_bwd(bufh[...], n, rinv, g_ref[...], mod_ref[1:2, :])
        dxi_ref[...] = dxo_v + dx
        st_ref[0:1, :] += dsh
        st_ref[1:2, :] += dsc
        st_ref[3:4, :] += dgain

        @pl.when(i == last)
        def _():
            r = st_ref[2:3, :]
            st_ref[4:5, :] = r * lsv
            st_ref[5:6, :] = r * gate
            dpw_ref[...] = acc[...].astype(BF16)

    tile = pl.BlockSpec((tm, d), lambda i: (i, 0))
    nxt = pl.BlockSpec((POOL_HALO, d), lambda i: (jnp.minimum((i + 1) * (tm // POOL_HALO), s // POOL_HALO - 1), 0))
    pws = pl.BlockSpec((ng, dg, dg), lambda i: (0, 0, 0))
    return _call(body, name="pool_bwd", grid=(s // tm,),
        in_specs=[tile, nxt, tile, tile, tile, _row(8, d), _row(1, d), pws, _row(1, d)],
        out_specs=[tile, pws, _row(8, d)],
        out_shape=[jax.ShapeDtypeStruct((s, d), F32), jax.ShapeDtypeStruct((ng, dg, dg), BF16),
                   jax.ShapeDtypeStruct((8, d), F32)],
        scratch_shapes=[pltpu.VMEM((tm + POOL_HALO, d), BF16), pltpu.VMEM((tm + POOL_HALO, d), F32),
                        pltpu.VMEM((tm, d), F32), pltpu.VMEM((ng, dg, dg), F32)],
        args=(dxo, dxo, x, yp, mixed, mod, gmix, pw, ls))[0]


def _conv_bwd_mid(dxo, y, dwc, sb, mod, ln_g, ln_b, gw, w2_rows, w2_idx, tm):
    s, d = dwc.shape
    last = s // tm - 1

    def body(dxo_ref, y_ref, dwc_ref, s_ref, mod_ref, lng_ref, lnb_ref, w_ref, dd_ref, dw_ref, st_ref, acc):
        i = pl.program_id(0)

        @pl.when(i == 0)
        def _():
            st_ref[...] = jnp.zeros_like(st_ref)
            acc[...] = jnp.zeros_like(acc)

        dxo_v = dxo_ref[...]
        st_ref[0:1, :] += _sum0(dxo_v * y_ref[...].astype(F32))
        dy = dxo_v * (1.0 + mod_ref[2:3, :])
        st_ref[1:2, :] += _sum0(dy)
        dyb = dy.astype(BF16)
        ds = _nt(dyb, w_ref[...].reshape(NDEV * w2_rows, d))
        acc[...] += _tn(s_ref[...], dyb)
        v = dwc_ref[...]
        mu = jnp.mean(v, axis=-1, keepdims=True)
        xc = v - mu
        rstd = lax.rsqrt(jnp.mean(xc * xc, axis=-1, keepdims=True) + EPS)
        xhat = xc * rstd
        ln = xhat * lng_ref[...] + lnb_ref[...]
        dln = ds * _silu_grad(ln, jax.nn.sigmoid(ln))
        st_ref[2:3, :] += _sum0(dln * xhat)
        st_ref[3:4, :] += _sum0(dln)
        dxh = dln * lng_ref[...]
        dd = rstd * (dxh - jnp.mean(dxh, axis=-1, keepdims=True) - xhat * jnp.mean(dxh * xhat, axis=-1, keepdims=True))
        dd_ref[...] = dd
        st_ref[4:5, :] += _sum0(dd)

        @pl.when(i == last)
        def _():
            dw_ref[...] = acc[...].astype(BF16)

    tile = pl.BlockSpec((tm, d), lambda i: (i, 0))
    return _call(body, name="conv_bwd_mid", grid=(s // tm,),
        in_specs=[tile, tile, tile, tile, _row(8, d), _row(1, d), _row(1, d), _weight_spec(w2_rows, d, w2_idx)],
        out_specs=[tile, pl.BlockSpec((d, d), lambda i: (0, 0)), _row(8, d)],
        out_shape=[jax.ShapeDtypeStruct((s, d), F32), jax.ShapeDtypeStruct((d, d), BF16),
                   jax.ShapeDtypeStruct((8, d), F32)],
        scratch_shapes=[pltpu.VMEM((d, d), F32)],
        args=(dxo, y, dwc, sb, mod, ln_g, ln_b, gw))[0]


def _conv_bwd_in(dd, glu, u, hb, x, dxo, wdw, gw, w1_rows, w1_idx, mod, gmix, tm, nsub, ride):
    s, d = x.shape
    rw = 32
    tap_group = 16
    tb = nsub * tm
    last = s // tb - 1

    def kernel_body(dd_ref, ddn_ref, glu_ref, u_ref, h_ref, x_ref, dxo_ref, wdw_ref, w_ref, mod_ref, g_ref,
                    dxi_ref, dw_ref, dwdw_ref, st_ref, bufd, shd, dgl, accw, acc):
        i = pl.program_id(0)

        @pl.when(i == 0)
        def _():
            st_ref[...] = jnp.zeros_like(st_ref)
            accw[...] = jnp.zeros_like(accw)
            acc[...] = jnp.zeros_like(acc)

        w1t = w_ref[...].reshape(NDEV * w1_rows, d)
        for sub in range(nsub):
            base = sub * tm
            tile_rows = pl.ds(base, tm)
            bufd[pl.ds(0, tm), :] = dd_ref[tile_rows, :]
            if sub == nsub - 1:
                bufd[pl.ds(tm, CONV_HALO), :] = jnp.where(i < last, ddn_ref[...], 0.0)
            else:
                bufd[pl.ds(tm, CONV_HALO), :] = dd_ref[pl.ds(base + tm, CONV_HALO), :]
            _shifted_copies(bufd, shd, tm + CONV_HALO)
            for cb in range(d // LANES):
                cols = pl.ds(cb * LANES, LANES)
                taps = wdw_ref[:, cols]
                for r in range(tm // rw):
                    part = jnp.zeros((rw, LANES), F32)
                    for k in range(CONV_WIDTH):
                        part = part + _window(shd, r * rw + CONV_WIDTH - 1 - k, rw, cols) * taps[k:k + 1, :]
                    dgl[sub, pl.ds(r * rw, rw), cols] = part
                for k0 in range(0, CONV_WIDTH, tap_group):
                    group = range(k0, min(CONV_WIDTH, k0 + tap_group))
                    sums = {k: jnp.zeros((8, LANES), F32) for k in group}
                    for r in range(tm // rw):
                        gch = glu_ref[pl.ds(base + r * rw, rw), cols].astype(F32)
                        for k in group:
                            p = _window(shd, r * rw + CONV_WIDTH - 1 - k, rw, cols) * gch
                            for q in range(rw // 8):
                                sums[k] = sums[k] + p[q * 8:(q + 1) * 8, :]
                    for k in group:
                        accw[k, :, cols] += sums[k]
            dglu = dgl[sub]
            uv = u_ref[tile_rows, :].astype(F32)
            a, g = uv[:, :d], uv[:, d:]
            sg = jax.nn.sigmoid(g)
            du = jnp.concatenate([dglu * sg, dglu * a * (sg * (1.0 - sg))], axis=1)
            st_ref[0:1, :] += _sum0(du)
            dub = du.astype(BF16)
            dh = _nn(dub, w1t)
            acc[...] += _tn(dub, h_ref[tile_rows, :])
            n, rinv = _rms(x_ref[tile_rows, :])
            dx, dsh, dsc, dgain = _rms_mod_bwd(dh, n, rinv, g_ref[...], mod_ref[1:2, :])
            dxi_ref[tile_rows, :] = dxo_ref[tile_rows, :] + dx
            st_ref[1:2, 0:d] += dsh
            st_ref[2:3, 0:d] += dsc
            st_ref[3:4, 0:d] += dgain

        @pl.when(i == last)
        def _():
            dw_ref[...] = acc[...].astype(BF16)
            dwdw_ref[...] = jnp.sum(accw[...], axis=1)

    tile = pl.BlockSpec((tb, d), lambda i: (i, 0))
    nxt = pl.BlockSpec((CONV_HALO, d), lambda i: (jnp.minimum((i + 1) * (tb // CONV_HALO), s // CONV_HALO - 1), 0))
    return _call(kernel_body, name="conv_bwd_in", grid=(s // tb,),
        in_specs=[tile, nxt, tile, pl.BlockSpec((tb, 2 * d), lambda i: (i, 0)), tile, tile, tile,
                  _row(CONV_WIDTH, d), _weight_spec(w1_rows, d, w1_idx), _row(8, d), _row(1, d)],
        out_specs=[tile, pl.BlockSpec((2 * d, d), lambda i: (0, 0)), _row(CONV_HALO, d), _row(8, 2 * d)],
        out_shape=[jax.ShapeDtypeStruct((s, d), F32), jax.ShapeDtypeStruct((2 * d, d), BF16),
                   jax.ShapeDtypeStruct((CONV_HALO, d), F32), jax.ShapeDtypeStruct((8, 2 * d), F32)],
        scratch_shapes=[pltpu.VMEM((tm + CONV_HALO, d), F32), pltpu.VMEM((8, tm + CONV_HALO, d), F32),
                        pltpu.VMEM((nsub, tm, d), F32), pltpu.VMEM((CONV_HALO, 8, d), F32),
                        pltpu.VMEM((2 * d, d), F32)],
        args=(dd, dd, glu, u, hb, x, dxo, wdw, gw, mod, gmix), ride=ride)


def kernel(x, c, ada_w, ada_b, norm_mix_g, norm_ffn_g, conv_w1, conv_b1, conv_wdw, conv_bdw, conv_ln_g, conv_ln_b, conv_w2, conv_b2, pool_w, pool_ls, ffn_w_gate, ffn_w_up, ffn_w_down, final_g, loss_target, m_ada_w, m_ada_b, m_norm_mix_g, m_norm_ffn_g, m_conv_w1, m_conv_b1, m_conv_wdw, m_conv_bdw, m_conv_ln_g, m_conv_ln_b, m_conv_w2, m_conv_b2, m_pool_w, m_pool_ls, m_ffn_w_gate, m_ffn_w_up, m_ffn_w_down, m_final_g, v_ada_w, v_ada_b, v_norm_mix_g, v_norm_ffn_g, v_conv_w1, v_conv_b1, v_conv_wdw, v_conv_bdw, v_conv_ln_g, v_conv_ln_b, v_conv_w2, v_conv_b2, v_pool_w, v_pool_ls, v_ffn_w_gate, v_ffn_w_up, v_ffn_w_down, v_final_g):
    _, s, d = x.shape
    f = ffn_w_down.shape[1] * NDEV
    fs = f // NDEV
    r1, r2 = 2 * d // NDEV, d // NDEV
    ng = len(POOL_WINDOWS)
    dg = d // ng
    pr = ng * (dg // NDEV) * dg // d
    ncol = ada_w.shape[2]
    dc = d // NDEV
    tm = min(256, s)
    me = _my_index()
    x0 = x.reshape(s, d)
    target = loss_target.reshape(s, d)

    small = jnp.concatenate([c.reshape(NDEV, dc), conv_wdw[0], pool_ls], axis=0)
    shard_a = conv_w1[0].T.astype(BF16)
    bias = lax.dynamic_slice_in_dim(ada_b, me * ncol, ncol, axis=1)[:, None, :]
    small_all, mod_all, (gwa,) = _prologue(small, ada_w, bias, _Ride("gather", [shard_a]))
    c_all = small_all[:, 0:NDEV, :].reshape(NDEV, d)
    wdw = small_all[:, NDEV:NDEV + CONV_WIDTH, :].transpose(1, 0, 2).reshape(CONV_WIDTH, d)
    ls = small_all[:, NDEV + CONV_WIDTH, :].reshape(1, d)
    mod_mine = lax.dynamic_index_in_dim(mod_all.reshape(NDEV, 2, NDEV, ncol), me, axis=2, keepdims=False)
    mod = mod_mine.transpose(1, 0, 2).reshape(2, 6, d)
    mod = jnp.concatenate([mod, jnp.zeros((2, 2, d), F32)], axis=1)

    shard_b1 = jnp.concatenate([ffn_w_gate[0].T, conv_w2[0]], axis=0).astype(BF16)
    shard_b2 = jnp.concatenate([ffn_w_up[0].T, ffn_w_down[0]], axis=0).astype(BF16)
    shard_c = jnp.concatenate([ffn_w_gate[1].T, ffn_w_up[1].T, ffn_w_down[1], pool_w.reshape(pr, d)], axis=0).astype(BF16)
    w1_at = (gwa, 0)

    (h0, u, glu), (gwb1,) = _conv_in(x0, mod[0], norm_mix_g[0:1], w1_at[0], r1, w1_at[1], conv_b1, tm,
                                    _Ride("gather", [shard_b1]))
    w2_at = (gwb1[:, fs:fs + r2, :], 0)
    (dwc, sb, y0, x1), (gwb2,) = _conv_mid(glu, wdw, conv_bdw, conv_ln_g, conv_ln_b, w2_at[0], r2, w2_at[1], conv_b2,
                                           x0, mod[0], tm, _Ride("gather", [shard_b2]))
    ffn0_w = [(gwb1, 0), (gwb2, 0), (gwb2, 1)]
    (h1, gg0, uu0, yf0, x2), (gwc,) = _ffn_fwd(x1, mod[0], norm_ffn_g[0:1], ffn0_w, fs, f, tm, "ffn_fwd0",
                                               _Ride("gather", [shard_c]))
    pw = gwc[:, 3 * fs:3 * fs + pr, :].reshape(NDEV, ng, dg // NDEV, dg).transpose(1, 0, 2, 3).reshape(ng, dg, dg)
    (mixed, yp, x3), _ = _pool_fwd(x2, mod[1], norm_mix_g[1:2], pw, ls, tm, None)
    ffn1_w = [(gwc, 0), (gwc, 1), (gwc, 2)]
    (h3, gg1, uu1, yf1, dx4, st_loss), _ = _ffn_fwd(x3, mod[1], norm_ffn_g[1:2], ffn1_w, fs, f, tm, "ffn_fwd1",
                                                    loss=(target, final_g.reshape(1, d)))

    fb = f // 2 if (f // 2) % 128 == 0 else f
    ts = min(512, s)
    (dgb, dub, ab, dyb, dx3, st_f1), _ = _ffn_bwd(dx4, x3, gg1, uu1, yf1, mod[1], norm_ffn_g[1:2], ffn1_w, fs, f, tm,
                                                  "ffn_bwd1")
    gf1 = _ffn_wgrad(dgb, dub, ab, h3, dyb, fb, ts, "ffn_wgrad1")
    dx2, gpw, st_p = _pool_bwd(dx3, x2, yp, mixed, mod[1], norm_mix_g[1:2], pw, ls, tm)
    (dgb, dub, ab, dyb, dx1, st_f0), (land_f1,) = _ffn_bwd(dx2, x1, gg0, uu0, yf0, mod[0], norm_ffn_g[0:1], ffn0_w, fs, f,
                                                           tm, "ffn_bwd0", _Ride("scatter", [gf1]))
    gf0 = _ffn_wgrad(dgb, dub, ab, h1, dyb, fb, ts, "ffn_wgrad0")
    dd, gw2, st_m = _conv_bwd_mid(dx1, y0, dwc, sb, mod[0], conv_ln_g, conv_ln_b, w2_at[0], r2, w2_at[1], tm)
    (dx0, gw1, gwdw, st_c), (land_f0, land_pw, land_w2) = _conv_bwd_in(
        dd, glu, u, h0, x0, dx1, wdw, w1_at[0], r1, w1_at[1], mod[0], norm_mix_g[0:1], tm, 1,
        _Ride("scatter", [gf0, gpw, gw2[None]]))

    prow = _pack_stats(st_c, st_m, st_f0, st_p, st_f1, st_loss, gwdw)
    p_all, (land_w1,) = _small_exchange(prow, _Ride("scatter", [gw1[None]]), "allgather_stats")
    psum = _sum_slots(p_all, "sum_stats")
    loss = psum[prow.shape[0] - 1, 0]

    tr = lambda a: jnp.swapaxes(a, 1, 2)
    ffn_out = _finalize_ffn(land_f0, land_f1, tr(ffn_w_gate), tr(ffn_w_up), ffn_w_down,
                            tr(m_ffn_w_gate), tr(m_ffn_w_up), m_ffn_w_down, tr(v_ffn_w_gate), tr(v_ffn_w_up), v_ffn_w_down)
    fin_w1 = _finalize(land_w1.reshape(NDEV, r1, d), conv_w1[0], m_conv_w1[0], v_conv_w1[0], True, "finalize_w1")
    fin_w2 = _finalize(land_w2.reshape(NDEV, r2, d), conv_w2[0], m_conv_w2[0], v_conv_w2[0], False, "finalize_w2")
    pshape = (ng * (dg // NDEV), dg)
    fin_pw = _finalize(land_pw.reshape((NDEV,) + pshape), pool_w.reshape(pshape), m_pool_w.reshape(pshape),
                       v_pool_w.reshape(pshape), False, "finalize_pool_w")
    dmod_all = p_all[:, 0:12, :].reshape(NDEV, 2, 6 * d)
    dmod_cols = lax.dynamic_slice_in_dim(dmod_all, me * ncol, ncol, axis=2).transpose(1, 0, 2)
    fin_ada = _ada_update(c_all, dmod_cols, ada_w, m_ada_w, v_ada_w)

    rep_names = ["ada_b", "norm_mix_g", "norm_ffn_g", "conv_b1", "conv_bdw", "conv_ln_g", "conv_ln_b", "conv_b2", "final_g"]
    rep_w = [ada_b, norm_mix_g, norm_ffn_g, conv_b1, conv_bdw, conv_ln_g, conv_ln_b, conv_b2, final_g]
    rep_m = [m_ada_b, m_norm_mix_g, m_norm_ffn_g, m_conv_b1, m_conv_bdw, m_conv_ln_g, m_conv_ln_b, m_conv_b2, m_final_g]
    rep_v = [v_ada_b, v_norm_mix_g, v_norm_ffn_g, v_conv_b1, v_conv_bdw, v_conv_ln_g, v_conv_ln_b, v_conv_b2, v_final_g]
    nrep = sum(w.size for w in rep_w) // d
    pad = jnp.zeros(((-nrep) % 8, d), F32)

    def pack(arrs, fill):
        return jnp.concatenate([a.reshape(-1, d) for a in arrs] + [pad + fill], axis=0)

    rep_g = jnp.concatenate([psum[0:nrep], pad], axis=0)
    rep_d, rep_mo, rep_vo = _adamw(pack(rep_w, 0.0), rep_g, pack(rep_m, 0.0), pack(rep_v, 1.0), "adamw_replicated")

    def unpack(packed):
        out, cur = [], 0
        for w in rep_w:
            k = w.size // d
            out.append(packed[cur:cur + k].reshape(w.shape))
            cur += k
        return out

    rep = dict(zip(rep_names, zip(unpack(psum), unpack(rep_d), unpack(rep_mo), unpack(rep_vo))))

    g_wdw_full = psum[nrep:nrep + CONV_WIDTH]
    g_wdw = lax.dynamic_slice_in_dim(g_wdw_full, me * dc, dc, axis=1)
    g_ls = lax.dynamic_slice_in_dim(psum[nrep + CONV_WIDTH:nrep + CONV_WIDTH + 1], me * dc, dc, axis=1)
    tiny = lambda a, b: jnp.concatenate([a.reshape(CONV_WIDTH, dc), b.reshape(1, dc)], axis=0)
    t_d, t_m, t_v = _adamw(tiny(conv_wdw, pool_ls), tiny(g_wdw, g_ls), tiny(m_conv_wdw, m_pool_ls),
                           tiny(v_conv_wdw, v_pool_ls), "adamw_taps")

    def taps(a):
        return a[0:CONV_WIDTH][None], a[CONV_WIDTH:CONV_WIDTH + 1]

    sharded = {
        "ada_w": tuple(fin_ada),
        "conv_w1": tuple(a[None] for a in fin_w1),
        "conv_w2": tuple(a[None] for a in fin_w2),
        "pool_w": tuple(a.reshape(pool_w.shape) for a in fin_pw),
        "ffn_w_gate": tuple(tr(a) for a in ffn_out[0::3]),
        "ffn_w_up": tuple(tr(a) for a in ffn_out[1::3]),
        "ffn_w_down": tuple(ffn_out[2::3]),
        "conv_wdw": (g_wdw[None], taps(t_d)[0], taps(t_m)[0], taps(t_v)[0]),
        "pool_ls": (g_ls, taps(t_d)[1], taps(t_m)[1], taps(t_v)[1]),
    }
    every = {**rep, **sharded}
    order = ["ada_w", "ada_b", "norm_mix_g", "norm_ffn_g", "conv_w1", "conv_b1", "conv_wdw", "conv_bdw", "conv_ln_g",
             "conv_ln_b", "conv_w2", "conv_b2", "pool_w", "pool_ls", "ffn_w_gate", "ffn_w_up", "ffn_w_down", "final_g"]
    grads = [every[n][0] for n in order]
    deltas = [every[n][1] for n in order]
    new_m = [every[n][2] for n in order]
    new_v = [every[n][3] for n in order]
    return (loss, dx0.reshape(1, s, d), *grads, *deltas, *new_m, *new_v)
```
